```python
import math
import jax, jax.numpy as jnp
from jax import lax
import numpy as np

D_MODEL = 1024
BATCH = 8
SEQ = 4096
DEPTH = 4

GRID_W = 64
CTX_LEN = 256
MIX_WIDTH = D_MODEL
S5_WIDTH = MIX_WIDTH // 2
S5_GROUP = 16
S5_GROUPS = S5_WIDTH // S5_GROUP
S5_STATE = 64
HG_WIDTH = MIX_WIDTH - S5_WIDTH
HG_HEAD_DIM = 128
HG_HEADS = HG_WIDTH // HG_HEAD_DIM
HG_CHUNK = 32
D_FF = 11 * D_MODEL // 4
CONV_W = 3
DT_MIN = 1e-3
DT_MAX = 1e-1
ALPHA = (2 * DEPTH) ** 0.25
BETA = (8 * DEPTH) ** -0.25
LN_EPS = 1e-5
RMS_EPS = 1e-6
U_END = S5_WIDTH
FF_END = U_END + HG_WIDTH
FB_END = FF_END + HG_WIDTH
I_END = FB_END + HG_WIDTH
Q_END = I_END + HG_WIDTH
IN_COLS = Q_END + HG_WIDTH

kernel_name = "hybrid_s5_hgrn2_deepnorm_dit"


def layer_norm(x, g, b):
    xf = x.astype(jnp.float32)
    mu = jnp.mean(xf, axis=-1, keepdims=True)
    var = jnp.mean(jnp.square(xf - mu), axis=-1, keepdims=True)
    return ((xf - mu) * lax.rsqrt(var + LN_EPS) * g + b).astype(x.dtype)


def rms_norm(x, g):
    xf = x.astype(jnp.float32)
    return xf * lax.rsqrt(jnp.mean(jnp.square(xf), axis=-1, keepdims=True) + RMS_EPS) * g


def modulate(h, shift, scale):
    return h * (1 + scale) + shift


def cmul(ar, ai, br, bi):
    return ar * br - ai * bi, ar * bi + ai * br


def s5_discretise(lam_re, lam_im, log_dt, b_re, b_im):
    f32 = jnp.float32
    lr, li = lam_re.astype(f32), lam_im.astype(f32)
    dt = jnp.exp(log_dt.astype(f32))[:, None]
    mag, ang = jnp.exp(lr * dt), li * dt
    abar_re, abar_im = mag * jnp.cos(ang), mag * jnp.sin(ang)
    den = lr * lr + li * li
    nr, ni = abar_re - 1.0, abar_im
    coef_re = (nr * lr + ni * li) / den
    coef_im = (ni * lr - nr * li) / den
    bbar_re, bbar_im = cmul(coef_re[..., None], coef_im[..., None], b_re.astype(f32), b_im.astype(f32))
    return abar_re, abar_im, bbar_re, bbar_im


def ssm_combine(e1, e2):
    a1r, a1i, b1r, b1i = e1
    a2r, a2i, b2r, b2i = e2
    ar, ai = cmul(a2r, a2i, a1r, a1i)
    br, bi = cmul(a2r, a2i, b1r, b1i)
    return ar, ai, br + b2r, bi + b2i


def s5_scan(u, abar_re, abar_im, bbar_re, bbar_im, s0):
    bu_re = jnp.einsum('bngh,gph->bngp', u, bbar_re)
    bu_im = jnp.einsum('bngh,gph->bngp', u, bbar_im)
    if s0 is not None:
        init_re, init_im = cmul(abar_re, abar_im, s0[0], s0[1])
        bu_re = bu_re.at[:, 0].add(init_re)
        bu_im = bu_im.at[:, 0].add(init_im)
    a_re = jnp.broadcast_to(abar_re, bu_re.shape)
    a_im = jnp.broadcast_to(abar_im, bu_im.shape)
    _, _, x_re, x_im = lax.associative_scan(ssm_combine, (a_re, a_im, bu_re, bu_im), axis=1)
    return x_re, x_im


def gla_direction(k, v, logf, s0, q):
    bsz, n = k.shape[:2]
    nc = n // HG_CHUNK
    chunk = lambda t: t.reshape(bsz, nc, HG_CHUNK, HG_HEADS, HG_HEAD_DIM)
    k, v, logf = chunk(k), chunk(v), chunk(logf)
    bcum = jnp.cumsum(logf, axis=2)
    bend = bcum[:, :, -1:]
    kd = k * jnp.exp(bend - bcum)
    ds = jnp.einsum('bcshk,bcshv->cbhkv', kd, v)
    dec = jnp.exp(bend[:, :, 0]).transpose(1, 0, 2, 3)
    want_out = q is not None

    def step(s, inp):
        dec_c, ds_c = inp
        return dec_c[..., None] * s + ds_c, (s if want_out else None)

    s_fin, s_start = lax.scan(step, s0, (dec, ds))
    if not want_out:
        return None, s_fin
    q = chunk(q)
    o_inter = jnp.einsum('bclhk,cbhkv->bclhv', q * jnp.exp(bcum), s_start)
    att = jnp.einsum('bclhk,bcshk->bchls', q * jnp.exp(bcum - bend), kd)
    tril = jnp.tril(jnp.ones((HG_CHUNK, HG_CHUNK), dtype=bool))
    att = jnp.where(tril, att, 0.0)
    o_intra = jnp.einsum('bchls,bcshv->bclhv', att, v)
    return (o_inter + o_intra).reshape(bsz, n, HG_HEADS, HG_HEAD_DIM), s_fin


def token_mixer(h, w_in, s5p, hgp, init, with_out, with_states):
    lam_re, lam_im, log_dt, b_re, b_im, c_re, c_im, d_skip, w_glu, b_glu = s5p
    lb, norm_w = hgp
    bsz, n = h.shape[:2]
    cols = IN_COLS if with_out else I_END
    proj = (h @ w_in[:, :cols]).astype(jnp.float32)
    u = proj[..., :U_END]
    f_raws = (proj[..., U_END:FF_END], proj[..., FF_END:FB_END])
    heads = lambda t: t.reshape(bsz, n, HG_HEADS, HG_HEAD_DIM)
    v = heads(proj[..., FB_END:I_END])
    q = heads(jax.nn.silu(proj[..., I_END:Q_END])) if with_out else None

    ug = u.reshape(bsz, n, S5_GROUPS, S5_GROUP)
    s5_ys, s5_fin, hg_os, hg_fin = [], [], [], []
    for dr in range(2):
        flip = (lambda t: t[:, ::-1]) if dr == 1 else (lambda t: t)
        abar_re, abar_im, bbar_re, bbar_im = s5_discretise(lam_re[dr], lam_im[dr], log_dt[dr], b_re[dr], b_im[dr])
        x_re, x_im = s5_scan(flip(ug), abar_re, abar_im, bbar_re, bbar_im, None if init is None else init[0][dr])
        if with_states:
            s5_fin.append((x_re[:, -1], x_im[:, -1]))
        if with_out:
            y = jnp.einsum('bngp,ghp->bngh', x_re, c_re[dr]) - jnp.einsum('bngp,ghp->bngh', x_im, c_im[dr])
            s5_ys.append(flip(y).reshape(bsz, n, S5_WIDTH))
        f = lb[dr] + (1.0 - lb[dr]) * jax.nn.sigmoid(f_raws[dr])
        s0 = jnp.zeros((bsz, HG_HEADS, HG_HEAD_DIM, HG_HEAD_DIM), jnp.float32) if init is None else init[1][dr]
        o, s_fin = gla_direction(flip(heads(1.0 - f)), flip(v), flip(heads(jnp.log(f))), s0,
                                 flip(q) if with_out else None)
        if with_states:
            hg_fin.append(s_fin)
        if with_out:
            hg_os.append(flip(o))
    states = (s5_fin, hg_fin) if with_states else None
    if not with_out:
        return None, states
    s5_y = jax.nn.gelu(s5_ys[0] + s5_ys[1] + u * d_skip)
    s5_out = s5_y * jax.nn.sigmoid(s5_y @ w_glu + b_glu)
    hg_out = rms_norm(hg_os[0] + hg_os[1], norm_w).reshape(bsz, n, HG_WIDTH) * jax.nn.silu(proj[..., Q_END:IN_COLS])
    return jnp.concatenate([s5_out, hg_out], axis=-1), states


def dwconv(u, w, b):
    m = u.shape[-2]
    pad = CONV_W // 2
    up = jnp.pad(u, [(0, 0)] * (u.ndim - 2) + [(pad, pad), (0, 0)])
    out = up[..., 0:m, :] * w[0]
    for j in range(1, CONV_W):
        out = out + up[..., j:j + m, :] * w[j]
    return out + b


def conv_ffn(h, w_up, conv_w, conv_b, w_down, on_grid):
    bsz, n = h.shape[:2]
    up = h @ w_up
    if on_grid:
        rows = n // GRID_W
        up = up.reshape(bsz, rows, GRID_W, 2 * D_FF)
    up = dwconv(up, conv_w, conv_b).reshape(bsz, n, 2 * D_FF)
    a, g = jnp.split(up, 2, axis=-1)
    return (jax.nn.silu(a) * g) @ w_down


def _fwd_setup_inputs(seed: int = 0) -> dict:
    key = jax.random.key(seed)
    ks = jax.random.split(key, 32)
    f32 = jnp.float32
    L, G, P, H = DEPTH, S5_GROUPS, S5_STATE, S5_GROUP
    nrm = lambda i, shape, scale: scale * jax.random.normal(ks[i], shape, f32)
    return {
        "x": nrm(0, (BATCH, SEQ, D_MODEL), 1.0),
        "c": nrm(1, (BATCH, D_MODEL), 1.0),
        "ctx": nrm(2, (BATCH, CTX_LEN, D_MODEL), 1.0),
        "c_ctx": nrm(3, (D_MODEL,), 1.0),
        "w_mod": nrm(4, (L, D_MODEL, 6 * D_MODEL), 0.5 * D_MODEL ** -0.5),
        "b_mod": nrm(5, (L, 6 * D_MODEL), 0.01),
        "w_in": nrm(6, (L, D_MODEL, IN_COLS), D_MODEL ** -0.5),
        "s5_lam_re": -0.5 + nrm(7, (L, 2, G, P), 0.01),
        "s5_lam_im": jnp.pi * jnp.arange(P, dtype=f32) + nrm(8, (L, 2, G, P), 0.01),
        "s5_log_dt": jax.random.uniform(ks[9], (L, 2, G), f32, math.log(DT_MIN), math.log(DT_MAX)),
        "s5_b_re": nrm(10, (L, 2, G, P, H), (2 * H) ** -0.5),
        "s5_b_im": nrm(11, (L, 2, G, P, H), (2 * H) ** -0.5),
        "s5_c_re": nrm(12, (L, 2, G, H, P), P ** -0.5),
        "s5_c_im": nrm(13, (L, 2, G, H, P), P ** -0.5),
        "s5_d": nrm(14, (L, S5_WIDTH), 1.0),
        "w_glu": nrm(15, (L, S5_WIDTH, S5_WIDTH), S5_WIDTH ** -0.5),
        "b_glu": nrm(16, (L, S5_WIDTH), 0.01),
        "hg_lb": nrm(17, (L, 2, HG_WIDTH), 0.1),
        "hg_norm_w": 1.0 + nrm(18, (L, HG_HEAD_DIM), 0.01),
        "w_out": nrm(19, (L, MIX_WIDTH, D_MODEL), BETA * MIX_WIDTH ** -0.5),
        "ln1_g": 1.0 + nrm(20, (L, D_MODEL), 0.01),
        "ln1_b": nrm(21, (L, D_MODEL), 0.01),
        "w_up": nrm(22, (L, D_MODEL, 2 * D_FF), D_MODEL ** -0.5),
        "conv_w": nrm(23, (L, CONV_W, 2 * D_FF), CONV_W ** -0.5),
        "conv_b": nrm(24, (L, 2 * D_FF), 0.01),
        "w_down": nrm(25, (L, D_FF, D_MODEL), BETA * D_FF ** -0.5),
        "ln2_g": 1.0 + nrm(26, (L, D_MODEL), 0.01),
        "ln2_b": nrm(27, (L, D_MODEL), 0.01),
    }


def _fwd_reference(x, c, ctx, c_ctx, w_mod, b_mod, w_in, s5_lam_re, s5_lam_im, s5_log_dt, s5_b_re, s5_b_im,
              s5_c_re, s5_c_im, s5_d, w_glu, b_glu, hg_lb, hg_norm_w, w_out, ln1_g, ln1_b,
              w_up, conv_w, conv_b, w_down, ln2_g, ln2_b):
    lb_all = jnp.cumsum(jax.nn.softmax(hg_lb.astype(jnp.float32), axis=0), axis=0)
    lb_all = lb_all - lb_all[:1]
    silu_c = jax.nn.silu(c)
    silu_cc = jax.nn.silu(c_ctx)
    for l in range(DEPTH):
        last = l == DEPTH - 1
        mod_x = (silu_c @ w_mod[l] + b_mod[l])[:, None, :]
        sh1, sc1, g1, sh2, sc2, g2 = jnp.split(mod_x, 6, axis=-1)
        mc = jnp.split(silu_cc @ w_mod[l] + b_mod[l], 6, axis=-1)
        s5p = (s5_lam_re[l], s5_lam_im[l], s5_log_dt[l], s5_b_re[l], s5_b_im[l],
               s5_c_re[l], s5_c_im[l], s5_d[l], w_glu[l], b_glu[l])
        hgp = (lb_all[l], hg_norm_w[l])
        y_c, ctx_states = token_mixer(modulate(ctx, mc[0], mc[1]), w_in[l], s5p, hgp, None, not last, True)
        y_x, _ = token_mixer(modulate(x, sh1, sc1), w_in[l], s5p, hgp, ctx_states, True, False)
        x = layer_norm(ALPHA * x + g1 * (y_x @ w_out[l]), ln1_g[l], ln1_b[l])
        x = layer_norm(ALPHA * x + g2 * conv_ffn(modulate(x, sh2, sc2), w_up[l], conv_w[l], conv_b[l], w_down[l], True),
                       ln2_g[l], ln2_b[l])
        if not last:
            ctx = layer_norm(ALPHA * ctx + mc[2] * (y_c @ w_out[l]), ln1_g[l], ln1_b[l])
            ctx = layer_norm(ALPHA * ctx + mc[5] * conv_ffn(modulate(ctx, mc[3], mc[4]), w_up[l], conv_w[l],
                                                             conv_b[l], w_down[l], False),
                             ln2_g[l], ln2_b[l])
    return x


import jax as _jax
import jax.numpy as _jnp

TWIN_FORMAT = 'train_step'
FWD_PARAMS = ['x', 'c', 'ctx', 'c_ctx', 'w_mod', 'b_mod', 'w_in', 's5_lam_re', 's5_lam_im', 's5_log_dt', 's5_b_re', 's5_b_im', 's5_c_re', 's5_c_im', 's5_d', 'w_glu', 'b_glu', 'hg_lb', 'hg_norm_w', 'w_out', 'ln1_g', 'ln1_b', 'w_up', 'conv_w', 'conv_b', 'w_down', 'ln2_g', 'ln2_b']
TWIN_WEIGHTS = ['c_ctx', 'w_mod', 'b_mod', 'w_in', 's5_lam_re', 's5_lam_im', 's5_log_dt', 's5_b_re', 's5_b_im', 's5_c_re', 's5_c_im', 's5_d', 'w_glu', 'b_glu', 'hg_lb', 'hg_norm_w', 'w_out', 'ln1_g', 'ln1_b', 'w_up', 'conv_w', 'conv_b', 'w_down', 'ln2_g', 'ln2_b']
TWIN_DIFF_INPUT = 'x'
TWIN_INPUTS = ['x', 'c', 'ctx', 'c_ctx', 'w_mod', 'b_mod', 'w_in', 's5_lam_re', 's5_lam_im', 's5_log_dt', 's5_b_re', 's5_b_im', 's5_c_re', 's5_c_im', 's5_d', 'w_glu', 'b_glu', 'hg_lb', 'hg_norm_w', 'w_out', 'ln1_g', 'ln1_b', 'w_up', 'conv_w', 'conv_b', 'w_down', 'ln2_g', 'ln2_b', 'loss_target', 'm_c_ctx', 'm_w_mod', 'm_b_mod', 'm_w_in', 'm_s5_lam_re', 'm_s5_lam_im', 'm_s5_log_dt', 'm_s5_b_re', 'm_s5_b_im', 'm_s5_c_re', 'm_s5_c_im', 'm_s5_d', 'm_w_glu', 'm_b_glu', 'm_hg_lb', 'm_hg_norm_w', 'm_w_out', 'm_ln1_g', 'm_ln1_b', 'm_w_up', 'm_conv_w', 'm_conv_b', 'm_w_down', 'm_ln2_g', 'm_ln2_b', 'v_c_ctx', 'v_w_mod', 'v_b_mod', 'v_w_in', 'v_s5_lam_re', 'v_s5_lam_im', 'v_s5_log_dt', 'v_s5_b_re', 'v_s5_b_im', 'v_s5_c_re', 'v_s5_c_im', 'v_s5_d', 'v_w_glu', 'v_b_glu', 'v_hg_lb', 'v_hg_norm_w', 'v_w_out', 'v_ln1_g', 'v_ln1_b', 'v_w_up', 'v_conv_w', 'v_conv_b', 'v_w_down', 'v_ln2_g', 'v_ln2_b']
TWIN_OUTPUTS = ['loss', 'grad_x', 'grad_c_ctx', 'grad_w_mod', 'grad_b_mod', 'grad_w_in', 'grad_s5_lam_re', 'grad_s5_lam_im', 'grad_s5_log_dt', 'grad_s5_b_re', 'grad_s5_b_im', 'grad_s5_c_re', 'grad_s5_c_im', 'grad_s5_d', 'grad_w_glu', 'grad_b_glu', 'grad_hg_lb', 'grad_hg_norm_w', 'grad_w_out', 'grad_ln1_g', 'grad_ln1_b', 'grad_w_up', 'grad_conv_w', 'grad_conv_b', 'grad_w_down', 'grad_ln2_g', 'grad_ln2_b', 'delta_c_ctx', 'delta_w_mod', 'delta_b_mod', 'delta_w_in', 'delta_s5_lam_re', 'delta_s5_lam_im', 'delta_s5_log_dt', 'delta_s5_b_re', 'delta_s5_b_im', 'delta_s5_c_re', 'delta_s5_c_im', 'delta_s5_d', 'delta_w_glu', 'delta_b_glu', 'delta_hg_lb', 'delta_hg_norm_w', 'delta_w_out', 'delta_ln1_g', 'delta_ln1_b', 'delta_w_up', 'delta_conv_w', 'delta_conv_b', 'delta_w_down', 'delta_ln2_g', 'delta_ln2_b', 'new_m_c_ctx', 'new_m_w_mod', 'new_m_b_mod', 'new_m_w_in', 'new_m_s5_lam_re', 'new_m_s5_lam_im', 'new_m_s5_log_dt', 'new_m_s5_b_re', 'new_m_s5_b_im', 'new_m_s5_c_re', 'new_m_s5_c_im', 'new_m_s5_d', 'new_m_w_glu', 'new_m_b_glu', 'new_m_hg_lb', 'new_m_hg_norm_w', 'new_m_w_out', 'new_m_ln1_g', 'new_m_ln1_b', 'new_m_w_up', 'new_m_conv_w', 'new_m_conv_b', 'new_m_w_down', 'new_m_ln2_g', 'new_m_ln2_b', 'new_v_c_ctx', 'new_v_w_mod', 'new_v_b_mod', 'new_v_w_in', 'new_v_s5_lam_re', 'new_v_s5_lam_im', 'new_v_s5_log_dt', 'new_v_s5_b_re', 'new_v_s5_b_im', 'new_v_s5_c_re', 'new_v_s5_c_im', 'new_v_s5_d', 'new_v_w_glu', 'new_v_b_glu', 'new_v_hg_lb', 'new_v_hg_norm_w', 'new_v_w_out', 'new_v_ln1_g', 'new_v_ln1_b', 'new_v_w_up', 'new_v_conv_w', 'new_v_conv_b', 'new_v_w_down', 'new_v_ln2_g', 'new_v_ln2_b']
TWIN_LEAF_KINDS = {'loss': 'loss', 'grad_x': 'grad_x', 'grad_c_ctx': 'grad_w', 'grad_w_mod': 'grad_w', 'grad_b_mod': 'grad_w', 'grad_w_in': 'grad_w', 'grad_s5_lam_re': 'grad_w', 'grad_s5_lam_im': 'grad_w', 'grad_s5_log_dt': 'grad_w', 'grad_s5_b_re': 'grad_w', 'grad_s5_b_im': 'grad_w', 'grad_s5_c_re': 'grad_w', 'grad_s5_c_im': 'grad_w', 'grad_s5_d': 'grad_w', 'grad_w_glu': 'grad_w', 'grad_b_glu': 'grad_w', 'grad_hg_lb': 'grad_w', 'grad_hg_norm_w': 'grad_w', 'grad_w_out': 'grad_w', 'grad_ln1_g': 'grad_w', 'grad_ln1_b': 'grad_w', 'grad_w_up': 'grad_w', 'grad_conv_w': 'grad_w', 'grad_conv_b': 'grad_w', 'grad_w_down': 'grad_w', 'grad_ln2_g': 'grad_w', 'grad_ln2_b': 'grad_w', 'delta_c_ctx': 'delta_w', 'delta_w_mod': 'delta_w', 'delta_b_mod': 'delta_w', 'delta_w_in': 'delta_w', 'delta_s5_lam_re': 'delta_w', 'delta_s5_lam_im': 'delta_w', 'delta_s5_log_dt': 'delta_w', 'delta_s5_b_re': 'delta_w', 'delta_s5_b_im': 'delta_w', 'delta_s5_c_re': 'delta_w', 'delta_s5_c_im': 'delta_w', 'delta_s5_d': 'delta_w', 'delta_w_glu': 'delta_w', 'delta_b_glu': 'delta_w', 'delta_hg_lb': 'delta_w', 'delta_hg_norm_w': 'delta_w', 'delta_w_out': 'delta_w', 'delta_ln1_g': 'delta_w', 'delta_ln1_b': 'delta_w', 'delta_w_up': 'delta_w', 'delta_conv_w': 'delta_w', 'delta_conv_b': 'delta_w', 'delta_w_down': 'delta_w', 'delta_ln2_g': 'delta_w', 'delta_ln2_b': 'delta_w', 'new_m_c_ctx': 'new_m', 'new_m_w_mod': 'new_m', 'new_m_b_mod': 'new_m', 'new_m_w_in': 'new_m', 'new_m_s5_lam_re': 'new_m', 'new_m_s5_lam_im': 'new_m', 'new_m_s5_log_dt': 'new_m', 'new_m_s5_b_re': 'new_m', 'new_m_s5_b_im': 'new_m', 'new_m_s5_c_re': 'new_m', 'new_m_s5_c_im': 'new_m', 'new_m_s5_d': 'new_m', 'new_m_w_glu': 'new_m', 'new_m_b_glu': 'new_m', 'new_m_hg_lb': 'new_m', 'new_m_hg_norm_w': 'new_m', 'new_m_w_out': 'new_m', 'new_m_ln1_g': 'new_m', 'new_m_ln1_b': 'new_m', 'new_m_w_up': 'new_m', 'new_m_conv_w': 'new_m', 'new_m_conv_b': 'new_m', 'new_m_w_down': 'new_m', 'new_m_ln2_g': 'new_m', 'new_m_ln2_b': 'new_m', 'new_v_c_ctx': 'new_v', 'new_v_w_mod': 'new_v', 'new_v_b_mod': 'new_v', 'new_v_w_in': 'new_v', 'new_v_s5_lam_re': 'new_v', 'new_v_s5_lam_im': 'new_v', 'new_v_s5_log_dt': 'new_v', 'new_v_s5_b_re': 'new_v', 'new_v_s5_b_im': 'new_v', 'new_v_s5_c_re': 'new_v', 'new_v_s5_c_im': 'new_v', 'new_v_s5_d': 'new_v', 'new_v_w_glu': 'new_v', 'new_v_b_glu': 'new_v', 'new_v_hg_lb': 'new_v', 'new_v_hg_norm_w': 'new_v', 'new_v_w_out': 'new_v', 'new_v_ln1_g': 'new_v', 'new_v_ln1_b': 'new_v', 'new_v_w_up': 'new_v', 'new_v_conv_w': 'new_v', 'new_v_conv_b': 'new_v', 'new_v_w_down': 'new_v', 'new_v_ln2_g': 'new_v', 'new_v_ln2_b': 'new_v'}


def _forward(args):
    return _fwd_reference(*[args[k] for k in FWD_PARAMS])


def _output_shape():
    def fwd():
        inp = _fwd_setup_inputs(0)
        return _fwd_reference(*[inp[k] for k in FWD_PARAMS])
    out = _jax.eval_shape(fwd)
    return out.shape, out.dtype

N_MICROBATCH = 1
ADAM_LR = 0.001
ADAM_B1 = 0.9
ADAM_B2 = 0.999
ADAM_EPS = 1e-08
ADAM_WD = 0.01
ADAM_STEP = 10
PER_EXAMPLE_BATCH_AXIS = {'x': 0, 'c': 0, 'ctx': 0, 'loss_target': 0}
SHARED_INPUTS = []
_WEIGHT_DTYPES = {'c_ctx': _jnp.float32, 'w_mod': _jnp.float32, 'b_mod': _jnp.float32, 'w_in': _jnp.float32, 's5_lam_re': _jnp.float32, 's5_lam_im': _jnp.float32, 's5_log_dt': _jnp.float32, 's5_b_re': _jnp.float32, 's5_b_im': _jnp.float32, 's5_c_re': _jnp.float32, 's5_c_im': _jnp.float32, 's5_d': _jnp.float32, 'w_glu': _jnp.float32, 'b_glu': _jnp.float32, 'hg_lb': _jnp.float32, 'hg_norm_w': _jnp.float32, 'w_out': _jnp.float32, 'ln1_g': _jnp.float32, 'ln1_b': _jnp.float32, 'w_up': _jnp.float32, 'conv_w': _jnp.float32, 'conv_b': _jnp.float32, 'w_down': _jnp.float32, 'ln2_g': _jnp.float32, 'ln2_b': _jnp.float32}
MOMENT_SCALE = {'c_ctx': 6.350262e-04, 'w_mod': 1.183908e-02, 'b_mod': 1.959727e-02, 'w_in': 5.712971e-03, 's5_lam_re': 4.573020e-04, 's5_lam_im': 4.330391e-04, 's5_log_dt': 2.178798e-01, 's5_b_re': 2.882859e-04, 's5_b_im': 2.923040e-04, 's5_c_re': 4.102826e-04, 's5_c_im': 4.133555e-04, 's5_d': 5.993309e-03, 'w_glu': 1.724628e-03, 'b_glu': 2.356079e-03, 'hg_lb': 3.789358e-04, 'hg_norm_w': 1.830341e-02, 'w_out': 1.746107e-02, 'ln1_g': 5.698758e-01, 'ln1_b': 3.078222e-01, 'w_up': 6.031618e-03, 'conv_w': 6.079525e-03, 'conv_b': 5.524067e-03, 'w_down': 2.344125e-02, 'ln2_g': 1.602697e+01, 'ln2_b': 5.162664e-01}


def _to_microbatches(a, axis):
    t = _jnp.moveaxis(a, axis, 0)
    t = t.reshape((N_MICROBATCH, t.shape[0] // N_MICROBATCH) + t.shape[1:])
    return _jnp.moveaxis(t, 1, axis + 1)


def setup_inputs(seed: int = 0) -> dict:
    inp = _fwd_setup_inputs(seed)
    key = _jax.random.fold_in(_jax.random.key(seed), 7919)
    shape, _ = _output_shape()
    out = dict(inp)
    out["loss_target"] = _jax.random.normal(_jax.random.fold_in(key, 0), shape, _jnp.float32)
    for i, name in enumerate(TWIN_WEIGHTS):
        w = inp[name].astype(_jnp.float32)
        if MOMENT_SCALE is None:
            s = _jnp.sqrt(_jnp.mean(_jnp.square(w)) + 1e-30)
        else:
            s = MOMENT_SCALE[name]
        km, kv = _jax.random.split(_jax.random.fold_in(key, i + 1))
        out[name] = w
        out["m_" + name] = s * _jax.random.normal(km, w.shape, _jnp.float32)
        out["v_" + name] = (s * s) * _jax.random.uniform(kv, w.shape, _jnp.float32, 0.5, 1.5)
    if N_MICROBATCH > 1:
        for name, axis in PER_EXAMPLE_BATCH_AXIS.items():
            out[name] = _to_microbatches(out[name], axis)
    return {'x': out['x'], 'c': out['c'], 'ctx': out['ctx'], 'c_ctx': out['c_ctx'], 'w_mod': out['w_mod'], 'b_mod': out['b_mod'], 'w_in': out['w_in'], 's5_lam_re': out['s5_lam_re'], 's5_lam_im': out['s5_lam_im'], 's5_log_dt': out['s5_log_dt'], 's5_b_re': out['s5_b_re'], 's5_b_im': out['s5_b_im'], 's5_c_re': out['s5_c_re'], 's5_c_im': out['s5_c_im'], 's5_d': out['s5_d'], 'w_glu': out['w_glu'], 'b_glu': out['b_glu'], 'hg_lb': out['hg_lb'], 'hg_norm_w': out['hg_norm_w'], 'w_out': out['w_out'], 'ln1_g': out['ln1_g'], 'ln1_b': out['ln1_b'], 'w_up': out['w_up'], 'conv_w': out['conv_w'], 'conv_b': out['conv_b'], 'w_down': out['w_down'], 'ln2_g': out['ln2_g'], 'ln2_b': out['ln2_b'], 'loss_target': out['loss_target'], 'm_c_ctx': out['m_c_ctx'], 'm_w_mod': out['m_w_mod'], 'm_b_mod': out['m_b_mod'], 'm_w_in': out['m_w_in'], 'm_s5_lam_re': out['m_s5_lam_re'], 'm_s5_lam_im': out['m_s5_lam_im'], 'm_s5_log_dt': out['m_s5_log_dt'], 'm_s5_b_re': out['m_s5_b_re'], 'm_s5_b_im': out['m_s5_b_im'], 'm_s5_c_re': out['m_s5_c_re'], 'm_s5_c_im': out['m_s5_c_im'], 'm_s5_d': out['m_s5_d'], 'm_w_glu': out['m_w_glu'], 'm_b_glu': out['m_b_glu'], 'm_hg_lb': out['m_hg_lb'], 'm_hg_norm_w': out['m_hg_norm_w'], 'm_w_out': out['m_w_out'], 'm_ln1_g': out['m_ln1_g'], 'm_ln1_b': out['m_ln1_b'], 'm_w_up': out['m_w_up'], 'm_conv_w': out['m_conv_w'], 'm_conv_b': out['m_conv_b'], 'm_w_down': out['m_w_down'], 'm_ln2_g': out['m_ln2_g'], 'm_ln2_b': out['m_ln2_b'], 'v_c_ctx': out['v_c_ctx'], 'v_w_mod': out['v_w_mod'], 'v_b_mod': out['v_b_mod'], 'v_w_in': out['v_w_in'], 'v_s5_lam_re': out['v_s5_lam_re'], 'v_s5_lam_im': out['v_s5_lam_im'], 'v_s5_log_dt': out['v_s5_log_dt'], 'v_s5_b_re': out['v_s5_b_re'], 'v_s5_b_im': out['v_s5_b_im'], 'v_s5_c_re': out['v_s5_c_re'], 'v_s5_c_im': out['v_s5_c_im'], 'v_s5_d': out['v_s5_d'], 'v_w_glu': out['v_w_glu'], 'v_b_glu': out['v_b_glu'], 'v_hg_lb': out['v_hg_lb'], 'v_hg_norm_w': out['v_hg_norm_w'], 'v_w_out': out['v_w_out'], 'v_ln1_g': out['v_ln1_g'], 'v_ln1_b': out['v_ln1_b'], 'v_w_up': out['v_w_up'], 'v_conv_w': out['v_conv_w'], 'v_conv_b': out['v_conv_b'], 'v_w_down': out['v_w_down'], 'v_ln2_g': out['v_ln2_g'], 'v_ln2_b': out['v_ln2_b']}


def _loss(weights, diff, rest, loss_target):
    with _jax.named_scope("forward"):
        args = {**rest, TWIN_DIFF_INPUT: diff, **{k: w.astype(_WEIGHT_DTYPES[k]) for k, w in weights.items()}}
        y = _forward(args)
    with _jax.named_scope("loss_head"):
        err = _jnp.square(y.astype(_jnp.float32) - loss_target)
        return 0.5 * _jnp.sum(_jnp.mean(err, axis=-1)) if err.ndim else 0.5 * err


def _adamw(w, g, m, v):
    m = ADAM_B1 * m + (1.0 - ADAM_B1) * g
    v = ADAM_B2 * v + (1.0 - ADAM_B2) * _jnp.square(g)
    m_hat = m / (1.0 - ADAM_B1 ** ADAM_STEP)
    v_hat = v / (1.0 - ADAM_B2 ** ADAM_STEP)
    delta = -ADAM_LR * (m_hat / (_jnp.sqrt(v_hat) + ADAM_EPS) + ADAM_WD * w)
    return delta, m, v


def reference(x, c, ctx, c_ctx, w_mod, b_mod, w_in, s5_lam_re, s5_lam_im, s5_log_dt, s5_b_re, s5_b_im, s5_c_re, s5_c_im, s5_d, w_glu, b_glu, hg_lb, hg_norm_w, w_out, ln1_g, ln1_b, w_up, conv_w, conv_b, w_down, ln2_g, ln2_b, loss_target, m_c_ctx, m_w_mod, m_b_mod, m_w_in, m_s5_lam_re, m_s5_lam_im, m_s5_log_dt, m_s5_b_re, m_s5_b_im, m_s5_c_re, m_s5_c_im, m_s5_d, m_w_glu, m_b_glu, m_hg_lb, m_hg_norm_w, m_w_out, m_ln1_g, m_ln1_b, m_w_up, m_conv_w, m_conv_b, m_w_down, m_ln2_g, m_ln2_b, v_c_ctx, v_w_mod, v_b_mod, v_w_in, v_s5_lam_re, v_s5_lam_im, v_s5_log_dt, v_s5_b_re, v_s5_b_im, v_s5_c_re, v_s5_c_im, v_s5_d, v_w_glu, v_b_glu, v_hg_lb, v_hg_norm_w, v_w_out, v_ln1_g, v_ln1_b, v_w_up, v_conv_w, v_conv_b, v_w_down, v_ln2_g, v_ln2_b):
    given = dict(x=x, c=c, ctx=ctx, c_ctx=c_ctx, w_mod=w_mod, b_mod=b_mod, w_in=w_in, s5_lam_re=s5_lam_re, s5_lam_im=s5_lam_im, s5_log_dt=s5_log_dt, s5_b_re=s5_b_re, s5_b_im=s5_b_im, s5_c_re=s5_c_re, s5_c_im=s5_c_im, s5_d=s5_d, w_glu=w_glu, b_glu=b_glu, hg_lb=hg_lb, hg_norm_w=hg_norm_w, w_out=w_out, ln1_g=ln1_g, ln1_b=ln1_b, w_up=w_up, conv_w=conv_w, conv_b=conv_b, w_down=w_down, ln2_g=ln2_g, ln2_b=ln2_b, loss_target=loss_target, m_c_ctx=m_c_ctx, m_w_mod=m_w_mod, m_b_mod=m_b_mod, m_w_in=m_w_in, m_s5_lam_re=m_s5_lam_re, m_s5_lam_im=m_s5_lam_im, m_s5_log_dt=m_s5_log_dt, m_s5_b_re=m_s5_b_re, m_s5_b_im=m_s5_b_im, m_s5_c_re=m_s5_c_re, m_s5_c_im=m_s5_c_im, m_s5_d=m_s5_d, m_w_glu=m_w_glu, m_b_glu=m_b_glu, m_hg_lb=m_hg_lb, m_hg_norm_w=m_hg_norm_w, m_w_out=m_w_out, m_ln1_g=m_ln1_g, m_ln1_b=m_ln1_b, m_w_up=m_w_up, m_conv_w=m_conv_w, m_conv_b=m_conv_b, m_w_down=m_w_down, m_ln2_g=m_ln2_g, m_ln2_b=m_ln2_b, v_c_ctx=v_c_ctx, v_w_mod=v_w_mod, v_b_mod=v_b_mod, v_w_in=v_w_in, v_s5_lam_re=v_s5_lam_re, v_s5_lam_im=v_s5_lam_im, v_s5_log_dt=v_s5_log_dt, v_s5_b_re=v_s5_b_re, v_s5_b_im=v_s5_b_im, v_s5_c_re=v_s5_c_re, v_s5_c_im=v_s5_c_im, v_s5_d=v_s5_d, v_w_glu=v_w_glu, v_b_glu=v_b_glu, v_hg_lb=v_hg_lb, v_hg_norm_w=v_hg_norm_w, v_w_out=v_w_out, v_ln1_g=v_ln1_g, v_ln1_b=v_ln1_b, v_w_up=v_w_up, v_conv_w=v_conv_w, v_conv_b=v_conv_b, v_w_down=v_w_down, v_ln2_g=v_ln2_g, v_ln2_b=v_ln2_b)
    weights = {n: given[n] for n in TWIN_WEIGHTS}
    shared = {n: given[n] for n in SHARED_INPUTS}
    per_example = {n: given[n] for n in ['x', 'c', 'ctx']}
    grad_fn = _jax.value_and_grad(_loss, argnums=(0, 1))

    def one_microbatch(ex, loss_target):
        ex = dict(ex)
        diff = ex.pop(TWIN_DIFF_INPUT)
        return grad_fn(weights, diff, {**shared, **ex}, loss_target)

    if N_MICROBATCH == 1:
        loss, (grad_w, grad_x) = one_microbatch(per_example, given["loss_target"])
    else:
        def body(carry, xs):
            loss_sum, grad_sum = carry
            l_k, (gw_k, gx_k) = one_microbatch(xs[0], xs[1])
            with _jax.named_scope("update"):
                return (loss_sum + l_k, _jax.tree.map(_jnp.add, grad_sum, gw_k)), gx_k

        init = (_jnp.zeros((), _jnp.float32), _jax.tree.map(_jnp.zeros_like, weights))
        (loss, grad_w), grad_x = _jax.lax.scan(body, init, (per_example, given["loss_target"]))
    with _jax.named_scope("update"):
        delta_w, new_m, new_v = {}, {}, {}
        for n in TWIN_WEIGHTS:
            delta_w[n], new_m[n], new_v[n] = _adamw(weights[n], grad_w[n], given["m_" + n], given["v_" + n])
    return (loss, grad_x, *[grad_w[n] for n in TWIN_WEIGHTS], *[delta_w[n] for n in TWIN_WEIGHTS],
            *[new_m[n] for n in TWIN_WEIGHTS], *[new_v[n] for n in TWIN_WEIGHTS])
```

```python
import functools
import math

import jax
import jax.numpy as jnp
from jax import lax
from jax.experimental import pallas as pl
from jax.experimental.pallas import tpu as pltpu

F32, BF16 = jnp.float32, jnp.bfloat16
N_DEV = 8
AXES = ("x", "y", "c")
D = 1024
S5W = 512
S5P = 2048
S5H = 16
HGW = 512
HD = 128
NH = 4
CK = 32
DFF = 2816
GRID_W = 64
INC = 3072
ALPHA = 8.0 ** 0.25
LN_EPS = 1e-5
RMS_EPS = 1e-6
LR, B1, B2, EPS, WD, STEP = 0.001, 0.9, 0.999, 1e-08, 0.01, 10
TT = 256
VMEM_MB = 56

NN = ((1,), (0,))
NT = ((1,), (1,))
TN = ((0,), (0,))


def _cparams(n_axes):
    return pltpu.CompilerParams(dimension_semantics=("arbitrary",) * n_axes, vmem_limit_bytes=VMEM_MB << 20)


def _dot(a, b, dims):
    return lax.dot_general(a.astype(BF16), b.astype(BF16), (dims, ((), ())), preferred_element_type=F32)


def _dot_hi(a, b):
    return lax.dot_general(a, b, (NN, ((), ())), precision=lax.Precision.HIGHEST, preferred_element_type=F32)


@jax.custom_vjp
def mm_nn(a, b):
    return _dot(a, b, NN)


@jax.custom_vjp
def mm_nt(a, b):
    return _dot(a, b, NT)


@jax.custom_vjp
def mm_tn(a, b):
    return _dot(a, b, TN)


mm_nn.defvjp(lambda a, b: (_dot(a, b, NN), (a, b)), lambda r, g: (_dot(g, r[1], NT), _dot(r[0], g, TN)))
mm_nt.defvjp(lambda a, b: (_dot(a, b, NT), (a, b)), lambda r, g: (_dot(g, r[1], NN), _dot(g, r[0], TN)))
mm_tn.defvjp(lambda a, b: (_dot(a, b, TN), (a, b)), lambda r, g: (_dot(r[1], g, NT), _dot(r[0], g, NN)))


def _roll_rows(u, s):
    return pltpu.roll(u, s % u.shape[0], 0)


@functools.partial(jax.custom_vjp, nondiff_argnums=(2,))
def shift_rows(u, m, s):
    return _roll_rows(u, s) * m


def _shift_fwd(u, m, s):
    return _roll_rows(u, s) * m, m


def _shift_bwd(s, m, g):
    return _roll_rows(g * m, -s), jnp.zeros_like(m)


shift_rows.defvjp(_shift_fwd, _shift_bwd)


def _tri(n, upper):
    r = lax.broadcasted_iota(jnp.int32, (n, n), 0)
    c = lax.broadcasted_iota(jnp.int32, (n, n), 1)
    return ((c >= r) if upper else (c <= r)).astype(F32)


@functools.partial(jax.custom_vjp, nondiff_argnums=(1,))
def cumsum_rows(a, rev):
    return _dot_hi(_tri(a.shape[0], rev), a)


cumsum_rows.defvjp(lambda a, rev: (_dot_hi(_tri(a.shape[0], rev), a), None),
                   lambda rev, _, g: (_dot_hi(_tri(g.shape[0], not rev), g),))


def _scan_tile(pos, nt, rev):
    return jnp.where(pos == 0, 0, nt - pos) if rev else pos


def _exchange(name, arrays, all_to_all):
    k_arr = len(arrays)

    def body(*refs):
        ins, outs = refs[:k_arr], refs[k_arr:2 * k_arr]
        send_sems, recv_sems, local_sems = refs[2 * k_arr:]
        me = lax.axis_index("x") * 4 + lax.axis_index("y") * 2 + lax.axis_index("c")
        local = []
        for k in range(k_arr):
            cp = pltpu.make_async_copy(ins[k].at[me] if all_to_all else ins[k], outs[k].at[me], local_sems.at[k])
            cp.start()
            local.append(cp)
        sends = []
        for d in range(1, N_DEV):
            p = (me + d) % N_DEV
            for k in range(k_arr):
                cp = pltpu.make_async_remote_copy(
                    src_ref=ins[k].at[p] if all_to_all else ins[k], dst_ref=outs[k].at[me],
                    send_sem=send_sems.at[k, d - 1], recv_sem=recv_sems.at[k, d - 1],
                    device_id=(p // 4, (p // 2) % 2, p % 2), device_id_type=pl.DeviceIdType.MESH)
                cp.start()
                sends.append(cp)
        for d in range(1, N_DEV):
            q = (me + N_DEV - d) % N_DEV
            for k in range(k_arr):
                pltpu.make_async_remote_copy(
                    src_ref=ins[k].at[q] if all_to_all else ins[k], dst_ref=outs[k].at[q],
                    send_sem=send_sems.at[k, d - 1], recv_sem=recv_sems.at[k, d - 1],
                    device_id=(q // 4, (q // 2) % 2, q % 2), device_id_type=pl.DeviceIdType.MESH).wait_recv()
        for cp in sends:
            cp.wait_send()
        for cp in local:
            cp.wait()

    shapes = [a.shape if all_to_all else (N_DEV,) + a.shape for a in arrays]
    return pl.pallas_call(
        body, name=name,
        out_shape=[jax.ShapeDtypeStruct(s, a.dtype) for s, a in zip(shapes, arrays)],
        in_specs=[pl.BlockSpec(memory_space=pl.ANY)] * k_arr,
        out_specs=[pl.BlockSpec(memory_space=pl.ANY)] * k_arr,
        scratch_shapes=[pltpu.SemaphoreType.DMA((k_arr, N_DEV - 1)), pltpu.SemaphoreType.DMA((k_arr, N_DEV - 1)),
                        pltpu.SemaphoreType.DMA((k_arr,))],
    )(*arrays)


def _tile(n, prefs):
    for t in prefs:
        if n % t == 0:
            return t
    raise ValueError(n)


def dense_nn(name, a, w, out_dtype=F32):
    n, k = a.shape
    m = w.shape[1]
    tm = _tile(m, (512, 256, 128))

    def body(a_ref, w_ref, o_ref):
        o_ref[...] = _dot(a_ref[...], w_ref[...], NN).astype(o_ref.dtype)

    return pl.pallas_call(
        body, name=name, grid=(m // tm, n // TT), out_shape=jax.ShapeDtypeStruct((n, m), out_dtype),
        in_specs=[pl.BlockSpec((TT, k), lambda j, i: (i, 0)), pl.BlockSpec((k, tm), lambda j, i: (0, j))],
        out_specs=pl.BlockSpec((TT, tm), lambda j, i: (i, j)), compiler_params=_cparams(2))(a, w)


def dense_nt(name, g, w, out_dtype=F32):
    n, m = g.shape
    k = w.shape[0]
    tk = _tile(k, (512, 256, 128))

    def body(g_ref, w_ref, o_ref):
        o_ref[...] = _dot(g_ref[...], w_ref[...], NT).astype(o_ref.dtype)

    return pl.pallas_call(
        body, name=name, grid=(k // tk, n // TT), out_shape=jax.ShapeDtypeStruct((n, k), out_dtype),
        in_specs=[pl.BlockSpec((TT, m), lambda j, i: (i, 0)), pl.BlockSpec((tk, m), lambda j, i: (j, 0))],
        out_specs=pl.BlockSpec((TT, tk), lambda j, i: (i, j)), compiler_params=_cparams(2))(g, w)


def dense_tn(name, a, g, out_dtype=BF16):
    n, k = a.shape
    m = g.shape[1]
    tk = _tile(k, (512, 1408, 256, 128))
    tm = _tile(m, (512, 256, 128))
    nt = n // TT

    def body(a_ref, g_ref, o_ref, acc_ref):
        t = pl.program_id(2)

        @pl.when(t == 0)
        def _():
            acc_ref[...] = jnp.zeros_like(acc_ref)

        acc_ref[...] += _dot(a_ref[...], g_ref[...], TN)

        @pl.when(t == nt - 1)
        def _():
            o_ref[...] = acc_ref[...].astype(o_ref.dtype)

    return pl.pallas_call(
        body, name=name, grid=(k // tk, m // tm, nt), out_shape=jax.ShapeDtypeStruct((k, m), out_dtype),
        in_specs=[pl.BlockSpec((TT, tk), lambda i, j, t: (t, i)), pl.BlockSpec((TT, tm), lambda i, j, t: (t, j))],
        out_specs=pl.BlockSpec((tk, tm), lambda i, j, t: (i, j)),
        scratch_shapes=[pltpu.VMEM((tk, tm), F32)], compiler_params=_cparams(3))(a, g)


def _c0(j):
    return 0


def _tspec(w, cb):
    return pl.BlockSpec((TT, w), lambda j, i: (i, cb(j)))


def _pspec(arr, w, cb):
    two = arr.shape[0] == 2
    return pl.BlockSpec((None, arr.shape[1], w), lambda j, i: (jnp.minimum(i, 1) if two else 0, 0, cb(j)))


def block_fwd(name, fn, n, tiled, params, outs, n_col=1):
    nt_, np_ = len(tiled), len(params)

    def body(*refs):
        i = pl.program_id(1)
        tv = [r[...].astype(F32) for r in refs[:nt_]]
        pv = [r[...].astype(F32) for r in refs[nt_:nt_ + np_]]
        for o_ref, r in zip(refs[nt_ + np_:], fn(tv, pv, i)):
            o_ref[...] = r.astype(o_ref.dtype)

    return pl.pallas_call(
        body, name=name, grid=(n_col, n // TT),
        out_shape=[jax.ShapeDtypeStruct((n, c), dt) for c, dt, _, _ in outs],
        in_specs=[_tspec(w, cb) for _, w, cb in tiled] + [_pspec(a, w, cb) for a, w, cb in params],
        out_specs=[_tspec(w, cb) for _, _, w, cb in outs], compiler_params=_cparams(2),
    )(*[a for a, _, _ in tiled], *[a for a, _, _ in params])


def block_bwd(name, fn, n, tiled, params, cots, grads, n_col=1):
    nt_, np_, nc_ = len(tiled), len(params), len(cots)
    want = [k for k, g in enumerate(grads) if g is not None]

    def body(*refs):
        i = pl.program_id(1)
        tv = [r[...].astype(F32) for r in refs[:nt_]]
        pv = [r[...].astype(F32) for r in refs[nt_:nt_ + np_]]
        cv = [r[...].astype(F32) for r in refs[nt_ + np_:nt_ + np_ + nc_]]
        o_refs = refs[nt_ + np_ + nc_:]
        _, vjp = jax.vjp(lambda t, p: list(fn(t, p, i)), tv, pv)
        dt, dp = vjp(cv)
        for o_ref, k in zip(o_refs, want):
            o_ref[...] = dt[k].astype(o_ref.dtype)
        for o_ref, g, (arr, _, _) in zip(o_refs[len(want):], dp, params):
            first = (i == 0) | (i == 1) if arr.shape[0] == 2 else i == 0

            @pl.when(first)
            def _(o_ref=o_ref):
                o_ref[...] = jnp.zeros_like(o_ref)

            o_ref[...] += g

    res = pl.pallas_call(
        body, name=name, grid=(n_col, n // TT),
        out_shape=[jax.ShapeDtypeStruct((n, grads[k][0]), grads[k][1]) for k in want]
        + [jax.ShapeDtypeStruct(a.shape, F32) for a, _, _ in params],
        in_specs=[_tspec(w, cb) for _, w, cb in tiled] + [_pspec(a, w, cb) for a, w, cb in params]
        + [_tspec(w, cb) for _, w, cb in cots],
        out_specs=[_tspec(grads[k][2], grads[k][3]) for k in want] + [_pspec(a, w, cb) for a, w, cb in params],
        compiler_params=_cparams(2),
    )(*[a for a, _, _ in tiled], *[a for a, _, _ in params], *[a for a, _, _ in cots])
    return res[:len(want)], res[len(want):]


def f_mod(tv, pv, i):
    (x,), (sh, sc) = tv, pv
    return (x * (1.0 + sc) + sh,)


def f_mod_id(tv, pv, i):
    return (f_mod(tv, pv, i)[0], tv[0])


def f_ln(tv, pv, i):
    (x, z), (gate, g, b, sh, sc) = tv, pv
    pre = ALPHA * x + gate * z
    mu = jnp.mean(pre, axis=-1, keepdims=True)
    var = jnp.mean(jnp.square(pre - mu), axis=-1, keepdims=True)
    xn = (pre - mu) * lax.rsqrt(var + LN_EPS) * g + b
    return xn, xn * (1.0 + sc) + sh


def f_mix(tv, pv, i):
    (pu, pg, y0, y1, o0, o1), (d_skip, w_glu, b_glu, norm_w) = tv, pv
    s5y = jax.nn.gelu(y0 + y1 + pu * d_skip)
    s5o = s5y * jax.nn.sigmoid(mm_nn(s5y, w_glu) + b_glu)
    o = o0 + o1
    heads = []
    for h in range(NH):
        oh = o[:, h * HD:(h + 1) * HD]
        heads.append(oh * lax.rsqrt(jnp.mean(jnp.square(oh), axis=-1, keepdims=True) + RMS_EPS) * norm_w)
    hg = jnp.concatenate(heads, axis=-1) * jax.nn.silu(pg)
    return (jnp.concatenate([s5o, hg], axis=-1),)


def f_act(tv, pv, i):
    (ua, ug), (cwa, cwg, cba, cbg) = tv, pv
    t = lax.broadcasted_iota(jnp.int32, (TT, 1), 0)
    lat = i > 0
    m_dn = jnp.where((t == 0) | (lat & (t % GRID_W == 0)), 0.0, 1.0)
    m_up = jnp.where((t == TT - 1) | (lat & (t % GRID_W == GRID_W - 1)), 0.0, 1.0)

    def conv(u, w, b):
        return shift_rows(u, m_dn, 1) * w[0:1] + u * w[1:2] + shift_rows(u, m_up, -1) * w[2:3] + b

    return (jax.nn.silu(conv(ua, cwa, cba)) * conv(ug, cwg, cbg),)


def loss_and_grad(xf, target):
    n = xf.shape[0]

    def body(x_ref, t_ref, dy_ref, l_ref):
        i = pl.program_id(0)

        @pl.when(i == 0)
        def _():
            l_ref[...] = jnp.zeros_like(l_ref)
            dy_ref[...] = jnp.zeros_like(dy_ref)

        @pl.when(i > 0)
        def _():
            e = x_ref[...] - t_ref[...]
            dy_ref[...] = e * (1.0 / D)
            l_ref[...] += 0.5 / D * jnp.sum(jnp.square(e))

    return pl.pallas_call(
        body, name="loss", grid=(n // TT,),
        out_shape=[jax.ShapeDtypeStruct((n, D), F32), jax.ShapeDtypeStruct((8, 128), F32)],
        in_specs=[pl.BlockSpec((TT, D), lambda i: (i, 0)), pl.BlockSpec((TT, D), lambda i: (jnp.maximum(i - 1, 0), 0))],
        out_specs=[pl.BlockSpec((TT, D), lambda i: (i, 0)), pl.BlockSpec((8, 128), lambda i: (0, 0))],
        compiler_params=_cparams(1))(xf, target)


def f_prep(lr, li, ldt, bre, bim, cre, cim):
    gi = lax.broadcasted_iota(jnp.int32, (S5W // S5H, S5P), 0)
    gc = lax.broadcasted_iota(jnp.int32, (S5W // S5H, S5P), 1) // 64
    dt = jnp.exp(jnp.sum(jnp.where(gi == gc, ldt, 0.0), axis=0, keepdims=True))
    mag, ang = jnp.exp(lr * dt), li * dt
    ar, ai = mag * jnp.cos(ang), mag * jnp.sin(ang)
    den = lr * lr + li * li
    nr, ni = ar - 1.0, ai
    cr = (nr * lr + ni * li) / den
    ci = (ni * lr - nr * li) / den
    bbr = cr * bre - ci * bim
    bbi = cr * bim + ci * bre
    rg = lax.broadcasted_iota(jnp.int32, (S5W, S5P), 0) // S5H
    cg = lax.broadcasted_iota(jnp.int32, (S5W, S5P), 1) // 64
    mask = (rg == cg).astype(F32)
    blk = lambda a: jnp.concatenate([a] * (S5W // S5H), axis=0) * mask
    return ar, ai, blk(bbr), blk(bbi), blk(cre), blk(-cim)


def s5_prep(lr, li, ldt, bre, bim, cre, cim):
    n2 = lr.shape[0]

    def body(lr_r, li_r, ldt_r, bre_r, bim_r, cre_r, cim_r, a_ref, b_ref, c_ref):
        ar, ai, bbr, bbi, cbr, cbi = f_prep(lr_r[...], li_r[...], ldt_r[...], bre_r[...], bim_r[...], cre_r[...], cim_r[...])
        a_ref[0], a_ref[1] = ar, ai
        b_ref[0], b_ref[1] = bbr.astype(BF16), bbi.astype(BF16)
        c_ref[0], c_ref[1] = cbr.astype(BF16), cbi.astype(BF16)

    sp = lambda r, c: pl.BlockSpec((None, r, c), lambda i: (i, 0, 0))
    sp4 = lambda r, c: pl.BlockSpec((None, 2, r, c), lambda i: (i, 0, 0, 0))
    return pl.pallas_call(
        body, name="s5_prep", grid=(n2,),
        out_shape=[jax.ShapeDtypeStruct((n2, 2, 1, S5P), F32), jax.ShapeDtypeStruct((n2, 2, S5W, S5P), BF16),
                   jax.ShapeDtypeStruct((n2, 2, S5W, S5P), BF16)],
        in_specs=[sp(1, S5P), sp(1, S5P), sp(32, 1), sp(S5H, S5P), sp(S5H, S5P), sp(S5H, S5P), sp(S5H, S5P)],
        out_specs=[sp4(1, S5P), sp4(S5W, S5P), sp4(S5W, S5P)], compiler_params=_cparams(1),
    )(lr, li, ldt, bre, bim, cre, cim)


def s5_prep_bwd(lr, li, ldt, bre, bim, cre, cim, da, db, dc):
    n2 = lr.shape[0]

    def body(lr_r, li_r, ldt_r, bre_r, bim_r, cre_r, cim_r, da_r, db_r, dc_r, *outs):
        args = [r[...] for r in (lr_r, li_r, ldt_r, bre_r, bim_r, cre_r, cim_r)]
        _, vjp = jax.vjp(f_prep, *args)
        for o_ref, g in zip(outs, vjp((da_r[0], da_r[1], db_r[0], db_r[1], dc_r[0], dc_r[1]))):
            o_ref[...] = g

    sp = lambda r, c: pl.BlockSpec((None, r, c), lambda i: (i, 0, 0))
    sp4 = lambda r, c: pl.BlockSpec((None, 2, r, c), lambda i: (i, 0, 0, 0))
    ins = [sp(1, S5P), sp(1, S5P), sp(32, 1), sp(S5H, S5P), sp(S5H, S5P), sp(S5H, S5P), sp(S5H, S5P)]
    return pl.pallas_call(
        body, name="s5_prep_bwd", grid=(n2,),
        out_shape=[jax.ShapeDtypeStruct(a.shape, F32) for a in (lr, li, ldt, bre, bim, cre, cim)],
        in_specs=ins + [sp4(1, S5P), sp4(S5W, S5P), sp4(S5W, S5P)], out_specs=ins, compiler_params=_cparams(1),
    )(lr, li, ldt, bre, bim, cre, cim, da, db, dc)


def s5_fwd(name, proj, a, bb, cb, ld, rev):
    n = proj.shape[0]
    nt = n // TT

    def body(u_ref, a_ref, b_ref, c_ref, xr_ref, xi_ref, y_ref, cr_ref, ci_ref):
        @pl.when(pl.program_id(0) == 0)
        def _():
            cr_ref[...] = jnp.zeros_like(cr_ref)
            ci_ref[...] = jnp.zeros_like(ci_ref)

        u = u_ref[...]
        xr_ref[...] = _dot(u, b_ref[0], NN)
        xi_ref[...] = _dot(u, b_ref[1], NN)
        ar, ai = a_ref[0], a_ref[1]

        def step(k, carry):
            cr, ci = carry
            t = TT - 1 - k if rev else k
            nr = ar * cr - ai * ci + xr_ref[pl.ds(t, 1), :]
            ni = ar * ci + ai * cr + xi_ref[pl.ds(t, 1), :]
            xr_ref[pl.ds(t, 1), :] = nr
            xi_ref[pl.ds(t, 1), :] = ni
            return nr, ni

        cr, ci = lax.fori_loop(0, TT, step, (cr_ref[...], ci_ref[...]), unroll=8)
        cr_ref[...] = cr
        ci_ref[...] = ci
        y_ref[...] = _dot(xr_ref[...], c_ref[0], NT) + _dot(xi_ref[...], c_ref[1], NT)

    tile = lambda w: pl.BlockSpec((TT, w), lambda s: (_scan_tile(s, nt, rev), 0))
    par = lambda r: pl.BlockSpec((None, 2, r, S5P), lambda s: (ld, 0, 0, 0))
    return pl.pallas_call(
        body, name=name, grid=(nt,),
        out_shape=[jax.ShapeDtypeStruct((n, S5P), F32), jax.ShapeDtypeStruct((n, S5P), F32),
                   jax.ShapeDtypeStruct((n, S5W), F32)],
        in_specs=[tile(S5W), par(1), par(S5W), par(S5W)], out_specs=[tile(S5P), tile(S5P), tile(S5W)],
        scratch_shapes=[pltpu.VMEM((1, S5P), F32), pltpu.VMEM((1, S5P), F32)], compiler_params=_cparams(1),
    )(proj, a, bb, cb)


def s5_bwd(name, proj, dy, xr, xi, a, bb, cb, ld, rev):
    n = proj.shape[0]
    nt = n // TT
    tb = TT // 8

    def tile_of(s):
        return _scan_tile(nt - 1 - s, nt, rev)

    def edge_of(s):
        pos = nt - 1 - s
        prev = _scan_tile(jnp.maximum(pos - 1, 0), nt, rev)
        return prev * tb if rev else jnp.maximum(pos * tb - 1, 0)

    def body(u_ref, dy_ref, xr_ref, xi_ref, er_ref, ei_ref, a_ref, b_ref, c_ref,
             du_ref, da_ref, db_ref, dc_ref, gr_ref, gi_ref, cr_ref, ci_ref):
        s = pl.program_id(0)

        @pl.when(s == 0)
        def _():
            cr_ref[...] = jnp.zeros_like(cr_ref)
            ci_ref[...] = jnp.zeros_like(ci_ref)
            da_ref[...] = jnp.zeros_like(da_ref)
            db_ref[...] = jnp.zeros_like(db_ref)
            dc_ref[...] = jnp.zeros_like(dc_ref)

        dyv, u = dy_ref[...], u_ref[...]
        xrv, xiv = xr_ref[...], xi_ref[...]
        gr_ref[...] = _dot(dyv, c_ref[0], NN)
        gi_ref[...] = _dot(dyv, c_ref[1], NN)
        dc_ref[0] += _dot(dyv, xrv, TN)
        dc_ref[1] += _dot(dyv, xiv, TN)
        ar, ai = a_ref[0], a_ref[1]

        def step(k, carry):
            cr, ci = carry
            t = k if rev else TT - 1 - k
            g_r = gr_ref[pl.ds(t, 1), :] + cr
            g_i = gi_ref[pl.ds(t, 1), :] + ci
            gr_ref[pl.ds(t, 1), :] = g_r
            gi_ref[pl.ds(t, 1), :] = g_i
            return ar * g_r + ai * g_i, ar * g_i - ai * g_r

        cr, ci = lax.fori_loop(0, TT, step, (cr_ref[...], ci_ref[...]), unroll=8)
        cr_ref[...] = cr
        ci_ref[...] = ci
        g_r, g_i = gr_ref[...], gi_ref[...]
        rows = lax.broadcasted_iota(jnp.int32, (TT, 1), 0)
        live = jnp.where(s == nt - 1, 0.0, 1.0)
        if rev:
            pr = jnp.where(rows == TT - 1, er_ref[0:1, :] * live, _roll_rows(xrv, -1))
            pi = jnp.where(rows == TT - 1, ei_ref[0:1, :] * live, _roll_rows(xiv, -1))
        else:
            pr = jnp.where(rows == 0, er_ref[7:8, :] * live, _roll_rows(xrv, 1))
            pi = jnp.where(rows == 0, ei_ref[7:8, :] * live, _roll_rows(xiv, 1))
        da_ref[0] += jnp.sum(g_r * pr + g_i * pi, axis=0, keepdims=True)
        da_ref[1] += jnp.sum(g_i * pr - g_r * pi, axis=0, keepdims=True)
        du_ref[...] = _dot(g_r, b_ref[0], NT) + _dot(g_i, b_ref[1], NT)
        db_ref[0] += _dot(u, g_r, TN)
        db_ref[1] += _dot(u, g_i, TN)

    tile = lambda w: pl.BlockSpec((TT, w), lambda s: (tile_of(s), 0))
    edge = pl.BlockSpec((8, S5P), lambda s: (edge_of(s), 0))
    par = lambda r: pl.BlockSpec((None, 2, r, S5P), lambda s: (ld, 0, 0, 0))
    acc = lambda r: pl.BlockSpec((2, r, S5P), lambda s: (0, 0, 0))
    return pl.pallas_call(
        body, name=name, grid=(nt,),
        out_shape=[jax.ShapeDtypeStruct((n, S5W), F32), jax.ShapeDtypeStruct((2, 1, S5P), F32),
                   jax.ShapeDtypeStruct((2, S5W, S5P), F32), jax.ShapeDtypeStruct((2, S5W, S5P), F32)],
        in_specs=[tile(S5W), tile(S5W), tile(S5P), tile(S5P), edge, edge, par(1), par(S5W), par(S5W)],
        out_specs=[tile(S5W), acc(1), acc(S5W), acc(S5W)],
        scratch_shapes=[pltpu.VMEM((TT, S5P), F32), pltpu.VMEM((TT, S5P), F32),
                        pltpu.VMEM((1, S5P), F32), pltpu.VMEM((1, S5P), F32)], compiler_params=_cparams(1),
    )(proj, dy, xr, xi, xr, xi, a, bb, cb)


def gla_chunk(r, v, qr, lb, st, rev):
    f = lb + (1.0 - lb) * jax.nn.sigmoid(r)
    k, lf, q = 1.0 - f, jnp.log(f), jax.nn.silu(qr)
    b = cumsum_rows(lf, rev)
    e = jnp.sum(lf, axis=0, keepdims=True)
    kd = k * jnp.exp(e - b)
    row = lax.broadcasted_iota(jnp.int32, (CK, CK), 0)
    col = lax.broadcasted_iota(jnp.int32, (CK, CK), 1)
    att = jnp.where((row <= col) if rev else (row >= col), mm_nt(q * jnp.exp(b - e), kd), 0.0)
    o = mm_nn(q * jnp.exp(b), st) + mm_nn(att, v)
    dec = jnp.transpose(jnp.broadcast_to(jnp.exp(e), (HD, HD)))
    return o, dec * st + mm_tn(kd, v)


def _gla_specs(n, rev, order):
    nt = n // TT
    fcol = 2 if rev else 1
    tile = lambda cbk: pl.BlockSpec((TT, HGW), lambda s: (order(s), cbk))
    return nt, [tile(fcol), tile(3), tile(4)], tile(0)


def gla_fwd(name, proj, lb, rev):
    n = proj.shape[0]
    cpt = TT // CK
    nt, in_tiles, out_tile = _gla_specs(n, rev, lambda s: _scan_tile(s, n // TT, rev))

    def body(r_ref, v_ref, q_ref, lb_ref, o_ref, st_ref, s_ref):
        @pl.when(pl.program_id(0) == 0)
        def _():
            s_ref[...] = jnp.zeros_like(s_ref)

        def chunk(kk, _):
            ci = cpt - 1 - kk if rev else kk
            rows = pl.ds(pl.multiple_of(ci * CK, CK), CK)
            for h in range(NH):
                ln = slice(h * HD, (h + 1) * HD)
                st = s_ref[h]
                st_ref[ci, h] = st
                o, sn = gla_chunk(r_ref[rows, ln], v_ref[rows, ln], q_ref[rows, ln], lb_ref[:, ln], st, rev)
                o_ref[rows, ln] = o
                s_ref[h] = sn
            return 0

        lax.fori_loop(0, cpt, chunk, 0)

    st_spec = pl.BlockSpec((cpt, NH, HD, HD), lambda s: (_scan_tile(s, nt, rev), 0, 0, 0))
    return pl.pallas_call(
        body, name=name, grid=(nt,),
        out_shape=[jax.ShapeDtypeStruct((n, HGW), F32), jax.ShapeDtypeStruct((n // CK, NH, HD, HD), F32)],
        in_specs=in_tiles + [pl.BlockSpec((1, HGW), lambda s: (0, 0))], out_specs=[out_tile, st_spec],
        scratch_shapes=[pltpu.VMEM((NH, HD, HD), F32)], compiler_params=_cparams(1),
    )(proj, proj, proj, lb)


def gla_bwd(name, proj, lb, st_all, do, rev):
    n = proj.shape[0]
    cpt = TT // CK
    order = lambda s: _scan_tile(n // TT - 1 - s, n // TT, rev)
    nt, in_tiles, out_tile = _gla_specs(n, rev, order)

    def body(r_ref, v_ref, q_ref, lb_ref, st_ref, do_ref, dr_ref, dv_ref, dq_ref, dlb_ref, ds_ref):
        @pl.when(pl.program_id(0) == 0)
        def _():
            ds_ref[...] = jnp.zeros_like(ds_ref)
            dlb_ref[...] = jnp.zeros_like(dlb_ref)

        def chunk(kk, _):
            ci = kk if rev else cpt - 1 - kk
            rows = pl.ds(pl.multiple_of(ci * CK, CK), CK)
            for h in range(NH):
                ln = slice(h * HD, (h + 1) * HD)
                _, vjp = jax.vjp(functools.partial(gla_chunk, rev=rev), r_ref[rows, ln], v_ref[rows, ln],
                                 q_ref[rows, ln], lb_ref[:, ln], st_ref[ci, h])
                dr, dv, dq, dlb, dst = vjp((do_ref[rows, ln], ds_ref[h]))
                dr_ref[rows, ln] = dr
                dv_ref[rows, ln] = dv
                dq_ref[rows, ln] = dq
                dlb_ref[:, ln] += dlb
                ds_ref[h] = dst
            return 0

        lax.fori_loop(0, cpt, chunk, 0)

    st_spec = pl.BlockSpec((cpt, NH, HD, HD), lambda s: (order(s), 0, 0, 0))
    row = pl.BlockSpec((1, HGW), lambda s: (0, 0))
    return pl.pallas_call(
        body, name=name, grid=(nt,),
        out_shape=[jax.ShapeDtypeStruct((n, HGW), F32)] * 3 + [jax.ShapeDtypeStruct((1, HGW), F32)],
        in_specs=in_tiles + [row, st_spec, out_tile], out_specs=[out_tile] * 3 + [row],
        scratch_shapes=[pltpu.VMEM((NH, HD, HD), F32)], compiler_params=_cparams(1),
    )(proj, proj, proj, lb, st_all, do)


def f_lb(rows):
    mx = functools.reduce(jnp.maximum, rows)
    ex = [jnp.exp(r - mx) for r in rows]
    tot = functools.reduce(jnp.add, ex)
    out, acc = [jnp.zeros_like(rows[0])], None
    for e in ex[1:]:
        acc = e / tot if acc is None else acc + e / tot
        out.append(acc)
    return out


def lb_call(hg, dlb=None):
    nl = hg.shape[0]

    def body(*refs):
        rows = [refs[0][l:l + 1, :] for l in range(nl)]
        if dlb is None:
            res = f_lb(rows)
        else:
            _, vjp = jax.vjp(f_lb, rows)
            (res,) = vjp([refs[1][l:l + 1, :] for l in range(nl)])
        for l in range(nl):
            refs[-1][l:l + 1, :] = res[l]

    args = (hg,) if dlb is None else (hg, dlb)
    return pl.pallas_call(body, name="lower_bounds" if dlb is None else "lower_bounds_bwd",
                          out_shape=jax.ShapeDtypeStruct(hg.shape, F32))(*args)


def mod_fwd(craw, w_mod, b_cols):
    nl, _, cols = w_mod.shape

    def body(c_ref, w_ref, b_ref, o_ref):
        o_ref[...] = _dot(jax.nn.silu(c_ref[...]), w_ref[...], NN) + b_ref[...]

    return pl.pallas_call(
        body, name="mod_fwd", grid=(nl,), out_shape=jax.ShapeDtypeStruct((nl, 16, cols), F32),
        in_specs=[pl.BlockSpec((16, D), lambda l: (0, 0)), pl.BlockSpec((None, D, cols), lambda l: (l, 0, 0)),
                  pl.BlockSpec((None, 1, cols), lambda l: (l, 0, 0))],
        out_specs=pl.BlockSpec((None, 16, cols), lambda l: (l, 0, 0)), compiler_params=_cparams(1))(craw, w_mod, b_cols)


def mod_bwd(craw, w_mod, g):
    nl, _, cols = w_mod.shape

    def body(c_ref, w_ref, g_ref, dw_ref, dc_ref, acc_ref):
        l = pl.program_id(0)

        @pl.when(l == 0)
        def _():
            acc_ref[...] = jnp.zeros_like(acc_ref)

        c = c_ref[...]
        s, vjp = jax.vjp(jax.nn.silu, c)
        dw_ref[...] = _dot(s, g_ref[...], TN)
        acc_ref[...] += _dot(g_ref[...], w_ref[...], NT)

        @pl.when(l == nl - 1)
        def _():
            dc_ref[...] = vjp(acc_ref[...])[0]

    return pl.pallas_call(
        body, name="mod_bwd", grid=(nl,),
        out_shape=[jax.ShapeDtypeStruct(w_mod.shape, F32), jax.ShapeDtypeStruct((16, D), F32)],
        in_specs=[pl.BlockSpec((16, D), lambda l: (0, 0)), pl.BlockSpec((None, D, cols), lambda l: (l, 0, 0)),
                  pl.BlockSpec((None, 16, cols), lambda l: (l, 0, 0))],
        out_specs=[pl.BlockSpec((None, D, cols), lambda l: (l, 0, 0)), pl.BlockSpec((16, D), lambda l: (0, 0))],
        scratch_shapes=[pltpu.VMEM((16, D), F32)], compiler_params=_cparams(1))(craw, w_mod, g)


def adamw(name, w, m, v, gs):
    r, c = w.shape
    s = gs.shape[0]
    tr = max([t for t in range(8, 257, 8) if r % t == 0], default=r)

    def body(w_ref, m_ref, v_ref, g_ref, go_ref, d_ref, mo_ref, vo_ref):
        g = g_ref[0].astype(F32)
        for k in range(1, s):
            g = g + g_ref[k].astype(F32)
        m_new = B1 * m_ref[...] + (1.0 - B1) * g
        v_new = B2 * v_ref[...] + (1.0 - B2) * jnp.square(g)
        m_hat = m_new / (1.0 - B1 ** STEP)
        v_hat = v_new / (1.0 - B2 ** STEP)
        go_ref[...] = g
        d_ref[...] = -LR * (m_hat / (jnp.sqrt(v_hat) + EPS) + WD * w_ref[...])
        mo_ref[...] = m_new
        vo_ref[...] = v_new

    blk = pl.BlockSpec((tr, c), lambda i: (i, 0))
    return pl.pallas_call(
        body, name=name, grid=(r // tr,), out_shape=[jax.ShapeDtypeStruct((r, c), F32)] * 4,
        in_specs=[blk, blk, blk, pl.BlockSpec((s, tr, c), lambda i: (0, i, 0))], out_specs=[blk] * 4,
        compiler_params=_cparams(1))(w, m, v, gs)


SMALL = ["c_ctx", "b_mod", "s5_lam_re", "s5_lam_im", "s5_log_dt", "s5_b_re", "s5_b_im", "s5_c_re", "s5_c_im", "s5_d",
         "b_glu", "hg_lb", "hg_norm_w", "ln1_g", "ln1_b", "conv_b", "ln2_g", "ln2_b"]
BIG = ["w_in", "w_glu", "w_out", "w_up", "w_down"]
WEIGHTS = ["c_ctx", "w_mod", "b_mod", "w_in", "s5_lam_re", "s5_lam_im", "s5_log_dt", "s5_b_re", "s5_b_im", "s5_c_re",
           "s5_c_im", "s5_d", "w_glu", "b_glu", "hg_lb", "hg_norm_w", "w_out", "ln1_g", "ln1_b", "w_up", "conv_w",
           "conv_b", "w_down", "ln2_g", "ln2_b"]
PACK_W = 1024


def _pack(arrs):
    flat = jnp.concatenate([a.reshape(-1) for a in arrs])
    rows = -(-flat.shape[0] // (8 * PACK_W)) * 8
    return jnp.pad(flat, (0, rows * PACK_W - flat.shape[0])).reshape(rows, PACK_W)


def _unpack(p, shapes):
    flat, out, o = p.reshape(-1), [], 0
    for s in shapes:
        k = math.prod(s)
        out.append(flat[o:o + k].reshape(s))
        o += k
    return out


def _gathered_cols(g):
    return jnp.moveaxis(g, 0, 2).reshape(g.shape[1], g.shape[2], -1)


def _gathered_rows(g):
    return jnp.moveaxis(g, 0, 1).reshape(g.shape[1], -1, g.shape[3])


def _step(p):
    nl = p["w_in"].shape[0]
    me = lax.axis_index("x") * 4 + lax.axis_index("y") * 2 + lax.axis_index("c")
    xc0 = jnp.concatenate([p["ctx"][0], p["x"][0]], axis=0)
    n = xc0.shape[0]
    target = p["loss_target"][0]

    hg3 = jnp.stack([p[k].reshape(-1) for k in ("hg_lb", "m_hg_lb", "v_hg_lb")])
    g_in, g_glu, g_out, g_up, g_down, g_cw, g_c, g_hg = _exchange(
        "gather_weights", [p["w_in"].astype(BF16), p["w_glu"].astype(BF16), p["w_out"].astype(BF16),
                           p["w_up"].astype(BF16), p["w_down"].astype(BF16), p["conv_w"], p["c"], hg3], False)
    w_in, w_up, conv_w = _gathered_cols(g_in), _gathered_cols(g_up), _gathered_cols(g_cw)
    w_glu, w_out, w_down = _gathered_rows(g_glu), _gathered_rows(g_out), _gathered_rows(g_down)
    hg_full = jnp.moveaxis(g_hg.reshape(N_DEV, 3, nl, 2, -1), 0, 3).reshape(3, nl, 2 * HGW)
    lb_all = lb_call(hg_full[0])

    craw = jnp.concatenate([g_c.reshape(N_DEV, D), jnp.broadcast_to(p["c_ctx"][None], (8, D))], axis=0)
    cols = p["w_mod"].shape[2]
    b_cols = lax.dynamic_slice_in_dim(p["b_mod"], me * cols, cols, axis=1)[:, None, :]
    (g_mod,) = _exchange("gather_mod", [mod_fwd(craw, p["w_mod"], b_cols)], False)
    mod_all = jnp.moveaxis(g_mod, 0, 2).reshape(nl, 16, 6 * D)
    mod_x = lax.dynamic_index_in_dim(mod_all, me, axis=1, keepdims=False)
    mod2 = jnp.stack([mod_all[:, 8], mod_x], axis=1)
    mvec = lambda l, k: mod2[l, :, k * D:(k + 1) * D][:, None, :]
    zvec = jnp.zeros((2, 1, D), F32)
    row = lambda a: a.reshape(1, 1, -1)

    to_hp = lambda a: jnp.moveaxis(a, -1, 2).reshape(nl * 2, S5H, S5P)
    prep_in = [p["s5_lam_re"].reshape(nl * 2, 1, S5P), p["s5_lam_im"].reshape(nl * 2, 1, S5P),
               p["s5_log_dt"].reshape(nl * 2, 32, 1), to_hp(p["s5_b_re"]), to_hp(p["s5_b_im"]),
               jnp.swapaxes(p["s5_c_re"], 2, 3).reshape(nl * 2, S5H, S5P),
               jnp.swapaxes(p["s5_c_im"], 2, 3).reshape(nl * 2, S5H, S5P)]
    s5a, s5b, s5c = s5_prep(*prep_in)

    T1 = lambda a, w=D, cb=_c0: (a, w, cb)
    saved = []
    xc = xc0
    (h,) = block_fwd("mod0", f_mod, n, [T1(xc)], [T1(mvec(0, 0)), T1(mvec(0, 1))], [(D, BF16, D, _c0)])
    for l in range(nl):
        proj = dense_nn(f"in_proj{l}", h, w_in[l])
        s5 = [s5_fwd(f"s5_fwd{l}_{d}", proj, s5a, s5b, s5c, 2 * l + d, d == 1) for d in range(2)]
        lbs = [lb_all[l, d * HGW:(d + 1) * HGW][None] for d in range(2)]
        gl = [gla_fwd(f"gla_fwd{l}_{d}", proj, lbs[d], d == 1) for d in range(2)]
        mix_t = [T1(proj, S5W), T1(proj, HGW, lambda j: 5), T1(s5[0][2], S5W), T1(s5[1][2], S5W),
                 T1(gl[0][0], HGW), T1(gl[1][0], HGW)]
        mix_p = [T1(row(p["s5_d"][l]), S5W), T1(w_glu[l][None], S5W), T1(row(p["b_glu"][l]), S5W),
                 T1(row(p["hg_norm_w"][l]), HD)]
        (y,) = block_fwd(f"mix{l}", f_mix, n, mix_t, mix_p, [(D, BF16, D, _c0)])
        z = dense_nn(f"out_proj{l}", y, w_out[l])
        ln1_p = [T1(mvec(l, 2)), T1(row(p["ln1_g"][l])), T1(row(p["ln1_b"][l])), T1(mvec(l, 3)), T1(mvec(l, 4))]
        x1, h2 = block_fwd(f"ln1_{l}", f_ln, n, [T1(xc), T1(z)], ln1_p, [(D, F32, D, _c0), (D, BF16, D, _c0)])
        up = dense_nn(f"up_proj{l}", h2, w_up[l])
        ct = DFF // 2
        act_t = [T1(up, ct, lambda j: j), T1(up, ct, lambda j: j + 2)]
        cb2 = p["conv_b"][l].reshape(1, 1, -1)
        act_p = [T1(conv_w[l][None, :, :DFF], ct, lambda j: j), T1(conv_w[l][None, :, DFF:], ct, lambda j: j),
                 T1(cb2[:, :, :DFF], ct, lambda j: j), T1(cb2[:, :, DFF:], ct, lambda j: j)]
        (act,) = block_fwd(f"act{l}", f_act, n, act_t, act_p, [(DFF, BF16, ct, lambda j: j)], n_col=2)
        dn = dense_nn(f"down_proj{l}", act, w_down[l])
        nxt = (mvec(l + 1, 0), mvec(l + 1, 1)) if l + 1 < nl else (zvec, zvec)
        ln2_p = [T1(mvec(l, 5)), T1(row(p["ln2_g"][l])), T1(row(p["ln2_b"][l])), T1(nxt[0]), T1(nxt[1])]
        x2, hn = block_fwd(f"ln2_{l}", f_ln, n, [T1(x1), T1(dn)], ln2_p, [(D, F32, D, _c0), (D, BF16, D, _c0)])
        saved.append(dict(xc=xc, h=h, proj=proj, s5=s5, gl=gl, lbs=lbs, mix_t=mix_t, mix_p=mix_p, y=y, z=z,
                          ln1_p=ln1_p, x1=x1, h2=h2, act_t=act_t, act_p=act_p, act=act, dn=dn, ln2_p=ln2_p))
        xc, h = x2, hn

    dxc, loss_part = loss_and_grad(xc, target)
    loss = lax.psum(loss_part[0, 0], AXES)

    g = {k: [None] * nl for k in ("w_in", "w_glu", "w_out", "w_up", "w_down", "conv_w", "conv_b", "s5_d", "b_glu",
                                  "hg_norm_w", "ln1_g", "ln1_b", "ln2_g", "ln2_b", "dlb", "s5")}
    dmod = [[None] * 6 for _ in range(nl)]
    dh_next = jnp.zeros((n, D), F32)
    fgrad = (D, F32, D, _c0)
    for l in reversed(range(nl)):
        sv = saved[l]
        (dx1, d_dn), dp = block_bwd(f"ln2_bwd{l}", f_ln, n, [T1(sv["x1"]), T1(sv["dn"])], sv["ln2_p"],
                                    [T1(dxc), T1(dh_next)], [fgrad, fgrad])
        dmod[l][5], g["ln2_g"][l], g["ln2_b"][l] = dp[0], dp[1], dp[2]
        if l + 1 < nl:
            dmod[l + 1][0], dmod[l + 1][1] = dp[3], dp[4]
        dact = dense_nt(f"down_bwd{l}", d_dn, w_down[l])
        g["w_down"][l] = dense_tn(f"down_wgrad{l}", sv["act"], d_dn)
        ct = DFF // 2
        cj = lambda j: j
        (dua, dug), dp = block_bwd(f"act_bwd{l}", f_act, n, sv["act_t"], sv["act_p"], [T1(dact, ct, cj)],
                                   [(DFF, BF16, ct, cj), (DFF, BF16, ct, cj)], n_col=2)
        g["conv_w"][l] = jnp.concatenate([dp[0][0], dp[1][0]], axis=-1)
        g["conv_b"][l] = jnp.concatenate([dp[2][0, 0], dp[3][0, 0]], axis=-1)
        dup = jnp.concatenate([dua, dug], axis=-1)
        dh2 = dense_nt(f"up_bwd{l}", dup, w_up[l])
        g["w_up"][l] = dense_tn(f"up_wgrad{l}", sv["h2"], dup)
        (dxc, dz), dp = block_bwd(f"ln1_bwd{l}", f_ln, n, [T1(sv["xc"]), T1(sv["z"])], sv["ln1_p"],
                                  [T1(dx1), T1(dh2)], [fgrad, fgrad])
        dmod[l][2], g["ln1_g"][l], g["ln1_b"][l], dmod[l][3], dmod[l][4] = dp
        dy = dense_nt(f"out_bwd{l}", dz, w_out[l])
        g["w_out"][l] = dense_tn(f"out_wgrad{l}", sv["y"], dz)
        half = (S5W, F32, S5W, _c0)
        (dpu, dpg, dys, dos), dp = block_bwd(f"mix_bwd{l}", f_mix, n, sv["mix_t"], sv["mix_p"], [T1(dy)],
                                                   [half, half, half, None, half, None])
        g["s5_d"][l], g["w_glu"][l], g["b_glu"][l], g["hg_norm_w"][l] = dp[0][0, 0], dp[1][0], dp[2][0, 0], dp[3][0, 0]
        gb = [gla_bwd(f"gla_bwd{l}_{d}", sv["proj"], sv["lbs"][d], sv["gl"][d][1], dos, d == 1) for d in range(2)]
        g["dlb"][l] = jnp.concatenate([gb[0][3], gb[1][3]], axis=-1)[0]
        sb = [s5_bwd(f"s5_bwd{l}_{d}", sv["proj"], dys, sv["s5"][d][0], sv["s5"][d][1], s5a, s5b, s5c, 2 * l + d, d == 1)
              for d in range(2)]
        g["s5"][l] = sb
        asm_t = [T1(dpu, S5W), T1(sb[0][0], S5W), T1(sb[1][0], S5W), T1(gb[0][0], HGW), T1(gb[1][0], HGW),
                 T1(gb[0][1], HGW), T1(gb[1][1], HGW), T1(gb[0][2], HGW), T1(gb[1][2], HGW), T1(dpg, HGW)]
        (dproj,) = block_fwd(
            f"dproj{l}", lambda tv, pv, i: (jnp.concatenate(
                [tv[0] + tv[1] + tv[2], tv[3], tv[4], tv[5] + tv[6], tv[7] + tv[8], tv[9]], axis=-1),),
            n, asm_t, [], [(INC, BF16, INC, _c0)])
        dh_next = dense_nt(f"in_bwd{l}", dproj, w_in[l])
        g["w_in"][l] = dense_tn(f"in_wgrad{l}", sv["h"], dproj)
    (dxc,), dp = block_bwd("mod0_bwd", f_mod_id, n, [T1(xc0)], [T1(mvec(0, 0)), T1(mvec(0, 1))],
                           [T1(dh_next), T1(dxc)], [fgrad])
    dmod[0][0], dmod[0][1] = dp
    grad_x = dxc[n - p["x"].shape[1]:][None]

    st = lambda k: jnp.stack([g["s5"][l][d][k] for l in range(nl) for d in range(2)])
    d_prep = s5_prep_bwd(*prep_in, st(1), st(2), st(3))
    from_hp = lambda a: jnp.moveaxis(a.reshape(nl, 2, S5H, S5W // S5H, 64), 2, -1)
    gs5 = {"s5_lam_re": d_prep[0].reshape(nl, 2, 32, 64), "s5_lam_im": d_prep[1].reshape(nl, 2, 32, 64),
           "s5_log_dt": d_prep[2].reshape(nl, 2, 32), "s5_b_re": from_hp(d_prep[3]), "s5_b_im": from_hp(d_prep[4]),
           "s5_c_re": jnp.swapaxes(d_prep[5].reshape(nl, 2, S5H, 32, 64), 2, 3),
           "s5_c_im": jnp.swapaxes(d_prep[6].reshape(nl, 2, S5H, 32, 64), 2, 3)}
    d_hg = lb_call(hg_full[0], jnp.stack(g["dlb"]))

    dmod_loc = jnp.stack([jnp.concatenate([dmod[l][k][:, 0] for k in range(6)], axis=-1) for l in range(nl)])
    (g_dmod,) = _exchange("gather_dmod", [dmod_loc], False)
    gcols = lax.dynamic_slice_in_dim(g_dmod, me * cols, cols, axis=3)
    g16 = jnp.concatenate([jnp.moveaxis(gcols[:, :, 1], 0, 1), jnp.moveaxis(gcols[:, :, 0], 0, 1)], axis=1)
    grad_w_mod, dcraw = mod_bwd(craw, p["w_mod"], g16)
    d_c_ctx = jnp.sum(dcraw[8:], axis=0)

    stk = lambda k: jnp.stack(g[k])
    by_cols = lambda a: jnp.moveaxis(a.reshape(a.shape[0], a.shape[1], N_DEV, -1), 2, 0)
    by_rows = lambda a: jnp.moveaxis(a.reshape(a.shape[0], N_DEV, -1, a.shape[2]), 1, 0)
    sends = [by_cols(stk("w_in")), by_rows(stk("w_glu").astype(BF16)), by_rows(stk("w_out")), by_cols(stk("w_up")),
             by_rows(stk("w_down")), by_cols(stk("conv_w"))]
    recv = _exchange("scatter_grads", sends, True)
    small_g = {"c_ctx": d_c_ctx, "b_mod": dmod_loc[:, 0] + dmod_loc[:, 1], "s5_d": stk("s5_d"), "b_glu": stk("b_glu"),
               "hg_lb": d_hg.reshape(nl, 2, HGW), "hg_norm_w": stk("hg_norm_w"), "ln1_g": stk("ln1_g")[:, 0, 0],
               "ln1_b": stk("ln1_b")[:, 0, 0], "conv_b": stk("conv_b"), "ln2_g": stk("ln2_g")[:, 0, 0],
               "ln2_b": stk("ln2_b")[:, 0, 0], **gs5}
    (g_small,) = _exchange("gather_small_grads", [_pack([small_g[k] for k in SMALL])], False)

    out = {}
    hgw = {"": hg_full[0].reshape(nl, 2, HGW), "m_": hg_full[1].reshape(nl, 2, HGW), "v_": hg_full[2].reshape(nl, 2, HGW)}
    full = lambda pre, k: hgw[pre] if k == "hg_lb" else p[pre + k]
    shapes = [full("", k).shape for k in SMALL]
    res = adamw("adamw_small", *[_pack([full(pre, k) for k in SMALL]) for pre in ("", "m_", "v_")], g_small)
    for kind, packed in zip(("grad_", "delta_", "new_m_", "new_v_"), res):
        for k, a in zip(SMALL, _unpack(packed, shapes)):
            if k == "hg_lb":
                a = lax.dynamic_slice_in_dim(a, me * (HGW // N_DEV), HGW // N_DEV, axis=2)
            out[kind + k] = a
    big = dict(zip(BIG + ["conv_w"], recv))
    big["w_mod"] = grad_w_mod[None]
    for k, gsum in big.items():
        shp = p[k].shape
        v2 = lambda a: a.reshape(-1, shp[-1])
        res = adamw("adamw_" + k, v2(p[k]), v2(p["m_" + k]), v2(p["v_" + k]), gsum.reshape(gsum.shape[0], -1, shp[-1]))
        for kind, a in zip(("grad_", "delta_", "new_m_", "new_v_"), res):
            out[kind + k] = a.reshape(shp)
    return (loss, grad_x, *[out[kind + k] for kind in ("grad_", "delta_", "new_m_", "new_v_") for k in WEIGHTS])


def kernel(x, c, ctx, c_ctx, w_mod, b_mod, w_in, s5_lam_re, s5_lam_im, s5_log_dt, s5_b_re, s5_b_im, s5_c_re, s5_c_im, s5_d, w_glu, b_glu, hg_lb, hg_norm_w, w_out, ln1_g, ln1_b, w_up, conv_w, conv_b, w_down, ln2_g, ln2_b, loss_target, m_c_ctx, m_w_mod, m_b_mod, m_w_in, m_s5_lam_re, m_s5_lam_im, m_s5_log_dt, m_s5_b_re, m_s5_b_im, m_s5_c_re, m_s5_c_im, m_s5_d, m_w_glu, m_b_glu, m_hg_lb, m_hg_norm_w, m_w_out, m_ln1_g, m_ln1_b, m_w_up, m_conv_w, m_conv_b, m_w_down, m_ln2_g, m_ln2_b, v_c_ctx, v_w_mod, v_b_mod, v_w_in, v_s5_lam_re, v_s5_lam_im, v_s5_log_dt, v_s5_b_re, v_s5_b_im, v_s5_c_re, v_s5_c_im, v_s5_d, v_w_glu, v_b_glu, v_hg_lb, v_hg_norm_w, v_w_out, v_ln1_g, v_ln1_b, v_w_up, v_conv_w, v_conv_b, v_w_down, v_ln2_g, v_ln2_b):
    return _step(dict(locals()))
```

```python
import functools
import math

import jax
import jax.numpy as jnp
from jax import lax
from jax.experimental import pallas as pl
from jax.experimental.pallas import tpu as pltpu

F32, BF16 = jnp.float32, jnp.bfloat16
N_DEV = 8
AXES = ("x", "y", "c")
D = 1024
S5W = 512
S5P = 2048
S5H = 16
HGW = 512
HD = 128
NH = 4
CK = 32
DFF = 2816
GRID_W = 64
INC = 3072
ALPHA = 8.0 ** 0.25
LN_EPS = 1e-5
RMS_EPS = 1e-6
LR, B1, B2, EPS, WD, STEP = 0.001, 0.9, 0.999, 1e-08, 0.01, 10
TT = 256
VMEM_MB = 56

NN = ((1,), (0,))
NT = ((1,), (1,))
TN = ((0,), (0,))


def _cparams(n_axes):
    return pltpu.CompilerParams(dimension_semantics=("arbitrary",) * n_axes, vmem_limit_bytes=VMEM_MB << 20)


def _dot(a, b, dims):
    return lax.dot_general(a.astype(BF16), b.astype(BF16), (dims, ((), ())), preferred_element_type=F32)


@jax.custom_vjp
def mm_nn(a, b):
    return _dot(a, b, NN)


@jax.custom_vjp
def mm_nt(a, b):
    return _dot(a, b, NT)


@jax.custom_vjp
def mm_tn(a, b):
    return _dot(a, b, TN)


mm_nn.defvjp(lambda a, b: (_dot(a, b, NN), (a, b)), lambda r, g: (_dot(g, r[1], NT), _dot(r[0], g, TN)))
mm_nt.defvjp(lambda a, b: (_dot(a, b, NT), (a, b)), lambda r, g: (_dot(g, r[1], NN), _dot(g, r[0], TN)))
mm_tn.defvjp(lambda a, b: (_dot(a, b, TN), (a, b)), lambda r, g: (_dot(r[1], g, NT), _dot(r[0], g, NN)))


def _roll_rows(u, s):
    return pltpu.roll(u, s % u.shape[0], 0)


@functools.partial(jax.custom_vjp, nondiff_argnums=(2,))
def shift_rows(u, m, s):
    return _roll_rows(u, s) * m


def _shift_fwd(u, m, s):
    return _roll_rows(u, s) * m, m


def _shift_bwd(s, m, g):
    return _roll_rows(g * m, -s), jnp.zeros_like(m)


shift_rows.defvjp(_shift_fwd, _shift_bwd)


def _scan_tile(pos, nt, rev):
    return jnp.where(pos == 0, 0, nt - pos) if rev else pos


def _exchange(name, arrays, all_to_all):
    k_arr = len(arrays)

    def body(*refs):
        ins, outs = refs[:k_arr], refs[k_arr:2 * k_arr]
        send_sems, recv_sems, local_sems = refs[2 * k_arr:]
        me = lax.axis_index("x") * 4 + lax.axis_index("y") * 2 + lax.axis_index("c")
        local = []
        for k in range(k_arr):
            cp = pltpu.make_async_copy(ins[k].at[me] if all_to_all else ins[k], outs[k].at[me], local_sems.at[k])
            cp.start()
            local.append(cp)
        sends = []
        for d in range(1, N_DEV):
            p = (me + d) % N_DEV
            for k in range(k_arr):
                cp = pltpu.make_async_remote_copy(
                    src_ref=ins[k].at[p] if all_to_all else ins[k], dst_ref=outs[k].at[me],
                    send_sem=send_sems.at[k, d - 1], recv_sem=recv_sems.at[k, d - 1],
                    device_id=(p // 4, (p // 2) % 2, p % 2), device_id_type=pl.DeviceIdType.MESH)
                cp.start()
                sends.append(cp)
        for d in range(1, N_DEV):
            q = (me + N_DEV - d) % N_DEV
            for k in range(k_arr):
                pltpu.make_async_remote_copy(
                    src_ref=ins[k].at[q] if all_to_all else ins[k], dst_ref=outs[k].at[q],
                    send_sem=send_sems.at[k, d - 1], recv_sem=recv_sems.at[k, d - 1],
                    device_id=(q // 4, (q // 2) % 2, q % 2), device_id_type=pl.DeviceIdType.MESH).wait_recv()
        for cp in sends:
            cp.wait_send()
        for cp in local:
            cp.wait()

    shapes = [a.shape if all_to_all else (N_DEV,) + a.shape for a in arrays]
    return pl.pallas_call(
        body, name=name,
        out_shape=[jax.ShapeDtypeStruct(s, a.dtype) for s, a in zip(shapes, arrays)],
        in_specs=[pl.BlockSpec(memory_space=pl.ANY)] * k_arr,
        out_specs=[pl.BlockSpec(memory_space=pl.ANY)] * k_arr,
        scratch_shapes=[pltpu.SemaphoreType.DMA((k_arr, N_DEV - 1)), pltpu.SemaphoreType.DMA((k_arr, N_DEV - 1)),
                        pltpu.SemaphoreType.DMA((k_arr,))],
    )(*arrays)


def _tile(n, prefs):
    for t in prefs:
        if n % t == 0:
            return t
    raise ValueError(n)


def dense_nn(name, a, w, out_dtype=F32):
    n, k = a.shape
    m = w.shape[1]
    tn, tm = _tile(n, (1088, 256)), _tile(m, (1024, 512, 256, 128))

    def body(a_ref, w_ref, o_ref):
        o_ref[...] = _dot(a_ref[...], w_ref[...], NN).astype(o_ref.dtype)

    return pl.pallas_call(
        body, name=name, grid=(m // tm, n // tn), out_shape=jax.ShapeDtypeStruct((n, m), out_dtype),
        in_specs=[pl.BlockSpec((tn, k), lambda j, i: (i, 0)), pl.BlockSpec((k, tm), lambda j, i: (0, j))],
        out_specs=pl.BlockSpec((tn, tm), lambda j, i: (i, j)), compiler_params=_cparams(2))(a, w)


def dense_nt(name, g, w, out_dtype=F32):
    n, m = g.shape
    k = w.shape[0]
    tn, tk = _tile(n, (544, 256)), _tile(k, (1024, 1408, 512, 256, 128))

    def body(g_ref, w_ref, o_ref):
        o_ref[...] = _dot(g_ref[...], w_ref[...], NT).astype(o_ref.dtype)

    return pl.pallas_call(
        body, name=name, grid=(k // tk, n // tn), out_shape=jax.ShapeDtypeStruct((n, k), out_dtype),
        in_specs=[pl.BlockSpec((tn, m), lambda j, i: (i, 0)), pl.BlockSpec((tk, m), lambda j, i: (j, 0))],
        out_specs=pl.BlockSpec((tn, tk), lambda j, i: (i, j)), compiler_params=_cparams(2))(g, w)


def dense_tn(name, a, g, out_dtype=BF16):
    n, k = a.shape
    m = g.shape[1]
    tn = _tile(n, (1088, 256))
    tk = _tile(k, (1024, 1408, 512, 256, 128))
    tm = _tile(m, (1024, 512, 256, 128))
    nt = n // tn

    def body(a_ref, g_ref, o_ref, acc_ref):
        t = pl.program_id(2)

        @pl.when(t == 0)
        def _():
            acc_ref[...] = jnp.zeros_like(acc_ref)

        acc_ref[...] += _dot(a_ref[...], g_ref[...], TN)

        @pl.when(t == nt - 1)
        def _():
            o_ref[...] = acc_ref[...].astype(o_ref.dtype)

    return pl.pallas_call(
        body, name=name, grid=(k // tk, m // tm, nt), out_shape=jax.ShapeDtypeStruct((k, m), out_dtype),
        in_specs=[pl.BlockSpec((tn, tk), lambda i, j, t: (t, i)), pl.BlockSpec((tn, tm), lambda i, j, t: (t, j))],
        out_specs=pl.BlockSpec((tk, tm), lambda i, j, t: (i, j)),
        scratch_shapes=[pltpu.VMEM((tk, tm), F32)], compiler_params=_cparams(3))(a, g)


def _c0(j):
    return 0


def _tspec(w, cb):
    return pl.BlockSpec((TT, w), lambda j, i: (i, cb(j)))


def _pspec(arr, w, cb):
    two = arr.shape[0] == 2
    return pl.BlockSpec((None, arr.shape[1], w), lambda j, i: (jnp.minimum(i, 1) if two else 0, 0, cb(j)))


def block_fwd(name, fn, n, tiled, params, outs, n_col=1):
    nt_, np_ = len(tiled), len(params)

    def body(*refs):
        i = pl.program_id(1)
        tv = [r[...].astype(F32) for r in refs[:nt_]]
        pv = [r[...].astype(F32) for r in refs[nt_:nt_ + np_]]
        for o_ref, r in zip(refs[nt_ + np_:], fn(tv, pv, i)):
            o_ref[...] = r.astype(o_ref.dtype)

    return pl.pallas_call(
        body, name=name, grid=(n_col, n // TT),
        out_shape=[jax.ShapeDtypeStruct((n, c), dt) for c, dt, _, _ in outs],
        in_specs=[_tspec(w, cb) for _, w, cb in tiled] + [_pspec(a, w, cb) for a, w, cb in params],
        out_specs=[_tspec(w, cb) for _, _, w, cb in outs], compiler_params=_cparams(2),
    )(*[a for a, _, _ in tiled], *[a for a, _, _ in params])


def block_bwd(name, fn, n, tiled, params, cots, grads, n_col=1):
    nt_, np_, nc_ = len(tiled), len(params), len(cots)
    want = [k for k, g in enumerate(grads) if g is not None]

    def body(*refs):
        i = pl.program_id(1)
        tv = [r[...].astype(F32) for r in refs[:nt_]]
        pv = [r[...].astype(F32) for r in refs[nt_:nt_ + np_]]
        cv = [r[...].astype(F32) for r in refs[nt_ + np_:nt_ + np_ + nc_]]
        o_refs = refs[nt_ + np_ + nc_:]
        _, vjp = jax.vjp(lambda t, p: list(fn(t, p, i)), tv, pv)
        dt, dp = vjp(cv)
        for o_ref, k in zip(o_refs, want):
            o_ref[...] = dt[k].astype(o_ref.dtype)
        for o_ref, g, (arr, _, _) in zip(o_refs[len(want):], dp, params):
            first = (i == 0) | (i == 1) if arr.shape[0] == 2 else i == 0

            @pl.when(first)
            def _(o_ref=o_ref):
                o_ref[...] = jnp.zeros_like(o_ref)

            o_ref[...] += g

    res = pl.pallas_call(
        body, name=name, grid=(n_col, n // TT),
        out_shape=[jax.ShapeDtypeStruct((n, grads[k][0]), grads[k][1]) for k in want]
        + [jax.ShapeDtypeStruct(a.shape, F32) for a, _, _ in params],
        in_specs=[_tspec(w, cb) for _, w, cb in tiled] + [_pspec(a, w, cb) for a, w, cb in params]
        + [_tspec(w, cb) for _, w, cb in cots],
        out_specs=[_tspec(grads[k][2], grads[k][3]) for k in want] + [_pspec(a, w, cb) for a, w, cb in params],
        compiler_params=_cparams(2),
    )(*[a for a, _, _ in tiled], *[a for a, _, _ in params], *[a for a, _, _ in cots])
    return res[:len(want)], res[len(want):]


def f_mod(tv, pv, i):
    (x,), (sh, sc) = tv, pv
    return (x * (1.0 + sc) + sh,)


def f_mod_id(tv, pv, i):
    return (f_mod(tv, pv, i)[0], tv[0])


def f_ln(tv, pv, i):
    (x, z), (gate, g, b, sh, sc) = tv, pv
    pre = ALPHA * x + gate * z
    mu = jnp.mean(pre, axis=-1, keepdims=True)
    var = jnp.mean(jnp.square(pre - mu), axis=-1, keepdims=True)
    xn = (pre - mu) * lax.rsqrt(var + LN_EPS) * g + b
    return xn, xn * (1.0 + sc) + sh


def f_mix(tv, pv, i):
    (pu, pg, y0, y1, o0, o1), (d_skip, w_glu, b_glu, norm_w) = tv, pv
    s5y = jax.nn.gelu(y0 + y1 + pu * d_skip)
    s5o = s5y * jax.nn.sigmoid(mm_nn(s5y, w_glu) + b_glu)
    o = o0 + o1
    heads = []
    for h in range(NH):
        oh = o[:, h * HD:(h + 1) * HD]
        heads.append(oh * lax.rsqrt(jnp.mean(jnp.square(oh), axis=-1, keepdims=True) + RMS_EPS) * norm_w)
    hg = jnp.concatenate(heads, axis=-1) * jax.nn.silu(pg)
    return (jnp.concatenate([s5o, hg], axis=-1),)


def f_act(tv, pv, i):
    (ua, ug), (cwa, cwg, cba, cbg) = tv, pv
    t = lax.broadcasted_iota(jnp.int32, (TT, 1), 0)
    lat = i > 0
    m_dn = jnp.where((t == 0) | (lat & (t % GRID_W == 0)), 0.0, 1.0)
    m_up = jnp.where((t == TT - 1) | (lat & (t % GRID_W == GRID_W - 1)), 0.0, 1.0)

    def conv(u, w, b):
        return shift_rows(u, m_dn, 1) * w[0:1] + u * w[1:2] + shift_rows(u, m_up, -1) * w[2:3] + b

    return (jax.nn.silu(conv(ua, cwa, cba)) * conv(ug, cwg, cbg),)


def loss_and_grad(xf, target):
    n = xf.shape[0]

    def body(x_ref, t_ref, dy_ref, l_ref):
        i = pl.program_id(0)

        @pl.when(i == 0)
        def _():
            l_ref[...] = jnp.zeros_like(l_ref)
            dy_ref[...] = jnp.zeros_like(dy_ref)

        @pl.when(i > 0)
        def _():
            e = x_ref[...] - t_ref[...]
            dy_ref[...] = e * (1.0 / D)
            l_ref[...] += 0.5 / D * jnp.sum(jnp.square(e))

    return pl.pallas_call(
        body, name="loss", grid=(n // TT,),
        out_shape=[jax.ShapeDtypeStruct((n, D), F32), jax.ShapeDtypeStruct((8, 128), F32)],
        in_specs=[pl.BlockSpec((TT, D), lambda i: (i, 0)), pl.BlockSpec((TT, D), lambda i: (jnp.maximum(i - 1, 0), 0))],
        out_specs=[pl.BlockSpec((TT, D), lambda i: (i, 0)), pl.BlockSpec((8, 128), lambda i: (0, 0))],
        compiler_params=_cparams(1))(xf, target)


def f_prep(lr, li, ldt, bre, bim, cre, cim):
    gi = lax.broadcasted_iota(jnp.int32, (S5W // S5H, S5P), 0)
    gc = lax.broadcasted_iota(jnp.int32, (S5W // S5H, S5P), 1) // 64
    dt = jnp.exp(jnp.sum(jnp.where(gi == gc, ldt, 0.0), axis=0, keepdims=True))
    mag, ang = jnp.exp(lr * dt), li * dt
    ar, ai = mag * jnp.cos(ang), mag * jnp.sin(ang)
    den = lr * lr + li * li
    nr, ni = ar - 1.0, ai
    cr = (nr * lr + ni * li) / den
    ci = (ni * lr - nr * li) / den
    bbr = cr * bre - ci * bim
    bbi = cr * bim + ci * bre
    rg = lax.broadcasted_iota(jnp.int32, (S5W, S5P), 0) // S5H
    cg = lax.broadcasted_iota(jnp.int32, (S5W, S5P), 1) // 64
    mask = (rg == cg).astype(F32)
    blk = lambda a: jnp.concatenate([a] * (S5W // S5H), axis=0) * mask
    return ar, ai, blk(bbr), blk(bbi), blk(cre), blk(-cim)


def s5_prep(lr, li, ldt, bre, bim, cre, cim):
    n2 = lr.shape[0]

    def body(lr_r, li_r, ldt_r, bre_r, bim_r, cre_r, cim_r, a_ref, b_ref, c_ref):
        ar, ai, bbr, bbi, cbr, cbi = f_prep(lr_r[...], li_r[...], ldt_r[...], bre_r[...], bim_r[...], cre_r[...], cim_r[...])
        a_ref[0], a_ref[1] = ar, ai
        b_ref[0], b_ref[1] = bbr.astype(BF16), bbi.astype(BF16)
        c_ref[0], c_ref[1] = cbr.astype(BF16), cbi.astype(BF16)

    sp = lambda r, c: pl.BlockSpec((None, r, c), lambda i: (i, 0, 0))
    sp4 = lambda r, c: pl.BlockSpec((None, 2, r, c), lambda i: (i, 0, 0, 0))
    return pl.pallas_call(
        body, name="s5_prep", grid=(n2,),
        out_shape=[jax.ShapeDtypeStruct((n2, 2, 1, S5P), F32), jax.ShapeDtypeStruct((n2, 2, S5W, S5P), BF16),
                   jax.ShapeDtypeStruct((n2, 2, S5W, S5P), BF16)],
        in_specs=[sp(1, S5P), sp(1, S5P), sp(32, 1), sp(S5H, S5P), sp(S5H, S5P), sp(S5H, S5P), sp(S5H, S5P)],
        out_specs=[sp4(1, S5P), sp4(S5W, S5P), sp4(S5W, S5P)], compiler_params=_cparams(1),
    )(lr, li, ldt, bre, bim, cre, cim)


def s5_prep_bwd(lr, li, ldt, bre, bim, cre, cim, da, db, dc):
    n2 = lr.shape[0]

    def body(lr_r, li_r, ldt_r, bre_r, bim_r, cre_r, cim_r, da_r, db_r, dc_r, *outs):
        args = [r[...] for r in (lr_r, li_r, ldt_r, bre_r, bim_r, cre_r, cim_r)]
        _, vjp = jax.vjp(f_prep, *args)
        for o_ref, g in zip(outs, vjp((da_r[0], da_r[1], db_r[0], db_r[1], dc_r[0], dc_r[1]))):
            o_ref[...] = g

    sp = lambda r, c: pl.BlockSpec((None, r, c), lambda i: (i, 0, 0))
    sp4 = lambda r, c: pl.BlockSpec((None, 2, r, c), lambda i: (i, 0, 0, 0))
    ins = [sp(1, S5P), sp(1, S5P), sp(32, 1), sp(S5H, S5P), sp(S5H, S5P), sp(S5H, S5P), sp(S5H, S5P)]
    return pl.pallas_call(
        body, name="s5_prep_bwd", grid=(n2,),
        out_shape=[jax.ShapeDtypeStruct(a.shape, F32) for a in (lr, li, ldt, bre, bim, cre, cim)],
        in_specs=ins + [sp4(1, S5P), sp4(S5W, S5P), sp4(S5W, S5P)], out_specs=ins, compiler_params=_cparams(1),
    )(lr, li, ldt, bre, bim, cre, cim, da, db, dc)


S5_DIAG = 2
_CU, _CP = S5W // S5_DIAG, S5P // S5_DIAG


def _bd_nn(u, w_ref, k):
    return jnp.concatenate([_dot(u[:, j * _CU:(j + 1) * _CU], w_ref[k, j * _CU:(j + 1) * _CU, j * _CP:(j + 1) * _CP], NN)
                            for j in range(S5_DIAG)], axis=1)


def _bd_nt(x, w_ref, k):
    return jnp.concatenate([_dot(x[:, j * _CP:(j + 1) * _CP], w_ref[k, j * _CU:(j + 1) * _CU, j * _CP:(j + 1) * _CP], NT)
                            for j in range(S5_DIAG)], axis=1)


def _bd_tn_acc(acc_ref, k, a, g):
    for j in range(S5_DIAG):
        acc_ref[k, j * _CU:(j + 1) * _CU, j * _CP:(j + 1) * _CP] += _dot(a[:, j * _CU:(j + 1) * _CU],
                                                                         g[:, j * _CP:(j + 1) * _CP], TN)


def s5_fwd(name, proj, a, bb, cb, ld, rev):
    n = proj.shape[0]
    nt = n // TT

    def body(u_ref, a_ref, b_ref, c_ref, xr_ref, xi_ref, y_ref, cr_ref, ci_ref):
        @pl.when(pl.program_id(0) == 0)
        def _():
            cr_ref[...] = jnp.zeros_like(cr_ref)
            ci_ref[...] = jnp.zeros_like(ci_ref)

        u = u_ref[...]
        xr_ref[...] = _bd_nn(u, b_ref, 0)
        xi_ref[...] = _bd_nn(u, b_ref, 1)
        ar, ai = a_ref[0], a_ref[1]

        def step(k, carry):
            cr, ci = carry
            t = TT - 1 - k if rev else k
            nr = ar * cr - ai * ci + xr_ref[pl.ds(t, 1), :]
            ni = ar * ci + ai * cr + xi_ref[pl.ds(t, 1), :]
            xr_ref[pl.ds(t, 1), :] = nr
            xi_ref[pl.ds(t, 1), :] = ni
            return nr, ni

        cr, ci = lax.fori_loop(0, TT, step, (cr_ref[...], ci_ref[...]), unroll=8)
        cr_ref[...] = cr
        ci_ref[...] = ci
        y_ref[...] = _bd_nt(xr_ref[...], c_ref, 0) + _bd_nt(xi_ref[...], c_ref, 1)

    tile = lambda w: pl.BlockSpec((TT, w), lambda s: (_scan_tile(s, nt, rev), 0))
    par = lambda r: pl.BlockSpec((None, 2, r, S5P), lambda s: (ld, 0, 0, 0))
    return pl.pallas_call(
        body, name=name, grid=(nt,),
        out_shape=[jax.ShapeDtypeStruct((n, S5P), F32), jax.ShapeDtypeStruct((n, S5P), F32),
                   jax.ShapeDtypeStruct((n, S5W), F32)],
        in_specs=[tile(S5W), par(1), par(S5W), par(S5W)], out_specs=[tile(S5P), tile(S5P), tile(S5W)],
        scratch_shapes=[pltpu.VMEM((1, S5P), F32), pltpu.VMEM((1, S5P), F32)], compiler_params=_cparams(1),
    )(proj, a, bb, cb)


def s5_bwd(name, proj, dy, xr, xi, a, bb, cb, ld, rev):
    n = proj.shape[0]
    nt = n // TT
    tb = TT // 8

    def tile_of(s):
        return _scan_tile(nt - 1 - s, nt, rev)

    def edge_of(s):
        pos = nt - 1 - s
        prev = _scan_tile(jnp.maximum(pos - 1, 0), nt, rev)
        return prev * tb if rev else jnp.maximum(pos * tb - 1, 0)

    def body(u_ref, dy_ref, xr_ref, xi_ref, er_ref, ei_ref, a_ref, b_ref, c_ref,
             du_ref, da_ref, db_ref, dc_ref, gr_ref, gi_ref, cr_ref, ci_ref):
        s = pl.program_id(0)

        @pl.when(s == 0)
        def _():
            cr_ref[...] = jnp.zeros_like(cr_ref)
            ci_ref[...] = jnp.zeros_like(ci_ref)
            da_ref[...] = jnp.zeros_like(da_ref)
            db_ref[...] = jnp.zeros_like(db_ref)
            dc_ref[...] = jnp.zeros_like(dc_ref)

        dyv, u = dy_ref[...], u_ref[...]
        xrv, xiv = xr_ref[...], xi_ref[...]
        gr_ref[...] = _bd_nn(dyv, c_ref, 0)
        gi_ref[...] = _bd_nn(dyv, c_ref, 1)
        _bd_tn_acc(dc_ref, 0, dyv, xrv)
        _bd_tn_acc(dc_ref, 1, dyv, xiv)
        ar, ai = a_ref[0], a_ref[1]

        def step(k, carry):
            cr, ci = carry
            t = k if rev else TT - 1 - k
            g_r = gr_ref[pl.ds(t, 1), :] + cr
            g_i = gi_ref[pl.ds(t, 1), :] + ci
            gr_ref[pl.ds(t, 1), :] = g_r
            gi_ref[pl.ds(t, 1), :] = g_i
            return ar * g_r + ai * g_i, ar * g_i - ai * g_r

        cr, ci = lax.fori_loop(0, TT, step, (cr_ref[...], ci_ref[...]), unroll=8)
        cr_ref[...] = cr
        ci_ref[...] = ci
        g_r, g_i = gr_ref[...], gi_ref[...]
        rows = lax.broadcasted_iota(jnp.int32, (TT, 1), 0)
        live = jnp.where(s == nt - 1, 0.0, 1.0)
        if rev:
            pr = jnp.where(rows == TT - 1, er_ref[0:1, :] * live, _roll_rows(xrv, -1))
            pi = jnp.where(rows == TT - 1, ei_ref[0:1, :] * live, _roll_rows(xiv, -1))
        else:
            pr = jnp.where(rows == 0, er_ref[7:8, :] * live, _roll_rows(xrv, 1))
            pi = jnp.where(rows == 0, ei_ref[7:8, :] * live, _roll_rows(xiv, 1))
        da_ref[0] += jnp.sum(g_r * pr + g_i * pi, axis=0, keepdims=True)
        da_ref[1] += jnp.sum(g_i * pr - g_r * pi, axis=0, keepdims=True)
        du_ref[...] = _bd_nt(g_r, b_ref, 0) + _bd_nt(g_i, b_ref, 1)
        _bd_tn_acc(db_ref, 0, u, g_r)
        _bd_tn_acc(db_ref, 1, u, g_i)

    tile = lambda w: pl.BlockSpec((TT, w), lambda s: (tile_of(s), 0))
    edge = pl.BlockSpec((8, S5P), lambda s: (edge_of(s), 0))
    par = lambda r: pl.BlockSpec((None, 2, r, S5P), lambda s: (ld, 0, 0, 0))
    acc = lambda r: pl.BlockSpec((2, r, S5P), lambda s: (0, 0, 0))
    return pl.pallas_call(
        body, name=name, grid=(nt,),
        out_shape=[jax.ShapeDtypeStruct((n, S5W), F32), jax.ShapeDtypeStruct((2, 1, S5P), F32),
                   jax.ShapeDtypeStruct((2, S5W, S5P), F32), jax.ShapeDtypeStruct((2, S5W, S5P), F32)],
        in_specs=[tile(S5W), tile(S5W), tile(S5P), tile(S5P), edge, edge, par(1), par(S5W), par(S5W)],
        out_specs=[tile(S5W), acc(1), acc(S5W), acc(S5W)],
        scratch_shapes=[pltpu.VMEM((TT, S5P), F32), pltpu.VMEM((TT, S5P), F32),
                        pltpu.VMEM((1, S5P), F32), pltpu.VMEM((1, S5P), F32)], compiler_params=_cparams(1),
    )(proj, dy, xr, xi, xr, xi, a, bb, cb)


def gla_tile(r, v, qr, lb, sts, rev):
    ncc = TT // CK
    f = lb + (1.0 - lb) * jax.nn.sigmoid(r)
    k, lf, q = 1.0 - f, jnp.log(f), jax.nn.silu(qr)
    rows = lax.broadcasted_iota(jnp.int32, (TT, 1), 0)
    pos = rows % CK
    b = lf
    for s in (1, 2, 4, 8, 16):
        m = ((pos < CK - s) if rev else (pos >= s)).astype(F32)
        b = b + shift_rows(b, m, -s if rev else s)
    etot = [jnp.sum(lf[c * CK:(c + 1) * CK], axis=0, keepdims=True) for c in range(ncc)]
    e = jnp.concatenate([jnp.broadcast_to(t, (CK, HGW)) for t in etot], axis=0)
    kd, qe, qa = k * jnp.exp(e - b), q * jnp.exp(b), q * jnp.exp(b - e)
    cm = [(rows // CK == c).astype(F32) for c in range(ncc)]
    r2 = lax.broadcasted_iota(jnp.int32, (TT, TT), 0)
    c2 = lax.broadcasted_iota(jnp.int32, (TT, TT), 1)
    amask = (r2 // CK == c2 // CK) & ((r2 <= c2) if rev else (r2 >= c2))
    outs, new_sts = [], []
    for h in range(NH):
        ln = slice(h * HD, (h + 1) * HD)
        kdh, qeh, vh = kd[:, ln], qe[:, ln], v[:, ln]
        att = jnp.where(amask, mm_nt(qa[:, ln], kdh), 0.0)
        ds = mm_tn(jnp.concatenate([kdh * cm[c] for c in range(ncc)], axis=1), vh)
        st, starts = sts[h], [None] * ncc
        for c in (reversed(range(ncc)) if rev else range(ncc)):
            starts[c] = st
            dec = jnp.transpose(jnp.broadcast_to(jnp.exp(etot[c][:, ln]), (HD, HD)))
            st = dec * st + ds[c * HD:(c + 1) * HD]
        new_sts.append(st)
        qex = jnp.concatenate([qeh * cm[c] for c in range(ncc)], axis=1)
        outs.append(mm_nn(att, vh) + mm_nn(qex, jnp.concatenate(starts, axis=0)))
    return jnp.concatenate(outs, axis=1), new_sts


def _gla_specs(n, rev, order):
    nt = n // TT
    fcol = 2 if rev else 1
    tile = lambda cbk: pl.BlockSpec((TT, HGW), lambda s: (order(s), cbk))
    return nt, [tile(fcol), tile(3), tile(4)], tile(0)


def gla_fwd(name, proj, lb, rev):
    n = proj.shape[0]
    nt, in_tiles, out_tile = _gla_specs(n, rev, lambda s: _scan_tile(s, n // TT, rev))

    def body(r_ref, v_ref, q_ref, lb_ref, o_ref, st_ref, s_ref):
        @pl.when(pl.program_id(0) == 0)
        def _():
            s_ref[...] = jnp.zeros_like(s_ref)

        sts = [s_ref[h] for h in range(NH)]
        for h in range(NH):
            st_ref[h] = sts[h]
        o, new = gla_tile(r_ref[...], v_ref[...], q_ref[...], lb_ref[...], sts, rev)
        o_ref[...] = o
        for h in range(NH):
            s_ref[h] = new[h]

    st_spec = pl.BlockSpec((None, NH, HD, HD), lambda s: (_scan_tile(s, nt, rev), 0, 0, 0))
    return pl.pallas_call(
        body, name=name, grid=(nt,),
        out_shape=[jax.ShapeDtypeStruct((n, HGW), F32), jax.ShapeDtypeStruct((nt, NH, HD, HD), F32)],
        in_specs=in_tiles + [pl.BlockSpec((1, HGW), lambda s: (0, 0))], out_specs=[out_tile, st_spec],
        scratch_shapes=[pltpu.VMEM((NH, HD, HD), F32)], compiler_params=_cparams(1),
    )(proj, proj, proj, lb)


def gla_bwd(name, proj, lb, st_all, do, rev):
    n = proj.shape[0]
    order = lambda s: _scan_tile(n // TT - 1 - s, n // TT, rev)
    nt, in_tiles, out_tile = _gla_specs(n, rev, order)

    def body(r_ref, v_ref, q_ref, lb_ref, st_ref, do_ref, dr_ref, dv_ref, dq_ref, dlb_ref, ds_ref):
        @pl.when(pl.program_id(0) == 0)
        def _():
            ds_ref[...] = jnp.zeros_like(ds_ref)
            dlb_ref[...] = jnp.zeros_like(dlb_ref)

        _, vjp = jax.vjp(functools.partial(gla_tile, rev=rev), r_ref[...], v_ref[...], q_ref[...], lb_ref[...],
                         [st_ref[h] for h in range(NH)])
        dr, dv, dq, dlb, dsts = vjp((do_ref[...], [ds_ref[h] for h in range(NH)]))
        dr_ref[...] = dr
        dv_ref[...] = dv
        dq_ref[...] = dq
        dlb_ref[...] += dlb
        for h in range(NH):
            ds_ref[h] = dsts[h]

    st_spec = pl.BlockSpec((None, NH, HD, HD), lambda s: (order(s), 0, 0, 0))
    row = pl.BlockSpec((1, HGW), lambda s: (0, 0))
    return pl.pallas_call(
        body, name=name, grid=(nt,),
        out_shape=[jax.ShapeDtypeStruct((n, HGW), F32)] * 3 + [jax.ShapeDtypeStruct((1, HGW), F32)],
        in_specs=in_tiles + [row, st_spec, out_tile], out_specs=[out_tile] * 3 + [row],
        scratch_shapes=[pltpu.VMEM((NH, HD, HD), F32)], compiler_params=_cparams(1),
    )(proj, proj, proj, lb, st_all, do)


def f_lb(rows):
    mx = functools.reduce(jnp.maximum, rows)
    ex = [jnp.exp(r - mx) for r in rows]
    tot = functools.reduce(jnp.add, ex)
    out, acc = [jnp.zeros_like(rows[0])], None
    for e in ex[1:]:
        acc = e / tot if acc is None else acc + e / tot
        out.append(acc)
    return out


def lb_call(hg, dlb=None):
    nl = hg.shape[0]

    def body(*refs):
        rows = [refs[0][l:l + 1, :] for l in range(nl)]
        if dlb is None:
            res = f_lb(rows)
        else:
            _, vjp = jax.vjp(f_lb, rows)
            (res,) = vjp([refs[1][l:l + 1, :] for l in range(nl)])
        for l in range(nl):
            refs[-1][l:l + 1, :] = res[l]

    args = (hg,) if dlb is None else (hg, dlb)
    return pl.pallas_call(body, name="lower_bounds" if dlb is None else "lower_bounds_bwd",
                          out_shape=jax.ShapeDtypeStruct(hg.shape, F32))(*args)


def mod_fwd(craw, w_mod, b_cols):
    nl, _, cols = w_mod.shape

    def body(c_ref, w_ref, b_ref, o_ref):
        o_ref[...] = _dot(jax.nn.silu(c_ref[...]), w_ref[...], NN) + b_ref[...]

    return pl.pallas_call(
        body, name="mod_fwd", grid=(nl,), out_shape=jax.ShapeDtypeStruct((nl, 16, cols), F32),
        in_specs=[pl.BlockSpec((16, D), lambda l: (0, 0)), pl.BlockSpec((None, D, cols), lambda l: (l, 0, 0)),
                  pl.BlockSpec((None, 1, cols), lambda l: (l, 0, 0))],
        out_specs=pl.BlockSpec((None, 16, cols), lambda l: (l, 0, 0)), compiler_params=_cparams(1))(craw, w_mod, b_cols)


def mod_bwd(craw, w_mod, g):
    nl, _, cols = w_mod.shape

    def body(c_ref, w_ref, g_ref, dw_ref, dc_ref, acc_ref):
        l = pl.program_id(0)

        @pl.when(l == 0)
        def _():
            acc_ref[...] = jnp.zeros_like(acc_ref)

        c = c_ref[...]
        s, vjp = jax.vjp(jax.nn.silu, c)
        dw_ref[...] = _dot(s, g_ref[...], TN)
        acc_ref[...] += _dot(g_ref[...], w_ref[...], NT)

        @pl.when(l == nl - 1)
        def _():
            dc_ref[...] = vjp(acc_ref[...])[0]

    return pl.pallas_call(
        body, name="mod_bwd", grid=(nl,),
        out_shape=[jax.ShapeDtypeStruct(w_mod.shape, F32), jax.ShapeDtypeStruct((16, D), F32)],
        in_specs=[pl.BlockSpec((16, D), lambda l: (0, 0)), pl.BlockSpec((None, D, cols), lambda l: (l, 0, 0)),
                  pl.BlockSpec((None, 16, cols), lambda l: (l, 0, 0))],
        out_specs=[pl.BlockSpec((None, D, cols), lambda l: (l, 0, 0)), pl.BlockSpec((16, D), lambda l: (0, 0))],
        scratch_shapes=[pltpu.VMEM((16, D), F32)], compiler_params=_cparams(1))(craw, w_mod, g)


def sum_parts(parts):
    def body(p_ref, o_ref):
        acc = p_ref[0]
        for k in range(1, parts.shape[0]):
            acc = acc + p_ref[k]
        o_ref[...] = acc

    return pl.pallas_call(body, name="sum_small_grads", out_shape=jax.ShapeDtypeStruct(parts.shape[1:], parts.dtype),
                          compiler_params=pltpu.CompilerParams(vmem_limit_bytes=VMEM_MB << 20))(parts)


def adamw(name, w, m, v, gs):
    r, c = w.shape
    s = gs.shape[0]
    tr = max([t for t in range(8, 257, 8) if r % t == 0], default=r)

    def body(w_ref, m_ref, v_ref, g_ref, go_ref, d_ref, mo_ref, vo_ref):
        g = g_ref[0].astype(F32)
        for k in range(1, s):
            g = g + g_ref[k].astype(F32)
        m_new = B1 * m_ref[...] + (1.0 - B1) * g
        v_new = B2 * v_ref[...] + (1.0 - B2) * jnp.square(g)
        m_hat = m_new / (1.0 - B1 ** STEP)
        v_hat = v_new / (1.0 - B2 ** STEP)
        go_ref[...] = g
        d_ref[...] = -LR * (m_hat / (jnp.sqrt(v_hat) + EPS) + WD * w_ref[...])
        mo_ref[...] = m_new
        vo_ref[...] = v_new

    blk = pl.BlockSpec((tr, c), lambda i: (i, 0))
    return pl.pallas_call(
        body, name=name, grid=(r // tr,), out_shape=[jax.ShapeDtypeStruct((r, c), F32)] * 4,
        in_specs=[blk, blk, blk, pl.BlockSpec((s, tr, c), lambda i: (0, i, 0))], out_specs=[blk] * 4,
        compiler_params=_cparams(1))(w, m, v, gs)


SMALL = ["c_ctx", "b_mod", "s5_lam_re", "s5_lam_im", "s5_log_dt", "s5_b_re", "s5_b_im", "s5_c_re", "s5_c_im", "s5_d",
         "b_glu", "hg_lb", "hg_norm_w", "ln1_g", "ln1_b", "conv_b", "ln2_g", "ln2_b"]
BIG = ["w_in", "w_glu", "w_out", "w_up", "w_down"]
WEIGHTS = ["c_ctx", "w_mod", "b_mod", "w_in", "s5_lam_re", "s5_lam_im", "s5_log_dt", "s5_b_re", "s5_b_im", "s5_c_re",
           "s5_c_im", "s5_d", "w_glu", "b_glu", "hg_lb", "hg_norm_w", "w_out", "ln1_g", "ln1_b", "w_up", "conv_w",
           "conv_b", "w_down", "ln2_g", "ln2_b"]
PACK_W = 1024


def _pack(arrs):
    flat = jnp.concatenate([a.reshape(-1) for a in arrs])
    rows = -(-flat.shape[0] // (8 * N_DEV * PACK_W)) * 8 * N_DEV
    return jnp.pad(flat, (0, rows * PACK_W - flat.shape[0])).reshape(rows, PACK_W)


def _unpack(p, shapes):
    flat, out, o = p.reshape(-1), [], 0
    for s in shapes:
        k = math.prod(s)
        out.append(flat[o:o + k].reshape(s))
        o += k
    return out


def _gathered_cols(g):
    return jnp.moveaxis(g, 0, 2).reshape(g.shape[1], g.shape[2], -1)


def _gathered_rows(g):
    return jnp.moveaxis(g, 0, 1).reshape(g.shape[1], -1, g.shape[3])


def _step(p):
    nl = p["w_in"].shape[0]
    me = lax.axis_index("x") * 4 + lax.axis_index("y") * 2 + lax.axis_index("c")
    xc0 = jnp.concatenate([p["ctx"][0], p["x"][0]], axis=0)
    n = xc0.shape[0]
    target = p["loss_target"][0]

    hg3 = jnp.stack([p[k].reshape(-1) for k in ("hg_lb", "m_hg_lb", "v_hg_lb")])
    g_in, g_glu, g_out, g_up, g_down, g_cw, g_c, g_hg = _exchange(
        "gather_weights", [p["w_in"].astype(BF16), p["w_glu"].astype(BF16), p["w_out"].astype(BF16),
                           p["w_up"].astype(BF16), p["w_down"].astype(BF16), p["conv_w"], p["c"], hg3], False)
    w_in, w_up, conv_w = _gathered_cols(g_in), _gathered_cols(g_up), _gathered_cols(g_cw)
    w_glu, w_out, w_down = _gathered_rows(g_glu), _gathered_rows(g_out), _gathered_rows(g_down)
    hg_full = jnp.moveaxis(g_hg.reshape(N_DEV, 3, nl, 2, -1), 0, 3).reshape(3, nl, 2 * HGW)
    lb_all = lb_call(hg_full[0])

    craw = jnp.concatenate([g_c.reshape(N_DEV, D), jnp.broadcast_to(p["c_ctx"][None], (8, D))], axis=0)
    cols = p["w_mod"].shape[2]
    b_cols = lax.dynamic_slice_in_dim(p["b_mod"], me * cols, cols, axis=1)[:, None, :]
    (g_mod,) = _exchange("gather_mod", [mod_fwd(craw, p["w_mod"], b_cols)], False)
    mod_all = jnp.moveaxis(g_mod, 0, 2).reshape(nl, 16, 6 * D)
    mod_x = lax.dynamic_index_in_dim(mod_all, me, axis=1, keepdims=False)
    mod2 = jnp.stack([mod_all[:, 8], mod_x], axis=1)
    mvec = lambda l, k: mod2[l, :, k * D:(k + 1) * D][:, None, :]
    zvec = jnp.zeros((2, 1, D), F32)
    row = lambda a: a.reshape(1, 1, -1)

    to_hp = lambda a: jnp.moveaxis(a, -1, 2).reshape(nl * 2, S5H, S5P)
    prep_in = [p["s5_lam_re"].reshape(nl * 2, 1, S5P), p["s5_lam_im"].reshape(nl * 2, 1, S5P),
               p["s5_log_dt"].reshape(nl * 2, 32, 1), to_hp(p["s5_b_re"]), to_hp(p["s5_b_im"]),
               jnp.swapaxes(p["s5_c_re"], 2, 3).reshape(nl * 2, S5H, S5P),
               jnp.swapaxes(p["s5_c_im"], 2, 3).reshape(nl * 2, S5H, S5P)]
    s5a, s5b, s5c = s5_prep(*prep_in)

    T1 = lambda a, w=D, cb=_c0: (a, w, cb)
    saved = []
    xc = xc0
    (h,) = block_fwd("mod0", f_mod, n, [T1(xc)], [T1(mvec(0, 0)), T1(mvec(0, 1))], [(D, BF16, D, _c0)])
    for l in range(nl):
        proj = dense_nn(f"in_proj{l}", h, w_in[l])
        s5 = [s5_fwd(f"s5_fwd{l}_{d}", proj, s5a, s5b, s5c, 2 * l + d, d == 1) for d in range(2)]
        lbs = [lb_all[l, d * HGW:(d + 1) * HGW][None] for d in range(2)]
        gl = [gla_fwd(f"gla_fwd{l}_{d}", proj, lbs[d], d == 1) for d in range(2)]
        mix_t = [T1(proj, S5W), T1(proj, HGW, lambda j: 5), T1(s5[0][2], S5W), T1(s5[1][2], S5W),
                 T1(gl[0][0], HGW), T1(gl[1][0], HGW)]
        mix_p = [T1(row(p["s5_d"][l]), S5W), T1(w_glu[l][None], S5W), T1(row(p["b_glu"][l]), S5W),
                 T1(row(p["hg_norm_w"][l]), HD)]
        (y,) = block_fwd(f"mix{l}", f_mix, n, mix_t, mix_p, [(D, BF16, D, _c0)])
        z = dense_nn(f"out_proj{l}", y, w_out[l])
        ln1_p = [T1(mvec(l, 2)), T1(row(p["ln1_g"][l])), T1(row(p["ln1_b"][l])), T1(mvec(l, 3)), T1(mvec(l, 4))]
        x1, h2 = block_fwd(f"ln1_{l}", f_ln, n, [T1(xc), T1(z)], ln1_p, [(D, F32, D, _c0), (D, BF16, D, _c0)])
        up = dense_nn(f"up_proj{l}", h2, w_up[l])
        ct = DFF // 2
        act_t = [T1(up, ct, lambda j: j), T1(up, ct, lambda j: j + 2)]
        cb2 = p["conv_b"][l].reshape(1, 1, -1)
        act_p = [T1(conv_w[l][None, :, :DFF], ct, lambda j: j), T1(conv_w[l][None, :, DFF:], ct, lambda j: j),
                 T1(cb2[:, :, :DFF], ct, lambda j: j), T1(cb2[:, :, DFF:], ct, lambda j: j)]
        (act,) = block_fwd(f"act{l}", f_act, n, act_t, act_p, [(DFF, BF16, ct, lambda j: j)], n_col=2)
        dn = dense_nn(f"down_proj{l}", act, w_down[l])
        nxt = (mvec(l + 1, 0), mvec(l + 1, 1)) if l + 1 < nl else (zvec, zvec)
        ln2_p = [T1(mvec(l, 5)), T1(row(p["ln2_g"][l])), T1(row(p["ln2_b"][l])), T1(nxt[0]), T1(nxt[1])]
        x2, hn = block_fwd(f"ln2_{l}", f_ln, n, [T1(x1), T1(dn)], ln2_p, [(D, F32, D, _c0), (D, BF16, D, _c0)])
        saved.append(dict(xc=xc, h=h, proj=proj, s5=s5, gl=gl, lbs=lbs, mix_t=mix_t, mix_p=mix_p, y=y, z=z,
                          ln1_p=ln1_p, x1=x1, h2=h2, act_t=act_t, act_p=act_p, act=act, dn=dn, ln2_p=ln2_p))
        xc, h = x2, hn

    dxc, loss_part = loss_and_grad(xc, target)
    loss = lax.psum(loss_part[0, 0], AXES)

    g = {k: [None] * nl for k in ("w_in", "w_glu", "w_out", "w_up", "w_down", "conv_w", "conv_b", "s5_d", "b_glu",
                                  "hg_norm_w", "ln1_g", "ln1_b", "ln2_g", "ln2_b", "dlb", "s5")}
    dmod = [[None] * 6 for _ in range(nl)]
    dh_next = jnp.zeros((n, D), F32)
    fgrad = (D, F32, D, _c0)
    for l in reversed(range(nl)):
        sv = saved[l]
        (dx1, d_dn), dp = block_bwd(f"ln2_bwd{l}", f_ln, n, [T1(sv["x1"]), T1(sv["dn"])], sv["ln2_p"],
                                    [T1(dxc), T1(dh_next)], [fgrad, fgrad])
        dmod[l][5], g["ln2_g"][l], g["ln2_b"][l] = dp[0], dp[1], dp[2]
        if l + 1 < nl:
            dmod[l + 1][0], dmod[l + 1][1] = dp[3], dp[4]
        dact = dense_nt(f"down_bwd{l}", d_dn, w_down[l])
        g["w_down"][l] = dense_tn(f"down_wgrad{l}", sv["act"], d_dn)
        ct = DFF // 2
        cj = lambda j: j
        (dua, dug), dp = block_bwd(f"act_bwd{l}", f_act, n, sv["act_t"], sv["act_p"], [T1(dact, ct, cj)],
                                   [(DFF, BF16, ct, cj), (DFF, BF16, ct, cj)], n_col=2)
        g["conv_w"][l] = jnp.concatenate([dp[0][0], dp[1][0]], axis=-1)
        g["conv_b"][l] = jnp.concatenate([dp[2][0, 0], dp[3][0, 0]], axis=-1)
        dup = jnp.concatenate([dua, dug], axis=-1)
        dh2 = dense_nt(f"up_bwd{l}", dup, w_up[l])
        g["w_up"][l] = dense_tn(f"up_wgrad{l}", sv["h2"], dup)
        (dxc, dz), dp = block_bwd(f"ln1_bwd{l}", f_ln, n, [T1(sv["xc"]), T1(sv["z"])], sv["ln1_p"],
                                  [T1(dx1), T1(dh2)], [fgrad, fgrad])
        dmod[l][2], g["ln1_g"][l], g["ln1_b"][l], dmod[l][3], dmod[l][4] = dp
        dy = dense_nt(f"out_bwd{l}", dz, w_out[l])
        g["w_out"][l] = dense_tn(f"out_wgrad{l}", sv["y"], dz)
        half = (S5W, F32, S5W, _c0)
        (dpu, dpg, dys, dos), dp = block_bwd(f"mix_bwd{l}", f_mix, n, sv["mix_t"], sv["mix_p"], [T1(dy)],
                                                   [half, half, half, None, half, None])
        g["s5_d"][l], g["w_glu"][l], g["b_glu"][l], g["hg_norm_w"][l] = dp[0][0, 0], dp[1][0], dp[2][0, 0], dp[3][0, 0]
        gb = [gla_bwd(f"gla_bwd{l}_{d}", sv["proj"], sv["lbs"][d], sv["gl"][d][1], dos, d == 1) for d in range(2)]
        g["dlb"][l] = jnp.concatenate([gb[0][3], gb[1][3]], axis=-1)[0]
        sb = [s5_bwd(f"s5_bwd{l}_{d}", sv["proj"], dys, sv["s5"][d][0], sv["s5"][d][1], s5a, s5b, s5c, 2 * l + d, d == 1)
              for d in range(2)]
        g["s5"][l] = sb
        asm_t = [T1(dpu, S5W), T1(sb[0][0], S5W), T1(sb[1][0], S5W), T1(gb[0][0], HGW), T1(gb[1][0], HGW),
                 T1(gb[0][1], HGW), T1(gb[1][1], HGW), T1(gb[0][2], HGW), T1(gb[1][2], HGW), T1(dpg, HGW)]
        (dproj,) = block_fwd(
            f"dproj{l}", lambda tv, pv, i: (jnp.concatenate(
                [tv[0] + tv[1] + tv[2], tv[3], tv[4], tv[5] + tv[6], tv[7] + tv[8], tv[9]], axis=-1),),
            n, asm_t, [], [(INC, BF16, INC, _c0)])
        dh_next = dense_nt(f"in_bwd{l}", dproj, w_in[l])
        g["w_in"][l] = dense_tn(f"in_wgrad{l}", sv["h"], dproj)
    (dxc,), dp = block_bwd("mod0_bwd", f_mod_id, n, [T1(xc0)], [T1(mvec(0, 0)), T1(mvec(0, 1))],
                           [T1(dh_next), T1(dxc)], [fgrad])
    dmod[0][0], dmod[0][1] = dp
    grad_x = dxc[n - p["x"].shape[1]:][None]

    st = lambda k: jnp.stack([g["s5"][l][d][k] for l in range(nl) for d in range(2)])
    d_prep = s5_prep_bwd(*prep_in, st(1), st(2), st(3))
    from_hp = lambda a: jnp.moveaxis(a.reshape(nl, 2, S5H, S5W // S5H, 64), 2, -1)
    gs5 = {"s5_lam_re": d_prep[0].reshape(nl, 2, 32, 64), "s5_lam_im": d_prep[1].reshape(nl, 2, 32, 64),
           "s5_log_dt": d_prep[2].reshape(nl, 2, 32), "s5_b_re": from_hp(d_prep[3]), "s5_b_im": from_hp(d_prep[4]),
           "s5_c_re": jnp.swapaxes(d_prep[5].reshape(nl, 2, S5H, 32, 64), 2, 3),
           "s5_c_im": jnp.swapaxes(d_prep[6].reshape(nl, 2, S5H, 32, 64), 2, 3)}
    d_hg = lb_call(hg_full[0], jnp.stack(g["dlb"]))

    dmod_loc = jnp.stack([jnp.concatenate([dmod[l][k][:, 0] for k in range(6)], axis=-1) for l in range(nl)])
    (g_dmod,) = _exchange("gather_dmod", [dmod_loc], False)
    gcols = lax.dynamic_slice_in_dim(g_dmod, me * cols, cols, axis=3)
    g16 = jnp.concatenate([jnp.moveaxis(gcols[:, :, 1], 0, 1), jnp.moveaxis(gcols[:, :, 0], 0, 1)], axis=1)
    grad_w_mod, dcraw = mod_bwd(craw, p["w_mod"], g16)
    d_c_ctx = jnp.sum(dcraw[8:], axis=0)

    stk = lambda k: jnp.stack(g[k])
    by_cols = lambda a: jnp.moveaxis(a.reshape(a.shape[0], a.shape[1], N_DEV, -1), 2, 0)
    by_rows = lambda a: jnp.moveaxis(a.reshape(a.shape[0], N_DEV, -1, a.shape[2]), 1, 0)
    sends = [by_cols(stk("w_in")), by_rows(stk("w_glu").astype(BF16)), by_rows(stk("w_out")), by_cols(stk("w_up")),
             by_rows(stk("w_down")), by_cols(stk("conv_w"))]
    recv = _exchange("scatter_grads", sends, True)
    small_g = {"c_ctx": d_c_ctx, "b_mod": dmod_loc[:, 0] + dmod_loc[:, 1], "s5_d": stk("s5_d"), "b_glu": stk("b_glu"),
               "hg_lb": d_hg.reshape(nl, 2, HGW), "hg_norm_w": stk("hg_norm_w"), "ln1_g": stk("ln1_g")[:, 0, 0],
               "ln1_b": stk("ln1_b")[:, 0, 0], "conv_b": stk("conv_b"), "ln2_g": stk("ln2_g")[:, 0, 0],
               "ln2_b": stk("ln2_b")[:, 0, 0], **gs5}
    g_pack = _pack([small_g[k] for k in SMALL])
    (g_parts,) = _exchange("scatter_small_grads", [g_pack.reshape(N_DEV, -1, PACK_W)], True)
    (g_small,) = _exchange("gather_small_grads", [sum_parts(g_parts)], False)
    g_small = g_small.reshape(1, -1, PACK_W)

    out = {}
    hgw = {"": hg_full[0].reshape(nl, 2, HGW), "m_": hg_full[1].reshape(nl, 2, HGW), "v_": hg_full[2].reshape(nl, 2, HGW)}
    full = lambda pre, k: hgw[pre] if k == "hg_lb" else p[pre + k]
    shapes = [full("", k).shape for k in SMALL]
    res = adamw("adamw_small", *[_pack([full(pre, k) for k in SMALL]) for pre in ("", "m_", "v_")], g_small)
    for kind, packed in zip(("grad_", "delta_", "new_m_", "new_v_"), res):
        for k, a in zip(SMALL, _unpack(packed, shapes)):
            if k == "hg_lb":
                a = lax.dynamic_slice_in_dim(a, me * (HGW // N_DEV), HGW // N_DEV, axis=2)
            out[kind + k] = a
    big = dict(zip(BIG + ["conv_w"], recv))
    big["w_mod"] = grad_w_mod[None]
    for k, gsum in big.items():
        shp = p[k].shape
        v2 = lambda a: a.reshape(-1, shp[-1])
        res = adamw("adamw_" + k, v2(p[k]), v2(p["m_" + k]), v2(p["v_" + k]), gsum.reshape(gsum.shape[0], -1, shp[-1]))
        for kind, a in zip(("grad_", "delta_", "new_m_", "new_v_"), res):
            out[kind + k] = a.reshape(shp)
    return (loss, grad_x, *[out[kind + k] for kind in ("grad_", "delta_", "new_m_", "new_v_") for k in WEIGHTS])


def kernel(x, c, ctx, c_ctx, w_mod, b_mod, w_in, s5_lam_re, s5_lam_im, s5_log_dt, s5_b_re, s5_b_im, s5_c_re, s5_c_im, s5_d, w_glu, b_glu, hg_lb, hg_norm_w, w_out, ln1_g, ln1_b, w_up, conv_w, conv_b, w_down, ln2_g, ln2_b, loss_target, m_c_ctx, m_w_mod, m_b_mod, m_w_in, m_s5_lam_re, m_s5_lam_im, m_s5_log_dt, m_s5_b_re, m_s5_b_im, m_s5_c_re, m_s5_c_im, m_s5_d, m_w_glu, m_b_glu, m_hg_lb, m_hg_norm_w, m_w_out, m_ln1_g, m_ln1_b, m_w_up, m_conv_w, m_conv_b, m_w_down, m_ln2_g, m_ln2_b, v_c_ctx, v_w_mod, v_b_mod, v_w_in, v_s5_lam_re, v_s5_lam_im, v_s5_log_dt, v_s5_b_re, v_s5_b_im, v_s5_c_re, v_s5_c_im, v_s5_d, v_w_glu, v_b_glu, v_hg_lb, v_hg_norm_w, v_w_out, v_ln1_g, v_ln1_b, v_w_up, v_conv_w, v_conv_b, v_w_down, v_ln2_g, v_ln2_b):
    return _step(dict(locals()))
```

```python
import functools
import math

import jax
import jax.numpy as jnp
from jax import lax
from jax.experimental import pallas as pl
from jax.experimental.pallas import tpu as pltpu

F32, BF16 = jnp.float32, jnp.bfloat16
N_DEV = 8
AXES = ("x", "y", "c")
D = 1024
S5W = 512
S5P = 2048
S5H = 16
HGW = 512
HD = 128
NH = 4
CK = 32
DFF = 2816
GRID_W = 64
INC = 3072
ALPHA = 8.0 ** 0.25
LN_EPS = 1e-5
RMS_EPS = 1e-6
LR, B1, B2, EPS, WD, STEP = 0.001, 0.9, 0.999, 1e-08, 0.01, 10
TT = 256
VMEM_MB = 56

NN = ((1,), (0,))
NT = ((1,), (1,))
TN = ((0,), (0,))


def _cparams(n_axes):
    return pltpu.CompilerParams(dimension_semantics=("arbitrary",) * n_axes, vmem_limit_bytes=VMEM_MB << 20)


def _dot(a, b, dims):
    return lax.dot_general(a.astype(BF16), b.astype(BF16), (dims, ((), ())), preferred_element_type=F32)


@jax.custom_vjp
def mm_nn(a, b):
    return _dot(a, b, NN)


@jax.custom_vjp
def mm_nt(a, b):
    return _dot(a, b, NT)


@jax.custom_vjp
def mm_tn(a, b):
    return _dot(a, b, TN)


mm_nn.defvjp(lambda a, b: (_dot(a, b, NN), (a, b)), lambda r, g: (_dot(g, r[1], NT), _dot(r[0], g, TN)))
mm_nt.defvjp(lambda a, b: (_dot(a, b, NT), (a, b)), lambda r, g: (_dot(g, r[1], NN), _dot(g, r[0], TN)))
mm_tn.defvjp(lambda a, b: (_dot(a, b, TN), (a, b)), lambda r, g: (_dot(r[1], g, NT), _dot(r[0], g, NN)))


def _roll_rows(u, s):
    return pltpu.roll(u, s % u.shape[0], 0)


@functools.partial(jax.custom_vjp, nondiff_argnums=(2,))
def shift_rows(u, m, s):
    return _roll_rows(u, s) * m


def _shift_fwd(u, m, s):
    return _roll_rows(u, s) * m, m


def _shift_bwd(s, m, g):
    return _roll_rows(g * m, -s), jnp.zeros_like(m)


shift_rows.defvjp(_shift_fwd, _shift_bwd)


def _scan_tile(pos, nt, rev):
    return jnp.where(pos == 0, 0, nt - pos) if rev else pos


def _exchange(name, arrays, all_to_all):
    k_arr = len(arrays)

    def body(*refs):
        ins, outs = refs[:k_arr], refs[k_arr:2 * k_arr]
        send_sems, recv_sems, local_sems = refs[2 * k_arr:]
        me = lax.axis_index("x") * 4 + lax.axis_index("y") * 2 + lax.axis_index("c")
        local = []
        for k in range(k_arr):
            cp = pltpu.make_async_copy(ins[k].at[me] if all_to_all else ins[k], outs[k].at[me], local_sems.at[k])
            cp.start()
            local.append(cp)
        sends = []
        for d in range(1, N_DEV):
            p = (me + d) % N_DEV
            for k in range(k_arr):
                cp = pltpu.make_async_remote_copy(
                    src_ref=ins[k].at[p] if all_to_all else ins[k], dst_ref=outs[k].at[me],
                    send_sem=send_sems.at[k, d - 1], recv_sem=recv_sems.at[k, d - 1],
                    device_id=(p // 4, (p // 2) % 2, p % 2), device_id_type=pl.DeviceIdType.MESH)
                cp.start()
                sends.append(cp)
        for d in range(1, N_DEV):
            q = (me + N_DEV - d) % N_DEV
            for k in range(k_arr):
                pltpu.make_async_remote_copy(
                    src_ref=ins[k].at[q] if all_to_all else ins[k], dst_ref=outs[k].at[q],
                    send_sem=send_sems.at[k, d - 1], recv_sem=recv_sems.at[k, d - 1],
                    device_id=(q // 4, (q // 2) % 2, q % 2), device_id_type=pl.DeviceIdType.MESH).wait_recv()
        for cp in sends:
            cp.wait_send()
        for cp in local:
            cp.wait()

    shapes = [a.shape if all_to_all else (N_DEV,) + a.shape for a in arrays]
    return pl.pallas_call(
        body, name=name,
        out_shape=[jax.ShapeDtypeStruct(s, a.dtype) for s, a in zip(shapes, arrays)],
        in_specs=[pl.BlockSpec(memory_space=pl.ANY)] * k_arr,
        out_specs=[pl.BlockSpec(memory_space=pl.ANY)] * k_arr,
        scratch_shapes=[pltpu.SemaphoreType.DMA((k_arr, N_DEV - 1)), pltpu.SemaphoreType.DMA((k_arr, N_DEV - 1)),
                        pltpu.SemaphoreType.DMA((k_arr,))],
    )(*arrays)


_HBM = pl.BlockSpec(memory_space=pltpu.HBM)
_SEM = pl.BlockSpec(memory_space=pltpu.SEMAPHORE)
_EFFECT = pltpu.SideEffectType.DATAFLOW_SIDE_EFFECTING


def _peer(i):
    return (i // 4, (i // 2) % 2, i % 2)


def exchange_start(name, srcs, lands, all_to_all):
    k_arr = len(srcs)
    n_sem = k_arr * (N_DEV - 1)

    def body(*refs):
        ins, lz = refs[:k_arr], refs[k_arr:2 * k_arr]
        send_sems = refs[2 * k_arr:2 * k_arr + n_sem]
        recv_sems = refs[2 * k_arr + n_sem:2 * k_arr + 2 * n_sem]
        me = lax.axis_index("x") * 4 + lax.axis_index("y") * 2 + lax.axis_index("c")
        for d in range(1, N_DEV):
            p = (me + d) % N_DEV
            for k in range(k_arr):
                s = k * (N_DEV - 1) + d - 1
                pltpu.make_async_remote_copy(
                    src_ref=ins[k].at[p] if all_to_all else ins[k], dst_ref=lz[k].at[me],
                    send_sem=send_sems[s], recv_sem=recv_sems[s],
                    device_id=_peer(p), device_id_type=pl.DeviceIdType.MESH).start()
        refs[-1][...] = jnp.zeros_like(refs[-1])

    arrs = list(srcs) + list(lands)
    res = pl.pallas_call(
        body, name=name,
        out_shape=(*[pltpu.SemaphoreType.DMA(())] * (2 * n_sem), *[pltpu.HBM(a.shape, a.dtype) for a in arrs],
                   jax.ShapeDtypeStruct((8, 128), F32)),
        in_specs=[_HBM] * len(arrs),
        out_specs=(*[_SEM] * (2 * n_sem), *[_HBM] * len(arrs), pl.BlockSpec(memory_space=pltpu.VMEM)),
        input_output_aliases={i: 2 * n_sem + i for i in range(len(arrs))},
        compiler_params=pltpu.CompilerParams(has_side_effects=_EFFECT),
    )(*[pltpu.with_memory_space_constraint(a, pltpu.HBM) for a in arrs])
    return res[:-1], res[-1]


def exchange_wait(name, handle, after, all_to_all):
    k_arr = len(handle) // (2 * N_DEV)
    n_sem = k_arr * (N_DEV - 1)
    sems, arrs = handle[:2 * n_sem], handle[2 * n_sem:]

    def body(*refs):
        ins, lz = refs[:k_arr], refs[k_arr:2 * k_arr]
        s_sems = refs[2 * k_arr:2 * k_arr + n_sem]
        r_sems = refs[2 * k_arr + n_sem:2 * k_arr + 2 * n_sem]
        me = lax.axis_index("x") * 4 + lax.axis_index("y") * 2 + lax.axis_index("c")
        for d in range(1, N_DEV):
            q = (me + N_DEV - d) % N_DEV
            for k in range(k_arr):
                s = k * (N_DEV - 1) + d - 1
                cp = pltpu.make_async_remote_copy(
                    src_ref=ins[k].at[q] if all_to_all else ins[k], dst_ref=lz[k].at[q],
                    send_sem=s_sems[s], recv_sem=r_sems[s],
                    device_id=_peer(q), device_id_type=pl.DeviceIdType.MESH)
                cp.wait_send()
                cp.wait_recv()

    res = pl.pallas_call(
        body, name=name, out_shape=tuple(pltpu.HBM(a.shape, a.dtype) for a in arrs),
        in_specs=[_HBM] * len(arrs) + [_SEM] * (2 * n_sem) + [pl.BlockSpec(memory_space=pl.ANY)],
        out_specs=tuple([_HBM] * len(arrs)),
        input_output_aliases={i: i for i in range(len(arrs))},
        compiler_params=pltpu.CompilerParams(has_side_effects=_EFFECT),
    )(*arrs, *sems, after)
    return res[k_arr:]


def _own_block_set(src, all_to_all):
    me = lax.axis_index("x") * 4 + lax.axis_index("y") * 2 + lax.axis_index("c")
    own = lax.dynamic_index_in_dim(src, me, 0, keepdims=False) if all_to_all else src
    shape = src.shape if all_to_all else (N_DEV,) + src.shape
    return lax.dynamic_update_index_in_dim(lax.empty(shape, src.dtype), own, me, 0)


def _tile(n, prefs):
    for t in prefs:
        if n % t == 0:
            return t
    raise ValueError(n)


def dense_nn(name, a, w, out_dtype=F32):
    n, k = a.shape
    m = w.shape[1]
    tn, tm = _tile(n, (1088, 256)), _tile(m, (1024, 512, 256, 128))

    def body(a_ref, w_ref, o_ref):
        o_ref[...] = _dot(a_ref[...], w_ref[...], NN).astype(o_ref.dtype)

    return pl.pallas_call(
        body, name=name, grid=(m // tm, n // tn), out_shape=jax.ShapeDtypeStruct((n, m), out_dtype),
        in_specs=[pl.BlockSpec((tn, k), lambda j, i: (i, 0)), pl.BlockSpec((k, tm), lambda j, i: (0, j))],
        out_specs=pl.BlockSpec((tn, tm), lambda j, i: (i, j)), compiler_params=_cparams(2))(a, w)


def dense_nt(name, g, w, out_dtype=F32):
    n, m = g.shape
    k = w.shape[0]
    tn, tk = _tile(n, (544, 256)), _tile(k, (1024, 1408, 512, 256, 128))

    def body(g_ref, w_ref, o_ref):
        o_ref[...] = _dot(g_ref[...], w_ref[...], NT).astype(o_ref.dtype)

    return pl.pallas_call(
        body, name=name, grid=(k // tk, n // tn), out_shape=jax.ShapeDtypeStruct((n, k), out_dtype),
        in_specs=[pl.BlockSpec((tn, m), lambda j, i: (i, 0)), pl.BlockSpec((tk, m), lambda j, i: (j, 0))],
        out_specs=pl.BlockSpec((tn, tk), lambda j, i: (i, j)), compiler_params=_cparams(2))(g, w)


def dense_tn(name, a, g, out_dtype=BF16):
    n, k = a.shape
    m = g.shape[1]
    tn = _tile(n, (1088, 256))
    tk = _tile(k, (1024, 1408, 512, 256, 128))
    tm = _tile(m, (1024, 512, 256, 128))
    nt = n // tn

    def body(a_ref, g_ref, o_ref, acc_ref):
        t = pl.program_id(2)

        @pl.when(t == 0)
        def _():
            acc_ref[...] = jnp.zeros_like(acc_ref)

        acc_ref[...] += _dot(a_ref[...], g_ref[...], TN)

        @pl.when(t == nt - 1)
        def _():
            o_ref[...] = acc_ref[...].astype(o_ref.dtype)

    return pl.pallas_call(
        body, name=name, grid=(k // tk, m // tm, nt), out_shape=jax.ShapeDtypeStruct((k, m), out_dtype),
        in_specs=[pl.BlockSpec((tn, tk), lambda i, j, t: (t, i)), pl.BlockSpec((tn, tm), lambda i, j, t: (t, j))],
        out_specs=pl.BlockSpec((tk, tm), lambda i, j, t: (i, j)),
        scratch_shapes=[pltpu.VMEM((tk, tm), F32)], compiler_params=_cparams(3))(a, g)


def _c0(j):
    return 0


def _tspec(w, cb):
    return pl.BlockSpec((TT, w), lambda j, i: (i, cb(j)))


def _pspec(arr, w, cb):
    two = arr.shape[0] == 2
    return pl.BlockSpec((None, arr.shape[1], w), lambda j, i: (jnp.minimum(i, 1) if two else 0, 0, cb(j)))


def block_fwd(name, fn, n, tiled, params, outs, n_col=1):
    nt_, np_ = len(tiled), len(params)

    def body(*refs):
        i = pl.program_id(1)
        tv = [r[...].astype(F32) for r in refs[:nt_]]
        pv = [r[...].astype(F32) for r in refs[nt_:nt_ + np_]]
        for o_ref, r in zip(refs[nt_ + np_:], fn(tv, pv, i)):
            o_ref[...] = r.astype(o_ref.dtype)

    return pl.pallas_call(
        body, name=name, grid=(n_col, n // TT),
        out_shape=[jax.ShapeDtypeStruct((n, c), dt) for c, dt, _, _ in outs],
        in_specs=[_tspec(w, cb) for _, w, cb in tiled] + [_pspec(a, w, cb) for a, w, cb in params],
        out_specs=[_tspec(w, cb) for _, _, w, cb in outs], compiler_params=_cparams(2),
    )(*[a for a, _, _ in tiled], *[a for a, _, _ in params])


def block_bwd(name, fn, n, tiled, params, cots, grads, n_col=1):
    nt_, np_, nc_ = len(tiled), len(params), len(cots)
    want = [k for k, g in enumerate(grads) if g is not None]

    def body(*refs):
        i = pl.program_id(1)
        tv = [r[...].astype(F32) for r in refs[:nt_]]
        pv = [r[...].astype(F32) for r in refs[nt_:nt_ + np_]]
        cv = [r[...].astype(F32) for r in refs[nt_ + np_:nt_ + np_ + nc_]]
        o_refs = refs[nt_ + np_ + nc_:]
        _, vjp = jax.vjp(lambda t, p: list(fn(t, p, i)), tv, pv)
        dt, dp = vjp(cv)
        for o_ref, k in zip(o_refs, want):
            o_ref[...] = dt[k].astype(o_ref.dtype)
        for o_ref, g, (arr, _, _) in zip(o_refs[len(want):], dp, params):
            first = (i == 0) | (i == 1) if arr.shape[0] == 2 else i == 0

            @pl.when(first)
            def _(o_ref=o_ref):
                o_ref[...] = jnp.zeros_like(o_ref)

            o_ref[...] += g

    res = pl.pallas_call(
        body, name=name, grid=(n_col, n // TT),
        out_shape=[jax.ShapeDtypeStruct((n, grads[k][0]), grads[k][1]) for k in want]
        + [jax.ShapeDtypeStruct(a.shape, F32) for a, _, _ in params],
        in_specs=[_tspec(w, cb) for _, w, cb in tiled] + [_pspec(a, w, cb) for a, w, cb in params]
        + [_tspec(w, cb) for _, w, cb in cots],
        out_specs=[_tspec(grads[k][2], grads[k][3]) for k in want] + [_pspec(a, w, cb) for a, w, cb in params],
        compiler_params=_cparams(2),
    )(*[a for a, _, _ in tiled], *[a for a, _, _ in params], *[a for a, _, _ in cots])
    return res[:len(want)], res[len(want):]


def f_mod(tv, pv, i):
    (x,), (sh, sc) = tv, pv
    return (x * (1.0 + sc) + sh,)


def f_mod_id(tv, pv, i):
    return (f_mod(tv, pv, i)[0], tv[0])


def f_ln(tv, pv, i):
    (x, z), (gate, g, b, sh, sc) = tv, pv
    pre = ALPHA * x + gate * z
    mu = jnp.mean(pre, axis=-1, keepdims=True)
    var = jnp.mean(jnp.square(pre - mu), axis=-1, keepdims=True)
    xn = (pre - mu) * lax.rsqrt(var + LN_EPS) * g + b
    return xn, xn * (1.0 + sc) + sh


def f_mix(tv, pv, i):
    (pu, pg, y0, y1, o0, o1), (d_skip, w_glu, b_glu, norm_w) = tv, pv
    s5y = jax.nn.gelu(y0 + y1 + pu * d_skip)
    s5o = s5y * jax.nn.sigmoid(mm_nn(s5y, w_glu) + b_glu)
    o = o0 + o1
    heads = []
    for h in range(NH):
        oh = o[:, h * HD:(h + 1) * HD]
        heads.append(oh * lax.rsqrt(jnp.mean(jnp.square(oh), axis=-1, keepdims=True) + RMS_EPS) * norm_w)
    hg = jnp.concatenate(heads, axis=-1) * jax.nn.silu(pg)
    return (jnp.concatenate([s5o, hg], axis=-1),)


def f_act(tv, pv, i):
    (ua, ug), (cwa, cwg, cba, cbg) = tv, pv
    t = lax.broadcasted_iota(jnp.int32, (TT, 1), 0)
    lat = i > 0
    m_dn = jnp.where((t == 0) | (lat & (t % GRID_W == 0)), 0.0, 1.0)
    m_up = jnp.where((t == TT - 1) | (lat & (t % GRID_W == GRID_W - 1)), 0.0, 1.0)

    def conv(u, w, b):
        return shift_rows(u, m_dn, 1) * w[0:1] + u * w[1:2] + shift_rows(u, m_up, -1) * w[2:3] + b

    return (jax.nn.silu(conv(ua, cwa, cba)) * conv(ug, cwg, cbg),)


def loss_and_grad(xf, target):
    n = xf.shape[0]

    def body(x_ref, t_ref, dy_ref, l_ref):
        i = pl.program_id(0)

        @pl.when(i == 0)
        def _():
            l_ref[...] = jnp.zeros_like(l_ref)
            dy_ref[...] = jnp.zeros_like(dy_ref)

        @pl.when(i > 0)
        def _():
            e = x_ref[...] - t_ref[...]
            dy_ref[...] = e * (1.0 / D)
            l_ref[...] += 0.5 / D * jnp.sum(jnp.square(e))

    return pl.pallas_call(
        body, name="loss", grid=(n // TT,),
        out_shape=[jax.ShapeDtypeStruct((n, D), F32), jax.ShapeDtypeStruct((8, 128), F32)],
        in_specs=[pl.BlockSpec((TT, D), lambda i: (i, 0)), pl.BlockSpec((TT, D), lambda i: (jnp.maximum(i - 1, 0), 0))],
        out_specs=[pl.BlockSpec((TT, D), lambda i: (i, 0)), pl.BlockSpec((8, 128), lambda i: (0, 0))],
        compiler_params=_cparams(1))(xf, target)


def f_prep(lr, li, ldt, bre, bim, cre, cim):
    gi = lax.broadcasted_iota(jnp.int32, (S5W // S5H, S5P), 0)
    gc = lax.broadcasted_iota(jnp.int32, (S5W // S5H, S5P), 1) // 64
    dt = jnp.exp(jnp.sum(jnp.where(gi == gc, ldt, 0.0), axis=0, keepdims=True))
    mag, ang = jnp.exp(lr * dt), li * dt
    ar, ai = mag * jnp.cos(ang), mag * jnp.sin(ang)
    den = lr * lr + li * li
    nr, ni = ar - 1.0, ai
    cr = (nr * lr + ni * li) / den
    ci = (ni * lr - nr * li) / den
    bbr = cr * bre - ci * bim
    bbi = cr * bim + ci * bre
    rg = lax.broadcasted_iota(jnp.int32, (S5W, S5P), 0) // S5H
    cg = lax.broadcasted_iota(jnp.int32, (S5W, S5P), 1) // 64
    mask = (rg == cg).astype(F32)
    blk = lambda a: jnp.concatenate([a] * (S5W // S5H), axis=0) * mask
    return ar, ai, blk(bbr), blk(bbi), blk(cre), blk(-cim)


def s5_prep(lr, li, ldt, bre, bim, cre, cim):
    n2 = lr.shape[0]

    def body(lr_r, li_r, ldt_r, bre_r, bim_r, cre_r, cim_r, a_ref, b_ref, c_ref):
        ar, ai, bbr, bbi, cbr, cbi = f_prep(lr_r[...], li_r[...], ldt_r[...], bre_r[...], bim_r[...], cre_r[...], cim_r[...])
        a_ref[0], a_ref[1] = ar, ai
        b_ref[0], b_ref[1] = bbr.astype(BF16), bbi.astype(BF16)
        c_ref[0], c_ref[1] = cbr.astype(BF16), cbi.astype(BF16)

    sp = lambda r, c: pl.BlockSpec((None, r, c), lambda i: (i, 0, 0))
    sp4 = lambda r, c: pl.BlockSpec((None, 2, r, c), lambda i: (i, 0, 0, 0))
    return pl.pallas_call(
        body, name="s5_prep", grid=(n2,),
        out_shape=[jax.ShapeDtypeStruct((n2, 2, 1, S5P), F32), jax.ShapeDtypeStruct((n2, 2, S5W, S5P), BF16),
                   jax.ShapeDtypeStruct((n2, 2, S5W, S5P), BF16)],
        in_specs=[sp(1, S5P), sp(1, S5P), sp(32, 1), sp(S5H, S5P), sp(S5H, S5P), sp(S5H, S5P), sp(S5H, S5P)],
        out_specs=[sp4(1, S5P), sp4(S5W, S5P), sp4(S5W, S5P)], compiler_params=_cparams(1),
    )(lr, li, ldt, bre, bim, cre, cim)


def s5_prep_bwd(lr, li, ldt, bre, bim, cre, cim, da, db, dc):
    n2 = lr.shape[0]

    def body(lr_r, li_r, ldt_r, bre_r, bim_r, cre_r, cim_r, da_r, db_r, dc_r, *outs):
        args = [r[...] for r in (lr_r, li_r, ldt_r, bre_r, bim_r, cre_r, cim_r)]
        _, vjp = jax.vjp(f_prep, *args)
        for o_ref, g in zip(outs, vjp((da_r[0], da_r[1], db_r[0], db_r[1], dc_r[0], dc_r[1]))):
            o_ref[...] = g

    sp = lambda r, c: pl.BlockSpec((None, r, c), lambda i: (i, 0, 0))
    sp4 = lambda r, c: pl.BlockSpec((None, 2, r, c), lambda i: (i, 0, 0, 0))
    ins = [sp(1, S5P), sp(1, S5P), sp(32, 1), sp(S5H, S5P), sp(S5H, S5P), sp(S5H, S5P), sp(S5H, S5P)]
    return pl.pallas_call(
        body, name="s5_prep_bwd", grid=(n2,),
        out_shape=[jax.ShapeDtypeStruct(a.shape, F32) for a in (lr, li, ldt, bre, bim, cre, cim)],
        in_specs=ins + [sp4(1, S5P), sp4(S5W, S5P), sp4(S5W, S5P)], out_specs=ins, compiler_params=_cparams(1),
    )(lr, li, ldt, bre, bim, cre, cim, da, db, dc)


S5_DIAG = 2
_CU, _CP = S5W // S5_DIAG, S5P // S5_DIAG


def _bd_nn(u, w_ref, k):
    return jnp.concatenate([_dot(u[:, j * _CU:(j + 1) * _CU], w_ref[k, j * _CU:(j + 1) * _CU, j * _CP:(j + 1) * _CP], NN)
                            for j in range(S5_DIAG)], axis=1)


def _bd_nt(x, w_ref, k):
    return jnp.concatenate([_dot(x[:, j * _CP:(j + 1) * _CP], w_ref[k, j * _CU:(j + 1) * _CU, j * _CP:(j + 1) * _CP], NT)
                            for j in range(S5_DIAG)], axis=1)


def _bd_tn_acc(acc_ref, k, a, g):
    for j in range(S5_DIAG):
        acc_ref[k, j * _CU:(j + 1) * _CU, j * _CP:(j + 1) * _CP] += _dot(a[:, j * _CU:(j + 1) * _CU],
                                                                         g[:, j * _CP:(j + 1) * _CP], TN)


def s5_fwd(name, proj, a, bb, cb, ld, rev):
    n = proj.shape[0]
    nt = n // TT

    def body(u_ref, a_ref, b_ref, c_ref, xr_ref, xi_ref, y_ref, cr_ref, ci_ref):
        @pl.when(pl.program_id(0) == 0)
        def _():
            cr_ref[...] = jnp.zeros_like(cr_ref)
            ci_ref[...] = jnp.zeros_like(ci_ref)

        u = u_ref[...]
        xr_ref[...] = _bd_nn(u, b_ref, 0)
        xi_ref[...] = _bd_nn(u, b_ref, 1)
        ar, ai = a_ref[0], a_ref[1]

        def step(k, carry):
            cr, ci = carry
            t = TT - 1 - k if rev else k
            nr = ar * cr - ai * ci + xr_ref[pl.ds(t, 1), :]
            ni = ar * ci + ai * cr + xi_ref[pl.ds(t, 1), :]
            xr_ref[pl.ds(t, 1), :] = nr
            xi_ref[pl.ds(t, 1), :] = ni
            return nr, ni

        cr, ci = lax.fori_loop(0, TT, step, (cr_ref[...], ci_ref[...]), unroll=8)
        cr_ref[...] = cr
        ci_ref[...] = ci
        y_ref[...] = _bd_nt(xr_ref[...], c_ref, 0) + _bd_nt(xi_ref[...], c_ref, 1)

    tile = lambda w: pl.BlockSpec((TT, w), lambda s: (_scan_tile(s, nt, rev), 0))
    par = lambda r: pl.BlockSpec((None, 2, r, S5P), lambda s: (ld, 0, 0, 0))
    return pl.pallas_call(
        body, name=name, grid=(nt,),
        out_shape=[jax.ShapeDtypeStruct((n, S5P), F32), jax.ShapeDtypeStruct((n, S5P), F32),
                   jax.ShapeDtypeStruct((n, S5W), F32)],
        in_specs=[tile(S5W), par(1), par(S5W), par(S5W)], out_specs=[tile(S5P), tile(S5P), tile(S5W)],
        scratch_shapes=[pltpu.VMEM((1, S5P), F32), pltpu.VMEM((1, S5P), F32)], compiler_params=_cparams(1),
    )(proj, a, bb, cb)


def s5_bwd(name, proj, dy, xr, xi, a, bb, cb, ld, rev):
    n = proj.shape[0]
    nt = n // TT
    tb = TT // 8

    def tile_of(s):
        return _scan_tile(nt - 1 - s, nt, rev)

    def edge_of(s):
        pos = nt - 1 - s
        prev = _scan_tile(jnp.maximum(pos - 1, 0), nt, rev)
        return prev * tb if rev else jnp.maximum(pos * tb - 1, 0)

    def body(u_ref, dy_ref, xr_ref, xi_ref, er_ref, ei_ref, a_ref, b_ref, c_ref,
             du_ref, da_ref, db_ref, dc_ref, gr_ref, gi_ref, cr_ref, ci_ref):
        s = pl.program_id(0)

        @pl.when(s == 0)
        def _():
            cr_ref[...] = jnp.zeros_like(cr_ref)
            ci_ref[...] = jnp.zeros_like(ci_ref)
            da_ref[...] = jnp.zeros_like(da_ref)
            db_ref[...] = jnp.zeros_like(db_ref)
            dc_ref[...] = jnp.zeros_like(dc_ref)

        dyv, u = dy_ref[...], u_ref[...]
        xrv, xiv = xr_ref[...], xi_ref[...]
        gr_ref[...] = _bd_nn(dyv, c_ref, 0)
        gi_ref[...] = _bd_nn(dyv, c_ref, 1)
        _bd_tn_acc(dc_ref, 0, dyv, xrv)
        _bd_tn_acc(dc_ref, 1, dyv, xiv)
        ar, ai = a_ref[0], a_ref[1]

        def step(k, carry):
            cr, ci = carry
            t = k if rev else TT - 1 - k
            g_r = gr_ref[pl.ds(t, 1), :] + cr
            g_i = gi_ref[pl.ds(t, 1), :] + ci
            gr_ref[pl.ds(t, 1), :] = g_r
            gi_ref[pl.ds(t, 1), :] = g_i
            return ar * g_r + ai * g_i, ar * g_i - ai * g_r

        cr, ci = lax.fori_loop(0, TT, step, (cr_ref[...], ci_ref[...]), unroll=8)
        cr_ref[...] = cr
        ci_ref[...] = ci
        g_r, g_i = gr_ref[...], gi_ref[...]
        rows = lax.broadcasted_iota(jnp.int32, (TT, 1), 0)
        live = jnp.where(s == nt - 1, 0.0, 1.0)
        if rev:
            pr = jnp.where(rows == TT - 1, er_ref[0:1, :] * live, _roll_rows(xrv, -1))
            pi = jnp.where(rows == TT - 1, ei_ref[0:1, :] * live, _roll_rows(xiv, -1))
        else:
            pr = jnp.where(rows == 0, er_ref[7:8, :] * live, _roll_rows(xrv, 1))
            pi = jnp.where(rows == 0, ei_ref[7:8, :] * live, _roll_rows(xiv, 1))
        da_ref[0] += jnp.sum(g_r * pr + g_i * pi, axis=0, keepdims=True)
        da_ref[1] += jnp.sum(g_i * pr - g_r * pi, axis=0, keepdims=True)
        du_ref[...] = _bd_nt(g_r, b_ref, 0) + _bd_nt(g_i, b_ref, 1)
        _bd_tn_acc(db_ref, 0, u, g_r)
        _bd_tn_acc(db_ref, 1, u, g_i)

    tile = lambda w: pl.BlockSpec((TT, w), lambda s: (tile_of(s), 0))
    edge = pl.BlockSpec((8, S5P), lambda s: (edge_of(s), 0))
    par = lambda r: pl.BlockSpec((None, 2, r, S5P), lambda s: (ld, 0, 0, 0))
    acc = lambda r: pl.BlockSpec((2, r, S5P), lambda s: (0, 0, 0))
    return pl.pallas_call(
        body, name=name, grid=(nt,),
        out_shape=[jax.ShapeDtypeStruct((n, S5W), F32), jax.ShapeDtypeStruct((2, 1, S5P), F32),
                   jax.ShapeDtypeStruct((2, S5W, S5P), F32), jax.ShapeDtypeStruct((2, S5W, S5P), F32)],
        in_specs=[tile(S5W), tile(S5W), tile(S5P), tile(S5P), edge, edge, par(1), par(S5W), par(S5W)],
        out_specs=[tile(S5W), acc(1), acc(S5W), acc(S5W)],
        scratch_shapes=[pltpu.VMEM((TT, S5P), F32), pltpu.VMEM((TT, S5P), F32),
                        pltpu.VMEM((1, S5P), F32), pltpu.VMEM((1, S5P), F32)], compiler_params=_cparams(1),
    )(proj, dy, xr, xi, xr, xi, a, bb, cb)


def gla_tile(r, v, qr, lb, sts, rev):
    ncc = TT // CK
    f = lb + (1.0 - lb) * jax.nn.sigmoid(r)
    k, lf, q = 1.0 - f, jnp.log(f), jax.nn.silu(qr)
    rows = lax.broadcasted_iota(jnp.int32, (TT, 1), 0)
    pos = rows % CK
    b = lf
    for s in (1, 2, 4, 8, 16):
        m = ((pos < CK - s) if rev else (pos >= s)).astype(F32)
        b = b + shift_rows(b, m, -s if rev else s)
    etot = [jnp.sum(lf[c * CK:(c + 1) * CK], axis=0, keepdims=True) for c in range(ncc)]
    e = jnp.concatenate([jnp.broadcast_to(t, (CK, HGW)) for t in etot], axis=0)
    kd, qe, qa = k * jnp.exp(e - b), q * jnp.exp(b), q * jnp.exp(b - e)
    cm = [(rows // CK == c).astype(F32) for c in range(ncc)]
    r2 = lax.broadcasted_iota(jnp.int32, (TT, TT), 0)
    c2 = lax.broadcasted_iota(jnp.int32, (TT, TT), 1)
    amask = (r2 // CK == c2 // CK) & ((r2 <= c2) if rev else (r2 >= c2))
    outs, new_sts = [], []
    for h in range(NH):
        ln = slice(h * HD, (h + 1) * HD)
        kdh, qeh, vh = kd[:, ln], qe[:, ln], v[:, ln]
        att = jnp.where(amask, mm_nt(qa[:, ln], kdh), 0.0)
        ds = mm_tn(jnp.concatenate([kdh * cm[c] for c in range(ncc)], axis=1), vh)
        st, starts = sts[h], [None] * ncc
        for c in (reversed(range(ncc)) if rev else range(ncc)):
            starts[c] = st
            dec = jnp.transpose(jnp.broadcast_to(jnp.exp(etot[c][:, ln]), (HD, HD)))
            st = dec * st + ds[c * HD:(c + 1) * HD]
        new_sts.append(st)
        qex = jnp.concatenate([qeh * cm[c] for c in range(ncc)], axis=1)
        outs.append(mm_nn(att, vh) + mm_nn(qex, jnp.concatenate(starts, axis=0)))
    return jnp.concatenate(outs, axis=1), new_sts


def _gla_specs(n, rev, order):
    nt = n // TT
    fcol = 2 if rev else 1
    tile = lambda cbk: pl.BlockSpec((TT, HGW), lambda s: (order(s), cbk))
    return nt, [tile(fcol), tile(3), tile(4)], tile(0)


def gla_fwd(name, proj, lb, rev):
    n = proj.shape[0]
    nt, in_tiles, out_tile = _gla_specs(n, rev, lambda s: _scan_tile(s, n // TT, rev))

    def body(r_ref, v_ref, q_ref, lb_ref, o_ref, st_ref, s_ref):
        @pl.when(pl.program_id(0) == 0)
        def _():
            s_ref[...] = jnp.zeros_like(s_ref)

        sts = [s_ref[h] for h in range(NH)]
        for h in range(NH):
            st_ref[h] = sts[h]
        o, new = gla_tile(r_ref[...], v_ref[...], q_ref[...], lb_ref[...], sts, rev)
        o_ref[...] = o
        for h in range(NH):
            s_ref[h] = new[h]

    st_spec = pl.BlockSpec((None, NH, HD, HD), lambda s: (_scan_tile(s, nt, rev), 0, 0, 0))
    return pl.pallas_call(
        body, name=name, grid=(nt,),
        out_shape=[jax.ShapeDtypeStruct((n, HGW), F32), jax.ShapeDtypeStruct((nt, NH, HD, HD), F32)],
        in_specs=in_tiles + [pl.BlockSpec((1, HGW), lambda s: (0, 0))], out_specs=[out_tile, st_spec],
        scratch_shapes=[pltpu.VMEM((NH, HD, HD), F32)], compiler_params=_cparams(1),
    )(proj, proj, proj, lb)


def gla_bwd(name, proj, lb, st_all, do, rev):
    n = proj.shape[0]
    order = lambda s: _scan_tile(n // TT - 1 - s, n // TT, rev)
    nt, in_tiles, out_tile = _gla_specs(n, rev, order)

    def body(r_ref, v_ref, q_ref, lb_ref, st_ref, do_ref, dr_ref, dv_ref, dq_ref, dlb_ref, ds_ref):
        @pl.when(pl.program_id(0) == 0)
        def _():
            ds_ref[...] = jnp.zeros_like(ds_ref)
            dlb_ref[...] = jnp.zeros_like(dlb_ref)

        _, vjp = jax.vjp(functools.partial(gla_tile, rev=rev), r_ref[...], v_ref[...], q_ref[...], lb_ref[...],
                         [st_ref[h] for h in range(NH)])
        dr, dv, dq, dlb, dsts = vjp((do_ref[...], [ds_ref[h] for h in range(NH)]))
        dr_ref[...] = dr
        dv_ref[...] = dv
        dq_ref[...] = dq
        dlb_ref[...] += dlb
        for h in range(NH):
            ds_ref[h] = dsts[h]

    st_spec = pl.BlockSpec((None, NH, HD, HD), lambda s: (order(s), 0, 0, 0))
    row = pl.BlockSpec((1, HGW), lambda s: (0, 0))
    return pl.pallas_call(
        body, name=name, grid=(nt,),
        out_shape=[jax.ShapeDtypeStruct((n, HGW), F32)] * 3 + [jax.ShapeDtypeStruct((1, HGW), F32)],
        in_specs=in_tiles + [row, st_spec, out_tile], out_specs=[out_tile] * 3 + [row],
        scratch_shapes=[pltpu.VMEM((NH, HD, HD), F32)], compiler_params=_cparams(1),
    )(proj, proj, proj, lb, st_all, do)


def f_lb(rows):
    mx = functools.reduce(jnp.maximum, rows)
    ex = [jnp.exp(r - mx) for r in rows]
    tot = functools.reduce(jnp.add, ex)
    out, acc = [jnp.zeros_like(rows[0])], None
    for e in ex[1:]:
        acc = e / tot if acc is None else acc + e / tot
        out.append(acc)
    return out


def lb_call(hg, dlb=None):
    nl = hg.shape[0]

    def body(*refs):
        rows = [refs[0][l:l + 1, :] for l in range(nl)]
        if dlb is None:
            res = f_lb(rows)
        else:
            _, vjp = jax.vjp(f_lb, rows)
            (res,) = vjp([refs[1][l:l + 1, :] for l in range(nl)])
        for l in range(nl):
            refs[-1][l:l + 1, :] = res[l]

    args = (hg,) if dlb is None else (hg, dlb)
    return pl.pallas_call(body, name="lower_bounds" if dlb is None else "lower_bounds_bwd",
                          out_shape=jax.ShapeDtypeStruct(hg.shape, F32))(*args)


def mod_fwd(craw, w_mod, b_cols):
    nl, _, cols = w_mod.shape

    def body(c_ref, w_ref, b_ref, o_ref):
        o_ref[...] = _dot(jax.nn.silu(c_ref[...]), w_ref[...], NN) + b_ref[...]

    return pl.pallas_call(
        body, name="mod_fwd", grid=(nl,), out_shape=jax.ShapeDtypeStruct((nl, 16, cols), F32),
        in_specs=[pl.BlockSpec((16, D), lambda l: (0, 0)), pl.BlockSpec((None, D, cols), lambda l: (l, 0, 0)),
                  pl.BlockSpec((None, 1, cols), lambda l: (l, 0, 0))],
        out_specs=pl.BlockSpec((None, 16, cols), lambda l: (l, 0, 0)), compiler_params=_cparams(1))(craw, w_mod, b_cols)


def mod_bwd(craw, w_mod, g):
    nl, _, cols = w_mod.shape

    def body(c_ref, w_ref, g_ref, dw_ref, dc_ref, acc_ref):
        l = pl.program_id(0)

        @pl.when(l == 0)
        def _():
            acc_ref[...] = jnp.zeros_like(acc_ref)

        c = c_ref[...]
        s, vjp = jax.vjp(jax.nn.silu, c)
        dw_ref[...] = _dot(s, g_ref[...], TN)
        acc_ref[...] += _dot(g_ref[...], w_ref[...], NT)

        @pl.when(l == nl - 1)
        def _():
            dc_ref[...] = vjp(acc_ref[...])[0]

    return pl.pallas_call(
        body, name="mod_bwd", grid=(nl,),
        out_shape=[jax.ShapeDtypeStruct(w_mod.shape, F32), jax.ShapeDtypeStruct((16, D), F32)],
        in_specs=[pl.BlockSpec((16, D), lambda l: (0, 0)), pl.BlockSpec((None, D, cols), lambda l: (l, 0, 0)),
                  pl.BlockSpec((None, 16, cols), lambda l: (l, 0, 0))],
        out_specs=[pl.BlockSpec((None, D, cols), lambda l: (l, 0, 0)), pl.BlockSpec((16, D), lambda l: (0, 0))],
        scratch_shapes=[pltpu.VMEM((16, D), F32)], compiler_params=_cparams(1))(craw, w_mod, g)


def sum_parts(parts):
    def body(p_ref, o_ref):
        acc = p_ref[0]
        for k in range(1, parts.shape[0]):
            acc = acc + p_ref[k]
        o_ref[...] = acc

    return pl.pallas_call(body, name="sum_small_grads", out_shape=jax.ShapeDtypeStruct(parts.shape[1:], parts.dtype),
                          compiler_params=pltpu.CompilerParams(vmem_limit_bytes=VMEM_MB << 20))(parts)


def adamw(name, w, m, v, gs):
    r, c = w.shape
    s = gs.shape[0]
    tr = max([t for t in range(8, 257, 8) if r % t == 0], default=r)

    def body(w_ref, m_ref, v_ref, g_ref, go_ref, d_ref, mo_ref, vo_ref):
        g = g_ref[0].astype(F32)
        for k in range(1, s):
            g = g + g_ref[k].astype(F32)
        m_new = B1 * m_ref[...] + (1.0 - B1) * g
        v_new = B2 * v_ref[...] + (1.0 - B2) * jnp.square(g)
        m_hat = m_new / (1.0 - B1 ** STEP)
        v_hat = v_new / (1.0 - B2 ** STEP)
        go_ref[...] = g
        d_ref[...] = -LR * (m_hat / (jnp.sqrt(v_hat) + EPS) + WD * w_ref[...])
        mo_ref[...] = m_new
        vo_ref[...] = v_new

    blk = pl.BlockSpec((tr, c), lambda i: (i, 0))
    return pl.pallas_call(
        body, name=name, grid=(r // tr,), out_shape=[jax.ShapeDtypeStruct((r, c), F32)] * 4,
        in_specs=[blk, blk, blk, pl.BlockSpec((s, tr, c), lambda i: (0, i, 0))], out_specs=[blk] * 4,
        compiler_params=_cparams(1))(w, m, v, gs)


SMALL = ["c_ctx", "b_mod", "s5_lam_re", "s5_lam_im", "s5_log_dt", "s5_b_re", "s5_b_im", "s5_c_re", "s5_c_im", "s5_d",
         "b_glu", "hg_lb", "hg_norm_w", "ln1_g", "ln1_b", "conv_b", "ln2_g", "ln2_b"]
BIG = ["w_in", "w_glu", "w_out", "w_up", "w_down"]
WEIGHTS = ["c_ctx", "w_mod", "b_mod", "w_in", "s5_lam_re", "s5_lam_im", "s5_log_dt", "s5_b_re", "s5_b_im", "s5_c_re",
           "s5_c_im", "s5_d", "w_glu", "b_glu", "hg_lb", "hg_norm_w", "w_out", "ln1_g", "ln1_b", "w_up", "conv_w",
           "conv_b", "w_down", "ln2_g", "ln2_b"]
PACK_W = 1024


def _pack_rows(k):
    return -(-k // (8 * PACK_W)) * 8


def _pack(arrs):
    parts = []
    for a in arrs:
        flat = a.reshape(-1)
        r = _pack_rows(flat.shape[0])
        parts.append(jnp.pad(flat, (0, r * PACK_W - flat.shape[0])).reshape(r, PACK_W))
    used = sum(q.shape[0] for q in parts)
    parts.append(jnp.zeros((-used % (8 * N_DEV), PACK_W), parts[0].dtype))
    return jnp.concatenate(parts, axis=0)


def _unpack(p, shapes):
    out, o = [], 0
    for s in shapes:
        k = math.prod(s)
        r = _pack_rows(k)
        out.append(p[o:o + r].reshape(-1)[:k].reshape(s))
        o += r
    return out


def _gathered_cols(g):
    return jnp.moveaxis(g, 0, 2).reshape(g.shape[1], g.shape[2], -1)


def _gathered_rows(g):
    return jnp.moveaxis(g, 0, 1).reshape(g.shape[1], -1, g.shape[3])


def _step(p):
    nl = p["w_in"].shape[0]
    me = lax.axis_index("x") * 4 + lax.axis_index("y") * 2 + lax.axis_index("c")
    xc0 = jnp.concatenate([p["ctx"][0], p["x"][0]], axis=0)
    n = xc0.shape[0]
    target = p["loss_target"][0]

    gathers, tokens = [], []
    for l in range(nl):
        srcs = [p[k][l].astype(BF16) for k in BIG]
        handle, token = exchange_start(f"gather_start{l}", srcs, [_own_block_set(s, False) for s in srcs], False)
        gathers.append(handle)
        tokens.append(token[0, 0])
    started = functools.reduce(jnp.add, tokens)

    hg3 = jnp.stack([p[k].reshape(-1) for k in ("hg_lb", "m_hg_lb", "v_hg_lb")])
    g_cw, g_c, g_hg = _exchange("gather_inputs", [p["conv_w"], p["c"] + started, hg3], False)
    conv_w = _gathered_cols(g_cw)
    hg_full = jnp.moveaxis(g_hg.reshape(N_DEV, 3, nl, 2, -1), 0, 3).reshape(3, nl, 2 * HGW)
    lb_all = lb_call(hg_full[0])

    def gathered_weights(l, after):
        g_in, g_glu, g_out, g_up, g_down = exchange_wait(f"gather_wait{l}", gathers[l], after, False)
        cols_ = lambda g: jnp.moveaxis(g, 0, 1).reshape(g.shape[1], -1)
        rows_ = lambda g: g.reshape(-1, g.shape[2])
        return cols_(g_in), rows_(g_glu), rows_(g_out), cols_(g_up), rows_(g_down)

    craw = jnp.concatenate([g_c.reshape(N_DEV, D), jnp.broadcast_to(p["c_ctx"][None], (8, D))], axis=0)
    cols = p["w_mod"].shape[2]
    b_cols = lax.dynamic_slice_in_dim(p["b_mod"], me * cols, cols, axis=1)[:, None, :]
    (g_mod,) = _exchange("gather_mod", [mod_fwd(craw, p["w_mod"], b_cols)], False)
    mod_all = jnp.moveaxis(g_mod, 0, 2).reshape(nl, 16, 6 * D)
    mod_x = lax.dynamic_index_in_dim(mod_all, me, axis=1, keepdims=False)
    mod2 = jnp.stack([mod_all[:, 8], mod_x], axis=1)
    mvec = lambda l, k: mod2[l, :, k * D:(k + 1) * D][:, None, :]
    zvec = jnp.zeros((2, 1, D), F32)
    row = lambda a: a.reshape(1, 1, -1)

    to_hp = lambda a: jnp.moveaxis(a, -1, 2).reshape(nl * 2, S5H, S5P)
    prep_in = [p["s5_lam_re"].reshape(nl * 2, 1, S5P), p["s5_lam_im"].reshape(nl * 2, 1, S5P),
               p["s5_log_dt"].reshape(nl * 2, 32, 1), to_hp(p["s5_b_re"]), to_hp(p["s5_b_im"]),
               jnp.swapaxes(p["s5_c_re"], 2, 3).reshape(nl * 2, S5H, S5P),
               jnp.swapaxes(p["s5_c_im"], 2, 3).reshape(nl * 2, S5H, S5P)]
    s5a, s5b, s5c = s5_prep(*prep_in)

    T1 = lambda a, w=D, cb=_c0: (a, w, cb)
    saved = []
    xc = xc0
    (h,) = block_fwd("mod0", f_mod, n, [T1(xc)], [T1(mvec(0, 0)), T1(mvec(0, 1))], [(D, BF16, D, _c0)])
    w_in, w_glu, w_out, w_up, w_down = ([None] * nl for _ in range(5))
    for l in range(nl):
        w_in[l], w_glu[l], w_out[l], w_up[l], w_down[l] = gathered_weights(l, xc if l else s5a)
        proj = dense_nn(f"in_proj{l}", h, w_in[l])
        s5 = [s5_fwd(f"s5_fwd{l}_{d}", proj, s5a, s5b, s5c, 2 * l + d, d == 1) for d in range(2)]
        lbs = [lb_all[l, d * HGW:(d + 1) * HGW][None] for d in range(2)]
        gl = [gla_fwd(f"gla_fwd{l}_{d}", proj, lbs[d], d == 1) for d in range(2)]
        mix_t = [T1(proj, S5W), T1(proj, HGW, lambda j: 5), T1(s5[0][2], S5W), T1(s5[1][2], S5W),
                 T1(gl[0][0], HGW), T1(gl[1][0], HGW)]
        mix_p = [T1(row(p["s5_d"][l]), S5W), T1(w_glu[l][None], S5W), T1(row(p["b_glu"][l]), S5W),
                 T1(row(p["hg_norm_w"][l]), HD)]
        (y,) = block_fwd(f"mix{l}", f_mix, n, mix_t, mix_p, [(D, BF16, D, _c0)])
        z = dense_nn(f"out_proj{l}", y, w_out[l])
        ln1_p = [T1(mvec(l, 2)), T1(row(p["ln1_g"][l])), T1(row(p["ln1_b"][l])), T1(mvec(l, 3)), T1(mvec(l, 4))]
        x1, h2 = block_fwd(f"ln1_{l}", f_ln, n, [T1(xc), T1(z)], ln1_p, [(D, F32, D, _c0), (D, BF16, D, _c0)])
        up = dense_nn(f"up_proj{l}", h2, w_up[l])
        ct = DFF // 2
        act_t = [T1(up, ct, lambda j: j), T1(up, ct, lambda j: j + 2)]
        cb2 = p["conv_b"][l].reshape(1, 1, -1)
        act_p = [T1(conv_w[l][None, :, :DFF], ct, lambda j: j), T1(conv_w[l][None, :, DFF:], ct, lambda j: j),
                 T1(cb2[:, :, :DFF], ct, lambda j: j), T1(cb2[:, :, DFF:], ct, lambda j: j)]
        (act,) = block_fwd(f"act{l}", f_act, n, act_t, act_p, [(DFF, BF16, ct, lambda j: j)], n_col=2)
        dn = dense_nn(f"down_proj{l}", act, w_down[l])
        nxt = (mvec(l + 1, 0), mvec(l + 1, 1)) if l + 1 < nl else (zvec, zvec)
        ln2_p = [T1(mvec(l, 5)), T1(row(p["ln2_g"][l])), T1(row(p["ln2_b"][l])), T1(nxt[0]), T1(nxt[1])]
        x2, hn = block_fwd(f"ln2_{l}", f_ln, n, [T1(x1), T1(dn)], ln2_p, [(D, F32, D, _c0), (D, BF16, D, _c0)])
        saved.append(dict(xc=xc, h=h, proj=proj, s5=s5, gl=gl, lbs=lbs, mix_t=mix_t, mix_p=mix_p, y=y, z=z,
                          ln1_p=ln1_p, x1=x1, h2=h2, act_t=act_t, act_p=act_p, act=act, dn=dn, ln2_p=ln2_p))
        xc, h = x2, hn

    dxc, loss_part = loss_and_grad(xc, target)
    loss = lax.psum(loss_part[0, 0], AXES)

    g = {k: [None] * nl for k in ("w_in", "w_glu", "w_out", "w_up", "w_down", "conv_w", "conv_b", "s5_d", "b_glu",
                                  "hg_norm_w", "ln1_g", "ln1_b", "ln2_g", "ln2_b", "dlb", "s5")}
    dmod = [[None] * 6 for _ in range(nl)]
    scatters = [None] * nl
    dh_next = jnp.zeros((n, D), F32)
    fgrad = (D, F32, D, _c0)
    for l in reversed(range(nl)):
        sv = saved[l]
        (dx1, d_dn), dp = block_bwd(f"ln2_bwd{l}", f_ln, n, [T1(sv["x1"]), T1(sv["dn"])], sv["ln2_p"],
                                    [T1(dxc), T1(dh_next)], [fgrad, fgrad])
        dmod[l][5], g["ln2_g"][l], g["ln2_b"][l] = dp[0], dp[1], dp[2]
        if l + 1 < nl:
            dmod[l + 1][0], dmod[l + 1][1] = dp[3], dp[4]
        dact = dense_nt(f"down_bwd{l}", d_dn, w_down[l])
        g["w_down"][l] = dense_tn(f"down_wgrad{l}", sv["act"], d_dn)
        ct = DFF // 2
        cj = lambda j: j
        (dua, dug), dp = block_bwd(f"act_bwd{l}", f_act, n, sv["act_t"], sv["act_p"], [T1(dact, ct, cj)],
                                   [(DFF, BF16, ct, cj), (DFF, BF16, ct, cj)], n_col=2)
        g["conv_w"][l] = jnp.concatenate([dp[0][0], dp[1][0]], axis=-1)
        g["conv_b"][l] = jnp.concatenate([dp[2][0, 0], dp[3][0, 0]], axis=-1)
        dup = jnp.concatenate([dua, dug], axis=-1)
        dh2 = dense_nt(f"up_bwd{l}", dup, w_up[l])
        g["w_up"][l] = dense_tn(f"up_wgrad{l}", sv["h2"], dup)
        (dxc, dz), dp = block_bwd(f"ln1_bwd{l}", f_ln, n, [T1(sv["xc"]), T1(sv["z"])], sv["ln1_p"],
                                  [T1(dx1), T1(dh2)], [fgrad, fgrad])
        dmod[l][2], g["ln1_g"][l], g["ln1_b"][l], dmod[l][3], dmod[l][4] = dp
        dy = dense_nt(f"out_bwd{l}", dz, w_out[l])
        g["w_out"][l] = dense_tn(f"out_wgrad{l}", sv["y"], dz)
        half = (S5W, F32, S5W, _c0)
        (dpu, dpg, dys, dos), dp = block_bwd(f"mix_bwd{l}", f_mix, n, sv["mix_t"], sv["mix_p"], [T1(dy)],
                                                   [half, half, half, None, half, None])
        g["s5_d"][l], g["w_glu"][l], g["b_glu"][l], g["hg_norm_w"][l] = dp[0][0, 0], dp[1][0], dp[2][0, 0], dp[3][0, 0]
        gb = [gla_bwd(f"gla_bwd{l}_{d}", sv["proj"], sv["lbs"][d], sv["gl"][d][1], dos, d == 1) for d in range(2)]
        g["dlb"][l] = jnp.concatenate([gb[0][3], gb[1][3]], axis=-1)[0]
        sb = [s5_bwd(f"s5_bwd{l}_{d}", sv["proj"], dys, sv["s5"][d][0], sv["s5"][d][1], s5a, s5b, s5c, 2 * l + d, d == 1)
              for d in range(2)]
        g["s5"][l] = sb
        asm_t = [T1(dpu, S5W), T1(sb[0][0], S5W), T1(sb[1][0], S5W), T1(gb[0][0], HGW), T1(gb[1][0], HGW),
                 T1(gb[0][1], HGW), T1(gb[1][1], HGW), T1(gb[0][2], HGW), T1(gb[1][2], HGW), T1(dpg, HGW)]
        (dproj,) = block_fwd(
            f"dproj{l}", lambda tv, pv, i: (jnp.concatenate(
                [tv[0] + tv[1] + tv[2], tv[3], tv[4], tv[5] + tv[6], tv[7] + tv[8], tv[9]], axis=-1),),
            n, asm_t, [], [(INC, BF16, INC, _c0)])
        dh_next = dense_nt(f"in_bwd{l}", dproj, w_in[l])
        g["w_in"][l] = dense_tn(f"in_wgrad{l}", sv["h"], dproj)
        by_cols = lambda a: jnp.moveaxis(a.reshape(a.shape[0], N_DEV, -1), 1, 0)
        by_rows = lambda a: a.reshape(N_DEV, -1, a.shape[1])
        sends = [by_cols(g["w_in"][l]), by_rows(g["w_glu"][l].astype(BF16)), by_rows(g["w_out"][l]),
                 by_cols(g["w_up"][l]), by_rows(g["w_down"][l]), by_cols(g["conv_w"][l])]
        scatters[l], token = exchange_start(f"scatter_start{l}", sends, [_own_block_set(s, True) for s in sends], True)
        if l:
            gate, wd_, cb_ = saved[l - 1]["ln2_p"][0]
            saved[l - 1]["ln2_p"][0] = (gate + token[0, 0], wd_, cb_)
    (dxc,), dp = block_bwd("mod0_bwd", f_mod_id, n, [T1(xc0)], [T1(mvec(0, 0)), T1(mvec(0, 1))],
                           [T1(dh_next), T1(dxc)], [fgrad])
    dmod[0][0], dmod[0][1] = dp
    grad_x = dxc[n - p["x"].shape[1]:][None]

    st = lambda k: jnp.stack([g["s5"][l][d][k] for l in range(nl) for d in range(2)])
    d_prep = s5_prep_bwd(*prep_in, st(1), st(2), st(3))
    from_hp = lambda a: jnp.moveaxis(a.reshape(nl, 2, S5H, S5W // S5H, 64), 2, -1)
    gs5 = {"s5_lam_re": d_prep[0].reshape(nl, 2, 32, 64), "s5_lam_im": d_prep[1].reshape(nl, 2, 32, 64),
           "s5_log_dt": d_prep[2].reshape(nl, 2, 32), "s5_b_re": from_hp(d_prep[3]), "s5_b_im": from_hp(d_prep[4]),
           "s5_c_re": jnp.swapaxes(d_prep[5].reshape(nl, 2, S5H, 32, 64), 2, 3),
           "s5_c_im": jnp.swapaxes(d_prep[6].reshape(nl, 2, S5H, 32, 64), 2, 3)}
    d_hg = lb_call(hg_full[0], jnp.stack(g["dlb"]))

    dmod_loc = jnp.stack([jnp.concatenate([dmod[l][k][:, 0] for k in range(6)], axis=-1) for l in range(nl)])
    (g_dmod,) = _exchange("gather_dmod", [dmod_loc], False)
    gcols = lax.dynamic_slice_in_dim(g_dmod, me * cols, cols, axis=3)
    g16 = jnp.concatenate([jnp.moveaxis(gcols[:, :, 1], 0, 1), jnp.moveaxis(gcols[:, :, 0], 0, 1)], axis=1)
    grad_w_mod, dcraw = mod_bwd(craw, p["w_mod"], g16)
    d_c_ctx = jnp.sum(dcraw[8:], axis=0)

    stk = lambda k: jnp.stack(g[k])
    small_g = {"c_ctx": d_c_ctx, "b_mod": dmod_loc[:, 0] + dmod_loc[:, 1], "s5_d": stk("s5_d"), "b_glu": stk("b_glu"),
               "hg_lb": d_hg.reshape(nl, 2, HGW), "hg_norm_w": stk("hg_norm_w"), "ln1_g": stk("ln1_g")[:, 0, 0],
               "ln1_b": stk("ln1_b")[:, 0, 0], "conv_b": stk("conv_b"), "ln2_g": stk("ln2_g")[:, 0, 0],
               "ln2_b": stk("ln2_b")[:, 0, 0], **gs5}
    g_pack = _pack([small_g[k] for k in SMALL])
    (g_parts,) = _exchange("scatter_small_grads", [g_pack.reshape(N_DEV, -1, PACK_W)], True)
    (g_small,) = _exchange("gather_small_grads", [sum_parts(g_parts)], False)
    g_small = g_small.reshape(1, -1, PACK_W)

    out = {}
    hgw = {"": hg_full[0].reshape(nl, 2, HGW), "m_": hg_full[1].reshape(nl, 2, HGW), "v_": hg_full[2].reshape(nl, 2, HGW)}
    full = lambda pre, k: hgw[pre] if k == "hg_lb" else p[pre + k]
    shapes = [full("", k).shape for k in SMALL]
    res = adamw("adamw_small", *[_pack([full(pre, k) for k in SMALL]) for pre in ("", "m_", "v_")], g_small)
    for kind, packed in zip(("grad_", "delta_", "new_m_", "new_v_"), res):
        for k, a in zip(SMALL, _unpack(packed, shapes)):
            if k == "hg_lb":
                a = lax.dynamic_slice_in_dim(a, me * (HGW // N_DEV), HGW // N_DEV, axis=2)
            out[kind + k] = a
    kinds = ("grad_", "delta_", "new_m_", "new_v_")
    res = adamw("adamw_w_mod", *[p[pre + "w_mod"].reshape(-1, cols) for pre in ("", "m_", "v_")],
                grad_w_mod.reshape(1, -1, cols))
    for kind, a in zip(kinds, res):
        out[kind + "w_mod"] = a.reshape(p["w_mod"].shape)
    per_layer = {k: [] for k in BIG + ["conv_w"]}
    for l in reversed(range(nl)):
        recv = exchange_wait(f"scatter_wait{l}", scatters[l], g_small, True)
        for k, gsum in zip(BIG + ["conv_w"], recv):
            per_layer[k].append(adamw(f"adamw_{k}{l}", p[k][l], p["m_" + k][l], p["v_" + k][l], gsum))
    for k, res_l in per_layer.items():
        for i, kind in enumerate(kinds):
            out[kind + k] = jnp.stack([r[i] for r in reversed(res_l)])
    return (loss, grad_x, *[out[kind + k] for kind in ("grad_", "delta_", "new_m_", "new_v_") for k in WEIGHTS])


def kernel(x, c, ctx, c_ctx, w_mod, b_mod, w_in, s5_lam_re, s5_lam_im, s5_log_dt, s5_b_re, s5_b_im, s5_c_re, s5_c_im, s5_d, w_glu, b_glu, hg_lb, hg_norm_w, w_out, ln1_g, ln1_b, w_up, conv_w, conv_b, w_down, ln2_g, ln2_b, loss_target, m_c_ctx, m_w_mod, m_b_mod, m_w_in, m_s5_lam_re, m_s5_lam_im, m_s5_log_dt, m_s5_b_re, m_s5_b_im, m_s5_c_re, m_s5_c_im, m_s5_d, m_w_glu, m_b_glu, m_hg_lb, m_hg_norm_w, m_w_out, m_ln1_g, m_ln1_b, m_w_up, m_conv_w, m_conv_b, m_w_down, m_ln2_g, m_ln2_b, v_c_ctx, v_w_mod, v_b_mod, v_w_in, v_s5_lam_re, v_s5_lam_im, v_s5_log_dt, v_s5_b_re, v_s5_b_im, v_s5_c_re, v_s5_c_im, v_s5_d, v_w_glu, v_b_glu, v_hg_lb, v_hg_norm_w, v_w_out, v_ln1_g, v_ln1_b, v_w_up, v_conv_w, v_conv_b, v_w_down, v_ln2_g, v_ln2_b):
    return _step(dict(locals()))
```

```python
import functools
import math

import jax
import jax.numpy as jnp
from jax import lax
from jax.experimental import pallas as pl
from jax.experimental.pallas import tpu as pltpu

F32, BF16 = jnp.float32, jnp.bfloat16
N_DEV = 8
AXES = ("x", "y", "c")
D = 1024
S5W = 512
S5P = 2048
S5H = 16
HGW = 512
HD = 128
NH = 4
CK = 32
DFF = 2816
GRID_W = 64
INC = 3072
ALPHA = 8.0 ** 0.25
LN_EPS = 1e-5
RMS_EPS = 1e-6
LR, B1, B2, EPS, WD, STEP = 0.001, 0.9, 0.999, 1e-08, 0.01, 10
TT = 256
VMEM_MB = 56

NN = ((1,), (0,))
NT = ((1,), (1,))
TN = ((0,), (0,))


def _cparams(n_axes):
    return pltpu.CompilerParams(dimension_semantics=("arbitrary",) * n_axes, vmem_limit_bytes=VMEM_MB << 20)


def _dot(a, b, dims):
    return lax.dot_general(a.astype(BF16), b.astype(BF16), (dims, ((), ())), preferred_element_type=F32)


@jax.custom_vjp
def mm_nn(a, b):
    return _dot(a, b, NN)


@jax.custom_vjp
def mm_nt(a, b):
    return _dot(a, b, NT)


@jax.custom_vjp
def mm_tn(a, b):
    return _dot(a, b, TN)


mm_nn.defvjp(lambda a, b: (_dot(a, b, NN), (a, b)), lambda r, g: (_dot(g, r[1], NT), _dot(r[0], g, TN)))
mm_nt.defvjp(lambda a, b: (_dot(a, b, NT), (a, b)), lambda r, g: (_dot(g, r[1], NN), _dot(g, r[0], TN)))
mm_tn.defvjp(lambda a, b: (_dot(a, b, TN), (a, b)), lambda r, g: (_dot(r[1], g, NT), _dot(r[0], g, NN)))


def _roll_rows(u, s):
    return pltpu.roll(u, s % u.shape[0], 0)


@functools.partial(jax.custom_vjp, nondiff_argnums=(2,))
def shift_rows(u, m, s):
    return _roll_rows(u, s) * m


def _shift_fwd(u, m, s):
    return _roll_rows(u, s) * m, m


def _shift_bwd(s, m, g):
    return _roll_rows(g * m, -s), jnp.zeros_like(m)


shift_rows.defvjp(_shift_fwd, _shift_bwd)


def _scan_tile(pos, nt, rev):
    return jnp.where(pos == 0, 0, nt - pos) if rev else pos


def _exchange(name, arrays, all_to_all):
    k_arr = len(arrays)

    def body(*refs):
        ins, outs = refs[:k_arr], refs[k_arr:2 * k_arr]
        send_sems, recv_sems, local_sems = refs[2 * k_arr:]
        me = lax.axis_index("x") * 4 + lax.axis_index("y") * 2 + lax.axis_index("c")
        local = []
        for k in range(k_arr):
            cp = pltpu.make_async_copy(ins[k].at[me] if all_to_all else ins[k], outs[k].at[me], local_sems.at[k])
            cp.start()
            local.append(cp)
        sends = []
        for d in range(1, N_DEV):
            p = (me + d) % N_DEV
            for k in range(k_arr):
                cp = pltpu.make_async_remote_copy(
                    src_ref=ins[k].at[p] if all_to_all else ins[k], dst_ref=outs[k].at[me],
                    send_sem=send_sems.at[k, d - 1], recv_sem=recv_sems.at[k, d - 1],
                    device_id=(p // 4, (p // 2) % 2, p % 2), device_id_type=pl.DeviceIdType.MESH)
                cp.start()
                sends.append(cp)
        for d in range(1, N_DEV):
            q = (me + N_DEV - d) % N_DEV
            for k in range(k_arr):
                pltpu.make_async_remote_copy(
                    src_ref=ins[k].at[q] if all_to_all else ins[k], dst_ref=outs[k].at[q],
                    send_sem=send_sems.at[k, d - 1], recv_sem=recv_sems.at[k, d - 1],
                    device_id=(q // 4, (q // 2) % 2, q % 2), device_id_type=pl.DeviceIdType.MESH).wait_recv()
        for cp in sends:
            cp.wait_send()
        for cp in local:
            cp.wait()

    shapes = [a.shape if all_to_all else (N_DEV,) + a.shape for a in arrays]
    return pl.pallas_call(
        body, name=name,
        out_shape=[jax.ShapeDtypeStruct(s, a.dtype) for s, a in zip(shapes, arrays)],
        in_specs=[pl.BlockSpec(memory_space=pl.ANY)] * k_arr,
        out_specs=[pl.BlockSpec(memory_space=pl.ANY)] * k_arr,
        scratch_shapes=[pltpu.SemaphoreType.DMA((k_arr, N_DEV - 1)), pltpu.SemaphoreType.DMA((k_arr, N_DEV - 1)),
                        pltpu.SemaphoreType.DMA((k_arr,))],
    )(*arrays)


_HBM = pl.BlockSpec(memory_space=pltpu.HBM)
_SEM = pl.BlockSpec(memory_space=pltpu.SEMAPHORE)
_EFFECT = pltpu.SideEffectType.DATAFLOW_SIDE_EFFECTING


def _peer(i):
    return (i // 4, (i // 2) % 2, i % 2)


def exchange_start(name, srcs, lands, all_to_all, after=None):
    k_arr = len(srcs)
    n_sem = k_arr * (N_DEV - 1)
    extra = [] if after is None else [after]

    def body(*refs):
        ins, lz = refs[:k_arr], refs[k_arr:2 * k_arr]
        first = 2 * k_arr + len(extra)
        send_sems = refs[first:first + n_sem]
        recv_sems = refs[first + n_sem:first + 2 * n_sem]
        me = lax.axis_index("x") * 4 + lax.axis_index("y") * 2 + lax.axis_index("c")
        for d in range(1, N_DEV):
            p = (me + d) % N_DEV
            for k in range(k_arr):
                s = k * (N_DEV - 1) + d - 1
                pltpu.make_async_remote_copy(
                    src_ref=ins[k].at[p] if all_to_all else ins[k], dst_ref=lz[k].at[me],
                    send_sem=send_sems[s], recv_sem=recv_sems[s],
                    device_id=_peer(p), device_id_type=pl.DeviceIdType.MESH).start()
        refs[-1][...] = jnp.zeros_like(refs[-1])

    arrs = list(srcs) + list(lands)
    res = pl.pallas_call(
        body, name=name,
        out_shape=(*[pltpu.SemaphoreType.DMA(())] * (2 * n_sem), *[pltpu.HBM(a.shape, a.dtype) for a in arrs],
                   jax.ShapeDtypeStruct((8, 128), F32)),
        in_specs=[_HBM] * len(arrs) + [pl.BlockSpec(memory_space=pl.ANY)] * len(extra),
        out_specs=(*[_SEM] * (2 * n_sem), *[_HBM] * len(arrs), pl.BlockSpec(memory_space=pltpu.VMEM)),
        input_output_aliases={i: 2 * n_sem + i for i in range(len(arrs))},
        compiler_params=pltpu.CompilerParams(has_side_effects=_EFFECT),
    )(*[pltpu.with_memory_space_constraint(a, pltpu.HBM) for a in arrs], *extra)
    return res[:-1], res[-1]


def exchange_wait(name, handle, after, all_to_all):
    k_arr = len(handle) // (2 * N_DEV)
    n_sem = k_arr * (N_DEV - 1)
    sems, arrs = handle[:2 * n_sem], handle[2 * n_sem:]

    def body(*refs):
        ins, lz = refs[:k_arr], refs[k_arr:2 * k_arr]
        s_sems = refs[2 * k_arr:2 * k_arr + n_sem]
        r_sems = refs[2 * k_arr + n_sem:2 * k_arr + 2 * n_sem]
        me = lax.axis_index("x") * 4 + lax.axis_index("y") * 2 + lax.axis_index("c")
        for d in range(1, N_DEV):
            q = (me + N_DEV - d) % N_DEV
            for k in range(k_arr):
                s = k * (N_DEV - 1) + d - 1
                cp = pltpu.make_async_remote_copy(
                    src_ref=ins[k].at[q] if all_to_all else ins[k], dst_ref=lz[k].at[q],
                    send_sem=s_sems[s], recv_sem=r_sems[s],
                    device_id=_peer(q), device_id_type=pl.DeviceIdType.MESH)
                cp.wait_send()
                cp.wait_recv()

    res = pl.pallas_call(
        body, name=name, out_shape=tuple(pltpu.HBM(a.shape, a.dtype) for a in arrs),
        in_specs=[_HBM] * len(arrs) + [_SEM] * (2 * n_sem) + [pl.BlockSpec(memory_space=pl.ANY)],
        out_specs=tuple([_HBM] * len(arrs)),
        input_output_aliases={i: i for i in range(len(arrs))},
        compiler_params=pltpu.CompilerParams(has_side_effects=_EFFECT),
    )(*arrs, *sems, after)
    return res[k_arr:]


def _own_block_set(src, all_to_all):
    me = lax.axis_index("x") * 4 + lax.axis_index("y") * 2 + lax.axis_index("c")
    own = lax.dynamic_index_in_dim(src, me, 0, keepdims=False) if all_to_all else src
    shape = src.shape if all_to_all else (N_DEV,) + src.shape
    return lax.dynamic_update_index_in_dim(lax.empty(shape, src.dtype), own, me, 0)


def _tile(n, prefs):
    for t in prefs:
        if n % t == 0:
            return t
    raise ValueError(n)


def dense_nn(name, a, w, out_dtype=F32):
    n, k = a.shape
    m = w.shape[1]
    tn, tm = _tile(n, (1088, 256)), _tile(m, (1024, 512, 256, 128))

    def body(a_ref, w_ref, o_ref):
        o_ref[...] = _dot(a_ref[...], w_ref[...], NN).astype(o_ref.dtype)

    return pl.pallas_call(
        body, name=name, grid=(m // tm, n // tn), out_shape=jax.ShapeDtypeStruct((n, m), out_dtype),
        in_specs=[pl.BlockSpec((tn, k), lambda j, i: (i, 0)), pl.BlockSpec((k, tm), lambda j, i: (0, j))],
        out_specs=pl.BlockSpec((tn, tm), lambda j, i: (i, j)), compiler_params=_cparams(2))(a, w)


def dense_nt(name, g, w, out_dtype=F32):
    n, m = g.shape
    k = w.shape[0]
    tn, tk = _tile(n, (544, 256)), _tile(k, (1024, 1408, 512, 256, 128))

    def body(g_ref, w_ref, o_ref):
        o_ref[...] = _dot(g_ref[...], w_ref[...], NT).astype(o_ref.dtype)

    return pl.pallas_call(
        body, name=name, grid=(k // tk, n // tn), out_shape=jax.ShapeDtypeStruct((n, k), out_dtype),
        in_specs=[pl.BlockSpec((tn, m), lambda j, i: (i, 0)), pl.BlockSpec((tk, m), lambda j, i: (j, 0))],
        out_specs=pl.BlockSpec((tn, tk), lambda j, i: (i, j)), compiler_params=_cparams(2))(g, w)


def dense_tn(name, a, g, out_dtype=BF16):
    n, k = a.shape
    m = g.shape[1]
    tn = _tile(n, (1088, 256))
    tk = _tile(k, (1024, 1408, 512, 256, 128))
    tm = _tile(m, (1024, 512, 256, 128))
    nt = n // tn

    def body(a_ref, g_ref, o_ref, acc_ref):
        t = pl.program_id(2)

        @pl.when(t == 0)
        def _():
            acc_ref[...] = jnp.zeros_like(acc_ref)

        acc_ref[...] += _dot(a_ref[...], g_ref[...], TN)

        @pl.when(t == nt - 1)
        def _():
            o_ref[...] = acc_ref[...].astype(o_ref.dtype)

    return pl.pallas_call(
        body, name=name, grid=(k // tk, m // tm, nt), out_shape=jax.ShapeDtypeStruct((k, m), out_dtype),
        in_specs=[pl.BlockSpec((tn, tk), lambda i, j, t: (t, i)), pl.BlockSpec((tn, tm), lambda i, j, t: (t, j))],
        out_specs=pl.BlockSpec((tk, tm), lambda i, j, t: (i, j)),
        scratch_shapes=[pltpu.VMEM((tk, tm), F32)], compiler_params=_cparams(3))(a, g)


def _c0(j):
    return 0


def _tspec(w, cb):
    return pl.BlockSpec((TT, w), lambda j, i: (i, cb(j)))


def _pspec(arr, w, cb):
    two = arr.shape[0] == 2
    return pl.BlockSpec((None, arr.shape[1], w), lambda j, i: (jnp.minimum(i, 1) if two else 0, 0, cb(j)))


def block_fwd(name, fn, n, tiled, params, outs, n_col=1):
    nt_, np_ = len(tiled), len(params)

    def body(*refs):
        i = pl.program_id(1)
        tv = [r[...].astype(F32) for r in refs[:nt_]]
        pv = [r[...].astype(F32) for r in refs[nt_:nt_ + np_]]
        for o_ref, r in zip(refs[nt_ + np_:], fn(tv, pv, i)):
            o_ref[...] = r.astype(o_ref.dtype)

    return pl.pallas_call(
        body, name=name, grid=(n_col, n // TT),
        out_shape=[jax.ShapeDtypeStruct((n, c), dt) for c, dt, _, _ in outs],
        in_specs=[_tspec(w, cb) for _, w, cb in tiled] + [_pspec(a, w, cb) for a, w, cb in params],
        out_specs=[_tspec(w, cb) for _, _, w, cb in outs], compiler_params=_cparams(2),
    )(*[a for a, _, _ in tiled], *[a for a, _, _ in params])


def block_bwd(name, fn, n, tiled, params, cots, grads, n_col=1):
    nt_, np_, nc_ = len(tiled), len(params), len(cots)
    want = [k for k, g in enumerate(grads) if g is not None]

    def body(*refs):
        i = pl.program_id(1)
        tv = [r[...].astype(F32) for r in refs[:nt_]]
        pv = [r[...].astype(F32) for r in refs[nt_:nt_ + np_]]
        cv = [r[...].astype(F32) for r in refs[nt_ + np_:nt_ + np_ + nc_]]
        o_refs = refs[nt_ + np_ + nc_:]
        _, vjp = jax.vjp(lambda t, p: list(fn(t, p, i)), tv, pv)
        dt, dp = vjp(cv)
        for o_ref, k in zip(o_refs, want):
            o_ref[...] = dt[k].astype(o_ref.dtype)
        for o_ref, g, (arr, _, _) in zip(o_refs[len(want):], dp, params):
            first = (i == 0) | (i == 1) if arr.shape[0] == 2 else i == 0

            @pl.when(first)
            def _(o_ref=o_ref):
                o_ref[...] = jnp.zeros_like(o_ref)

            o_ref[...] += g

    res = pl.pallas_call(
        body, name=name, grid=(n_col, n // TT),
        out_shape=[jax.ShapeDtypeStruct((n, grads[k][0]), grads[k][1]) for k in want]
        + [jax.ShapeDtypeStruct(a.shape, F32) for a, _, _ in params],
        in_specs=[_tspec(w, cb) for _, w, cb in tiled] + [_pspec(a, w, cb) for a, w, cb in params]
        + [_tspec(w, cb) for _, w, cb in cots],
        out_specs=[_tspec(grads[k][2], grads[k][3]) for k in want] + [_pspec(a, w, cb) for a, w, cb in params],
        compiler_params=_cparams(2),
    )(*[a for a, _, _ in tiled], *[a for a, _, _ in params], *[a for a, _, _ in cots])
    return res[:len(want)], res[len(want):]


def f_mod(tv, pv, i):
    (x,), (sh, sc) = tv, pv
    return (x * (1.0 + sc) + sh,)


def f_mod_id(tv, pv, i):
    return (f_mod(tv, pv, i)[0], tv[0])


def f_ln(tv, pv, i):
    (x, z), (gate, g, b, sh, sc) = tv, pv
    pre = ALPHA * x + gate * z
    mu = jnp.mean(pre, axis=-1, keepdims=True)
    var = jnp.mean(jnp.square(pre - mu), axis=-1, keepdims=True)
    xn = (pre - mu) * lax.rsqrt(var + LN_EPS) * g + b
    return xn, xn * (1.0 + sc) + sh


def f_mix(tv, pv, i):
    (pu, pg, y0, y1, o0, o1), (d_skip, w_glu, b_glu, norm_w) = tv, pv
    s5y = jax.nn.gelu(y0 + y1 + pu * d_skip)
    s5o = s5y * jax.nn.sigmoid(mm_nn(s5y, w_glu) + b_glu)
    o = o0 + o1
    heads = []
    for h in range(NH):
        oh = o[:, h * HD:(h + 1) * HD]
        heads.append(oh * lax.rsqrt(jnp.mean(jnp.square(oh), axis=-1, keepdims=True) + RMS_EPS) * norm_w)
    hg = jnp.concatenate(heads, axis=-1) * jax.nn.silu(pg)
    return (jnp.concatenate([s5o, hg], axis=-1),)


def f_act(tv, pv, i):
    (ua, ug), (cwa, cwg, cba, cbg) = tv, pv
    t = lax.broadcasted_iota(jnp.int32, (TT, 1), 0)
    lat = i > 0
    m_dn = jnp.where((t == 0) | (lat & (t % GRID_W == 0)), 0.0, 1.0)
    m_up = jnp.where((t == TT - 1) | (lat & (t % GRID_W == GRID_W - 1)), 0.0, 1.0)

    def conv(u, w, b):
        return shift_rows(u, m_dn, 1) * w[0:1] + u * w[1:2] + shift_rows(u, m_up, -1) * w[2:3] + b

    return (jax.nn.silu(conv(ua, cwa, cba)) * conv(ug, cwg, cbg),)


def loss_and_grad(xf, target):
    n = xf.shape[0]

    def body(x_ref, t_ref, dy_ref, l_ref):
        i = pl.program_id(0)

        @pl.when(i == 0)
        def _():
            l_ref[...] = jnp.zeros_like(l_ref)
            dy_ref[...] = jnp.zeros_like(dy_ref)

        @pl.when(i > 0)
        def _():
            e = x_ref[...] - t_ref[...]
            dy_ref[...] = e * (1.0 / D)
            l_ref[...] += 0.5 / D * jnp.sum(jnp.square(e))

    return pl.pallas_call(
        body, name="loss", grid=(n // TT,),
        out_shape=[jax.ShapeDtypeStruct((n, D), F32), jax.ShapeDtypeStruct((8, 128), F32)],
        in_specs=[pl.BlockSpec((TT, D), lambda i: (i, 0)), pl.BlockSpec((TT, D), lambda i: (jnp.maximum(i - 1, 0), 0))],
        out_specs=[pl.BlockSpec((TT, D), lambda i: (i, 0)), pl.BlockSpec((8, 128), lambda i: (0, 0))],
        compiler_params=_cparams(1))(xf, target)


def f_prep(lr, li, ldt, bre, bim, cre, cim):
    gi = lax.broadcasted_iota(jnp.int32, (S5W // S5H, S5P), 0)
    gc = lax.broadcasted_iota(jnp.int32, (S5W // S5H, S5P), 1) // 64
    dt = jnp.exp(jnp.sum(jnp.where(gi == gc, ldt, 0.0), axis=0, keepdims=True))
    mag, ang = jnp.exp(lr * dt), li * dt
    ar, ai = mag * jnp.cos(ang), mag * jnp.sin(ang)
    den = lr * lr + li * li
    nr, ni = ar - 1.0, ai
    cr = (nr * lr + ni * li) / den
    ci = (ni * lr - nr * li) / den
    bbr = cr * bre - ci * bim
    bbi = cr * bim + ci * bre
    rg = lax.broadcasted_iota(jnp.int32, (S5W, S5P), 0) // S5H
    cg = lax.broadcasted_iota(jnp.int32, (S5W, S5P), 1) // 64
    mask = (rg == cg).astype(F32)
    blk = lambda a: jnp.concatenate([a] * (S5W // S5H), axis=0) * mask
    return ar, ai, blk(bbr), blk(bbi), blk(cre), blk(-cim)


def s5_prep(lr, li, ldt, bre, bim, cre, cim):
    n2 = lr.shape[0]

    def body(lr_r, li_r, ldt_r, bre_r, bim_r, cre_r, cim_r, a_ref, b_ref, c_ref):
        ar, ai, bbr, bbi, cbr, cbi = f_prep(lr_r[...], li_r[...], ldt_r[...], bre_r[...], bim_r[...], cre_r[...], cim_r[...])
        a_ref[0], a_ref[1] = ar, ai
        b_ref[0], b_ref[1] = bbr.astype(BF16), bbi.astype(BF16)
        c_ref[0], c_ref[1] = cbr.astype(BF16), cbi.astype(BF16)

    sp = lambda r, c: pl.BlockSpec((None, r, c), lambda i: (i, 0, 0))
    sp4 = lambda r, c: pl.BlockSpec((None, 2, r, c), lambda i: (i, 0, 0, 0))
    return pl.pallas_call(
        body, name="s5_prep", grid=(n2,),
        out_shape=[jax.ShapeDtypeStruct((n2, 2, 1, S5P), F32), jax.ShapeDtypeStruct((n2, 2, S5W, S5P), BF16),
                   jax.ShapeDtypeStruct((n2, 2, S5W, S5P), BF16)],
        in_specs=[sp(1, S5P), sp(1, S5P), sp(32, 1), sp(S5H, S5P), sp(S5H, S5P), sp(S5H, S5P), sp(S5H, S5P)],
        out_specs=[sp4(1, S5P), sp4(S5W, S5P), sp4(S5W, S5P)], compiler_params=_cparams(1),
    )(lr, li, ldt, bre, bim, cre, cim)


def s5_prep_bwd(lr, li, ldt, bre, bim, cre, cim, da, db, dc):
    n2 = lr.shape[0]

    def body(lr_r, li_r, ldt_r, bre_r, bim_r, cre_r, cim_r, da_r, db_r, dc_r, *outs):
        args = [r[...] for r in (lr_r, li_r, ldt_r, bre_r, bim_r, cre_r, cim_r)]
        _, vjp = jax.vjp(f_prep, *args)
        for o_ref, g in zip(outs, vjp((da_r[0], da_r[1], db_r[0], db_r[1], dc_r[0], dc_r[1]))):
            o_ref[...] = g

    sp = lambda r, c: pl.BlockSpec((None, r, c), lambda i: (i, 0, 0))
    sp4 = lambda r, c: pl.BlockSpec((None, 2, r, c), lambda i: (i, 0, 0, 0))
    ins = [sp(1, S5P), sp(1, S5P), sp(32, 1), sp(S5H, S5P), sp(S5H, S5P), sp(S5H, S5P), sp(S5H, S5P)]
    return pl.pallas_call(
        body, name="s5_prep_bwd", grid=(n2,),
        out_shape=[jax.ShapeDtypeStruct(a.shape, F32) for a in (lr, li, ldt, bre, bim, cre, cim)],
        in_specs=ins + [sp4(1, S5P), sp4(S5W, S5P), sp4(S5W, S5P)], out_specs=ins, compiler_params=_cparams(1),
    )(lr, li, ldt, bre, bim, cre, cim, da, db, dc)


S5_DIAG = 2
_CU, _CP = S5W // S5_DIAG, S5P // S5_DIAG


def _bd_nn(u, w_ref, k):
    return jnp.concatenate([_dot(u[:, j * _CU:(j + 1) * _CU], w_ref[k, j * _CU:(j + 1) * _CU, j * _CP:(j + 1) * _CP], NN)
                            for j in range(S5_DIAG)], axis=1)


def _bd_nt(x, w_ref, k):
    return jnp.concatenate([_dot(x[:, j * _CP:(j + 1) * _CP], w_ref[k, j * _CU:(j + 1) * _CU, j * _CP:(j + 1) * _CP], NT)
                            for j in range(S5_DIAG)], axis=1)


def _bd_tn_acc(acc_ref, k, a, g):
    for j in range(S5_DIAG):
        acc_ref[k, j * _CU:(j + 1) * _CU, j * _CP:(j + 1) * _CP] += _dot(a[:, j * _CU:(j + 1) * _CU],
                                                                         g[:, j * _CP:(j + 1) * _CP], TN)


_SCAN_LANES = 512


def _scan_rows(xr_ref, xi_ref, ar, ai, desc, cr_ref, ci_ref):
    def cmul(p, q):
        return p[0] * q[0] - p[1] * q[1], p[0] * q[1] + p[1] * q[0]

    pw = [(ar, ai)]
    for k in range(1, 8):
        pw.append(cmul(pw[k // 2], pw[k - 1 - k // 2]))
    row = lax.broadcasted_iota(jnp.int32, (8, 1), 0)
    dist = (7 - row) if desc else row
    zero = jnp.zeros_like(ar)
    carry_tab = [functools.reduce(jnp.add, [jnp.where(dist == d, pw[d][c], zero) for d in range(8)]) for c in range(2)]
    step_tab = [[jnp.where(dist >= s, pw[s - 1][c], zero) for c in range(2)] for s in (1, 2, 4)]

    def group(gi, _):
        g8 = (TT // 8 - 1 - gi) if desc else gi
        rows = pl.ds(pl.multiple_of(g8 * 8, 8), 8)
        last = g8 * 8 + (0 if desc else 7)
        for lc in range(S5P // _SCAN_LANES):
            ln = slice(lc * _SCAN_LANES, (lc + 1) * _SCAN_LANES)
            xr, xi = xr_ref[rows, ln], xi_ref[rows, ln]
            for s, (mr, mi) in zip((1, 2, 4), step_tab):
                sh = 8 - s if desc else s
                sr, si = pltpu.roll(xr, sh, 0), pltpu.roll(xi, sh, 0)
                xr, xi = xr + mr[:, ln] * sr - mi[:, ln] * si, xi + mr[:, ln] * si + mi[:, ln] * sr
            cr, ci = cr_ref[:, ln], ci_ref[:, ln]
            xr = xr + carry_tab[0][:, ln] * cr - carry_tab[1][:, ln] * ci
            xi = xi + carry_tab[0][:, ln] * ci + carry_tab[1][:, ln] * cr
            xr_ref[rows, ln] = xr
            xi_ref[rows, ln] = xi
            cr_ref[:, ln] = xr_ref[pl.ds(last, 1), ln]
            ci_ref[:, ln] = xi_ref[pl.ds(last, 1), ln]
        return 0

    lax.fori_loop(0, TT // 8, group, 0)


def s5_fwd(name, proj, a, bb, cb, ld, rev):
    n = proj.shape[0]
    nt = n // TT

    def body(u_ref, a_ref, b_ref, c_ref, xr_ref, xi_ref, y_ref, cr_ref, ci_ref):
        @pl.when(pl.program_id(0) == 0)
        def _():
            cr_ref[...] = jnp.zeros_like(cr_ref)
            ci_ref[...] = jnp.zeros_like(ci_ref)

        u = u_ref[...]
        xr_ref[...] = _bd_nn(u, b_ref, 0)
        xi_ref[...] = _bd_nn(u, b_ref, 1)
        _scan_rows(xr_ref, xi_ref, a_ref[0], a_ref[1], rev, cr_ref, ci_ref)
        y_ref[...] = _bd_nt(xr_ref[...], c_ref, 0) + _bd_nt(xi_ref[...], c_ref, 1)

    tile = lambda w: pl.BlockSpec((TT, w), lambda s: (_scan_tile(s, nt, rev), 0))
    par = lambda r: pl.BlockSpec((None, 2, r, S5P), lambda s: (ld, 0, 0, 0))
    return pl.pallas_call(
        body, name=name, grid=(nt,),
        out_shape=[jax.ShapeDtypeStruct((n, S5P), F32), jax.ShapeDtypeStruct((n, S5P), F32),
                   jax.ShapeDtypeStruct((n, S5W), F32)],
        in_specs=[tile(S5W), par(1), par(S5W), par(S5W)], out_specs=[tile(S5P), tile(S5P), tile(S5W)],
        scratch_shapes=[pltpu.VMEM((1, S5P), F32), pltpu.VMEM((1, S5P), F32)], compiler_params=_cparams(1),
    )(proj, a, bb, cb)


def s5_bwd(name, proj, dy, xr, xi, a, bb, cb, ld, rev):
    n = proj.shape[0]
    nt = n // TT
    tb = TT // 8

    def tile_of(s):
        return _scan_tile(nt - 1 - s, nt, rev)

    def edge_of(s):
        pos = nt - 1 - s
        prev = _scan_tile(jnp.maximum(pos - 1, 0), nt, rev)
        return prev * tb if rev else jnp.maximum(pos * tb - 1, 0)

    def body(u_ref, dy_ref, xr_ref, xi_ref, er_ref, ei_ref, a_ref, b_ref, c_ref,
             du_ref, da_ref, db_ref, dc_ref, gr_ref, gi_ref, cr_ref, ci_ref):
        s = pl.program_id(0)

        @pl.when(s == 0)
        def _():
            cr_ref[...] = jnp.zeros_like(cr_ref)
            ci_ref[...] = jnp.zeros_like(ci_ref)
            da_ref[...] = jnp.zeros_like(da_ref)
            db_ref[...] = jnp.zeros_like(db_ref)
            dc_ref[...] = jnp.zeros_like(dc_ref)

        dyv, u = dy_ref[...], u_ref[...]
        xrv, xiv = xr_ref[...], xi_ref[...]
        gr_ref[...] = _bd_nn(dyv, c_ref, 0)
        gi_ref[...] = _bd_nn(dyv, c_ref, 1)
        _bd_tn_acc(dc_ref, 0, dyv, xrv)
        _bd_tn_acc(dc_ref, 1, dyv, xiv)
        _scan_rows(gr_ref, gi_ref, a_ref[0], -a_ref[1], not rev, cr_ref, ci_ref)
        g_r, g_i = gr_ref[...], gi_ref[...]
        rows = lax.broadcasted_iota(jnp.int32, (TT, 1), 0)
        live = jnp.where(s == nt - 1, 0.0, 1.0)
        if rev:
            pr = jnp.where(rows == TT - 1, er_ref[0:1, :] * live, _roll_rows(xrv, -1))
            pi = jnp.where(rows == TT - 1, ei_ref[0:1, :] * live, _roll_rows(xiv, -1))
        else:
            pr = jnp.where(rows == 0, er_ref[7:8, :] * live, _roll_rows(xrv, 1))
            pi = jnp.where(rows == 0, ei_ref[7:8, :] * live, _roll_rows(xiv, 1))
        da_ref[0] += jnp.sum(g_r * pr + g_i * pi, axis=0, keepdims=True)
        da_ref[1] += jnp.sum(g_i * pr - g_r * pi, axis=0, keepdims=True)
        du_ref[...] = _bd_nt(g_r, b_ref, 0) + _bd_nt(g_i, b_ref, 1)
        _bd_tn_acc(db_ref, 0, u, g_r)
        _bd_tn_acc(db_ref, 1, u, g_i)

    tile = lambda w: pl.BlockSpec((TT, w), lambda s: (tile_of(s), 0))
    edge = pl.BlockSpec((8, S5P), lambda s: (edge_of(s), 0))
    par = lambda r: pl.BlockSpec((None, 2, r, S5P), lambda s: (ld, 0, 0, 0))
    acc = lambda r: pl.BlockSpec((2, r, S5P), lambda s: (0, 0, 0))
    return pl.pallas_call(
        body, name=name, grid=(nt,),
        out_shape=[jax.ShapeDtypeStruct((n, S5W), F32), jax.ShapeDtypeStruct((2, 1, S5P), F32),
                   jax.ShapeDtypeStruct((2, S5W, S5P), F32), jax.ShapeDtypeStruct((2, S5W, S5P), F32)],
        in_specs=[tile(S5W), tile(S5W), tile(S5P), tile(S5P), edge, edge, par(1), par(S5W), par(S5W)],
        out_specs=[tile(S5W), acc(1), acc(S5W), acc(S5W)],
        scratch_shapes=[pltpu.VMEM((TT, S5P), F32), pltpu.VMEM((TT, S5P), F32),
                        pltpu.VMEM((1, S5P), F32), pltpu.VMEM((1, S5P), F32)], compiler_params=_cparams(1),
    )(proj, dy, xr, xi, xr, xi, a, bb, cb)


def gla_tile(r, v, qr, lb, sts, rev):
    ncc = TT // CK
    f = lb + (1.0 - lb) * jax.nn.sigmoid(r)
    k, lf, q = 1.0 - f, jnp.log(f), jax.nn.silu(qr)
    rows = lax.broadcasted_iota(jnp.int32, (TT, 1), 0)
    pos = rows % CK
    b = lf
    for s in (1, 2, 4, 8, 16):
        m = ((pos < CK - s) if rev else (pos >= s)).astype(F32)
        b = b + shift_rows(b, m, -s if rev else s)
    etot = [jnp.sum(lf[c * CK:(c + 1) * CK], axis=0, keepdims=True) for c in range(ncc)]
    e = jnp.concatenate([jnp.broadcast_to(t, (CK, HGW)) for t in etot], axis=0)
    kd, qe, qa = k * jnp.exp(e - b), q * jnp.exp(b), q * jnp.exp(b - e)
    cm = [(rows // CK == c).astype(F32) for c in range(ncc)]
    r2 = lax.broadcasted_iota(jnp.int32, (TT, TT), 0)
    c2 = lax.broadcasted_iota(jnp.int32, (TT, TT), 1)
    amask = (r2 // CK == c2 // CK) & ((r2 <= c2) if rev else (r2 >= c2))
    outs, new_sts = [], []
    for h in range(NH):
        ln = slice(h * HD, (h + 1) * HD)
        kdh, qeh, vh = kd[:, ln], qe[:, ln], v[:, ln]
        att = jnp.where(amask, mm_nt(qa[:, ln], kdh), 0.0)
        ds = mm_tn(jnp.concatenate([kdh * cm[c] for c in range(ncc)], axis=1), vh)
        st, starts = sts[h], [None] * ncc
        for c in (reversed(range(ncc)) if rev else range(ncc)):
            starts[c] = st
            dec = jnp.transpose(jnp.broadcast_to(jnp.exp(etot[c][:, ln]), (HD, HD)))
            st = dec * st + ds[c * HD:(c + 1) * HD]
        new_sts.append(st)
        qex = jnp.concatenate([qeh * cm[c] for c in range(ncc)], axis=1)
        outs.append(mm_nn(att, vh) + mm_nn(qex, jnp.concatenate(starts, axis=0)))
    return jnp.concatenate(outs, axis=1), new_sts


def _gla_specs(n, rev, order):
    nt = n // TT
    fcol = 2 if rev else 1
    tile = lambda cbk: pl.BlockSpec((TT, HGW), lambda s: (order(s), cbk))
    return nt, [tile(fcol), tile(3), tile(4)], tile(0)


def gla_fwd(name, proj, lb, rev):
    n = proj.shape[0]
    nt, in_tiles, out_tile = _gla_specs(n, rev, lambda s: _scan_tile(s, n // TT, rev))

    def body(r_ref, v_ref, q_ref, lb_ref, o_ref, st_ref, s_ref):
        @pl.when(pl.program_id(0) == 0)
        def _():
            s_ref[...] = jnp.zeros_like(s_ref)

        sts = [s_ref[h] for h in range(NH)]
        for h in range(NH):
            st_ref[h] = sts[h]
        o, new = gla_tile(r_ref[...], v_ref[...], q_ref[...], lb_ref[...], sts, rev)
        o_ref[...] = o
        for h in range(NH):
            s_ref[h] = new[h]

    st_spec = pl.BlockSpec((None, NH, HD, HD), lambda s: (_scan_tile(s, nt, rev), 0, 0, 0))
    return pl.pallas_call(
        body, name=name, grid=(nt,),
        out_shape=[jax.ShapeDtypeStruct((n, HGW), F32), jax.ShapeDtypeStruct((nt, NH, HD, HD), F32)],
        in_specs=in_tiles + [pl.BlockSpec((1, HGW), lambda s: (0, 0))], out_specs=[out_tile, st_spec],
        scratch_shapes=[pltpu.VMEM((NH, HD, HD), F32)], compiler_params=_cparams(1),
    )(proj, proj, proj, lb)


def gla_bwd(name, proj, lb, st_all, do, rev):
    n = proj.shape[0]
    order = lambda s: _scan_tile(n // TT - 1 - s, n // TT, rev)
    nt, in_tiles, out_tile = _gla_specs(n, rev, order)

    def body(r_ref, v_ref, q_ref, lb_ref, st_ref, do_ref, dr_ref, dv_ref, dq_ref, dlb_ref, ds_ref):
        @pl.when(pl.program_id(0) == 0)
        def _():
            ds_ref[...] = jnp.zeros_like(ds_ref)
            dlb_ref[...] = jnp.zeros_like(dlb_ref)

        _, vjp = jax.vjp(functools.partial(gla_tile, rev=rev), r_ref[...], v_ref[...], q_ref[...], lb_ref[...],
                         [st_ref[h] for h in range(NH)])
        dr, dv, dq, dlb, dsts = vjp((do_ref[...], [ds_ref[h] for h in range(NH)]))
        dr_ref[...] = dr
        dv_ref[...] = dv
        dq_ref[...] = dq
        dlb_ref[...] += dlb
        for h in range(NH):
            ds_ref[h] = dsts[h]

    st_spec = pl.BlockSpec((None, NH, HD, HD), lambda s: (order(s), 0, 0, 0))
    row = pl.BlockSpec((1, HGW), lambda s: (0, 0))
    return pl.pallas_call(
        body, name=name, grid=(nt,),
        out_shape=[jax.ShapeDtypeStruct((n, HGW), F32)] * 3 + [jax.ShapeDtypeStruct((1, HGW), F32)],
        in_specs=in_tiles + [row, st_spec, out_tile], out_specs=[out_tile] * 3 + [row],
        scratch_shapes=[pltpu.VMEM((NH, HD, HD), F32)], compiler_params=_cparams(1),
    )(proj, proj, proj, lb, st_all, do)


def f_lb(rows):
    mx = functools.reduce(jnp.maximum, rows)
    ex = [jnp.exp(r - mx) for r in rows]
    tot = functools.reduce(jnp.add, ex)
    out, acc = [jnp.zeros_like(rows[0])], None
    for e in ex[1:]:
        acc = e / tot if acc is None else acc + e / tot
        out.append(acc)
    return out


def lb_call(hg, dlb=None):
    nl = hg.shape[0]

    def body(*refs):
        rows = [refs[0][l:l + 1, :] for l in range(nl)]
        if dlb is None:
            res = f_lb(rows)
        else:
            _, vjp = jax.vjp(f_lb, rows)
            (res,) = vjp([refs[1][l:l + 1, :] for l in range(nl)])
        for l in range(nl):
            refs[-1][l:l + 1, :] = res[l]

    args = (hg,) if dlb is None else (hg, dlb)
    return pl.pallas_call(body, name="lower_bounds" if dlb is None else "lower_bounds_bwd",
                          out_shape=jax.ShapeDtypeStruct(hg.shape, F32))(*args)


def mod_fwd(craw, w_mod, b_cols):
    nl, _, cols = w_mod.shape

    def body(c_ref, w_ref, b_ref, o_ref):
        o_ref[...] = _dot(jax.nn.silu(c_ref[...]), w_ref[...], NN) + b_ref[...]

    return pl.pallas_call(
        body, name="mod_fwd", grid=(nl,), out_shape=jax.ShapeDtypeStruct((nl, 16, cols), F32),
        in_specs=[pl.BlockSpec((16, D), lambda l: (0, 0)), pl.BlockSpec((None, D, cols), lambda l: (l, 0, 0)),
                  pl.BlockSpec((None, 1, cols), lambda l: (l, 0, 0))],
        out_specs=pl.BlockSpec((None, 16, cols), lambda l: (l, 0, 0)), compiler_params=_cparams(1))(craw, w_mod, b_cols)


def mod_bwd(craw, w_mod, g):
    nl, _, cols = w_mod.shape

    def body(c_ref, w_ref, g_ref, dw_ref, dc_ref, acc_ref):
        l = pl.program_id(0)

        @pl.when(l == 0)
        def _():
            acc_ref[...] = jnp.zeros_like(acc_ref)

        c = c_ref[...]
        s, vjp = jax.vjp(jax.nn.silu, c)
        dw_ref[...] = _dot(s, g_ref[...], TN)
        acc_ref[...] += _dot(g_ref[...], w_ref[...], NT)

        @pl.when(l == nl - 1)
        def _():
            dc_ref[...] = vjp(acc_ref[...])[0]

    return pl.pallas_call(
        body, name="mod_bwd", grid=(nl,),
        out_shape=[jax.ShapeDtypeStruct(w_mod.shape, F32), jax.ShapeDtypeStruct((16, D), F32)],
        in_specs=[pl.BlockSpec((16, D), lambda l: (0, 0)), pl.BlockSpec((None, D, cols), lambda l: (l, 0, 0)),
                  pl.BlockSpec((None, 16, cols), lambda l: (l, 0, 0))],
        out_specs=[pl.BlockSpec((None, D, cols), lambda l: (l, 0, 0)), pl.BlockSpec((16, D), lambda l: (0, 0))],
        scratch_shapes=[pltpu.VMEM((16, D), F32)], compiler_params=_cparams(1))(craw, w_mod, g)


def sum_parts(parts):
    def body(p_ref, o_ref):
        acc = p_ref[0]
        for k in range(1, parts.shape[0]):
            acc = acc + p_ref[k]
        o_ref[...] = acc

    return pl.pallas_call(body, name="sum_small_grads", out_shape=jax.ShapeDtypeStruct(parts.shape[1:], parts.dtype),
                          compiler_params=pltpu.CompilerParams(vmem_limit_bytes=VMEM_MB << 20))(parts)


def adamw(name, w, m, v, gs):
    r, c = w.shape
    s = gs.shape[0]
    tr = max([t for t in range(8, 257, 8) if r % t == 0], default=r)

    def body(w_ref, m_ref, v_ref, g_ref, go_ref, d_ref, mo_ref, vo_ref):
        g = g_ref[0].astype(F32)
        for k in range(1, s):
            g = g + g_ref[k].astype(F32)
        m_new = B1 * m_ref[...] + (1.0 - B1) * g
        v_new = B2 * v_ref[...] + (1.0 - B2) * jnp.square(g)
        m_hat = m_new / (1.0 - B1 ** STEP)
        v_hat = v_new / (1.0 - B2 ** STEP)
        go_ref[...] = g
        d_ref[...] = -LR * (m_hat / (jnp.sqrt(v_hat) + EPS) + WD * w_ref[...])
        mo_ref[...] = m_new
        vo_ref[...] = v_new

    blk = pl.BlockSpec((tr, c), lambda i: (i, 0))
    return pl.pallas_call(
        body, name=name, grid=(r // tr,), out_shape=[jax.ShapeDtypeStruct((r, c), F32)] * 4,
        in_specs=[blk, blk, blk, pl.BlockSpec((s, tr, c), lambda i: (0, i, 0))], out_specs=[blk] * 4,
        compiler_params=_cparams(1))(w, m, v, gs)


SMALL = ["c_ctx", "b_mod", "s5_lam_re", "s5_lam_im", "s5_log_dt", "s5_b_re", "s5_b_im", "s5_c_re", "s5_c_im", "s5_d",
         "b_glu", "hg_lb", "hg_norm_w", "ln1_g", "ln1_b", "conv_b", "ln2_g", "ln2_b"]
BIG = ["w_in", "w_glu", "w_out", "w_up", "w_down"]
WEIGHTS = ["c_ctx", "w_mod", "b_mod", "w_in", "s5_lam_re", "s5_lam_im", "s5_log_dt", "s5_b_re", "s5_b_im", "s5_c_re",
           "s5_c_im", "s5_d", "w_glu", "b_glu", "hg_lb", "hg_norm_w", "w_out", "ln1_g", "ln1_b", "w_up", "conv_w",
           "conv_b", "w_down", "ln2_g", "ln2_b"]
PACK_W = 1024


def _pack_rows(k):
    return -(-k // (8 * PACK_W)) * 8


def _pack(arrs):
    parts = []
    for a in arrs:
        flat = a.reshape(-1)
        r = _pack_rows(flat.shape[0])
        parts.append(jnp.pad(flat, (0, r * PACK_W - flat.shape[0])).reshape(r, PACK_W))
    used = sum(q.shape[0] for q in parts)
    parts.append(jnp.zeros((-used % (8 * N_DEV), PACK_W), parts[0].dtype))
    return jnp.concatenate(parts, axis=0)


def _unpack(p, shapes):
    out, o = [], 0
    for s in shapes:
        k = math.prod(s)
        r = _pack_rows(k)
        out.append(p[o:o + r].reshape(-1)[:k].reshape(s))
        o += r
    return out


def _gathered_cols(g):
    return jnp.moveaxis(g, 0, 2).reshape(g.shape[1], g.shape[2], -1)


def _gathered_rows(g):
    return jnp.moveaxis(g, 0, 1).reshape(g.shape[1], -1, g.shape[3])


def _step(p):
    nl = p["w_in"].shape[0]
    me = lax.axis_index("x") * 4 + lax.axis_index("y") * 2 + lax.axis_index("c")
    xc0 = jnp.concatenate([p["ctx"][0], p["x"][0]], axis=0)
    n = xc0.shape[0]
    target = p["loss_target"][0]

    hg3 = jnp.stack([p[k].reshape(-1) for k in ("hg_lb", "m_hg_lb", "v_hg_lb")])
    g_cw, g_c, g_hg = _exchange("gather_inputs", [p["conv_w"], p["c"], hg3], False)
    conv_w = _gathered_cols(g_cw)
    hg_full = jnp.moveaxis(g_hg.reshape(N_DEV, 3, nl, 2, -1), 0, 3).reshape(3, nl, 2 * HGW)
    lb_all = lb_call(hg_full[0])

    craw = jnp.concatenate([g_c.reshape(N_DEV, D), jnp.broadcast_to(p["c_ctx"][None], (8, D))], axis=0)
    cols = p["w_mod"].shape[2]
    b_cols = lax.dynamic_slice_in_dim(p["b_mod"], me * cols, cols, axis=1)[:, None, :]
    (g_mod,) = _exchange("gather_mod", [mod_fwd(craw, p["w_mod"], b_cols)], False)
    mod_all = jnp.moveaxis(g_mod, 0, 2).reshape(nl, 16, 6 * D)
    mod_x = lax.dynamic_index_in_dim(mod_all, me, axis=1, keepdims=False)
    mod2 = jnp.stack([mod_all[:, 8], mod_x], axis=1)
    mvec = lambda l, k: mod2[l, :, k * D:(k + 1) * D][:, None, :]
    gathers = {}
    for l in range(nl):
        for part, ks in ((("a", BIG[:1]), ("b", BIG[1:])) if l == 0 else (("", BIG),)):
            srcs = [p[k][l].astype(BF16) for k in ks]
            gathers[l, part], _ = exchange_start(f"gather_start{l}{part}", srcs, [_own_block_set(s, False) for s in srcs],
                                                 False, after=g_mod)

    def gathered(l, part, after):
        res = exchange_wait(f"gather_wait{l}{part}", gathers[l, part], after, False)
        ks = {"a": BIG[:1], "b": BIG[1:], "": BIG}[part]
        cols_ = lambda g: jnp.moveaxis(g, 0, 1).reshape(g.shape[1], -1)
        rows_ = lambda g: g.reshape(-1, g.shape[2])
        return {k: (cols_ if k in ("w_in", "w_up") else rows_)(g) for k, g in zip(ks, res)}

    zvec = jnp.zeros((2, 1, D), F32)
    row = lambda a: a.reshape(1, 1, -1)

    to_hp = lambda a: jnp.moveaxis(a, -1, 2).reshape(nl * 2, S5H, S5P)
    prep_in = [p["s5_lam_re"].reshape(nl * 2, 1, S5P), p["s5_lam_im"].reshape(nl * 2, 1, S5P),
               p["s5_log_dt"].reshape(nl * 2, 32, 1), to_hp(p["s5_b_re"]), to_hp(p["s5_b_im"]),
               jnp.swapaxes(p["s5_c_re"], 2, 3).reshape(nl * 2, S5H, S5P),
               jnp.swapaxes(p["s5_c_im"], 2, 3).reshape(nl * 2, S5H, S5P)]
    s5a, s5b, s5c = s5_prep(*prep_in)

    T1 = lambda a, w=D, cb=_c0: (a, w, cb)
    saved = []
    xc = xc0
    (h,) = block_fwd("mod0", f_mod, n, [T1(xc)], [T1(mvec(0, 0)), T1(mvec(0, 1))], [(D, BF16, D, _c0)])
    w_in, w_glu, w_out, w_up, w_down = ([None] * nl for _ in range(5))
    for l in range(nl):
        wl = gathered(l, "", xc) if l else gathered(0, "a", s5a)
        w_in[l] = wl["w_in"]
        proj = dense_nn(f"in_proj{l}", h, w_in[l])
        s5 = [s5_fwd(f"s5_fwd{l}_{d}", proj, s5a, s5b, s5c, 2 * l + d, d == 1) for d in range(2)]
        lbs = [lb_all[l, d * HGW:(d + 1) * HGW][None] for d in range(2)]
        gl = [gla_fwd(f"gla_fwd{l}_{d}", proj, lbs[d], d == 1) for d in range(2)]
        if l == 0:
            wl = gathered(0, "b", gl[1][0])
        w_glu[l], w_out[l], w_up[l], w_down[l] = wl["w_glu"], wl["w_out"], wl["w_up"], wl["w_down"]
        mix_t = [T1(proj, S5W), T1(proj, HGW, lambda j: 5), T1(s5[0][2], S5W), T1(s5[1][2], S5W),
                 T1(gl[0][0], HGW), T1(gl[1][0], HGW)]
        mix_p = [T1(row(p["s5_d"][l]), S5W), T1(w_glu[l][None], S5W), T1(row(p["b_glu"][l]), S5W),
                 T1(row(p["hg_norm_w"][l]), HD)]
        (y,) = block_fwd(f"mix{l}", f_mix, n, mix_t, mix_p, [(D, BF16, D, _c0)])
        z = dense_nn(f"out_proj{l}", y, w_out[l])
        ln1_p = [T1(mvec(l, 2)), T1(row(p["ln1_g"][l])), T1(row(p["ln1_b"][l])), T1(mvec(l, 3)), T1(mvec(l, 4))]
        x1, h2 = block_fwd(f"ln1_{l}", f_ln, n, [T1(xc), T1(z)], ln1_p, [(D, F32, D, _c0), (D, BF16, D, _c0)])
        up = dense_nn(f"up_proj{l}", h2, w_up[l])
        ct = DFF // 2
        act_t = [T1(up, ct, lambda j: j), T1(up, ct, lambda j: j + 2)]
        cb2 = p["conv_b"][l].reshape(1, 1, -1)
        act_p = [T1(conv_w[l][None, :, :DFF], ct, lambda j: j), T1(conv_w[l][None, :, DFF:], ct, lambda j: j),
                 T1(cb2[:, :, :DFF], ct, lambda j: j), T1(cb2[:, :, DFF:], ct, lambda j: j)]
        (act,) = block_fwd(f"act{l}", f_act, n, act_t, act_p, [(DFF, BF16, ct, lambda j: j)], n_col=2)
        dn = dense_nn(f"down_proj{l}", act, w_down[l])
        nxt = (mvec(l + 1, 0), mvec(l + 1, 1)) if l + 1 < nl else (zvec, zvec)
        ln2_p = [T1(mvec(l, 5)), T1(row(p["ln2_g"][l])), T1(row(p["ln2_b"][l])), T1(nxt[0]), T1(nxt[1])]
        x2, hn = block_fwd(f"ln2_{l}", f_ln, n, [T1(x1), T1(dn)], ln2_p, [(D, F32, D, _c0), (D, BF16, D, _c0)])
        saved.append(dict(xc=xc, h=h, proj=proj, s5=s5, gl=gl, lbs=lbs, mix_t=mix_t, mix_p=mix_p, y=y, z=z,
                          ln1_p=ln1_p, x1=x1, h2=h2, act_t=act_t, act_p=act_p, act=act, dn=dn, ln2_p=ln2_p))
        xc, h = x2, hn

    dxc, loss_part = loss_and_grad(xc, target)
    loss = lax.psum(loss_part[0, 0], AXES)

    g = {k: [None] * nl for k in ("w_in", "w_glu", "w_out", "w_up", "w_down", "conv_w", "conv_b", "s5_d", "b_glu",
                                  "hg_norm_w", "ln1_g", "ln1_b", "ln2_g", "ln2_b", "dlb", "s5")}
    dmod = [[None] * 6 for _ in range(nl)]
    scatters = [[] for _ in range(nl)]
    dh_next = jnp.zeros((n, D), F32)
    fgrad = (D, F32, D, _c0)
    for l in reversed(range(nl)):
        sv = saved[l]
        (dx1, d_dn), dp = block_bwd(f"ln2_bwd{l}", f_ln, n, [T1(sv["x1"]), T1(sv["dn"])], sv["ln2_p"],
                                    [T1(dxc), T1(dh_next)], [fgrad, fgrad])
        dmod[l][5], g["ln2_g"][l], g["ln2_b"][l] = dp[0], dp[1], dp[2]
        if l + 1 < nl:
            dmod[l + 1][0], dmod[l + 1][1] = dp[3], dp[4]
        dact = dense_nt(f"down_bwd{l}", d_dn, w_down[l])
        g["w_down"][l] = dense_tn(f"down_wgrad{l}", sv["act"], d_dn)
        ct = DFF // 2
        cj = lambda j: j
        (dua, dug), dp = block_bwd(f"act_bwd{l}", f_act, n, sv["act_t"], sv["act_p"], [T1(dact, ct, cj)],
                                   [(DFF, BF16, ct, cj), (DFF, BF16, ct, cj)], n_col=2)
        g["conv_w"][l] = jnp.concatenate([dp[0][0], dp[1][0]], axis=-1)
        g["conv_b"][l] = jnp.concatenate([dp[2][0, 0], dp[3][0, 0]], axis=-1)
        dup = jnp.concatenate([dua, dug], axis=-1)
        dh2 = dense_nt(f"up_bwd{l}", dup, w_up[l])
        g["w_up"][l] = dense_tn(f"up_wgrad{l}", sv["h2"], dup)
        (dxc, dz), dp = block_bwd(f"ln1_bwd{l}", f_ln, n, [T1(sv["xc"]), T1(sv["z"])], sv["ln1_p"],
                                  [T1(dx1), T1(dh2)], [fgrad, fgrad])
        dmod[l][2], g["ln1_g"][l], g["ln1_b"][l], dmod[l][3], dmod[l][4] = dp
        dy = dense_nt(f"out_bwd{l}", dz, w_out[l])
        g["w_out"][l] = dense_tn(f"out_wgrad{l}", sv["y"], dz)
        half = (S5W, F32, S5W, _c0)
        (dpu, dpg, dys, dos), dp = block_bwd(f"mix_bwd{l}", f_mix, n, sv["mix_t"], sv["mix_p"], [T1(dy)],
                                                   [half, half, half, None, half, None])
        g["s5_d"][l], g["w_glu"][l], g["b_glu"][l], g["hg_norm_w"][l] = dp[0][0, 0], dp[1][0], dp[2][0, 0], dp[3][0, 0]

        def start_scatter(tag, ks):
            by_cols = lambda a: jnp.moveaxis(a.reshape(a.shape[0], N_DEV, -1), 1, 0)
            by_rows = lambda a: a.reshape(N_DEV, -1, a.shape[1])
            sends = [(by_rows if k in ("w_glu", "w_out", "w_down") else by_cols)(g[k][l].astype(BF16 if k != "conv_w" else F32))
                     for k in ks]
            handle, token = exchange_start(f"scatter_start{l}{tag}", sends, [_own_block_set(s, True) for s in sends], True)
            scatters[l].append((ks, handle))
            return token[0, 0]

        lbs_b = sv["lbs"]
        if l == 0:
            started = start_scatter("a", ["w_glu", "w_out", "w_up", "w_down", "conv_w"])
            lbs_b = [b + started for b in lbs_b]
        gb = [gla_bwd(f"gla_bwd{l}_{d}", sv["proj"], lbs_b[d], sv["gl"][d][1], dos, d == 1) for d in range(2)]
        g["dlb"][l] = jnp.concatenate([gb[0][3], gb[1][3]], axis=-1)[0]
        sb = [s5_bwd(f"s5_bwd{l}_{d}", sv["proj"], dys, sv["s5"][d][0], sv["s5"][d][1], s5a, s5b, s5c, 2 * l + d, d == 1)
              for d in range(2)]
        g["s5"][l] = sb
        asm_t = [T1(dpu, S5W), T1(sb[0][0], S5W), T1(sb[1][0], S5W), T1(gb[0][0], HGW), T1(gb[1][0], HGW),
                 T1(gb[0][1], HGW), T1(gb[1][1], HGW), T1(gb[0][2], HGW), T1(gb[1][2], HGW), T1(dpg, HGW)]
        (dproj,) = block_fwd(
            f"dproj{l}", lambda tv, pv, i: (jnp.concatenate(
                [tv[0] + tv[1] + tv[2], tv[3], tv[4], tv[5] + tv[6], tv[7] + tv[8], tv[9]], axis=-1),),
            n, asm_t, [], [(INC, BF16, INC, _c0)])
        dh_next = dense_nt(f"in_bwd{l}", dproj, w_in[l])
        g["w_in"][l] = dense_tn(f"in_wgrad{l}", sv["h"], dproj)
        started = start_scatter("b", ["w_in"]) if l == 0 else start_scatter("", BIG + ["conv_w"])
        if l:
            gate, wd_, cb_ = saved[l - 1]["ln2_p"][0]
            saved[l - 1]["ln2_p"][0] = (gate + started, wd_, cb_)
    (dxc,), dp = block_bwd("mod0_bwd", f_mod_id, n, [T1(xc0)], [T1(mvec(0, 0)), T1(mvec(0, 1))],
                           [T1(dh_next), T1(dxc)], [fgrad])
    dmod[0][0], dmod[0][1] = dp
    grad_x = dxc[n - p["x"].shape[1]:][None]

    st = lambda k: jnp.stack([g["s5"][l][d][k] for l in range(nl) for d in range(2)])
    d_prep = s5_prep_bwd(*prep_in, st(1), st(2), st(3))
    from_hp = lambda a: jnp.moveaxis(a.reshape(nl, 2, S5H, S5W // S5H, 64), 2, -1)
    gs5 = {"s5_lam_re": d_prep[0].reshape(nl, 2, 32, 64), "s5_lam_im": d_prep[1].reshape(nl, 2, 32, 64),
           "s5_log_dt": d_prep[2].reshape(nl, 2, 32), "s5_b_re": from_hp(d_prep[3]), "s5_b_im": from_hp(d_prep[4]),
           "s5_c_re": jnp.swapaxes(d_prep[5].reshape(nl, 2, S5H, 32, 64), 2, 3),
           "s5_c_im": jnp.swapaxes(d_prep[6].reshape(nl, 2, S5H, 32, 64), 2, 3)}
    d_hg = lb_call(hg_full[0], jnp.stack(g["dlb"]))

    dmod_loc = jnp.stack([jnp.concatenate([dmod[l][k][:, 0] for k in range(6)], axis=-1) for l in range(nl)])
    (g_dmod,) = _exchange("gather_dmod", [dmod_loc], False)
    gcols = lax.dynamic_slice_in_dim(g_dmod, me * cols, cols, axis=3)
    g16 = jnp.concatenate([jnp.moveaxis(gcols[:, :, 1], 0, 1), jnp.moveaxis(gcols[:, :, 0], 0, 1)], axis=1)
    grad_w_mod, dcraw = mod_bwd(craw, p["w_mod"], g16)
    d_c_ctx = jnp.sum(dcraw[8:], axis=0)

    stk = lambda k: jnp.stack(g[k])
    small_g = {"c_ctx": d_c_ctx, "b_mod": dmod_loc[:, 0] + dmod_loc[:, 1], "s5_d": stk("s5_d"), "b_glu": stk("b_glu"),
               "hg_lb": d_hg.reshape(nl, 2, HGW), "hg_norm_w": stk("hg_norm_w"), "ln1_g": stk("ln1_g")[:, 0, 0],
               "ln1_b": stk("ln1_b")[:, 0, 0], "conv_b": stk("conv_b"), "ln2_g": stk("ln2_g")[:, 0, 0],
               "ln2_b": stk("ln2_b")[:, 0, 0], **gs5}
    g_pack = _pack([small_g[k] for k in SMALL])
    (g_parts,) = _exchange("scatter_small_grads", [g_pack.reshape(N_DEV, -1, PACK_W)], True)
    (g_small,) = _exchange("gather_small_grads", [sum_parts(g_parts)], False)
    g_small = g_small.reshape(1, -1, PACK_W)

    out = {}
    hgw = {"": hg_full[0].reshape(nl, 2, HGW), "m_": hg_full[1].reshape(nl, 2, HGW), "v_": hg_full[2].reshape(nl, 2, HGW)}
    full = lambda pre, k: hgw[pre] if k == "hg_lb" else p[pre + k]
    shapes = [full("", k).shape for k in SMALL]
    res = adamw("adamw_small", *[_pack([full(pre, k) for k in SMALL]) for pre in ("", "m_", "v_")], g_small)
    for kind, packed in zip(("grad_", "delta_", "new_m_", "new_v_"), res):
        for k, a in zip(SMALL, _unpack(packed, shapes)):
            if k == "hg_lb":
                a = lax.dynamic_slice_in_dim(a, me * (HGW // N_DEV), HGW // N_DEV, axis=2)
            out[kind + k] = a
    kinds = ("grad_", "delta_", "new_m_", "new_v_")
    res = adamw("adamw_w_mod", *[p[pre + "w_mod"].reshape(-1, cols) for pre in ("", "m_", "v_")],
                grad_w_mod.reshape(1, -1, cols))
    for kind, a in zip(kinds, res):
        out[kind + "w_mod"] = a.reshape(p["w_mod"].shape)
    per_layer = {k: [None] * nl for k in BIG + ["conv_w"]}
    for l in reversed(range(nl)):
        for i, (ks, handle) in enumerate(scatters[l]):
            recv = exchange_wait(f"scatter_wait{l}_{i}", handle, g_small, True)
            for k, gsum in zip(ks, recv):
                per_layer[k][l] = adamw(f"adamw_{k}{l}", p[k][l], p["m_" + k][l], p["v_" + k][l], gsum)
    for k, res_l in per_layer.items():
        for i, kind in enumerate(kinds):
            out[kind + k] = jnp.stack([r[i] for r in res_l])
    return (loss, grad_x, *[out[kind + k] for kind in ("grad_", "delta_", "new_m_", "new_v_") for k in WEIGHTS])


def kernel(x, c, ctx, c_ctx, w_mod, b_mod, w_in, s5_lam_re, s5_lam_im, s5_log_dt, s5_b_re, s5_b_im, s5_c_re, s5_c_im, s5_d, w_glu, b_glu, hg_lb, hg_norm_w, w_out, ln1_g, ln1_b, w_up, conv_w, conv_b, w_down, ln2_g, ln2_b, loss_target, m_c_ctx, m_w_mod, m_b_mod, m_w_in, m_s5_lam_re, m_s5_lam_im, m_s5_log_dt, m_s5_b_re, m_s5_b_im, m_s5_c_re, m_s5_c_im, m_s5_d, m_w_glu, m_b_glu, m_hg_lb, m_hg_norm_w, m_w_out, m_ln1_g, m_ln1_b, m_w_up, m_conv_w, m_conv_b, m_w_down, m_ln2_g, m_ln2_b, v_c_ctx, v_w_mod, v_b_mod, v_w_in, v_s5_lam_re, v_s5_lam_im, v_s5_log_dt, v_s5_b_re, v_s5_b_im, v_s5_c_re, v_s5_c_im, v_s5_d, v_w_glu, v_b_glu, v_hg_lb, v_hg_norm_w, v_w_out, v_ln1_g, v_ln1_b, v_w_up, v_conv_w, v_conv_b, v_w_down, v_ln2_g, v_ln2_b):
    return _step(dict(locals()))
```

```python
import functools
import math

import jax
import jax.numpy as jnp
from jax import lax
from jax.experimental import pallas as pl
from jax.experimental.pallas import tpu as pltpu

F32, BF16 = jnp.float32, jnp.bfloat16
N_DEV = 8
AXES = ("x", "y", "c")
D = 1024
S5W = 512
S5P = 2048
S5H = 16
HGW = 512
HD = 128
NH = 4
CK = 32
DFF = 2816
GRID_W = 64
INC = 3072
ALPHA = 8.0 ** 0.25
LN_EPS = 1e-5
RMS_EPS = 1e-6
LR, B1, B2, EPS, WD, STEP = 0.001, 0.9, 0.999, 1e-08, 0.01, 10
TT = 256
VMEM_MB = 56

NN = ((1,), (0,))
NT = ((1,), (1,))
TN = ((0,), (0,))


def _cparams(n_axes):
    return pltpu.CompilerParams(dimension_semantics=("arbitrary",) * n_axes, vmem_limit_bytes=VMEM_MB << 20)


def _dot(a, b, dims):
    return lax.dot_general(a.astype(BF16), b.astype(BF16), (dims, ((), ())), preferred_element_type=F32)


@jax.custom_vjp
def mm_nn(a, b):
    return _dot(a, b, NN)


@jax.custom_vjp
def mm_nt(a, b):
    return _dot(a, b, NT)


@jax.custom_vjp
def mm_tn(a, b):
    return _dot(a, b, TN)


mm_nn.defvjp(lambda a, b: (_dot(a, b, NN), (a, b)), lambda r, g: (_dot(g, r[1], NT), _dot(r[0], g, TN)))
mm_nt.defvjp(lambda a, b: (_dot(a, b, NT), (a, b)), lambda r, g: (_dot(g, r[1], NN), _dot(g, r[0], TN)))
mm_tn.defvjp(lambda a, b: (_dot(a, b, TN), (a, b)), lambda r, g: (_dot(r[1], g, NT), _dot(r[0], g, NN)))


def _roll_rows(u, s):
    return pltpu.roll(u, s % u.shape[0], 0)


@functools.partial(jax.custom_vjp, nondiff_argnums=(2,))
def shift_rows(u, m, s):
    return _roll_rows(u, s) * m


def _shift_fwd(u, m, s):
    return _roll_rows(u, s) * m, m


def _shift_bwd(s, m, g):
    return _roll_rows(g * m, -s), jnp.zeros_like(m)


shift_rows.defvjp(_shift_fwd, _shift_bwd)


def _scan_tile(pos, nt, rev):
    return jnp.where(pos == 0, 0, nt - pos) if rev else pos


def _exchange(name, arrays, all_to_all):
    k_arr = len(arrays)

    def body(*refs):
        ins, outs = refs[:k_arr], refs[k_arr:2 * k_arr]
        send_sems, recv_sems, local_sems = refs[2 * k_arr:]
        me = lax.axis_index("x") * 4 + lax.axis_index("y") * 2 + lax.axis_index("c")
        local = []
        for k in range(k_arr):
            cp = pltpu.make_async_copy(ins[k].at[me] if all_to_all else ins[k], outs[k].at[me], local_sems.at[k])
            cp.start()
            local.append(cp)
        sends = []
        for d in range(1, N_DEV):
            p = (me + d) % N_DEV
            for k in range(k_arr):
                cp = pltpu.make_async_remote_copy(
                    src_ref=ins[k].at[p] if all_to_all else ins[k], dst_ref=outs[k].at[me],
                    send_sem=send_sems.at[k, d - 1], recv_sem=recv_sems.at[k, d - 1],
                    device_id=(p // 4, (p // 2) % 2, p % 2), device_id_type=pl.DeviceIdType.MESH)
                cp.start()
                sends.append(cp)
        for d in range(1, N_DEV):
            q = (me + N_DEV - d) % N_DEV
            for k in range(k_arr):
                pltpu.make_async_remote_copy(
                    src_ref=ins[k].at[q] if all_to_all else ins[k], dst_ref=outs[k].at[q],
                    send_sem=send_sems.at[k, d - 1], recv_sem=recv_sems.at[k, d - 1],
                    device_id=(q // 4, (q // 2) % 2, q % 2), device_id_type=pl.DeviceIdType.MESH).wait_recv()
        for cp in sends:
            cp.wait_send()
        for cp in local:
            cp.wait()

    shapes = [a.shape if all_to_all else (N_DEV,) + a.shape for a in arrays]
    return pl.pallas_call(
        body, name=name,
        out_shape=[jax.ShapeDtypeStruct(s, a.dtype) for s, a in zip(shapes, arrays)],
        in_specs=[pl.BlockSpec(memory_space=pl.ANY)] * k_arr,
        out_specs=[pl.BlockSpec(memory_space=pl.ANY)] * k_arr,
        scratch_shapes=[pltpu.SemaphoreType.DMA((k_arr, N_DEV - 1)), pltpu.SemaphoreType.DMA((k_arr, N_DEV - 1)),
                        pltpu.SemaphoreType.DMA((k_arr,))],
    )(*arrays)


_HBM = pl.BlockSpec(memory_space=pltpu.HBM)
_SEM = pl.BlockSpec(memory_space=pltpu.SEMAPHORE)
_EFFECT = pltpu.SideEffectType.DATAFLOW_SIDE_EFFECTING


def _peer(i):
    return (i // 4, (i // 2) % 2, i % 2)


def exchange_start(name, srcs, lands, all_to_all, after=None):
    k_arr = len(srcs)
    n_sem = k_arr * (N_DEV - 1)
    extra = [] if after is None else [after]

    def body(*refs):
        ins, lz = refs[:k_arr], refs[k_arr:2 * k_arr]
        first = 2 * k_arr + len(extra)
        send_sems = refs[first:first + n_sem]
        recv_sems = refs[first + n_sem:first + 2 * n_sem]
        me = lax.axis_index("x") * 4 + lax.axis_index("y") * 2 + lax.axis_index("c")
        for d in range(1, N_DEV):
            p = (me + d) % N_DEV
            for k in range(k_arr):
                s = k * (N_DEV - 1) + d - 1
                pltpu.make_async_remote_copy(
                    src_ref=ins[k].at[p] if all_to_all else ins[k], dst_ref=lz[k].at[me],
                    send_sem=send_sems[s], recv_sem=recv_sems[s],
                    device_id=_peer(p), device_id_type=pl.DeviceIdType.MESH).start()
        refs[-1][...] = jnp.zeros_like(refs[-1])

    arrs = list(srcs) + list(lands)
    res = pl.pallas_call(
        body, name=name,
        out_shape=(*[pltpu.SemaphoreType.DMA(())] * (2 * n_sem), *[pltpu.HBM(a.shape, a.dtype) for a in arrs],
                   jax.ShapeDtypeStruct((8, 128), F32)),
        in_specs=[_HBM] * len(arrs) + [pl.BlockSpec(memory_space=pl.ANY)] * len(extra),
        out_specs=(*[_SEM] * (2 * n_sem), *[_HBM] * len(arrs), pl.BlockSpec(memory_space=pltpu.VMEM)),
        input_output_aliases={i: 2 * n_sem + i for i in range(len(arrs))},
        compiler_params=pltpu.CompilerParams(has_side_effects=_EFFECT),
    )(*[pltpu.with_memory_space_constraint(a, pltpu.HBM) for a in arrs], *extra)
    return res[:-1], res[-1]


def exchange_wait(name, handle, after, all_to_all):
    k_arr = len(handle) // (2 * N_DEV)
    n_sem = k_arr * (N_DEV - 1)
    sems, arrs = handle[:2 * n_sem], handle[2 * n_sem:]

    def body(*refs):
        ins, lz = refs[:k_arr], refs[k_arr:2 * k_arr]
        s_sems = refs[2 * k_arr:2 * k_arr + n_sem]
        r_sems = refs[2 * k_arr + n_sem:2 * k_arr + 2 * n_sem]
        me = lax.axis_index("x") * 4 + lax.axis_index("y") * 2 + lax.axis_index("c")
        for d in range(1, N_DEV):
            q = (me + N_DEV - d) % N_DEV
            for k in range(k_arr):
                s = k * (N_DEV - 1) + d - 1
                cp = pltpu.make_async_remote_copy(
                    src_ref=ins[k].at[q] if all_to_all else ins[k], dst_ref=lz[k].at[q],
                    send_sem=s_sems[s], recv_sem=r_sems[s],
                    device_id=_peer(q), device_id_type=pl.DeviceIdType.MESH)
                cp.wait_send()
                cp.wait_recv()

    res = pl.pallas_call(
        body, name=name, out_shape=tuple(pltpu.HBM(a.shape, a.dtype) for a in arrs),
        in_specs=[_HBM] * len(arrs) + [_SEM] * (2 * n_sem) + [pl.BlockSpec(memory_space=pl.ANY)],
        out_specs=tuple([_HBM] * len(arrs)),
        input_output_aliases={i: i for i in range(len(arrs))},
        compiler_params=pltpu.CompilerParams(has_side_effects=_EFFECT),
    )(*arrs, *sems, after)
    return res[k_arr:]


def _own_block_set(src, all_to_all):
    me = lax.axis_index("x") * 4 + lax.axis_index("y") * 2 + lax.axis_index("c")
    own = lax.dynamic_index_in_dim(src, me, 0, keepdims=False) if all_to_all else src
    shape = src.shape if all_to_all else (N_DEV,) + src.shape
    return lax.dynamic_update_index_in_dim(lax.empty(shape, src.dtype), own, me, 0)


def _tile(n, prefs):
    for t in prefs:
        if n % t == 0:
            return t
    raise ValueError(n)


def dense_nn(name, a, w, out_dtype=F32):
    n, k = a.shape
    m = w.shape[1]
    tn, tm = _tile(n, (1088, 256)), _tile(m, (1024, 512, 256, 128))

    def body(a_ref, w_ref, o_ref):
        o_ref[...] = _dot(a_ref[...], w_ref[...], NN).astype(o_ref.dtype)

    return pl.pallas_call(
        body, name=name, grid=(m // tm, n // tn), out_shape=jax.ShapeDtypeStruct((n, m), out_dtype),
        in_specs=[pl.BlockSpec((tn, k), lambda j, i: (i, 0)), pl.BlockSpec((k, tm), lambda j, i: (0, j))],
        out_specs=pl.BlockSpec((tn, tm), lambda j, i: (i, j)), compiler_params=_cparams(2))(a, w)


def dense_nt(name, g, w, out_dtype=F32):
    n, m = g.shape
    k = w.shape[0]
    tn, tk = _tile(n, (544, 256)), _tile(k, (1024, 1408, 512, 256, 128))

    def body(g_ref, w_ref, o_ref):
        o_ref[...] = _dot(g_ref[...], w_ref[...], NT).astype(o_ref.dtype)

    return pl.pallas_call(
        body, name=name, grid=(k // tk, n // tn), out_shape=jax.ShapeDtypeStruct((n, k), out_dtype),
        in_specs=[pl.BlockSpec((tn, m), lambda j, i: (i, 0)), pl.BlockSpec((tk, m), lambda j, i: (j, 0))],
        out_specs=pl.BlockSpec((tn, tk), lambda j, i: (i, j)), compiler_params=_cparams(2))(g, w)


def dense_tn(name, a, g, out_dtype=BF16):
    n, k = a.shape
    m = g.shape[1]
    tn = _tile(n, (1088, 256))
    tk = _tile(k, (1024, 1408, 512, 256, 128))
    tm = _tile(m, (1024, 512, 256, 128))
    nt = n // tn

    def body(a_ref, g_ref, o_ref, acc_ref):
        t = pl.program_id(2)

        @pl.when(t == 0)
        def _():
            acc_ref[...] = jnp.zeros_like(acc_ref)

        acc_ref[...] += _dot(a_ref[...], g_ref[...], TN)

        @pl.when(t == nt - 1)
        def _():
            o_ref[...] = acc_ref[...].astype(o_ref.dtype)

    return pl.pallas_call(
        body, name=name, grid=(k // tk, m // tm, nt), out_shape=jax.ShapeDtypeStruct((k, m), out_dtype),
        in_specs=[pl.BlockSpec((tn, tk), lambda i, j, t: (t, i)), pl.BlockSpec((tn, tm), lambda i, j, t: (t, j))],
        out_specs=pl.BlockSpec((tk, tm), lambda i, j, t: (i, j)),
        scratch_shapes=[pltpu.VMEM((tk, tm), F32)], compiler_params=_cparams(3))(a, g)


def _c0(j):
    return 0


def _tspec(w, cb):
    return pl.BlockSpec((TT, w), lambda j, i: (i, cb(j)))


def _pspec(arr, w, cb):
    two = arr.shape[0] == 2
    return pl.BlockSpec((None, arr.shape[1], w), lambda j, i: (jnp.minimum(i, 1) if two else 0, 0, cb(j)))


def block_fwd(name, fn, n, tiled, params, outs, n_col=1):
    nt_, np_ = len(tiled), len(params)

    def body(*refs):
        i = pl.program_id(1)
        tv = [r[...].astype(F32) for r in refs[:nt_]]
        pv = [r[...].astype(F32) for r in refs[nt_:nt_ + np_]]
        for o_ref, r in zip(refs[nt_ + np_:], fn(tv, pv, i)):
            o_ref[...] = r.astype(o_ref.dtype)

    return pl.pallas_call(
        body, name=name, grid=(n_col, n // TT),
        out_shape=[jax.ShapeDtypeStruct((n, c), dt) for c, dt, _, _ in outs],
        in_specs=[_tspec(w, cb) for _, w, cb in tiled] + [_pspec(a, w, cb) for a, w, cb in params],
        out_specs=[_tspec(w, cb) for _, _, w, cb in outs], compiler_params=_cparams(2),
    )(*[a for a, _, _ in tiled], *[a for a, _, _ in params])


def block_bwd(name, fn, n, tiled, params, cots, grads, n_col=1):
    nt_, np_, nc_ = len(tiled), len(params), len(cots)
    want = [k for k, g in enumerate(grads) if g is not None]

    def body(*refs):
        i = pl.program_id(1)
        tv = [r[...].astype(F32) for r in refs[:nt_]]
        pv = [r[...].astype(F32) for r in refs[nt_:nt_ + np_]]
        cv = [r[...].astype(F32) for r in refs[nt_ + np_:nt_ + np_ + nc_]]
        o_refs = refs[nt_ + np_ + nc_:]
        _, vjp = jax.vjp(lambda t, p: list(fn(t, p, i)), tv, pv)
        dt, dp = vjp(cv)
        for o_ref, k in zip(o_refs, want):
            o_ref[...] = dt[k].astype(o_ref.dtype)
        for o_ref, g, (arr, _, _) in zip(o_refs[len(want):], dp, params):
            first = (i == 0) | (i == 1) if arr.shape[0] == 2 else i == 0

            @pl.when(first)
            def _(o_ref=o_ref):
                o_ref[...] = jnp.zeros_like(o_ref)

            o_ref[...] += g

    res = pl.pallas_call(
        body, name=name, grid=(n_col, n // TT),
        out_shape=[jax.ShapeDtypeStruct((n, grads[k][0]), grads[k][1]) for k in want]
        + [jax.ShapeDtypeStruct(a.shape, F32) for a, _, _ in params],
        in_specs=[_tspec(w, cb) for _, w, cb in tiled] + [_pspec(a, w, cb) for a, w, cb in params]
        + [_tspec(w, cb) for _, w, cb in cots],
        out_specs=[_tspec(grads[k][2], grads[k][3]) for k in want] + [_pspec(a, w, cb) for a, w, cb in params],
        compiler_params=_cparams(2),
    )(*[a for a, _, _ in tiled], *[a for a, _, _ in params], *[a for a, _, _ in cots])
    return res[:len(want)], res[len(want):]


def f_mod(tv, pv, i):
    (x,), (sh, sc) = tv, pv
    return (x * (1.0 + sc) + sh,)


def f_mod_id(tv, pv, i):
    return (f_mod(tv, pv, i)[0], tv[0])


def f_ln(tv, pv, i):
    (x, z), (gate, g, b, sh, sc) = tv, pv
    pre = ALPHA * x + gate * z
    mu = jnp.mean(pre, axis=-1, keepdims=True)
    var = jnp.mean(jnp.square(pre - mu), axis=-1, keepdims=True)
    xn = (pre - mu) * lax.rsqrt(var + LN_EPS) * g + b
    return xn, xn * (1.0 + sc) + sh


def f_mix(tv, pv, i):
    (pu, pg, y0, y1, o0, o1), (d_skip, w_glu, b_glu, norm_w) = tv, pv
    s5y = jax.nn.gelu(y0 + y1 + pu * d_skip)
    s5o = s5y * jax.nn.sigmoid(mm_nn(s5y, w_glu) + b_glu)
    o = o0 + o1
    heads = []
    for h in range(NH):
        oh = o[:, h * HD:(h + 1) * HD]
        heads.append(oh * lax.rsqrt(jnp.mean(jnp.square(oh), axis=-1, keepdims=True) + RMS_EPS) * norm_w)
    hg = jnp.concatenate(heads, axis=-1) * jax.nn.silu(pg)
    return (jnp.concatenate([s5o, hg], axis=-1),)


def f_act(tv, pv, i):
    (ua, ug), (cwa, cwg, cba, cbg) = tv, pv
    t = lax.broadcasted_iota(jnp.int32, (TT, 1), 0)
    lat = i > 0
    m_dn = jnp.where((t == 0) | (lat & (t % GRID_W == 0)), 0.0, 1.0)
    m_up = jnp.where((t == TT - 1) | (lat & (t % GRID_W == GRID_W - 1)), 0.0, 1.0)

    def conv(u, w, b):
        return shift_rows(u, m_dn, 1) * w[0:1] + u * w[1:2] + shift_rows(u, m_up, -1) * w[2:3] + b

    return (jax.nn.silu(conv(ua, cwa, cba)) * conv(ug, cwg, cbg),)


def loss_and_grad(xf, target):
    n = xf.shape[0]

    def body(x_ref, t_ref, dy_ref, l_ref):
        i = pl.program_id(0)

        @pl.when(i == 0)
        def _():
            l_ref[...] = jnp.zeros_like(l_ref)
            dy_ref[...] = jnp.zeros_like(dy_ref)

        @pl.when(i > 0)
        def _():
            e = x_ref[...] - t_ref[...]
            dy_ref[...] = e * (1.0 / D)
            l_ref[...] += 0.5 / D * jnp.sum(jnp.square(e))

    return pl.pallas_call(
        body, name="loss", grid=(n // TT,),
        out_shape=[jax.ShapeDtypeStruct((n, D), F32), jax.ShapeDtypeStruct((8, 128), F32)],
        in_specs=[pl.BlockSpec((TT, D), lambda i: (i, 0)), pl.BlockSpec((TT, D), lambda i: (jnp.maximum(i - 1, 0), 0))],
        out_specs=[pl.BlockSpec((TT, D), lambda i: (i, 0)), pl.BlockSpec((8, 128), lambda i: (0, 0))],
        compiler_params=_cparams(1))(xf, target)


def f_prep(lr, li, ldt, bre, bim, cre, cim):
    gi = lax.broadcasted_iota(jnp.int32, (S5W // S5H, S5P), 0)
    gc = lax.broadcasted_iota(jnp.int32, (S5W // S5H, S5P), 1) // 64
    dt = jnp.exp(jnp.sum(jnp.where(gi == gc, ldt, 0.0), axis=0, keepdims=True))
    mag, ang = jnp.exp(lr * dt), li * dt
    ar, ai = mag * jnp.cos(ang), mag * jnp.sin(ang)
    den = lr * lr + li * li
    nr, ni = ar - 1.0, ai
    cr = (nr * lr + ni * li) / den
    ci = (ni * lr - nr * li) / den
    bbr = cr * bre - ci * bim
    bbi = cr * bim + ci * bre
    rg = lax.broadcasted_iota(jnp.int32, (S5W, S5P), 0) // S5H
    cg = lax.broadcasted_iota(jnp.int32, (S5W, S5P), 1) // 64
    mask = (rg == cg).astype(F32)
    blk = lambda a: jnp.concatenate([a] * (S5W // S5H), axis=0) * mask
    return ar, ai, blk(bbr), blk(bbi), blk(cre), blk(-cim)


def s5_prep(lr, li, ldt, bre, bim, cre, cim):
    n2 = lr.shape[0]

    def body(lr_r, li_r, ldt_r, bre_r, bim_r, cre_r, cim_r, a_ref, b_ref, c_ref):
        ar, ai, bbr, bbi, cbr, cbi = f_prep(lr_r[...], li_r[...], ldt_r[...], bre_r[...], bim_r[...], cre_r[...], cim_r[...])
        a_ref[0], a_ref[1] = ar, ai
        b_ref[0], b_ref[1] = bbr.astype(BF16), bbi.astype(BF16)
        c_ref[0], c_ref[1] = cbr.astype(BF16), cbi.astype(BF16)

    sp = lambda r, c: pl.BlockSpec((None, r, c), lambda i: (i, 0, 0))
    sp4 = lambda r, c: pl.BlockSpec((None, 2, r, c), lambda i: (i, 0, 0, 0))
    return pl.pallas_call(
        body, name="s5_prep", grid=(n2,),
        out_shape=[jax.ShapeDtypeStruct((n2, 2, 1, S5P), F32), jax.ShapeDtypeStruct((n2, 2, S5W, S5P), BF16),
                   jax.ShapeDtypeStruct((n2, 2, S5W, S5P), BF16)],
        in_specs=[sp(1, S5P), sp(1, S5P), sp(32, 1), sp(S5H, S5P), sp(S5H, S5P), sp(S5H, S5P), sp(S5H, S5P)],
        out_specs=[sp4(1, S5P), sp4(S5W, S5P), sp4(S5W, S5P)], compiler_params=_cparams(1),
    )(lr, li, ldt, bre, bim, cre, cim)


def s5_prep_bwd(lr, li, ldt, bre, bim, cre, cim, da, db, dc):
    n2 = lr.shape[0]

    def body(lr_r, li_r, ldt_r, bre_r, bim_r, cre_r, cim_r, da_r, db_r, dc_r, *outs):
        args = [r[...] for r in (lr_r, li_r, ldt_r, bre_r, bim_r, cre_r, cim_r)]
        _, vjp = jax.vjp(f_prep, *args)
        for o_ref, g in zip(outs, vjp((da_r[0], da_r[1], db_r[0], db_r[1], dc_r[0], dc_r[1]))):
            o_ref[...] = g

    sp = lambda r, c: pl.BlockSpec((None, r, c), lambda i: (i, 0, 0))
    sp4 = lambda r, c: pl.BlockSpec((None, 2, r, c), lambda i: (i, 0, 0, 0))
    ins = [sp(1, S5P), sp(1, S5P), sp(32, 1), sp(S5H, S5P), sp(S5H, S5P), sp(S5H, S5P), sp(S5H, S5P)]
    return pl.pallas_call(
        body, name="s5_prep_bwd", grid=(n2,),
        out_shape=[jax.ShapeDtypeStruct(a.shape, F32) for a in (lr, li, ldt, bre, bim, cre, cim)],
        in_specs=ins + [sp4(1, S5P), sp4(S5W, S5P), sp4(S5W, S5P)], out_specs=ins, compiler_params=_cparams(1),
    )(lr, li, ldt, bre, bim, cre, cim, da, db, dc)


S5_DIAG = 2
_CU, _CP = S5W // S5_DIAG, S5P // S5_DIAG


def _bd_nn(u, w_ref, k):
    return jnp.concatenate([_dot(u[:, j * _CU:(j + 1) * _CU], w_ref[k, j * _CU:(j + 1) * _CU, j * _CP:(j + 1) * _CP], NN)
                            for j in range(S5_DIAG)], axis=1)


def _bd_nt(x, w_ref, k):
    return jnp.concatenate([_dot(x[:, j * _CP:(j + 1) * _CP], w_ref[k, j * _CU:(j + 1) * _CU, j * _CP:(j + 1) * _CP], NT)
                            for j in range(S5_DIAG)], axis=1)


def _bd_tn_acc(acc_ref, k, a, g):
    for j in range(S5_DIAG):
        acc_ref[k, j * _CU:(j + 1) * _CU, j * _CP:(j + 1) * _CP] += _dot(a[:, j * _CU:(j + 1) * _CU],
                                                                         g[:, j * _CP:(j + 1) * _CP], TN)


def _scan_rows(xr_ref, xi_ref, ar, ai, desc, cr_ref, ci_ref):
    unroll = 8

    def group(gi, carry):
        cr, ci = carry
        base = gi * unroll
        for j in range(unroll):
            t = TT - 1 - (base + j) if desc else base + j
            nr = ar * cr - ai * ci + xr_ref[pl.ds(t, 1), :]
            ni = ar * ci + ai * cr + xi_ref[pl.ds(t, 1), :]
            xr_ref[pl.ds(t, 1), :] = nr
            xi_ref[pl.ds(t, 1), :] = ni
            cr, ci = nr, ni
        return cr, ci

    cr, ci = lax.fori_loop(0, TT // unroll, group, (cr_ref[...], ci_ref[...]))
    cr_ref[...] = cr
    ci_ref[...] = ci


def s5_fwd(name, proj, a, bb, cb, ld, rev):
    n = proj.shape[0]
    nt = n // TT

    def body(u_ref, a_ref, b_ref, c_ref, xr_ref, xi_ref, y_ref, cr_ref, ci_ref):
        @pl.when(pl.program_id(0) == 0)
        def _():
            cr_ref[...] = jnp.zeros_like(cr_ref)
            ci_ref[...] = jnp.zeros_like(ci_ref)

        u = u_ref[...]
        xr_ref[...] = _bd_nn(u, b_ref, 0)
        xi_ref[...] = _bd_nn(u, b_ref, 1)
        _scan_rows(xr_ref, xi_ref, a_ref[0], a_ref[1], rev, cr_ref, ci_ref)
        y_ref[...] = _bd_nt(xr_ref[...], c_ref, 0) + _bd_nt(xi_ref[...], c_ref, 1)

    tile = lambda w: pl.BlockSpec((TT, w), lambda s: (_scan_tile(s, nt, rev), 0))
    par = lambda r: pl.BlockSpec((None, 2, r, S5P), lambda s: (ld, 0, 0, 0))
    return pl.pallas_call(
        body, name=name, grid=(nt,),
        out_shape=[jax.ShapeDtypeStruct((n, S5P), F32), jax.ShapeDtypeStruct((n, S5P), F32),
                   jax.ShapeDtypeStruct((n, S5W), F32)],
        in_specs=[tile(S5W), par(1), par(S5W), par(S5W)], out_specs=[tile(S5P), tile(S5P), tile(S5W)],
        scratch_shapes=[pltpu.VMEM((1, S5P), F32), pltpu.VMEM((1, S5P), F32)], compiler_params=_cparams(1),
    )(proj, a, bb, cb)


def s5_bwd(name, proj, dy, xr, xi, a, bb, cb, ld, rev, totals):
    n = proj.shape[0]
    nt = n // TT
    tb = TT // 8

    def tile_of(s):
        return _scan_tile(nt - 1 - s, nt, rev)

    def edge_of(s):
        pos = nt - 1 - s
        prev = _scan_tile(jnp.maximum(pos - 1, 0), nt, rev)
        return prev * tb if rev else jnp.maximum(pos * tb - 1, 0)

    def body(u_ref, dy_ref, xr_ref, xi_ref, er_ref, ei_ref, a_ref, b_ref, c_ref, _ta, _tb, _tc,
             du_ref, da_ref, db_ref, dc_ref, gr_ref, gi_ref, cr_ref, ci_ref):
        s = pl.program_id(0)

        @pl.when(s == 0)
        def _():
            cr_ref[...] = jnp.zeros_like(cr_ref)
            ci_ref[...] = jnp.zeros_like(ci_ref)
            da_ref[...] = jnp.zeros_like(da_ref)
            db_ref[...] = jnp.zeros_like(db_ref)
            dc_ref[...] = jnp.zeros_like(dc_ref)

        dyv, u = dy_ref[...], u_ref[...]
        xrv, xiv = xr_ref[...], xi_ref[...]
        gr_ref[...] = _bd_nn(dyv, c_ref, 0)
        gi_ref[...] = _bd_nn(dyv, c_ref, 1)
        _bd_tn_acc(dc_ref, 0, dyv, xrv)
        _bd_tn_acc(dc_ref, 1, dyv, xiv)
        _scan_rows(gr_ref, gi_ref, a_ref[0], -a_ref[1], not rev, cr_ref, ci_ref)
        g_r, g_i = gr_ref[...], gi_ref[...]
        rows = lax.broadcasted_iota(jnp.int32, (TT, 1), 0)
        live = jnp.where(s == nt - 1, 0.0, 1.0)
        if rev:
            pr = jnp.where(rows == TT - 1, er_ref[0:1, :] * live, _roll_rows(xrv, -1))
            pi = jnp.where(rows == TT - 1, ei_ref[0:1, :] * live, _roll_rows(xiv, -1))
        else:
            pr = jnp.where(rows == 0, er_ref[7:8, :] * live, _roll_rows(xrv, 1))
            pi = jnp.where(rows == 0, ei_ref[7:8, :] * live, _roll_rows(xiv, 1))
        da_ref[0] += jnp.sum(g_r * pr + g_i * pi, axis=0, keepdims=True)
        da_ref[1] += jnp.sum(g_i * pr - g_r * pi, axis=0, keepdims=True)
        du_ref[...] = _bd_nt(g_r, b_ref, 0) + _bd_nt(g_i, b_ref, 1)
        _bd_tn_acc(db_ref, 0, u, g_r)
        _bd_tn_acc(db_ref, 1, u, g_i)

    tile = lambda w: pl.BlockSpec((TT, w), lambda s: (tile_of(s), 0))
    edge = pl.BlockSpec((8, S5P), lambda s: (edge_of(s), 0))
    par = lambda r: pl.BlockSpec((None, 2, r, S5P), lambda s: (ld, 0, 0, 0))
    whole = pl.BlockSpec(memory_space=pl.ANY)
    return pl.pallas_call(
        body, name=name, grid=(nt,),
        out_shape=[jax.ShapeDtypeStruct((n, S5W), F32)] + [jax.ShapeDtypeStruct(t.shape, F32) for t in totals],
        in_specs=[tile(S5W), tile(S5W), tile(S5P), tile(S5P), edge, edge, par(1), par(S5W), par(S5W), whole, whole, whole],
        out_specs=[tile(S5W), par(1), par(S5W), par(S5W)], input_output_aliases={9: 1, 10: 2, 11: 3},
        scratch_shapes=[pltpu.VMEM((TT, S5P), F32), pltpu.VMEM((TT, S5P), F32),
                        pltpu.VMEM((1, S5P), F32), pltpu.VMEM((1, S5P), F32)], compiler_params=_cparams(1),
    )(proj, dy, xr, xi, xr, xi, a, bb, cb, *totals)


def gla_tile(r, v, qr, lb, sts, rev):
    ncc = TT // CK
    f = lb + (1.0 - lb) * jax.nn.sigmoid(r)
    k, lf, q = 1.0 - f, jnp.log(f), jax.nn.silu(qr)
    rows = lax.broadcasted_iota(jnp.int32, (TT, 1), 0)
    pos = rows % CK
    b = lf
    for s in (1, 2, 4, 8, 16):
        m = ((pos < CK - s) if rev else (pos >= s)).astype(F32)
        b = b + shift_rows(b, m, -s if rev else s)
    etot = [jnp.sum(lf[c * CK:(c + 1) * CK], axis=0, keepdims=True) for c in range(ncc)]
    e = jnp.concatenate([jnp.broadcast_to(t, (CK, HGW)) for t in etot], axis=0)
    kd, qe, qa = k * jnp.exp(e - b), q * jnp.exp(b), q * jnp.exp(b - e)
    cm = [(rows // CK == c).astype(F32) for c in range(ncc)]
    r2 = lax.broadcasted_iota(jnp.int32, (TT, TT), 0)
    c2 = lax.broadcasted_iota(jnp.int32, (TT, TT), 1)
    amask = (r2 // CK == c2 // CK) & ((r2 <= c2) if rev else (r2 >= c2))
    outs, new_sts = [], []
    for h in range(NH):
        ln = slice(h * HD, (h + 1) * HD)
        kdh, qeh, vh = kd[:, ln], qe[:, ln], v[:, ln]
        att = jnp.where(amask, mm_nt(qa[:, ln], kdh), 0.0)
        ds = mm_tn(jnp.concatenate([kdh * cm[c] for c in range(ncc)], axis=1), vh)
        st, starts = sts[h], [None] * ncc
        for c in (reversed(range(ncc)) if rev else range(ncc)):
            starts[c] = st
            dec = jnp.transpose(jnp.broadcast_to(jnp.exp(etot[c][:, ln]), (HD, HD)))
            st = dec * st + ds[c * HD:(c + 1) * HD]
        new_sts.append(st)
        qex = jnp.concatenate([qeh * cm[c] for c in range(ncc)], axis=1)
        outs.append(mm_nn(att, vh) + mm_nn(qex, jnp.concatenate(starts, axis=0)))
    return jnp.concatenate(outs, axis=1), new_sts


def _gla_specs(n, rev, order):
    nt = n // TT
    fcol = 2 if rev else 1
    tile = lambda cbk: pl.BlockSpec((TT, HGW), lambda s: (order(s), cbk))
    return nt, [tile(fcol), tile(3), tile(4)], tile(0)


def gla_fwd(name, proj, lb, rev):
    n = proj.shape[0]
    nt, in_tiles, out_tile = _gla_specs(n, rev, lambda s: _scan_tile(s, n // TT, rev))

    def body(r_ref, v_ref, q_ref, lb_ref, o_ref, st_ref, s_ref):
        @pl.when(pl.program_id(0) == 0)
        def _():
            s_ref[...] = jnp.zeros_like(s_ref)

        sts = [s_ref[h] for h in range(NH)]
        for h in range(NH):
            st_ref[h] = sts[h]
        o, new = gla_tile(r_ref[...], v_ref[...], q_ref[...], lb_ref[...], sts, rev)
        o_ref[...] = o
        for h in range(NH):
            s_ref[h] = new[h]

    st_spec = pl.BlockSpec((None, NH, HD, HD), lambda s: (_scan_tile(s, nt, rev), 0, 0, 0))
    return pl.pallas_call(
        body, name=name, grid=(nt,),
        out_shape=[jax.ShapeDtypeStruct((n, HGW), F32), jax.ShapeDtypeStruct((nt, NH, HD, HD), F32)],
        in_specs=in_tiles + [pl.BlockSpec((1, HGW), lambda s: (0, 0))], out_specs=[out_tile, st_spec],
        scratch_shapes=[pltpu.VMEM((NH, HD, HD), F32)], compiler_params=_cparams(1),
    )(proj, proj, proj, lb)


def gla_bwd(name, proj, lb, st_all, do, rev):
    n = proj.shape[0]
    order = lambda s: _scan_tile(n // TT - 1 - s, n // TT, rev)
    nt, in_tiles, out_tile = _gla_specs(n, rev, order)

    def body(r_ref, v_ref, q_ref, lb_ref, st_ref, do_ref, dr_ref, dv_ref, dq_ref, dlb_ref, ds_ref):
        @pl.when(pl.program_id(0) == 0)
        def _():
            ds_ref[...] = jnp.zeros_like(ds_ref)
            dlb_ref[...] = jnp.zeros_like(dlb_ref)

        _, vjp = jax.vjp(functools.partial(gla_tile, rev=rev), r_ref[...], v_ref[...], q_ref[...], lb_ref[...],
                         [st_ref[h] for h in range(NH)])
        dr, dv, dq, dlb, dsts = vjp((do_ref[...], [ds_ref[h] for h in range(NH)]))
        dr_ref[...] = dr
        dv_ref[...] = dv
        dq_ref[...] = dq
        dlb_ref[...] += dlb
        for h in range(NH):
            ds_ref[h] = dsts[h]

    st_spec = pl.BlockSpec((None, NH, HD, HD), lambda s: (order(s), 0, 0, 0))
    row = pl.BlockSpec((1, HGW), lambda s: (0, 0))
    return pl.pallas_call(
        body, name=name, grid=(nt,),
        out_shape=[jax.ShapeDtypeStruct((n, HGW), F32)] * 3 + [jax.ShapeDtypeStruct((1, HGW), F32)],
        in_specs=in_tiles + [row, st_spec, out_tile], out_specs=[out_tile] * 3 + [row],
        scratch_shapes=[pltpu.VMEM((NH, HD, HD), F32)], compiler_params=_cparams(1),
    )(proj, proj, proj, lb, st_all, do)


def f_lb(rows):
    mx = functools.reduce(jnp.maximum, rows)
    ex = [jnp.exp(r - mx) for r in rows]
    tot = functools.reduce(jnp.add, ex)
    out, acc = [jnp.zeros_like(rows[0])], None
    for e in ex[1:]:
        acc = e / tot if acc is None else acc + e / tot
        out.append(acc)
    return out


def lb_call(hg, dlb=None):
    nl = hg.shape[0]

    def body(*refs):
        rows = [refs[0][l:l + 1, :] for l in range(nl)]
        if dlb is None:
            res = f_lb(rows)
        else:
            _, vjp = jax.vjp(f_lb, rows)
            (res,) = vjp([refs[1][l:l + 1, :] for l in range(nl)])
        for l in range(nl):
            refs[-1][l:l + 1, :] = res[l]

    args = (hg,) if dlb is None else (hg, dlb)
    return pl.pallas_call(body, name="lower_bounds" if dlb is None else "lower_bounds_bwd",
                          out_shape=jax.ShapeDtypeStruct(hg.shape, F32))(*args)


def mod_fwd(craw, w_mod, b_cols):
    nl, _, cols = w_mod.shape

    def body(c_ref, w_ref, b_ref, o_ref):
        o_ref[...] = _dot(jax.nn.silu(c_ref[...]), w_ref[...], NN) + b_ref[...]

    return pl.pallas_call(
        body, name="mod_fwd", grid=(nl,), out_shape=jax.ShapeDtypeStruct((nl, 16, cols), F32),
        in_specs=[pl.BlockSpec((16, D), lambda l: (0, 0)), pl.BlockSpec((None, D, cols), lambda l: (l, 0, 0)),
                  pl.BlockSpec((None, 1, cols), lambda l: (l, 0, 0))],
        out_specs=pl.BlockSpec((None, 16, cols), lambda l: (l, 0, 0)), compiler_params=_cparams(1))(craw, w_mod, b_cols)


def mod_bwd(craw, w_mod, g):
    nl, _, cols = w_mod.shape

    def body(c_ref, w_ref, g_ref, dw_ref, dc_ref, acc_ref):
        l = pl.program_id(0)

        @pl.when(l == 0)
        def _():
            acc_ref[...] = jnp.zeros_like(acc_ref)

        c = c_ref[...]
        s, vjp = jax.vjp(jax.nn.silu, c)
        dw_ref[...] = _dot(s, g_ref[...], TN)
        acc_ref[...] += _dot(g_ref[...], w_ref[...], NT)

        @pl.when(l == nl - 1)
        def _():
            dc_ref[...] = vjp(acc_ref[...])[0]

    return pl.pallas_call(
        body, name="mod_bwd", grid=(nl,),
        out_shape=[jax.ShapeDtypeStruct(w_mod.shape, F32), jax.ShapeDtypeStruct((16, D), F32)],
        in_specs=[pl.BlockSpec((16, D), lambda l: (0, 0)), pl.BlockSpec((None, D, cols), lambda l: (l, 0, 0)),
                  pl.BlockSpec((None, 16, cols), lambda l: (l, 0, 0))],
        out_specs=[pl.BlockSpec((None, D, cols), lambda l: (l, 0, 0)), pl.BlockSpec((16, D), lambda l: (0, 0))],
        scratch_shapes=[pltpu.VMEM((16, D), F32)], compiler_params=_cparams(1))(craw, w_mod, g)


def sum_parts(parts):
    def body(p_ref, o_ref):
        acc = p_ref[0]
        for k in range(1, parts.shape[0]):
            acc = acc + p_ref[k]
        o_ref[...] = acc

    return pl.pallas_call(body, name="sum_small_grads", out_shape=jax.ShapeDtypeStruct(parts.shape[1:], parts.dtype),
                          compiler_params=pltpu.CompilerParams(vmem_limit_bytes=VMEM_MB << 20))(parts)


def adamw(name, w, m, v, gs):
    r, c = w.shape
    s = gs.shape[0]
    tr = max([t for t in range(8, 257, 8) if r % t == 0], default=r)

    def body(w_ref, m_ref, v_ref, g_ref, go_ref, d_ref, mo_ref, vo_ref):
        g = g_ref[0].astype(F32)
        for k in range(1, s):
            g = g + g_ref[k].astype(F32)
        m_new = B1 * m_ref[...] + (1.0 - B1) * g
        v_new = B2 * v_ref[...] + (1.0 - B2) * jnp.square(g)
        m_hat = m_new / (1.0 - B1 ** STEP)
        v_hat = v_new / (1.0 - B2 ** STEP)
        go_ref[...] = g
        d_ref[...] = -LR * (m_hat / (jnp.sqrt(v_hat) + EPS) + WD * w_ref[...])
        mo_ref[...] = m_new
        vo_ref[...] = v_new

    blk = pl.BlockSpec((tr, c), lambda i: (i, 0))
    return pl.pallas_call(
        body, name=name, grid=(r // tr,), out_shape=[jax.ShapeDtypeStruct((r, c), F32)] * 4,
        in_specs=[blk, blk, blk, pl.BlockSpec((s, tr, c), lambda i: (0, i, 0))], out_specs=[blk] * 4,
        compiler_params=_cparams(1))(w, m, v, gs)


SMALL = ["c_ctx", "b_mod", "s5_lam_re", "s5_lam_im", "s5_log_dt", "s5_b_re", "s5_b_im", "s5_c_re", "s5_c_im", "s5_d",
         "b_glu", "hg_lb", "hg_norm_w", "ln1_g", "ln1_b", "conv_b", "ln2_g", "ln2_b"]
BIG = ["w_in", "w_glu", "w_out", "w_up", "w_down"]
WEIGHTS = ["c_ctx", "w_mod", "b_mod", "w_in", "s5_lam_re", "s5_lam_im", "s5_log_dt", "s5_b_re", "s5_b_im", "s5_c_re",
           "s5_c_im", "s5_d", "w_glu", "b_glu", "hg_lb", "hg_norm_w", "w_out", "ln1_g", "ln1_b", "w_up", "conv_w",
           "conv_b", "w_down", "ln2_g", "ln2_b"]
PACK_W = 1024


def _pack_rows(k):
    return -(-k // (8 * PACK_W)) * 8


def _pack(arrs):
    parts = []
    for a in arrs:
        flat = a.reshape(-1)
        r = _pack_rows(flat.shape[0])
        parts.append(jnp.pad(flat, (0, r * PACK_W - flat.shape[0])).reshape(r, PACK_W))
    used = sum(q.shape[0] for q in parts)
    parts.append(jnp.zeros((-used % (8 * N_DEV), PACK_W), parts[0].dtype))
    return jnp.concatenate(parts, axis=0)


def _unpack(p, shapes):
    out, o = [], 0
    for s in shapes:
        k = math.prod(s)
        r = _pack_rows(k)
        out.append(p[o:o + r].reshape(-1)[:k].reshape(s))
        o += r
    return out


def _gathered_cols(g):
    return jnp.moveaxis(g, 0, 2).reshape(g.shape[1], g.shape[2], -1)


def _gathered_rows(g):
    return jnp.moveaxis(g, 0, 1).reshape(g.shape[1], -1, g.shape[3])


def _step(p):
    nl = p["w_in"].shape[0]
    me = lax.axis_index("x") * 4 + lax.axis_index("y") * 2 + lax.axis_index("c")
    xc0 = jnp.concatenate([p["ctx"][0], p["x"][0]], axis=0)
    n = xc0.shape[0]
    target = p["loss_target"][0]

    hg3 = jnp.stack([p[k].reshape(-1) for k in ("hg_lb", "m_hg_lb", "v_hg_lb")])
    g_cw, g_c, g_hg = _exchange("gather_inputs", [p["conv_w"], p["c"], hg3], False)
    conv_w = _gathered_cols(g_cw)
    hg_full = jnp.moveaxis(g_hg.reshape(N_DEV, 3, nl, 2, -1), 0, 3).reshape(3, nl, 2 * HGW)
    lb_all = lb_call(hg_full[0])

    craw = jnp.concatenate([g_c.reshape(N_DEV, D), jnp.broadcast_to(p["c_ctx"][None], (8, D))], axis=0)
    cols = p["w_mod"].shape[2]
    b_cols = lax.dynamic_slice_in_dim(p["b_mod"], me * cols, cols, axis=1)[:, None, :]
    (g_mod,) = _exchange("gather_mod", [mod_fwd(craw, p["w_mod"], b_cols)], False)
    mod_all = jnp.moveaxis(g_mod, 0, 2).reshape(nl, 16, 6 * D)
    mod_x = lax.dynamic_index_in_dim(mod_all, me, axis=1, keepdims=False)
    mod2 = jnp.stack([mod_all[:, 8], mod_x], axis=1)
    mvec = lambda l, k: mod2[l, :, k * D:(k + 1) * D][:, None, :]
    gathers = {}

    def start_gather(l, part, ks, after):
        srcs = [p[k][l].astype(BF16) for k in ks]
        gathers[l, part], token = exchange_start(f"gather_start{l}{part}", srcs, [_own_block_set(s, False) for s in srcs],
                                                 False, after=after)
        return token[0, 0]

    start_gather(0, "a", BIG[:1], g_mod)
    start_gather(0, "b", BIG[1:], g_mod)

    def gathered(l, part, after):
        res = exchange_wait(f"gather_wait{l}{part}", gathers[l, part], after, False)
        ks = {"a": BIG[:1], "b": BIG[1:], "": BIG}[part]
        cols_ = lambda g: jnp.moveaxis(g, 0, 1).reshape(g.shape[1], -1)
        rows_ = lambda g: g.reshape(-1, g.shape[2])
        return {k: (cols_ if k in ("w_in", "w_up") else rows_)(g) for k, g in zip(ks, res)}

    zvec = jnp.zeros((2, 1, D), F32)
    row = lambda a: a.reshape(1, 1, -1)

    to_hp = lambda a: jnp.moveaxis(a, -1, 2).reshape(nl * 2, S5H, S5P)
    prep_in = [p["s5_lam_re"].reshape(nl * 2, 1, S5P), p["s5_lam_im"].reshape(nl * 2, 1, S5P),
               p["s5_log_dt"].reshape(nl * 2, 32, 1), to_hp(p["s5_b_re"]), to_hp(p["s5_b_im"]),
               jnp.swapaxes(p["s5_c_re"], 2, 3).reshape(nl * 2, S5H, S5P),
               jnp.swapaxes(p["s5_c_im"], 2, 3).reshape(nl * 2, S5H, S5P)]
    s5a, s5b, s5c = s5_prep(*prep_in)

    T1 = lambda a, w=D, cb=_c0: (a, w, cb)
    saved = []
    xc = xc0
    (h,) = block_fwd("mod0", f_mod, n, [T1(xc)], [T1(mvec(0, 0)), T1(mvec(0, 1))], [(D, BF16, D, _c0)])
    w_in, w_glu, w_out, w_up, w_down = ([None] * nl for _ in range(5))
    for l in range(nl):
        wl = gathered(l, "", xc) if l else gathered(0, "a", s5a)
        w_in[l] = wl["w_in"]
        proj = dense_nn(f"in_proj{l}", h, w_in[l])
        s5 = [s5_fwd(f"s5_fwd{l}_{d}", proj, s5a, s5b, s5c, 2 * l + d, d == 1) for d in range(2)]
        lbs = [lb_all[l, d * HGW:(d + 1) * HGW][None] for d in range(2)]
        gl = [gla_fwd(f"gla_fwd{l}_{d}", proj, lbs[d], d == 1) for d in range(2)]
        if l == 0:
            wl = gathered(0, "b", gl[1][0])
        w_glu[l], w_out[l], w_up[l], w_down[l] = wl["w_glu"], wl["w_out"], wl["w_up"], wl["w_down"]
        started = start_gather(l + 1, "", BIG, w_down[l]) if l + 1 < nl else 0.0
        mix_t = [T1(proj, S5W), T1(proj, HGW, lambda j: 5), T1(s5[0][2], S5W), T1(s5[1][2], S5W),
                 T1(gl[0][0], HGW), T1(gl[1][0], HGW)]
        mix_p = [T1(row(p["s5_d"][l]) + started, S5W), T1(w_glu[l][None], S5W), T1(row(p["b_glu"][l]), S5W),
                 T1(row(p["hg_norm_w"][l]), HD)]
        (y,) = block_fwd(f"mix{l}", f_mix, n, mix_t, mix_p, [(D, BF16, D, _c0)])
        z = dense_nn(f"out_proj{l}", y, w_out[l])
        ln1_p = [T1(mvec(l, 2)), T1(row(p["ln1_g"][l])), T1(row(p["ln1_b"][l])), T1(mvec(l, 3)), T1(mvec(l, 4))]
        x1, h2 = block_fwd(f"ln1_{l}", f_ln, n, [T1(xc), T1(z)], ln1_p, [(D, F32, D, _c0), (D, BF16, D, _c0)])
        up = dense_nn(f"up_proj{l}", h2, w_up[l])
        ct = DFF // 2
        act_t = [T1(up, ct, lambda j: j), T1(up, ct, lambda j: j + 2)]
        cb2 = p["conv_b"][l].reshape(1, 1, -1)
        act_p = [T1(conv_w[l][None, :, :DFF], ct, lambda j: j), T1(conv_w[l][None, :, DFF:], ct, lambda j: j),
                 T1(cb2[:, :, :DFF], ct, lambda j: j), T1(cb2[:, :, DFF:], ct, lambda j: j)]
        (act,) = block_fwd(f"act{l}", f_act, n, act_t, act_p, [(DFF, BF16, ct, lambda j: j)], n_col=2)
        dn = dense_nn(f"down_proj{l}", act, w_down[l])
        nxt = (mvec(l + 1, 0), mvec(l + 1, 1)) if l + 1 < nl else (zvec, zvec)
        ln2_p = [T1(mvec(l, 5)), T1(row(p["ln2_g"][l])), T1(row(p["ln2_b"][l])), T1(nxt[0]), T1(nxt[1])]
        x2, hn = block_fwd(f"ln2_{l}", f_ln, n, [T1(x1), T1(dn)], ln2_p, [(D, F32, D, _c0), (D, BF16, D, _c0)])
        saved.append(dict(xc=xc, h=h, proj=proj, s5=s5, gl=gl, lbs=lbs, mix_t=mix_t, mix_p=mix_p, y=y, z=z,
                          ln1_p=ln1_p, x1=x1, h2=h2, act_t=act_t, act_p=act_p, act=act, dn=dn, ln2_p=ln2_p))
        xc, h = x2, hn

    dxc, loss_part = loss_and_grad(xc, target)
    loss = lax.psum(loss_part[0, 0], AXES)

    g = {k: [None] * nl for k in ("w_in", "w_glu", "w_out", "w_up", "w_down", "conv_w", "conv_b", "s5_d", "b_glu",
                                  "hg_norm_w", "ln1_g", "ln1_b", "ln2_g", "ln2_b", "dlb", "s5")}
    dmod = [[None] * 6 for _ in range(nl)]
    scatters = [[] for _ in range(nl)]
    s5_totals = [lax.empty((2 * nl, 2, r, S5P), F32) for r in (1, S5W, S5W)]
    dh_next = jnp.zeros((n, D), F32)
    fgrad = (D, F32, D, _c0)
    for l in reversed(range(nl)):
        sv = saved[l]
        (dx1, d_dn), dp = block_bwd(f"ln2_bwd{l}", f_ln, n, [T1(sv["x1"]), T1(sv["dn"])], sv["ln2_p"],
                                    [T1(dxc), T1(dh_next)], [fgrad, fgrad])
        dmod[l][5], g["ln2_g"][l], g["ln2_b"][l] = dp[0], dp[1], dp[2]
        if l + 1 < nl:
            dmod[l + 1][0], dmod[l + 1][1] = dp[3], dp[4]
        dact = dense_nt(f"down_bwd{l}", d_dn, w_down[l])
        g["w_down"][l] = dense_tn(f"down_wgrad{l}", sv["act"], d_dn)
        ct = DFF // 2
        cj = lambda j: j
        (dua, dug), dp = block_bwd(f"act_bwd{l}", f_act, n, sv["act_t"], sv["act_p"], [T1(dact, ct, cj)],
                                   [(DFF, BF16, ct, cj), (DFF, BF16, ct, cj)], n_col=2)
        g["conv_w"][l] = jnp.concatenate([dp[0][0], dp[1][0]], axis=-1)
        g["conv_b"][l] = jnp.concatenate([dp[2][0, 0], dp[3][0, 0]], axis=-1)
        dup = jnp.concatenate([dua, dug], axis=-1)
        dh2 = dense_nt(f"up_bwd{l}", dup, w_up[l])
        g["w_up"][l] = dense_tn(f"up_wgrad{l}", sv["h2"], dup)
        (dxc, dz), dp = block_bwd(f"ln1_bwd{l}", f_ln, n, [T1(sv["xc"]), T1(sv["z"])], sv["ln1_p"],
                                  [T1(dx1), T1(dh2)], [fgrad, fgrad])
        dmod[l][2], g["ln1_g"][l], g["ln1_b"][l], dmod[l][3], dmod[l][4] = dp
        dy = dense_nt(f"out_bwd{l}", dz, w_out[l])
        g["w_out"][l] = dense_tn(f"out_wgrad{l}", sv["y"], dz)
        half = (S5W, F32, S5W, _c0)
        (dpu, dpg, dys, dos), dp = block_bwd(f"mix_bwd{l}", f_mix, n, sv["mix_t"], sv["mix_p"], [T1(dy)],
                                                   [half, half, half, None, half, None])
        g["s5_d"][l], g["w_glu"][l], g["b_glu"][l], g["hg_norm_w"][l] = dp[0][0, 0], dp[1][0], dp[2][0, 0], dp[3][0, 0]

        def start_scatter(tag, ks):
            by_cols = lambda a: jnp.moveaxis(a.reshape(a.shape[0], N_DEV, -1), 1, 0)
            by_rows = lambda a: a.reshape(N_DEV, -1, a.shape[1])
            sends = [(by_rows if k in ("w_glu", "w_out", "w_down") else by_cols)(g[k][l].astype(BF16 if k != "conv_w" else F32))
                     for k in ks]
            handle, token = exchange_start(f"scatter_start{l}{tag}", sends, [_own_block_set(s, True) for s in sends], True)
            scatters[l].append((ks, handle))
            return token[0, 0]

        lbs_b = sv["lbs"]
        if l == 0:
            started = start_scatter("a", ["w_glu", "w_out", "w_up", "w_down", "conv_w"])
            lbs_b = [b + started for b in lbs_b]
        gb = [gla_bwd(f"gla_bwd{l}_{d}", sv["proj"], lbs_b[d], sv["gl"][d][1], dos, d == 1) for d in range(2)]
        g["dlb"][l] = jnp.concatenate([gb[0][3], gb[1][3]], axis=-1)[0]
        sb = [None, None]
        for d in range(2):
            sb[d], *s5_totals = s5_bwd(f"s5_bwd{l}_{d}", sv["proj"], dys, sv["s5"][d][0], sv["s5"][d][1], s5a, s5b, s5c,
                                       2 * l + d, d == 1, s5_totals)
        asm_t = [T1(dpu, S5W), T1(sb[0], S5W), T1(sb[1], S5W), T1(gb[0][0], HGW), T1(gb[1][0], HGW),
                 T1(gb[0][1], HGW), T1(gb[1][1], HGW), T1(gb[0][2], HGW), T1(gb[1][2], HGW), T1(dpg, HGW)]
        (dproj,) = block_fwd(
            f"dproj{l}", lambda tv, pv, i: (jnp.concatenate(
                [tv[0] + tv[1] + tv[2], tv[3], tv[4], tv[5] + tv[6], tv[7] + tv[8], tv[9]], axis=-1),),
            n, asm_t, [], [(INC, BF16, INC, _c0)])
        dh_next = dense_nt(f"in_bwd{l}", dproj, w_in[l])
        g["w_in"][l] = dense_tn(f"in_wgrad{l}", sv["h"], dproj)
        started = start_scatter("b", ["w_in"]) if l == 0 else start_scatter("", BIG + ["conv_w"])
        if l:
            gate, wd_, cb_ = saved[l - 1]["ln2_p"][0]
            saved[l - 1]["ln2_p"][0] = (gate + started, wd_, cb_)
    (dxc,), dp = block_bwd("mod0_bwd", f_mod_id, n, [T1(xc0)], [T1(mvec(0, 0)), T1(mvec(0, 1))],
                           [T1(dh_next), T1(dxc)], [fgrad])
    dmod[0][0], dmod[0][1] = dp
    grad_x = dxc[n - p["x"].shape[1]:][None]

    d_prep = s5_prep_bwd(*prep_in, *s5_totals)
    from_hp = lambda a: jnp.moveaxis(a.reshape(nl, 2, S5H, S5W // S5H, 64), 2, -1)
    gs5 = {"s5_lam_re": d_prep[0].reshape(nl, 2, 32, 64), "s5_lam_im": d_prep[1].reshape(nl, 2, 32, 64),
           "s5_log_dt": d_prep[2].reshape(nl, 2, 32), "s5_b_re": from_hp(d_prep[3]), "s5_b_im": from_hp(d_prep[4]),
           "s5_c_re": jnp.swapaxes(d_prep[5].reshape(nl, 2, S5H, 32, 64), 2, 3),
           "s5_c_im": jnp.swapaxes(d_prep[6].reshape(nl, 2, S5H, 32, 64), 2, 3)}
    d_hg = lb_call(hg_full[0], jnp.stack(g["dlb"]))

    dmod_loc = jnp.stack([jnp.concatenate([dmod[l][k][:, 0] for k in range(6)], axis=-1) for l in range(nl)])
    (g_dmod,) = _exchange("gather_dmod", [dmod_loc], False)
    gcols = lax.dynamic_slice_in_dim(g_dmod, me * cols, cols, axis=3)
    g16 = jnp.concatenate([jnp.moveaxis(gcols[:, :, 1], 0, 1), jnp.moveaxis(gcols[:, :, 0], 0, 1)], axis=1)
    grad_w_mod, dcraw = mod_bwd(craw, p["w_mod"], g16)
    d_c_ctx = jnp.sum(dcraw[8:], axis=0)

    stk = lambda k: jnp.stack(g[k])
    small_g = {"c_ctx": d_c_ctx, "b_mod": dmod_loc[:, 0] + dmod_loc[:, 1], "s5_d": stk("s5_d"), "b_glu": stk("b_glu"),
               "hg_lb": d_hg.reshape(nl, 2, HGW), "hg_norm_w": stk("hg_norm_w"), "ln1_g": stk("ln1_g")[:, 0, 0],
               "ln1_b": stk("ln1_b")[:, 0, 0], "conv_b": stk("conv_b"), "ln2_g": stk("ln2_g")[:, 0, 0],
               "ln2_b": stk("ln2_b")[:, 0, 0], **gs5}
    g_pack = _pack([small_g[k] for k in SMALL])
    (g_parts,) = _exchange("scatter_small_grads", [g_pack.reshape(N_DEV, -1, PACK_W)], True)
    (g_small,) = _exchange("gather_small_grads", [sum_parts(g_parts)], False)
    g_small = g_small.reshape(1, -1, PACK_W)

    out = {}
    hgw = {"": hg_full[0].reshape(nl, 2, HGW), "m_": hg_full[1].reshape(nl, 2, HGW), "v_": hg_full[2].reshape(nl, 2, HGW)}
    full = lambda pre, k: hgw[pre] if k == "hg_lb" else p[pre + k]
    shapes = [full("", k).shape for k in SMALL]
    res = adamw("adamw_small", *[_pack([full(pre, k) for k in SMALL]) for pre in ("", "m_", "v_")], g_small)
    for kind, packed in zip(("grad_", "delta_", "new_m_", "new_v_"), res):
        for k, a in zip(SMALL, _unpack(packed, shapes)):
            if k == "hg_lb":
                a = lax.dynamic_slice_in_dim(a, me * (HGW // N_DEV), HGW // N_DEV, axis=2)
            out[kind + k] = a
    kinds = ("grad_", "delta_", "new_m_", "new_v_")
    res = adamw("adamw_w_mod", *[p[pre + "w_mod"].reshape(-1, cols) for pre in ("", "m_", "v_")],
                grad_w_mod.reshape(1, -1, cols))
    for kind, a in zip(kinds, res):
        out[kind + "w_mod"] = a.reshape(p["w_mod"].shape)
    per_layer = {k: [None] * nl for k in BIG + ["conv_w"]}
    for l in reversed(range(nl)):
        for i, (ks, handle) in enumerate(scatters[l]):
            recv = exchange_wait(f"scatter_wait{l}_{i}", handle, g_small, True)
            for k, gsum in zip(ks, recv):
                per_layer[k][l] = adamw(f"adamw_{k}{l}", p[k][l], p["m_" + k][l], p["v_" + k][l], gsum)
    for k, res_l in per_layer.items():
        for i, kind in enumerate(kinds):
            out[kind + k] = jnp.stack([r[i] for r in res_l])
    return (loss, grad_x, *[out[kind + k] for kind in ("grad_", "delta_", "new_m_", "new_v_") for k in WEIGHTS])


def kernel(x, c, ctx, c_ctx, w_mod, b_mod, w_in, s5_lam_re, s5_lam_im, s5_log_dt, s5_b_re, s5_b_im, s5_c_re, s5_c_im, s5_d, w_glu, b_glu, hg_lb, hg_norm_w, w_out, ln1_g, ln1_b, w_up, conv_w, conv_b, w_down, ln2_g, ln2_b, loss_target, m_c_ctx, m_w_mod, m_b_mod, m_w_in, m_s5_lam_re, m_s5_lam_im, m_s5_log_dt, m_s5_b_re, m_s5_b_im, m_s5_c_re, m_s5_c_im, m_s5_d, m_w_glu, m_b_glu, m_hg_lb, m_hg_norm_w, m_w_out, m_ln1_g, m_ln1_b, m_w_up, m_conv_w, m_conv_b, m_w_down, m_ln2_g, m_ln2_b, v_c_ctx, v_w_mod, v_b_mod, v_w_in, v_s5_lam_re, v_s5_lam_im, v_s5_log_dt, v_s5_b_re, v_s5_b_im, v_s5_c_re, v_s5_c_im, v_s5_d, v_w_glu, v_b_glu, v_hg_lb, v_hg_norm_w, v_w_out, v_ln1_g, v_ln1_b, v_w_up, v_conv_w, v_conv_b, v_w_down, v_ln2_g, v_ln2_b):
    return _step(dict(locals()))
```

```python
import functools
import math

import jax
import jax.numpy as jnp
from jax import lax
from jax.experimental import pallas as pl
from jax.experimental.pallas import tpu as pltpu

F32, BF16 = jnp.float32, jnp.bfloat16
N_DEV = 8
AXES = ("x", "y", "c")
D = 1024
S5W = 512
S5P = 2048
S5H = 16
HGW = 512
HD = 128
NH = 4
CK = 32
DFF = 2816
GRID_W = 64
INC = 3072
ALPHA = 8.0 ** 0.25
LN_EPS = 1e-5
RMS_EPS = 1e-6
LR, B1, B2, EPS, WD, STEP = 0.001, 0.9, 0.999, 1e-08, 0.01, 10
TT = 256
VMEM_MB = 56

NN = ((1,), (0,))
NT = ((1,), (1,))
TN = ((0,), (0,))


def _cparams(n_axes):
    return pltpu.CompilerParams(dimension_semantics=("arbitrary",) * n_axes, vmem_limit_bytes=VMEM_MB << 20)


def _dot(a, b, dims):
    return lax.dot_general(a.astype(BF16), b.astype(BF16), (dims, ((), ())), preferred_element_type=F32)


@jax.custom_vjp
def mm_nn(a, b):
    return _dot(a, b, NN)


@jax.custom_vjp
def mm_nt(a, b):
    return _dot(a, b, NT)


@jax.custom_vjp
def mm_tn(a, b):
    return _dot(a, b, TN)


mm_nn.defvjp(lambda a, b: (_dot(a, b, NN), (a, b)), lambda r, g: (_dot(g, r[1], NT), _dot(r[0], g, TN)))
mm_nt.defvjp(lambda a, b: (_dot(a, b, NT), (a, b)), lambda r, g: (_dot(g, r[1], NN), _dot(g, r[0], TN)))
mm_tn.defvjp(lambda a, b: (_dot(a, b, TN), (a, b)), lambda r, g: (_dot(r[1], g, NT), _dot(r[0], g, NN)))


def _roll_rows(u, s):
    return pltpu.roll(u, s % u.shape[0], 0)


@functools.partial(jax.custom_vjp, nondiff_argnums=(2,))
def shift_rows(u, m, s):
    return _roll_rows(u, s) * m


def _shift_fwd(u, m, s):
    return _roll_rows(u, s) * m, m


def _shift_bwd(s, m, g):
    return _roll_rows(g * m, -s), jnp.zeros_like(m)


shift_rows.defvjp(_shift_fwd, _shift_bwd)


def _scan_tile(pos, nt, rev):
    return jnp.where(pos == 0, 0, nt - pos) if rev else pos


def _exchange(name, arrays, all_to_all):
    k_arr = len(arrays)

    def body(*refs):
        ins, outs = refs[:k_arr], refs[k_arr:2 * k_arr]
        send_sems, recv_sems, local_sems = refs[2 * k_arr:]
        me = lax.axis_index("x") * 4 + lax.axis_index("y") * 2 + lax.axis_index("c")
        local = []
        for k in range(k_arr):
            cp = pltpu.make_async_copy(ins[k].at[me] if all_to_all else ins[k], outs[k].at[me], local_sems.at[k])
            cp.start()
            local.append(cp)
        sends = []
        for d in range(1, N_DEV):
            p = (me + d) % N_DEV
            for k in range(k_arr):
                cp = pltpu.make_async_remote_copy(
                    src_ref=ins[k].at[p] if all_to_all else ins[k], dst_ref=outs[k].at[me],
                    send_sem=send_sems.at[k, d - 1], recv_sem=recv_sems.at[k, d - 1],
                    device_id=(p // 4, (p // 2) % 2, p % 2), device_id_type=pl.DeviceIdType.MESH)
                cp.start()
                sends.append(cp)
        for d in range(1, N_DEV):
            q = (me + N_DEV - d) % N_DEV
            for k in range(k_arr):
                pltpu.make_async_remote_copy(
                    src_ref=ins[k].at[q] if all_to_all else ins[k], dst_ref=outs[k].at[q],
                    send_sem=send_sems.at[k, d - 1], recv_sem=recv_sems.at[k, d - 1],
                    device_id=(q // 4, (q // 2) % 2, q % 2), device_id_type=pl.DeviceIdType.MESH).wait_recv()
        for cp in sends:
            cp.wait_send()
        for cp in local:
            cp.wait()

    shapes = [a.shape if all_to_all else (N_DEV,) + a.shape for a in arrays]
    return pl.pallas_call(
        body, name=name,
        out_shape=[jax.ShapeDtypeStruct(s, a.dtype) for s, a in zip(shapes, arrays)],
        in_specs=[pl.BlockSpec(memory_space=pl.ANY)] * k_arr,
        out_specs=[pl.BlockSpec(memory_space=pl.ANY)] * k_arr,
        scratch_shapes=[pltpu.SemaphoreType.DMA((k_arr, N_DEV - 1)), pltpu.SemaphoreType.DMA((k_arr, N_DEV - 1)),
                        pltpu.SemaphoreType.DMA((k_arr,))],
    )(*arrays)


_HBM = pl.BlockSpec(memory_space=pltpu.HBM)
_SEM = pl.BlockSpec(memory_space=pltpu.SEMAPHORE)
_EFFECT = pltpu.SideEffectType.DATAFLOW_SIDE_EFFECTING


def _peer(i):
    return (i // 4, (i // 2) % 2, i % 2)


def exchange_start(name, srcs, lands, all_to_all, after=None):
    k_arr = len(srcs)
    n_sem = k_arr * (N_DEV - 1)
    extra = [] if after is None else [after]

    def body(*refs):
        ins, lz = refs[:k_arr], refs[k_arr:2 * k_arr]
        first = 2 * k_arr + len(extra)
        send_sems = refs[first:first + n_sem]
        recv_sems = refs[first + n_sem:first + 2 * n_sem]
        me = lax.axis_index("x") * 4 + lax.axis_index("y") * 2 + lax.axis_index("c")
        for d in range(1, N_DEV):
            p = (me + d) % N_DEV
            for k in range(k_arr):
                s = k * (N_DEV - 1) + d - 1
                pltpu.make_async_remote_copy(
                    src_ref=ins[k].at[p] if all_to_all else ins[k], dst_ref=lz[k].at[me],
                    send_sem=send_sems[s], recv_sem=recv_sems[s],
                    device_id=_peer(p), device_id_type=pl.DeviceIdType.MESH).start()
        refs[-1][...] = jnp.zeros_like(refs[-1])

    arrs = list(srcs) + list(lands)
    res = pl.pallas_call(
        body, name=name,
        out_shape=(*[pltpu.SemaphoreType.DMA(())] * (2 * n_sem), *[pltpu.HBM(a.shape, a.dtype) for a in arrs],
                   jax.ShapeDtypeStruct((8, 128), F32)),
        in_specs=[_HBM] * len(arrs) + [pl.BlockSpec(memory_space=pl.ANY)] * len(extra),
        out_specs=(*[_SEM] * (2 * n_sem), *[_HBM] * len(arrs), pl.BlockSpec(memory_space=pltpu.VMEM)),
        input_output_aliases={i: 2 * n_sem + i for i in range(len(arrs))},
        compiler_params=pltpu.CompilerParams(has_side_effects=_EFFECT),
    )(*[pltpu.with_memory_space_constraint(a, pltpu.HBM) for a in arrs], *extra)
    return res[:-1], res[-1]


def exchange_wait(name, handle, after, all_to_all):
    k_arr = len(handle) // (2 * N_DEV)
    n_sem = k_arr * (N_DEV - 1)
    sems, arrs = handle[:2 * n_sem], handle[2 * n_sem:]

    def body(*refs):
        ins, lz = refs[:k_arr], refs[k_arr:2 * k_arr]
        s_sems = refs[2 * k_arr:2 * k_arr + n_sem]
        r_sems = refs[2 * k_arr + n_sem:2 * k_arr + 2 * n_sem]
        me = lax.axis_index("x") * 4 + lax.axis_index("y") * 2 + lax.axis_index("c")
        for d in range(1, N_DEV):
            q = (me + N_DEV - d) % N_DEV
            for k in range(k_arr):
                s = k * (N_DEV - 1) + d - 1
                cp = pltpu.make_async_remote_copy(
                    src_ref=ins[k].at[q] if all_to_all else ins[k], dst_ref=lz[k].at[q],
                    send_sem=s_sems[s], recv_sem=r_sems[s],
                    device_id=_peer(q), device_id_type=pl.DeviceIdType.MESH)
                cp.wait_send()
                cp.wait_recv()

    res = pl.pallas_call(
        body, name=name, out_shape=tuple(pltpu.HBM(a.shape, a.dtype) for a in arrs),
        in_specs=[_HBM] * len(arrs) + [_SEM] * (2 * n_sem) + [pl.BlockSpec(memory_space=pl.ANY)],
        out_specs=tuple([_HBM] * len(arrs)),
        input_output_aliases={i: i for i in range(len(arrs))},
        compiler_params=pltpu.CompilerParams(has_side_effects=_EFFECT),
    )(*arrs, *sems, after)
    return res[k_arr:]


def _own_block_set(src, all_to_all):
    me = lax.axis_index("x") * 4 + lax.axis_index("y") * 2 + lax.axis_index("c")
    own = lax.dynamic_index_in_dim(src, me, 0, keepdims=False) if all_to_all else src
    shape = src.shape if all_to_all else (N_DEV,) + src.shape
    return lax.dynamic_update_index_in_dim(lax.empty(shape, src.dtype), own, me, 0)


def _tile(n, prefs):
    for t in prefs:
        if n % t == 0:
            return t
    raise ValueError(n)


def dense_nn(name, a, w, out_dtype=F32):
    n, k = a.shape
    m = w.shape[1]
    tn, tm = _tile(n, (1088, 256)), _tile(m, (1024, 512, 256, 128))

    def body(a_ref, w_ref, o_ref):
        o_ref[...] = _dot(a_ref[...], w_ref[...], NN).astype(o_ref.dtype)

    return pl.pallas_call(
        body, name=name, grid=(m // tm, n // tn), out_shape=jax.ShapeDtypeStruct((n, m), out_dtype),
        in_specs=[pl.BlockSpec((tn, k), lambda j, i: (i, 0)), pl.BlockSpec((k, tm), lambda j, i: (0, j))],
        out_specs=pl.BlockSpec((tn, tm), lambda j, i: (i, j)), compiler_params=_cparams(2))(a, w)


def dense_nt(name, g, w, out_dtype=F32):
    n, m = g.shape
    k = w.shape[0]
    tn, tk = _tile(n, (544, 256)), _tile(k, (1024, 1408, 512, 256, 128))

    def body(g_ref, w_ref, o_ref):
        o_ref[...] = _dot(g_ref[...], w_ref[...], NT).astype(o_ref.dtype)

    return pl.pallas_call(
        body, name=name, grid=(k // tk, n // tn), out_shape=jax.ShapeDtypeStruct((n, k), out_dtype),
        in_specs=[pl.BlockSpec((tn, m), lambda j, i: (i, 0)), pl.BlockSpec((tk, m), lambda j, i: (j, 0))],
        out_specs=pl.BlockSpec((tn, tk), lambda j, i: (i, j)), compiler_params=_cparams(2))(g, w)


def dense_nt2(name, g1, g2, w, out_dtype=F32):
    n, m = g1.shape
    k = w.shape[0]
    tn, tk = _tile(n, (544, 256)), _tile(k, (1024, 1408, 512, 256, 128))

    def body(g1_ref, g2_ref, w1_ref, w2_ref, o_ref):
        o_ref[...] = (_dot(g1_ref[...], w1_ref[...], NT) + _dot(g2_ref[...], w2_ref[...], NT)).astype(o_ref.dtype)

    half = lambda h: pl.BlockSpec((tk, m), lambda j, i: (j, h))
    rows = pl.BlockSpec((tn, m), lambda j, i: (i, 0))
    return pl.pallas_call(
        body, name=name, grid=(k // tk, n // tn), out_shape=jax.ShapeDtypeStruct((n, k), out_dtype),
        in_specs=[rows, rows, half(0), half(1)], out_specs=pl.BlockSpec((tn, tk), lambda j, i: (i, j)),
        compiler_params=_cparams(2))(g1, g2, w, w)


def dense_tn(name, a, g, out_dtype=BF16):
    n, k = a.shape
    m = g.shape[1]
    tn = _tile(n, (1088, 256))
    tk = _tile(k, (1024, 1408, 512, 256, 128))
    tm = _tile(m, (1024, 1408, 512, 256, 128))
    nt = n // tn

    def body(a_ref, g_ref, o_ref, acc_ref):
        t = pl.program_id(2)

        @pl.when(t == 0)
        def _():
            acc_ref[...] = jnp.zeros_like(acc_ref)

        acc_ref[...] += _dot(a_ref[...], g_ref[...], TN)

        @pl.when(t == nt - 1)
        def _():
            o_ref[...] = acc_ref[...].astype(o_ref.dtype)

    return pl.pallas_call(
        body, name=name, grid=(k // tk, m // tm, nt), out_shape=jax.ShapeDtypeStruct((k, m), out_dtype),
        in_specs=[pl.BlockSpec((tn, tk), lambda i, j, t: (t, i)), pl.BlockSpec((tn, tm), lambda i, j, t: (t, j))],
        out_specs=pl.BlockSpec((tk, tm), lambda i, j, t: (i, j)),
        scratch_shapes=[pltpu.VMEM((tk, tm), F32)], compiler_params=_cparams(3))(a, g)


def _c0(j):
    return 0


def _tspec(w, cb):
    return pl.BlockSpec((TT, w), lambda j, i: (i, cb(j)))


def _pspec(arr, w, cb):
    two = arr.shape[0] == 2
    return pl.BlockSpec((None, arr.shape[1], w), lambda j, i: (jnp.minimum(i, 1) if two else 0, 0, cb(j)))


def block_fwd(name, fn, n, tiled, params, outs, n_col=1):
    nt_, np_ = len(tiled), len(params)

    def body(*refs):
        i = pl.program_id(1)
        tv = [r[...].astype(F32) for r in refs[:nt_]]
        pv = [r[...].astype(F32) for r in refs[nt_:nt_ + np_]]
        for o_ref, r in zip(refs[nt_ + np_:], fn(tv, pv, i)):
            o_ref[...] = r.astype(o_ref.dtype)

    return pl.pallas_call(
        body, name=name, grid=(n_col, n // TT),
        out_shape=[jax.ShapeDtypeStruct((n, c), dt) for c, dt, _, _ in outs],
        in_specs=[_tspec(w, cb) for _, w, cb in tiled] + [_pspec(a, w, cb) for a, w, cb in params],
        out_specs=[_tspec(w, cb) for _, _, w, cb in outs], compiler_params=_cparams(2),
    )(*[a for a, _, _ in tiled], *[a for a, _, _ in params])


def block_bwd(name, fn, n, tiled, params, cots, grads, n_col=1):
    nt_, np_, nc_ = len(tiled), len(params), len(cots)
    want = [k for k, g in enumerate(grads) if g is not None]

    def body(*refs):
        i = pl.program_id(1)
        tv = [r[...].astype(F32) for r in refs[:nt_]]
        pv = [r[...].astype(F32) for r in refs[nt_:nt_ + np_]]
        cv = [r[...].astype(F32) for r in refs[nt_ + np_:nt_ + np_ + nc_]]
        o_refs = refs[nt_ + np_ + nc_:]
        _, vjp = jax.vjp(lambda t, p: list(fn(t, p, i)), tv, pv)
        dt, dp = vjp(cv)
        for o_ref, k in zip(o_refs, want):
            o_ref[...] = dt[k].astype(o_ref.dtype)
        for o_ref, g, (arr, _, _) in zip(o_refs[len(want):], dp, params):
            first = (i == 0) | (i == 1) if arr.shape[0] == 2 else i == 0

            @pl.when(first)
            def _(o_ref=o_ref):
                o_ref[...] = jnp.zeros_like(o_ref)

            o_ref[...] += g

    res = pl.pallas_call(
        body, name=name, grid=(n_col, n // TT),
        out_shape=[jax.ShapeDtypeStruct((n, grads[k][0]), grads[k][1]) for k in want]
        + [jax.ShapeDtypeStruct(a.shape, F32) for a, _, _ in params],
        in_specs=[_tspec(w, cb) for _, w, cb in tiled] + [_pspec(a, w, cb) for a, w, cb in params]
        + [_tspec(w, cb) for _, w, cb in cots],
        out_specs=[_tspec(grads[k][2], grads[k][3]) for k in want] + [_pspec(a, w, cb) for a, w, cb in params],
        compiler_params=_cparams(2),
    )(*[a for a, _, _ in tiled], *[a for a, _, _ in params], *[a for a, _, _ in cots])
    return res[:len(want)], res[len(want):]


def f_mod(tv, pv, i):
    (x,), (sh, sc) = tv, pv
    return (x * (1.0 + sc) + sh,)


def f_mod_id(tv, pv, i):
    return (f_mod(tv, pv, i)[0], tv[0])


def f_ln(tv, pv, i):
    (x, z), (gate, g, b, sh, sc) = tv, pv
    pre = ALPHA * x + gate * z
    mu = jnp.mean(pre, axis=-1, keepdims=True)
    var = jnp.mean(jnp.square(pre - mu), axis=-1, keepdims=True)
    xn = (pre - mu) * lax.rsqrt(var + LN_EPS) * g + b
    return xn, xn * (1.0 + sc) + sh


def f_mix(tv, pv, i):
    (pu, pg, y0, y1, o0, o1), (d_skip, w_glu, b_glu, norm_w) = tv, pv
    s5y = jax.nn.gelu(y0 + y1 + pu * d_skip)
    s5o = s5y * jax.nn.sigmoid(mm_nn(s5y, w_glu) + b_glu)
    o = o0 + o1
    heads = []
    for h in range(NH):
        oh = o[:, h * HD:(h + 1) * HD]
        heads.append(oh * lax.rsqrt(jnp.mean(jnp.square(oh), axis=-1, keepdims=True) + RMS_EPS) * norm_w)
    hg = jnp.concatenate(heads, axis=-1) * jax.nn.silu(pg)
    return (jnp.concatenate([s5o, hg], axis=-1),)


def f_act(tv, pv, i):
    (ua, ug), (cwa, cwg, cba, cbg) = tv, pv
    t = lax.broadcasted_iota(jnp.int32, (TT, 1), 0)
    lat = i > 0
    m_dn = jnp.where((t == 0) | (lat & (t % GRID_W == 0)), 0.0, 1.0)
    m_up = jnp.where((t == TT - 1) | (lat & (t % GRID_W == GRID_W - 1)), 0.0, 1.0)

    def conv(u, w, b):
        return shift_rows(u, m_dn, 1) * w[0:1] + u * w[1:2] + shift_rows(u, m_up, -1) * w[2:3] + b

    return (jax.nn.silu(conv(ua, cwa, cba)) * conv(ug, cwg, cbg),)


def loss_and_grad(xf, target):
    n = xf.shape[0]

    def body(x_ref, t_ref, dy_ref, l_ref):
        i = pl.program_id(0)

        @pl.when(i == 0)
        def _():
            l_ref[...] = jnp.zeros_like(l_ref)
            dy_ref[...] = jnp.zeros_like(dy_ref)

        @pl.when(i > 0)
        def _():
            e = x_ref[...] - t_ref[...]
            dy_ref[...] = e * (1.0 / D)
            l_ref[...] += 0.5 / D * jnp.sum(jnp.square(e))

    return pl.pallas_call(
        body, name="loss", grid=(n // TT,),
        out_shape=[jax.ShapeDtypeStruct((n, D), F32), jax.ShapeDtypeStruct((8, 128), F32)],
        in_specs=[pl.BlockSpec((TT, D), lambda i: (i, 0)), pl.BlockSpec((TT, D), lambda i: (jnp.maximum(i - 1, 0), 0))],
        out_specs=[pl.BlockSpec((TT, D), lambda i: (i, 0)), pl.BlockSpec((8, 128), lambda i: (0, 0))],
        compiler_params=_cparams(1))(xf, target)


def f_prep(lr, li, ldt, bre, bim, cre, cim):
    gi = lax.broadcasted_iota(jnp.int32, (S5W // S5H, S5P), 0)
    gc = lax.broadcasted_iota(jnp.int32, (S5W // S5H, S5P), 1) // 64
    dt = jnp.exp(jnp.sum(jnp.where(gi == gc, ldt, 0.0), axis=0, keepdims=True))
    mag, ang = jnp.exp(lr * dt), li * dt
    ar, ai = mag * jnp.cos(ang), mag * jnp.sin(ang)
    den = lr * lr + li * li
    nr, ni = ar - 1.0, ai
    cr = (nr * lr + ni * li) / den
    ci = (ni * lr - nr * li) / den
    bbr = cr * bre - ci * bim
    bbi = cr * bim + ci * bre
    rg = lax.broadcasted_iota(jnp.int32, (S5W, S5P), 0) // S5H
    cg = lax.broadcasted_iota(jnp.int32, (S5W, S5P), 1) // 64
    mask = (rg == cg).astype(F32)
    blk = lambda a: jnp.concatenate([a] * (S5W // S5H), axis=0) * mask
    return ar, ai, blk(bbr), blk(bbi), blk(cre), blk(-cim)


def s5_prep(lr, li, ldt, bre, bim, cre, cim):
    n2 = lr.shape[0]

    def body(lr_r, li_r, ldt_r, bre_r, bim_r, cre_r, cim_r, a_ref, b_ref, c_ref):
        ar, ai, bbr, bbi, cbr, cbi = f_prep(lr_r[...], li_r[...], ldt_r[...], bre_r[...], bim_r[...], cre_r[...], cim_r[...])
        a_ref[0], a_ref[1] = ar, ai
        b_ref[0], b_ref[1] = bbr.astype(BF16), bbi.astype(BF16)
        c_ref[0], c_ref[1] = cbr.astype(BF16), cbi.astype(BF16)

    sp = lambda r, c: pl.BlockSpec((None, r, c), lambda i: (i, 0, 0))
    sp4 = lambda r, c: pl.BlockSpec((None, 2, r, c), lambda i: (i, 0, 0, 0))
    return pl.pallas_call(
        body, name="s5_prep", grid=(n2,),
        out_shape=[jax.ShapeDtypeStruct((n2, 2, 1, S5P), F32), jax.ShapeDtypeStruct((n2, 2, S5W, S5P), BF16),
                   jax.ShapeDtypeStruct((n2, 2, S5W, S5P), BF16)],
        in_specs=[sp(1, S5P), sp(1, S5P), sp(32, 1), sp(S5H, S5P), sp(S5H, S5P), sp(S5H, S5P), sp(S5H, S5P)],
        out_specs=[sp4(1, S5P), sp4(S5W, S5P), sp4(S5W, S5P)], compiler_params=_cparams(1),
    )(lr, li, ldt, bre, bim, cre, cim)


def s5_prep_bwd(lr, li, ldt, bre, bim, cre, cim, da, db, dc):
    n2 = lr.shape[0]

    def body(lr_r, li_r, ldt_r, bre_r, bim_r, cre_r, cim_r, da_r, db_r, dc_r, *outs):
        args = [r[...] for r in (lr_r, li_r, ldt_r, bre_r, bim_r, cre_r, cim_r)]
        _, vjp = jax.vjp(f_prep, *args)
        for o_ref, g in zip(outs, vjp((da_r[0], da_r[1], db_r[0], db_r[1], dc_r[0], dc_r[1]))):
            o_ref[...] = g

    sp = lambda r, c: pl.BlockSpec((None, r, c), lambda i: (i, 0, 0))
    sp4 = lambda r, c: pl.BlockSpec((None, 2, r, c), lambda i: (i, 0, 0, 0))
    ins = [sp(1, S5P), sp(1, S5P), sp(32, 1), sp(S5H, S5P), sp(S5H, S5P), sp(S5H, S5P), sp(S5H, S5P)]
    return pl.pallas_call(
        body, name="s5_prep_bwd", grid=(n2,),
        out_shape=[jax.ShapeDtypeStruct(a.shape, F32) for a in (lr, li, ldt, bre, bim, cre, cim)],
        in_specs=ins + [sp4(1, S5P), sp4(S5W, S5P), sp4(S5W, S5P)], out_specs=ins, compiler_params=_cparams(1),
    )(lr, li, ldt, bre, bim, cre, cim, da, db, dc)


S5_DIAG = 2
_CU, _CP = S5W // S5_DIAG, S5P // S5_DIAG


def _bd_nn(u, w_ref, k):
    return jnp.concatenate([_dot(u[:, j * _CU:(j + 1) * _CU], w_ref[k, j * _CU:(j + 1) * _CU, j * _CP:(j + 1) * _CP], NN)
                            for j in range(S5_DIAG)], axis=1)


def _bd_nt(x, w_ref, k):
    return jnp.concatenate([_dot(x[:, j * _CP:(j + 1) * _CP], w_ref[k, j * _CU:(j + 1) * _CU, j * _CP:(j + 1) * _CP], NT)
                            for j in range(S5_DIAG)], axis=1)


def _bd_tn_acc(acc_ref, k, a, g):
    for j in range(S5_DIAG):
        acc_ref[k, j * _CU:(j + 1) * _CU, j * _CP:(j + 1) * _CP] += _dot(a[:, j * _CU:(j + 1) * _CU],
                                                                         g[:, j * _CP:(j + 1) * _CP], TN)


def _scan_rows(xr_ref, xi_ref, ar, ai, desc, cr_ref, ci_ref):
    unroll = 8

    def group(gi, carry):
        cr, ci = carry
        base = gi * unroll
        for j in range(unroll):
            t = TT - 1 - (base + j) if desc else base + j
            nr = ar * cr - ai * ci + xr_ref[pl.ds(t, 1), :]
            ni = ar * ci + ai * cr + xi_ref[pl.ds(t, 1), :]
            xr_ref[pl.ds(t, 1), :] = nr
            xi_ref[pl.ds(t, 1), :] = ni
            cr, ci = nr, ni
        return cr, ci

    cr, ci = lax.fori_loop(0, TT // unroll, group, (cr_ref[...], ci_ref[...]))
    cr_ref[...] = cr
    ci_ref[...] = ci


def s5_fwd(name, proj, a, bb, cb, ld, rev):
    n = proj.shape[0]
    nt = n // TT

    def body(u_ref, a_ref, b_ref, c_ref, xr_ref, xi_ref, y_ref, cr_ref, ci_ref):
        @pl.when(pl.program_id(0) == 0)
        def _():
            cr_ref[...] = jnp.zeros_like(cr_ref)
            ci_ref[...] = jnp.zeros_like(ci_ref)

        u = u_ref[...]
        xr_ref[...] = _bd_nn(u, b_ref, 0)
        xi_ref[...] = _bd_nn(u, b_ref, 1)
        _scan_rows(xr_ref, xi_ref, a_ref[0], a_ref[1], rev, cr_ref, ci_ref)
        y_ref[...] = _bd_nt(xr_ref[...], c_ref, 0) + _bd_nt(xi_ref[...], c_ref, 1)

    tile = lambda w: pl.BlockSpec((TT, w), lambda s: (_scan_tile(s, nt, rev), 0))
    par = lambda r: pl.BlockSpec((None, 2, r, S5P), lambda s: (ld, 0, 0, 0))
    return pl.pallas_call(
        body, name=name, grid=(nt,),
        out_shape=[jax.ShapeDtypeStruct((n, S5P), F32), jax.ShapeDtypeStruct((n, S5P), F32),
                   jax.ShapeDtypeStruct((n, S5W), F32)],
        in_specs=[tile(S5W), par(1), par(S5W), par(S5W)], out_specs=[tile(S5P), tile(S5P), tile(S5W)],
        scratch_shapes=[pltpu.VMEM((1, S5P), F32), pltpu.VMEM((1, S5P), F32)], compiler_params=_cparams(1),
    )(proj, a, bb, cb)


def s5_bwd(name, proj, dy, xr, xi, a, bb, cb, ld, rev, totals):
    n = proj.shape[0]
    nt = n // TT
    tb = TT // 8

    def tile_of(s):
        return _scan_tile(nt - 1 - s, nt, rev)

    def edge_of(s):
        pos = nt - 1 - s
        prev = _scan_tile(jnp.maximum(pos - 1, 0), nt, rev)
        return prev * tb if rev else jnp.maximum(pos * tb - 1, 0)

    def body(u_ref, dy_ref, xr_ref, xi_ref, er_ref, ei_ref, a_ref, b_ref, c_ref, _ta, _tb, _tc,
             du_ref, da_ref, db_ref, dc_ref, gr_ref, gi_ref, cr_ref, ci_ref):
        s = pl.program_id(0)

        @pl.when(s == 0)
        def _():
            cr_ref[...] = jnp.zeros_like(cr_ref)
            ci_ref[...] = jnp.zeros_like(ci_ref)
            da_ref[...] = jnp.zeros_like(da_ref)
            db_ref[...] = jnp.zeros_like(db_ref)
            dc_ref[...] = jnp.zeros_like(dc_ref)

        dyv, u = dy_ref[...], u_ref[...]
        xrv, xiv = xr_ref[...], xi_ref[...]
        gr_ref[...] = _bd_nn(dyv, c_ref, 0)
        gi_ref[...] = _bd_nn(dyv, c_ref, 1)
        _bd_tn_acc(dc_ref, 0, dyv, xrv)
        _bd_tn_acc(dc_ref, 1, dyv, xiv)
        _scan_rows(gr_ref, gi_ref, a_ref[0], -a_ref[1], not rev, cr_ref, ci_ref)
        g_r, g_i = gr_ref[...], gi_ref[...]
        rows = lax.broadcasted_iota(jnp.int32, (TT, 1), 0)
        live = jnp.where(s == nt - 1, 0.0, 1.0)
        if rev:
            pr = jnp.where(rows == TT - 1, er_ref[0:1, :] * live, _roll_rows(xrv, -1))
            pi = jnp.where(rows == TT - 1, ei_ref[0:1, :] * live, _roll_rows(xiv, -1))
        else:
            pr = jnp.where(rows == 0, er_ref[7:8, :] * live, _roll_rows(xrv, 1))
            pi = jnp.where(rows == 0, ei_ref[7:8, :] * live, _roll_rows(xiv, 1))
        da_ref[0] += jnp.sum(g_r * pr + g_i * pi, axis=0, keepdims=True)
        da_ref[1] += jnp.sum(g_i * pr - g_r * pi, axis=0, keepdims=True)
        du_ref[...] = _bd_nt(g_r, b_ref, 0) + _bd_nt(g_i, b_ref, 1)
        _bd_tn_acc(db_ref, 0, u, g_r)
        _bd_tn_acc(db_ref, 1, u, g_i)

    tile = lambda w: pl.BlockSpec((TT, w), lambda s: (tile_of(s), 0))
    edge = pl.BlockSpec((8, S5P), lambda s: (edge_of(s), 0))
    par = lambda r: pl.BlockSpec((None, 2, r, S5P), lambda s: (ld, 0, 0, 0))
    whole = pl.BlockSpec(memory_space=pl.ANY)
    return pl.pallas_call(
        body, name=name, grid=(nt,),
        out_shape=[jax.ShapeDtypeStruct((n, S5W), F32)] + [jax.ShapeDtypeStruct(t.shape, F32) for t in totals],
        in_specs=[tile(S5W), tile(S5W), tile(S5P), tile(S5P), edge, edge, par(1), par(S5W), par(S5W), whole, whole, whole],
        out_specs=[tile(S5W), par(1), par(S5W), par(S5W)], input_output_aliases={9: 1, 10: 2, 11: 3},
        scratch_shapes=[pltpu.VMEM((TT, S5P), F32), pltpu.VMEM((TT, S5P), F32),
                        pltpu.VMEM((1, S5P), F32), pltpu.VMEM((1, S5P), F32)], compiler_params=_cparams(1),
    )(proj, dy, xr, xi, xr, xi, a, bb, cb, *totals)


def gla_tile(r, v, qr, lb, sts, rev):
    ncc = TT // CK
    f = lb + (1.0 - lb) * jax.nn.sigmoid(r)
    k, lf, q = 1.0 - f, jnp.log(f), jax.nn.silu(qr)
    rows = lax.broadcasted_iota(jnp.int32, (TT, 1), 0)
    pos = rows % CK
    b = lf
    for s in (1, 2, 4, 8, 16):
        m = ((pos < CK - s) if rev else (pos >= s)).astype(F32)
        b = b + shift_rows(b, m, -s if rev else s)
    etot = [jnp.sum(lf[c * CK:(c + 1) * CK], axis=0, keepdims=True) for c in range(ncc)]
    e = jnp.concatenate([jnp.broadcast_to(t, (CK, HGW)) for t in etot], axis=0)
    kd, qe, qa = k * jnp.exp(e - b), q * jnp.exp(b), q * jnp.exp(b - e)
    cm = [(rows // CK == c).astype(F32) for c in range(ncc)]
    r2 = lax.broadcasted_iota(jnp.int32, (TT, TT), 0)
    c2 = lax.broadcasted_iota(jnp.int32, (TT, TT), 1)
    amask = (r2 // CK == c2 // CK) & ((r2 <= c2) if rev else (r2 >= c2))
    outs, new_sts = [], []
    for h in range(NH):
        ln = slice(h * HD, (h + 1) * HD)
        kdh, qeh, vh = kd[:, ln], qe[:, ln], v[:, ln]
        att = jnp.where(amask, mm_nt(qa[:, ln], kdh), 0.0)
        ds = mm_tn(jnp.concatenate([kdh * cm[c] for c in range(ncc)], axis=1), vh)
        st, starts = sts[h], [None] * ncc
        for c in (reversed(range(ncc)) if rev else range(ncc)):
            starts[c] = st
            dec = jnp.transpose(jnp.broadcast_to(jnp.exp(etot[c][:, ln]), (HD, HD)))
            st = dec * st + ds[c * HD:(c + 1) * HD]
        new_sts.append(st)
        qex = jnp.concatenate([qeh * cm[c] for c in range(ncc)], axis=1)
        outs.append(mm_nn(att, vh) + mm_nn(qex, jnp.concatenate(starts, axis=0)))
    return jnp.concatenate(outs, axis=1), new_sts


def _gla_specs(n, rev, order):
    nt = n // TT
    fcol = 2 if rev else 1
    tile = lambda cbk: pl.BlockSpec((TT, HGW), lambda s: (order(s), cbk))
    return nt, [tile(fcol), tile(3), tile(4)], tile(0)


def gla_fwd(name, proj, lb, rev):
    n = proj.shape[0]
    nt, in_tiles, out_tile = _gla_specs(n, rev, lambda s: _scan_tile(s, n // TT, rev))

    def body(r_ref, v_ref, q_ref, lb_ref, o_ref, st_ref, s_ref):
        @pl.when(pl.program_id(0) == 0)
        def _():
            s_ref[...] = jnp.zeros_like(s_ref)

        sts = [s_ref[h] for h in range(NH)]
        for h in range(NH):
            st_ref[h] = sts[h]
        o, new = gla_tile(r_ref[...], v_ref[...], q_ref[...], lb_ref[...], sts, rev)
        o_ref[...] = o
        for h in range(NH):
            s_ref[h] = new[h]

    st_spec = pl.BlockSpec((None, NH, HD, HD), lambda s: (_scan_tile(s, nt, rev), 0, 0, 0))
    return pl.pallas_call(
        body, name=name, grid=(nt,),
        out_shape=[jax.ShapeDtypeStruct((n, HGW), F32), jax.ShapeDtypeStruct((nt, NH, HD, HD), F32)],
        in_specs=in_tiles + [pl.BlockSpec((1, HGW), lambda s: (0, 0))], out_specs=[out_tile, st_spec],
        scratch_shapes=[pltpu.VMEM((NH, HD, HD), F32)], compiler_params=_cparams(1),
    )(proj, proj, proj, lb)


def gla_bwd(name, proj, lb, st_all, do, rev):
    n = proj.shape[0]
    order = lambda s: _scan_tile(n // TT - 1 - s, n // TT, rev)
    nt, in_tiles, out_tile = _gla_specs(n, rev, order)

    def body(r_ref, v_ref, q_ref, lb_ref, st_ref, do_ref, dr_ref, dv_ref, dq_ref, dlb_ref, ds_ref):
        @pl.when(pl.program_id(0) == 0)
        def _():
            ds_ref[...] = jnp.zeros_like(ds_ref)
            dlb_ref[...] = jnp.zeros_like(dlb_ref)

        _, vjp = jax.vjp(functools.partial(gla_tile, rev=rev), r_ref[...], v_ref[...], q_ref[...], lb_ref[...],
                         [st_ref[h] for h in range(NH)])
        dr, dv, dq, dlb, dsts = vjp((do_ref[...], [ds_ref[h] for h in range(NH)]))
        dr_ref[...] = dr
        dv_ref[...] = dv
        dq_ref[...] = dq
        dlb_ref[...] += dlb
        for h in range(NH):
            ds_ref[h] = dsts[h]

    st_spec = pl.BlockSpec((None, NH, HD, HD), lambda s: (order(s), 0, 0, 0))
    row = pl.BlockSpec((1, HGW), lambda s: (0, 0))
    return pl.pallas_call(
        body, name=name, grid=(nt,),
        out_shape=[jax.ShapeDtypeStruct((n, HGW), F32)] * 3 + [jax.ShapeDtypeStruct((1, HGW), F32)],
        in_specs=in_tiles + [row, st_spec, out_tile], out_specs=[out_tile] * 3 + [row],
        scratch_shapes=[pltpu.VMEM((NH, HD, HD), F32)], compiler_params=_cparams(1),
    )(proj, proj, proj, lb, st_all, do)


def f_lb(rows):
    mx = functools.reduce(jnp.maximum, rows)
    ex = [jnp.exp(r - mx) for r in rows]
    tot = functools.reduce(jnp.add, ex)
    out, acc = [jnp.zeros_like(rows[0])], None
    for e in ex[1:]:
        acc = e / tot if acc is None else acc + e / tot
        out.append(acc)
    return out


def lb_call(hg, dlb=None):
    nl = hg.shape[0]

    def body(*refs):
        rows = [refs[0][l:l + 1, :] for l in range(nl)]
        if dlb is None:
            res = f_lb(rows)
        else:
            _, vjp = jax.vjp(f_lb, rows)
            (res,) = vjp([refs[1][l:l + 1, :] for l in range(nl)])
        for l in range(nl):
            refs[-1][l:l + 1, :] = res[l]

    args = (hg,) if dlb is None else (hg, dlb)
    return pl.pallas_call(body, name="lower_bounds" if dlb is None else "lower_bounds_bwd",
                          out_shape=jax.ShapeDtypeStruct(hg.shape, F32))(*args)


def mod_fwd(craw, w_mod, b_cols):
    nl, _, cols = w_mod.shape

    def body(c_ref, w_ref, b_ref, o_ref):
        o_ref[...] = _dot(jax.nn.silu(c_ref[...]), w_ref[...], NN) + b_ref[...]

    return pl.pallas_call(
        body, name="mod_fwd", grid=(nl,), out_shape=jax.ShapeDtypeStruct((nl, 16, cols), F32),
        in_specs=[pl.BlockSpec((16, D), lambda l: (0, 0)), pl.BlockSpec((None, D, cols), lambda l: (l, 0, 0)),
                  pl.BlockSpec((None, 1, cols), lambda l: (l, 0, 0))],
        out_specs=pl.BlockSpec((None, 16, cols), lambda l: (l, 0, 0)), compiler_params=_cparams(1))(craw, w_mod, b_cols)


def mod_bwd(craw, w_mod, g):
    nl, _, cols = w_mod.shape

    def body(c_ref, w_ref, g_ref, dw_ref, dc_ref, acc_ref):
        l = pl.program_id(0)

        @pl.when(l == 0)
        def _():
            acc_ref[...] = jnp.zeros_like(acc_ref)

        c = c_ref[...]
        s, vjp = jax.vjp(jax.nn.silu, c)
        dw_ref[...] = _dot(s, g_ref[...], TN)
        acc_ref[...] += _dot(g_ref[...], w_ref[...], NT)

        @pl.when(l == nl - 1)
        def _():
            dc_ref[...] = vjp(acc_ref[...])[0]

    return pl.pallas_call(
        body, name="mod_bwd", grid=(nl,),
        out_shape=[jax.ShapeDtypeStruct(w_mod.shape, F32), jax.ShapeDtypeStruct((16, D), F32)],
        in_specs=[pl.BlockSpec((16, D), lambda l: (0, 0)), pl.BlockSpec((None, D, cols), lambda l: (l, 0, 0)),
                  pl.BlockSpec((None, 16, cols), lambda l: (l, 0, 0))],
        out_specs=[pl.BlockSpec((None, D, cols), lambda l: (l, 0, 0)), pl.BlockSpec((16, D), lambda l: (0, 0))],
        scratch_shapes=[pltpu.VMEM((16, D), F32)], compiler_params=_cparams(1))(craw, w_mod, g)


def sum_parts(parts):
    def body(p_ref, o_ref):
        acc = p_ref[0]
        for k in range(1, parts.shape[0]):
            acc = acc + p_ref[k]
        o_ref[...] = acc

    return pl.pallas_call(body, name="sum_small_grads", out_shape=jax.ShapeDtypeStruct(parts.shape[1:], parts.dtype),
                          compiler_params=pltpu.CompilerParams(vmem_limit_bytes=VMEM_MB << 20))(parts)


def _adamw_body(s):
    def body(w_ref, m_ref, v_ref, g_ref, *rest):
        go_ref, d_ref, mo_ref, vo_ref = rest[-4:]
        g = g_ref[0].astype(F32)
        for k in range(1, s):
            g = g + g_ref[k].astype(F32)
        m_new = B1 * m_ref[...] + (1.0 - B1) * g
        v_new = B2 * v_ref[...] + (1.0 - B2) * jnp.square(g)
        m_hat = m_new / (1.0 - B1 ** STEP)
        v_hat = v_new / (1.0 - B2 ** STEP)
        go_ref[...] = g
        d_ref[...] = -LR * (m_hat / (jnp.sqrt(v_hat) + EPS) + WD * w_ref[...])
        mo_ref[...] = m_new
        vo_ref[...] = v_new

    return body


def _row_tile(r):
    return max([t for t in range(8, 257, 8) if r % t == 0], default=r)


def adamw(name, w, m, v, gs):
    r, c = w.shape
    s = gs.shape[0]
    tr = _row_tile(r)
    blk = pl.BlockSpec((tr, c), lambda i: (i, 0))
    return pl.pallas_call(
        _adamw_body(s), name=name, grid=(r // tr,), out_shape=[jax.ShapeDtypeStruct((r, c), F32)] * 4,
        in_specs=[blk, blk, blk, pl.BlockSpec((s, tr, c), lambda i: (0, i, 0))], out_specs=[blk] * 4,
        compiler_params=_cparams(1))(w, m, v, gs)


def adamw_layer(name, l, w, m, v, gs, outs):
    _, r, c = w.shape
    s = gs.shape[0]
    tr = _row_tile(r)
    blk = pl.BlockSpec((None, tr, c), lambda i: (l, i, 0))
    whole = pl.BlockSpec(memory_space=pl.ANY)
    return pl.pallas_call(
        _adamw_body(s), name=name, grid=(r // tr,), out_shape=[jax.ShapeDtypeStruct(w.shape, F32)] * 4,
        in_specs=[blk, blk, blk, pl.BlockSpec((s, tr, c), lambda i: (0, i, 0))] + [whole] * 4, out_specs=[blk] * 4,
        input_output_aliases={4 + i: i for i in range(4)}, compiler_params=_cparams(1))(w, m, v, gs, *outs)


SMALL = ["c_ctx", "b_mod", "s5_lam_re", "s5_lam_im", "s5_log_dt", "s5_b_re", "s5_b_im", "s5_c_re", "s5_c_im", "s5_d",
         "b_glu", "hg_lb", "hg_norm_w", "ln1_g", "ln1_b", "conv_b", "ln2_g", "ln2_b"]
BIG = ["w_in", "w_glu", "w_out", "w_up", "w_down"]
WEIGHTS = ["c_ctx", "w_mod", "b_mod", "w_in", "s5_lam_re", "s5_lam_im", "s5_log_dt", "s5_b_re", "s5_b_im", "s5_c_re",
           "s5_c_im", "s5_d", "w_glu", "b_glu", "hg_lb", "hg_norm_w", "w_out", "ln1_g", "ln1_b", "w_up", "conv_w",
           "conv_b", "w_down", "ln2_g", "ln2_b"]
PACK_W = 1024


def _pack_rows(k):
    return -(-k // (8 * PACK_W)) * 8


def _pack(arrs):
    parts = []
    for a in arrs:
        flat = a.reshape(-1)
        r = _pack_rows(flat.shape[0])
        parts.append(jnp.pad(flat, (0, r * PACK_W - flat.shape[0])).reshape(r, PACK_W))
    used = sum(q.shape[0] for q in parts)
    parts.append(jnp.zeros((-used % (8 * N_DEV), PACK_W), parts[0].dtype))
    return jnp.concatenate(parts, axis=0)


def _unpack(p, shapes):
    out, o = [], 0
    for s in shapes:
        k = math.prod(s)
        r = _pack_rows(k)
        out.append(p[o:o + r].reshape(-1)[:k].reshape(s))
        o += r
    return out


def _gathered_cols(g):
    return jnp.moveaxis(g, 0, 2).reshape(g.shape[1], g.shape[2], -1)


def _gathered_rows(g):
    return jnp.moveaxis(g, 0, 1).reshape(g.shape[1], -1, g.shape[3])


def _step(p):
    nl = p["w_in"].shape[0]
    me = lax.axis_index("x") * 4 + lax.axis_index("y") * 2 + lax.axis_index("c")
    xc0 = jnp.concatenate([p["ctx"][0], p["x"][0]], axis=0)
    n = xc0.shape[0]
    target = p["loss_target"][0]

    hg3 = jnp.stack([p[k].reshape(-1) for k in ("hg_lb", "m_hg_lb", "v_hg_lb")])
    g_cw, g_c, g_hg = _exchange("gather_inputs", [p["conv_w"], p["c"], hg3], False)
    conv_w = _gathered_cols(g_cw)
    hg_full = jnp.moveaxis(g_hg.reshape(N_DEV, 3, nl, 2, -1), 0, 3).reshape(3, nl, 2 * HGW)
    lb_all = lb_call(hg_full[0])

    craw = jnp.concatenate([g_c.reshape(N_DEV, D), jnp.broadcast_to(p["c_ctx"][None], (8, D))], axis=0)
    cols = p["w_mod"].shape[2]
    b_cols = lax.dynamic_slice_in_dim(p["b_mod"], me * cols, cols, axis=1)[:, None, :]
    (g_mod,) = _exchange("gather_mod", [mod_fwd(craw, p["w_mod"], b_cols)], False)
    mod_all = jnp.moveaxis(g_mod, 0, 2).reshape(nl, 16, 6 * D)
    mod_x = lax.dynamic_index_in_dim(mod_all, me, axis=1, keepdims=False)
    mod2 = jnp.stack([mod_all[:, 8], mod_x], axis=1)
    mvec = lambda l, k: mod2[l, :, k * D:(k + 1) * D][:, None, :]
    gathers = {}

    def start_gather(l, part, ks, after):
        srcs = [p[k][l].astype(BF16) for k in ks]
        gathers[l, part], token = exchange_start(f"gather_start{l}{part}", srcs, [_own_block_set(s, False) for s in srcs],
                                                 False, after=after)
        return token[0, 0]

    start_gather(0, "a", BIG[:1], g_mod)
    start_gather(0, "b", BIG[1:], g_mod)

    def gathered(l, part, after):
        res = exchange_wait(f"gather_wait{l}{part}", gathers[l, part], after, False)
        ks = {"a": BIG[:1], "b": BIG[1:], "": BIG}[part]
        cols_ = lambda g: jnp.moveaxis(g, 0, 1).reshape(g.shape[1], -1)
        rows_ = lambda g: g.reshape(-1, g.shape[2])
        return {k: (cols_ if k in ("w_in", "w_up") else rows_)(g) for k, g in zip(ks, res)}

    zvec = jnp.zeros((2, 1, D), F32)
    row = lambda a: a.reshape(1, 1, -1)

    to_hp = lambda a: jnp.moveaxis(a, -1, 2).reshape(nl * 2, S5H, S5P)
    prep_in = [p["s5_lam_re"].reshape(nl * 2, 1, S5P), p["s5_lam_im"].reshape(nl * 2, 1, S5P),
               p["s5_log_dt"].reshape(nl * 2, 32, 1), to_hp(p["s5_b_re"]), to_hp(p["s5_b_im"]),
               jnp.swapaxes(p["s5_c_re"], 2, 3).reshape(nl * 2, S5H, S5P),
               jnp.swapaxes(p["s5_c_im"], 2, 3).reshape(nl * 2, S5H, S5P)]
    s5a, s5b, s5c = s5_prep(*prep_in)

    T1 = lambda a, w=D, cb=_c0: (a, w, cb)
    saved = []
    xc = xc0
    (h,) = block_fwd("mod0", f_mod, n, [T1(xc)], [T1(mvec(0, 0)), T1(mvec(0, 1))], [(D, BF16, D, _c0)])
    w_in, w_glu, w_out, w_up, w_down = ([None] * nl for _ in range(5))
    for l in range(nl):
        wl = gathered(l, "", xc) if l else gathered(0, "a", s5a)
        w_in[l] = wl["w_in"]
        proj = dense_nn(f"in_proj{l}", h, w_in[l])
        s5 = [s5_fwd(f"s5_fwd{l}_{d}", proj, s5a, s5b, s5c, 2 * l + d, d == 1) for d in range(2)]
        lbs = [lb_all[l, d * HGW:(d + 1) * HGW][None] for d in range(2)]
        gl = [gla_fwd(f"gla_fwd{l}_{d}", proj, lbs[d], d == 1) for d in range(2)]
        if l == 0:
            wl = gathered(0, "b", gl[1][0])
        w_glu[l], w_out[l], w_up[l], w_down[l] = wl["w_glu"], wl["w_out"], wl["w_up"], wl["w_down"]
        started = start_gather(l + 1, "", BIG, w_down[l]) if l + 1 < nl else 0.0
        mix_t = [T1(proj, S5W), T1(proj, HGW, lambda j: 5), T1(s5[0][2], S5W), T1(s5[1][2], S5W),
                 T1(gl[0][0], HGW), T1(gl[1][0], HGW)]
        mix_p = [T1(row(p["s5_d"][l]) + started, S5W), T1(w_glu[l][None], S5W), T1(row(p["b_glu"][l]), S5W),
                 T1(row(p["hg_norm_w"][l]), HD)]
        (y,) = block_fwd(f"mix{l}", f_mix, n, mix_t, mix_p, [(D, BF16, D, _c0)])
        z = dense_nn(f"out_proj{l}", y, w_out[l])
        ln1_p = [T1(mvec(l, 2)), T1(row(p["ln1_g"][l])), T1(row(p["ln1_b"][l])), T1(mvec(l, 3)), T1(mvec(l, 4))]
        x1, h2 = block_fwd(f"ln1_{l}", f_ln, n, [T1(xc), T1(z)], ln1_p, [(D, F32, D, _c0), (D, BF16, D, _c0)])
        up = dense_nn(f"up_proj{l}", h2, w_up[l])
        ct = DFF // 2
        act_t = [T1(up, ct, lambda j: j), T1(up, ct, lambda j: j + 2)]
        cb2 = p["conv_b"][l].reshape(1, 1, -1)
        act_p = [T1(conv_w[l][None, :, :DFF], ct, lambda j: j), T1(conv_w[l][None, :, DFF:], ct, lambda j: j),
                 T1(cb2[:, :, :DFF], ct, lambda j: j), T1(cb2[:, :, DFF:], ct, lambda j: j)]
        (act,) = block_fwd(f"act{l}", f_act, n, act_t, act_p, [(DFF, BF16, ct, lambda j: j)], n_col=2)
        dn = dense_nn(f"down_proj{l}", act, w_down[l])
        nxt = (mvec(l + 1, 0), mvec(l + 1, 1)) if l + 1 < nl else (zvec, zvec)
        ln2_p = [T1(mvec(l, 5)), T1(row(p["ln2_g"][l])), T1(row(p["ln2_b"][l])), T1(nxt[0]), T1(nxt[1])]
        x2, hn = block_fwd(f"ln2_{l}", f_ln, n, [T1(x1), T1(dn)], ln2_p, [(D, F32, D, _c0), (D, BF16, D, _c0)])
        saved.append(dict(xc=xc, h=h, proj=proj, s5=s5, gl=gl, lbs=lbs, mix_t=mix_t, mix_p=mix_p, y=y, z=z,
                          ln1_p=ln1_p, x1=x1, h2=h2, act_t=act_t, act_p=act_p, act=act, dn=dn, ln2_p=ln2_p))
        xc, h = x2, hn

    dxc, loss_part = loss_and_grad(xc, target)
    loss = lax.psum(loss_part[0, 0], AXES)

    g = {k: [None] * nl for k in ("w_in", "w_glu", "w_out", "w_up", "w_down", "conv_w", "conv_b", "s5_d", "b_glu",
                                  "hg_norm_w", "ln1_g", "ln1_b", "ln2_g", "ln2_b", "dlb", "s5")}
    dmod = [[None] * 6 for _ in range(nl)]
    scatters = [[] for _ in range(nl)]
    s5_totals = [lax.empty((2 * nl, 2, r, S5P), F32) for r in (1, S5W, S5W)]
    dh_next = jnp.zeros((n, D), F32)
    fgrad = (D, F32, D, _c0)
    for l in reversed(range(nl)):
        sv = saved[l]
        (dx1, d_dn), dp = block_bwd(f"ln2_bwd{l}", f_ln, n, [T1(sv["x1"]), T1(sv["dn"])], sv["ln2_p"],
                                    [T1(dxc), T1(dh_next)], [fgrad, fgrad])
        dmod[l][5], g["ln2_g"][l], g["ln2_b"][l] = dp[0], dp[1], dp[2]
        if l + 1 < nl:
            dmod[l + 1][0], dmod[l + 1][1] = dp[3], dp[4]
        dact = dense_nt(f"down_bwd{l}", d_dn, w_down[l])
        g["w_down"][l] = dense_tn(f"down_wgrad{l}", sv["act"], d_dn)
        ct = DFF // 2
        cj = lambda j: j
        (dua, dug), dp = block_bwd(f"act_bwd{l}", f_act, n, sv["act_t"], sv["act_p"], [T1(dact, ct, cj)],
                                   [(DFF, BF16, ct, cj), (DFF, BF16, ct, cj)], n_col=2)
        g["conv_w"][l] = jnp.concatenate([dp[0][0], dp[1][0]], axis=-1)
        g["conv_b"][l] = jnp.concatenate([dp[2][0, 0], dp[3][0, 0]], axis=-1)
        dh2 = dense_nt2(f"up_bwd{l}", dua, dug, w_up[l])
        g["w_up"][l] = [dense_tn(f"up_wgrad{l}{part}", sv["h2"], du_) for part, du_ in (("a", dua), ("g", dug))]
        (dxc, dz), dp = block_bwd(f"ln1_bwd{l}", f_ln, n, [T1(sv["xc"]), T1(sv["z"])], sv["ln1_p"],
                                  [T1(dx1), T1(dh2)], [fgrad, fgrad])
        dmod[l][2], g["ln1_g"][l], g["ln1_b"][l], dmod[l][3], dmod[l][4] = dp
        dy = dense_nt(f"out_bwd{l}", dz, w_out[l])
        g["w_out"][l] = dense_tn(f"out_wgrad{l}", sv["y"], dz)
        half = (S5W, F32, S5W, _c0)
        (dpu, dpg, dys, dos), dp = block_bwd(f"mix_bwd{l}", f_mix, n, sv["mix_t"], sv["mix_p"], [T1(dy)],
                                                   [half, half, half, None, half, None])
        g["s5_d"][l], g["w_glu"][l], g["b_glu"][l], g["hg_norm_w"][l] = dp[0][0, 0], dp[1][0], dp[2][0, 0], dp[3][0, 0]

        def start_scatter(tag, ks):
            by_cols = lambda a, nb=N_DEV: jnp.moveaxis(a.reshape(a.shape[0], nb, -1), 1, 0)
            by_rows = lambda a: a.reshape(N_DEV, -1, a.shape[1])

            def blocks(k):
                if k == "w_up":
                    return jnp.concatenate([by_cols(half, N_DEV // 2) for half in g[k][l]], axis=0)
                return (by_rows if k in ("w_glu", "w_out", "w_down") else by_cols)(g[k][l].astype(BF16 if k != "conv_w" else F32))

            sends = [blocks(k) for k in ks]
            handle, token = exchange_start(f"scatter_start{l}{tag}", sends, [_own_block_set(s, True) for s in sends], True)
            scatters[l].append((ks, handle))
            return token[0, 0]

        lbs_b = sv["lbs"]
        if l == 0:
            started = start_scatter("a", ["w_glu", "w_out", "w_up", "w_down", "conv_w"])
            lbs_b = [b + started for b in lbs_b]
        gb = [gla_bwd(f"gla_bwd{l}_{d}", sv["proj"], lbs_b[d], sv["gl"][d][1], dos, d == 1) for d in range(2)]
        g["dlb"][l] = jnp.concatenate([gb[0][3], gb[1][3]], axis=-1)[0]
        sb = [None, None]
        for d in range(2):
            sb[d], *s5_totals = s5_bwd(f"s5_bwd{l}_{d}", sv["proj"], dys, sv["s5"][d][0], sv["s5"][d][1], s5a, s5b, s5c,
                                       2 * l + d, d == 1, s5_totals)
        asm_t = [T1(dpu, S5W), T1(sb[0], S5W), T1(sb[1], S5W), T1(gb[0][0], HGW), T1(gb[1][0], HGW),
                 T1(gb[0][1], HGW), T1(gb[1][1], HGW), T1(gb[0][2], HGW), T1(gb[1][2], HGW), T1(dpg, HGW)]
        (dproj,) = block_fwd(
            f"dproj{l}", lambda tv, pv, i: (jnp.concatenate(
                [tv[0] + tv[1] + tv[2], tv[3], tv[4], tv[5] + tv[6], tv[7] + tv[8], tv[9]], axis=-1),),
            n, asm_t, [], [(INC, BF16, INC, _c0)])
        dh_next = dense_nt(f"in_bwd{l}", dproj, w_in[l])
        g["w_in"][l] = dense_tn(f"in_wgrad{l}", sv["h"], dproj)
        started = start_scatter("b", ["w_in"]) if l == 0 else start_scatter("", BIG + ["conv_w"])
        if l:
            gate, wd_, cb_ = saved[l - 1]["ln2_p"][0]
            saved[l - 1]["ln2_p"][0] = (gate + started, wd_, cb_)
    (dxc,), dp = block_bwd("mod0_bwd", f_mod_id, n, [T1(xc0)], [T1(mvec(0, 0)), T1(mvec(0, 1))],
                           [T1(dh_next), T1(dxc)], [fgrad])
    dmod[0][0], dmod[0][1] = dp
    grad_x = dxc[n - p["x"].shape[1]:][None]

    d_prep = s5_prep_bwd(*prep_in, *s5_totals)
    from_hp = lambda a: jnp.moveaxis(a.reshape(nl, 2, S5H, S5W // S5H, 64), 2, -1)
    gs5 = {"s5_lam_re": d_prep[0].reshape(nl, 2, 32, 64), "s5_lam_im": d_prep[1].reshape(nl, 2, 32, 64),
           "s5_log_dt": d_prep[2].reshape(nl, 2, 32), "s5_b_re": from_hp(d_prep[3]), "s5_b_im": from_hp(d_prep[4]),
           "s5_c_re": jnp.swapaxes(d_prep[5].reshape(nl, 2, S5H, 32, 64), 2, 3),
           "s5_c_im": jnp.swapaxes(d_prep[6].reshape(nl, 2, S5H, 32, 64), 2, 3)}
    d_hg = lb_call(hg_full[0], jnp.stack(g["dlb"]))

    dmod_loc = jnp.stack([jnp.concatenate([dmod[l][k][:, 0] for k in range(6)], axis=-1) for l in range(nl)])
    (g_dmod,) = _exchange("gather_dmod", [dmod_loc], False)
    gcols = lax.dynamic_slice_in_dim(g_dmod, me * cols, cols, axis=3)
    g16 = jnp.concatenate([jnp.moveaxis(gcols[:, :, 1], 0, 1), jnp.moveaxis(gcols[:, :, 0], 0, 1)], axis=1)
    grad_w_mod, dcraw = mod_bwd(craw, p["w_mod"], g16)
    d_c_ctx = jnp.sum(dcraw[8:], axis=0)

    stk = lambda k: jnp.stack(g[k])
    small_g = {"c_ctx": d_c_ctx, "b_mod": dmod_loc[:, 0] + dmod_loc[:, 1], "s5_d": stk("s5_d"), "b_glu": stk("b_glu"),
               "hg_lb": d_hg.reshape(nl, 2, HGW), "hg_norm_w": stk("hg_norm_w"), "ln1_g": stk("ln1_g")[:, 0, 0],
               "ln1_b": stk("ln1_b")[:, 0, 0], "conv_b": stk("conv_b"), "ln2_g": stk("ln2_g")[:, 0, 0],
               "ln2_b": stk("ln2_b")[:, 0, 0], **gs5}
    g_pack = _pack([small_g[k] for k in SMALL])
    (g_parts,) = _exchange("scatter_small_grads", [g_pack.reshape(N_DEV, -1, PACK_W)], True)
    (g_small,) = _exchange("gather_small_grads", [sum_parts(g_parts)], False)
    g_small = g_small.reshape(1, -1, PACK_W)

    out = {}
    hgw = {"": hg_full[0].reshape(nl, 2, HGW), "m_": hg_full[1].reshape(nl, 2, HGW), "v_": hg_full[2].reshape(nl, 2, HGW)}
    full = lambda pre, k: hgw[pre] if k == "hg_lb" else p[pre + k]
    shapes = [full("", k).shape for k in SMALL]
    res = adamw("adamw_small", *[_pack([full(pre, k) for k in SMALL]) for pre in ("", "m_", "v_")], g_small)
    for kind, packed in zip(("grad_", "delta_", "new_m_", "new_v_"), res):
        for k, a in zip(SMALL, _unpack(packed, shapes)):
            if k == "hg_lb":
                a = lax.dynamic_slice_in_dim(a, me * (HGW // N_DEV), HGW // N_DEV, axis=2)
            out[kind + k] = a
    kinds = ("grad_", "delta_", "new_m_", "new_v_")
    results = {}

    def update_layer(k, l, gs):
        prev = results.get(k) or [lax.empty(p[k].shape, F32) for _ in kinds]
        results[k] = adamw_layer(f"adamw_{k}{l}", l, p[k], p["m_" + k], p["v_" + k], gs, prev)

    for l in range(nl):
        update_layer("w_mod", l, grad_w_mod[l][None])
    for l in reversed(range(nl)):
        for i, (ks, handle) in enumerate(scatters[l]):
            recv = exchange_wait(f"scatter_wait{l}_{i}", handle, g_small, True)
            for k, gsum in zip(ks, recv):
                update_layer(k, l, gsum)
    for k, res in results.items():
        for kind, a in zip(kinds, res):
            out[kind + k] = a
    return (loss, grad_x, *[out[kind + k] for kind in ("grad_", "delta_", "new_m_", "new_v_") for k in WEIGHTS])


def kernel(x, c, ctx, c_ctx, w_mod, b_mod, w_in, s5_lam_re, s5_lam_im, s5_log_dt, s5_b_re, s5_b_im, s5_c_re, s5_c_im, s5_d, w_glu, b_glu, hg_lb, hg_norm_w, w_out, ln1_g, ln1_b, w_up, conv_w, conv_b, w_down, ln2_g, ln2_b, loss_target, m_c_ctx, m_w_mod, m_b_mod, m_w_in, m_s5_lam_re, m_s5_lam_im, m_s5_log_dt, m_s5_b_re, m_s5_b_im, m_s5_c_re, m_s5_c_im, m_s5_d, m_w_glu, m_b_glu, m_hg_lb, m_hg_norm_w, m_w_out, m_ln1_g, m_ln1_b, m_w_up, m_conv_w, m_conv_b, m_w_down, m_ln2_g, m_ln2_b, v_c_ctx, v_w_mod, v_b_mod, v_w_in, v_s5_lam_re, v_s5_lam_im, v_s5_log_dt, v_s5_b_re, v_s5_b_im, v_s5_c_re, v_s5_c_im, v_s5_d, v_w_glu, v_b_glu, v_hg_lb, v_hg_norm_w, v_w_out, v_ln1_g, v_ln1_b, v_w_up, v_conv_w, v_conv_b, v_w_down, v_ln2_g, v_ln2_b):
    return _step(dict(locals()))
```

```python
import functools
import math

import jax
import jax.numpy as jnp
from jax import lax
from jax.experimental import pallas as pl
from jax.experimental.pallas import tpu as pltpu

F32, BF16 = jnp.float32, jnp.bfloat16
N_DEV = 8
AXES = ("x", "y", "c")
D = 1024
S5W = 512
S5P = 2048
S5H = 16
HGW = 512
HD = 128
NH = 4
CK = 32
DFF = 2816
GRID_W = 64
INC = 3072
ALPHA = 8.0 ** 0.25
LN_EPS = 1e-5
RMS_EPS = 1e-6
LR, B1, B2, EPS, WD, STEP = 0.001, 0.9, 0.999, 1e-08, 0.01, 10
TT = 256
VMEM_MB = 56

NN = ((1,), (0,))
NT = ((1,), (1,))
TN = ((0,), (0,))


def _cparams(n_axes):
    return pltpu.CompilerParams(dimension_semantics=("arbitrary",) * n_axes, vmem_limit_bytes=VMEM_MB << 20)


def _dot(a, b, dims):
    return lax.dot_general(a.astype(BF16), b.astype(BF16), (dims, ((), ())), preferred_element_type=F32)


@jax.custom_vjp
def mm_nn(a, b):
    return _dot(a, b, NN)


@jax.custom_vjp
def mm_nt(a, b):
    return _dot(a, b, NT)


@jax.custom_vjp
def mm_tn(a, b):
    return _dot(a, b, TN)


mm_nn.defvjp(lambda a, b: (_dot(a, b, NN), (a, b)), lambda r, g: (_dot(g, r[1], NT), _dot(r[0], g, TN)))
mm_nt.defvjp(lambda a, b: (_dot(a, b, NT), (a, b)), lambda r, g: (_dot(g, r[1], NN), _dot(g, r[0], TN)))
mm_tn.defvjp(lambda a, b: (_dot(a, b, TN), (a, b)), lambda r, g: (_dot(r[1], g, NT), _dot(r[0], g, NN)))


def _roll_rows(u, s):
    return pltpu.roll(u, s % u.shape[0], 0)


@functools.partial(jax.custom_vjp, nondiff_argnums=(2,))
def shift_rows(u, m, s):
    return _roll_rows(u, s) * m


def _shift_fwd(u, m, s):
    return _roll_rows(u, s) * m, m


def _shift_bwd(s, m, g):
    return _roll_rows(g * m, -s), jnp.zeros_like(m)


shift_rows.defvjp(_shift_fwd, _shift_bwd)


def _scan_tile(pos, nt, rev):
    return jnp.where(pos == 0, 0, nt - pos) if rev else pos


def _exchange(name, arrays, all_to_all):
    k_arr = len(arrays)

    def body(*refs):
        ins, outs = refs[:k_arr], refs[k_arr:2 * k_arr]
        send_sems, recv_sems, local_sems = refs[2 * k_arr:]
        me = lax.axis_index("x") * 4 + lax.axis_index("y") * 2 + lax.axis_index("c")
        local = []
        for k in range(k_arr):
            cp = pltpu.make_async_copy(ins[k].at[me] if all_to_all else ins[k], outs[k].at[me], local_sems.at[k])
            cp.start()
            local.append(cp)
        sends = []
        for d in range(1, N_DEV):
            p = (me + d) % N_DEV
            for k in range(k_arr):
                cp = pltpu.make_async_remote_copy(
                    src_ref=ins[k].at[p] if all_to_all else ins[k], dst_ref=outs[k].at[me],
                    send_sem=send_sems.at[k, d - 1], recv_sem=recv_sems.at[k, d - 1],
                    device_id=(p // 4, (p // 2) % 2, p % 2), device_id_type=pl.DeviceIdType.MESH)
                cp.start()
                sends.append(cp)
        for d in range(1, N_DEV):
            q = (me + N_DEV - d) % N_DEV
            for k in range(k_arr):
                pltpu.make_async_remote_copy(
                    src_ref=ins[k].at[q] if all_to_all else ins[k], dst_ref=outs[k].at[q],
                    send_sem=send_sems.at[k, d - 1], recv_sem=recv_sems.at[k, d - 1],
                    device_id=(q // 4, (q // 2) % 2, q % 2), device_id_type=pl.DeviceIdType.MESH).wait_recv()
        for cp in sends:
            cp.wait_send()
        for cp in local:
            cp.wait()

    shapes = [a.shape if all_to_all else (N_DEV,) + a.shape for a in arrays]
    return pl.pallas_call(
        body, name=name,
        out_shape=[jax.ShapeDtypeStruct(s, a.dtype) for s, a in zip(shapes, arrays)],
        in_specs=[pl.BlockSpec(memory_space=pl.ANY)] * k_arr,
        out_specs=[pl.BlockSpec(memory_space=pl.ANY)] * k_arr,
        scratch_shapes=[pltpu.SemaphoreType.DMA((k_arr, N_DEV - 1)), pltpu.SemaphoreType.DMA((k_arr, N_DEV - 1)),
                        pltpu.SemaphoreType.DMA((k_arr,))],
    )(*arrays)


_HBM = pl.BlockSpec(memory_space=pltpu.HBM)
_SEM = pl.BlockSpec(memory_space=pltpu.SEMAPHORE)
_EFFECT = pltpu.SideEffectType.DATAFLOW_SIDE_EFFECTING


def _peer(i):
    return (i // 4, (i // 2) % 2, i % 2)


def exchange_start(name, srcs, lands, all_to_all, after=None):
    k_arr = len(srcs)
    n_sem = k_arr * (N_DEV - 1)
    extra = [] if after is None else [after]

    def body(*refs):
        ins, lz = refs[:k_arr], refs[k_arr:2 * k_arr]
        first = 2 * k_arr + len(extra)
        send_sems = refs[first:first + n_sem]
        recv_sems = refs[first + n_sem:first + 2 * n_sem]
        me = lax.axis_index("x") * 4 + lax.axis_index("y") * 2 + lax.axis_index("c")
        for d in range(1, N_DEV):
            p = (me + d) % N_DEV
            for k in range(k_arr):
                s = k * (N_DEV - 1) + d - 1
                pltpu.make_async_remote_copy(
                    src_ref=ins[k].at[p] if all_to_all else ins[k], dst_ref=lz[k].at[me],
                    send_sem=send_sems[s], recv_sem=recv_sems[s],
                    device_id=_peer(p), device_id_type=pl.DeviceIdType.MESH).start()
        refs[-1][...] = jnp.zeros_like(refs[-1])

    arrs = list(srcs) + list(lands)
    res = pl.pallas_call(
        body, name=name,
        out_shape=(*[pltpu.SemaphoreType.DMA(())] * (2 * n_sem), *[pltpu.HBM(a.shape, a.dtype) for a in arrs],
                   jax.ShapeDtypeStruct((8, 128), F32)),
        in_specs=[_HBM] * len(arrs) + [pl.BlockSpec(memory_space=pl.ANY)] * len(extra),
        out_specs=(*[_SEM] * (2 * n_sem), *[_HBM] * len(arrs), pl.BlockSpec(memory_space=pltpu.VMEM)),
        input_output_aliases={i: 2 * n_sem + i for i in range(len(arrs))},
        compiler_params=pltpu.CompilerParams(has_side_effects=_EFFECT),
    )(*[pltpu.with_memory_space_constraint(a, pltpu.HBM) for a in arrs], *extra)
    return res[:-1], res[-1]


def exchange_wait(name, handle, after, all_to_all):
    k_arr = len(handle) // (2 * N_DEV)
    n_sem = k_arr * (N_DEV - 1)
    sems, arrs = handle[:2 * n_sem], handle[2 * n_sem:]

    def body(*refs):
        ins, lz = refs[:k_arr], refs[k_arr:2 * k_arr]
        s_sems = refs[2 * k_arr:2 * k_arr + n_sem]
        r_sems = refs[2 * k_arr + n_sem:2 * k_arr + 2 * n_sem]
        me = lax.axis_index("x") * 4 + lax.axis_index("y") * 2 + lax.axis_index("c")
        for d in range(1, N_DEV):
            q = (me + N_DEV - d) % N_DEV
            for k in range(k_arr):
                s = k * (N_DEV - 1) + d - 1
                cp = pltpu.make_async_remote_copy(
                    src_ref=ins[k].at[q] if all_to_all else ins[k], dst_ref=lz[k].at[q],
                    send_sem=s_sems[s], recv_sem=r_sems[s],
                    device_id=_peer(q), device_id_type=pl.DeviceIdType.MESH)
                cp.wait_send()
                cp.wait_recv()

    res = pl.pallas_call(
        body, name=name, out_shape=tuple(pltpu.HBM(a.shape, a.dtype) for a in arrs),
        in_specs=[_HBM] * len(arrs) + [_SEM] * (2 * n_sem) + [pl.BlockSpec(memory_space=pl.ANY)],
        out_specs=tuple([_HBM] * len(arrs)),
        input_output_aliases={i: i for i in range(len(arrs))},
        compiler_params=pltpu.CompilerParams(has_side_effects=_EFFECT),
    )(*arrs, *sems, after)
    return res[k_arr:]


def _own_block_set(src, all_to_all):
    me = lax.axis_index("x") * 4 + lax.axis_index("y") * 2 + lax.axis_index("c")
    own = lax.dynamic_index_in_dim(src, me, 0, keepdims=False) if all_to_all else src
    shape = src.shape if all_to_all else (N_DEV,) + src.shape
    return lax.dynamic_update_index_in_dim(lax.empty(shape, src.dtype), own, me, 0)


def _tile(n, prefs):
    for t in prefs:
        if n % t == 0:
            return t
    raise ValueError(n)


def dense_nn(name, a, w, out_dtype=F32):
    n, k = a.shape
    m = w.shape[1]
    tn, tm = _tile(n, (1088, 256)), _tile(m, (1024, 512, 256, 128))

    def body(a_ref, w_ref, o_ref):
        o_ref[...] = _dot(a_ref[...], w_ref[...], NN).astype(o_ref.dtype)

    return pl.pallas_call(
        body, name=name, grid=(m // tm, n // tn), out_shape=jax.ShapeDtypeStruct((n, m), out_dtype),
        in_specs=[pl.BlockSpec((tn, k), lambda j, i: (i, 0)), pl.BlockSpec((k, tm), lambda j, i: (0, j))],
        out_specs=pl.BlockSpec((tn, tm), lambda j, i: (i, j)), compiler_params=_cparams(2))(a, w)


def dense_nt(name, g, w, out_dtype=F32):
    n, m = g.shape
    k = w.shape[0]
    tn, tk = _tile(n, (544, 256)), _tile(k, (1024, 1408, 512, 256, 128))

    def body(g_ref, w_ref, o_ref):
        o_ref[...] = _dot(g_ref[...], w_ref[...], NT).astype(o_ref.dtype)

    return pl.pallas_call(
        body, name=name, grid=(k // tk, n // tn), out_shape=jax.ShapeDtypeStruct((n, k), out_dtype),
        in_specs=[pl.BlockSpec((tn, m), lambda j, i: (i, 0)), pl.BlockSpec((tk, m), lambda j, i: (j, 0))],
        out_specs=pl.BlockSpec((tn, tk), lambda j, i: (i, j)), compiler_params=_cparams(2))(g, w)


def dense_nt2(name, g1, g2, w, out_dtype=F32):
    n, m = g1.shape
    k = w.shape[0]
    tn, tk = _tile(n, (544, 256)), _tile(k, (1024, 1408, 512, 256, 128))

    def body(g1_ref, g2_ref, w1_ref, w2_ref, o_ref):
        o_ref[...] = (_dot(g1_ref[...], w1_ref[...], NT) + _dot(g2_ref[...], w2_ref[...], NT)).astype(o_ref.dtype)

    half = lambda h: pl.BlockSpec((tk, m), lambda j, i: (j, h))
    rows = pl.BlockSpec((tn, m), lambda j, i: (i, 0))
    return pl.pallas_call(
        body, name=name, grid=(k // tk, n // tn), out_shape=jax.ShapeDtypeStruct((n, k), out_dtype),
        in_specs=[rows, rows, half(0), half(1)], out_specs=pl.BlockSpec((tn, tk), lambda j, i: (i, j)),
        compiler_params=_cparams(2))(g1, g2, w, w)


def dense_tn(name, a, g, out_dtype=BF16):
    n, k = a.shape
    m = g.shape[1]
    tn = _tile(n, (1088, 256))
    tk = _tile(k, (1024, 1408, 512, 256, 128))
    tm = _tile(m, (1024, 1408, 512, 256, 128))
    nt = n // tn

    def body(a_ref, g_ref, o_ref, acc_ref):
        t = pl.program_id(2)

        @pl.when(t == 0)
        def _():
            acc_ref[...] = jnp.zeros_like(acc_ref)

        acc_ref[...] += _dot(a_ref[...], g_ref[...], TN)

        @pl.when(t == nt - 1)
        def _():
            o_ref[...] = acc_ref[...].astype(o_ref.dtype)

    return pl.pallas_call(
        body, name=name, grid=(k // tk, m // tm, nt), out_shape=jax.ShapeDtypeStruct((k, m), out_dtype),
        in_specs=[pl.BlockSpec((tn, tk), lambda i, j, t: (t, i)), pl.BlockSpec((tn, tm), lambda i, j, t: (t, j))],
        out_specs=pl.BlockSpec((tk, tm), lambda i, j, t: (i, j)),
        scratch_shapes=[pltpu.VMEM((tk, tm), F32)], compiler_params=_cparams(3))(a, g)


def _c0(j):
    return 0


def _tspec(w, cb):
    return pl.BlockSpec((TT, w), lambda j, i: (i, cb(j)))


def _pspec(arr, w, cb):
    two = arr.shape[0] == 2
    return pl.BlockSpec((None, arr.shape[1], w), lambda j, i: (jnp.minimum(i, 1) if two else 0, 0, cb(j)))


def block_fwd(name, fn, n, tiled, params, outs, n_col=1):
    nt_, np_ = len(tiled), len(params)

    def body(*refs):
        i = pl.program_id(1)
        tv = [r[...].astype(F32) for r in refs[:nt_]]
        pv = [r[...].astype(F32) for r in refs[nt_:nt_ + np_]]
        for o_ref, r in zip(refs[nt_ + np_:], fn(tv, pv, i)):
            o_ref[...] = r.astype(o_ref.dtype)

    return pl.pallas_call(
        body, name=name, grid=(n_col, n // TT),
        out_shape=[jax.ShapeDtypeStruct((n, c), dt) for c, dt, _, _ in outs],
        in_specs=[_tspec(w, cb) for _, w, cb in tiled] + [_pspec(a, w, cb) for a, w, cb in params],
        out_specs=[_tspec(w, cb) for _, _, w, cb in outs], compiler_params=_cparams(2),
    )(*[a for a, _, _ in tiled], *[a for a, _, _ in params])


def block_bwd(name, fn, n, tiled, params, cots, grads, n_col=1):
    nt_, np_, nc_ = len(tiled), len(params), len(cots)
    want = [k for k, g in enumerate(grads) if g is not None]

    def body(*refs):
        i = pl.program_id(1)
        tv = [r[...].astype(F32) for r in refs[:nt_]]
        pv = [r[...].astype(F32) for r in refs[nt_:nt_ + np_]]
        cv = [r[...].astype(F32) for r in refs[nt_ + np_:nt_ + np_ + nc_]]
        o_refs = refs[nt_ + np_ + nc_:]
        _, vjp = jax.vjp(lambda t, p: list(fn(t, p, i)), tv, pv)
        dt, dp = vjp(cv)
        for o_ref, k in zip(o_refs, want):
            o_ref[...] = dt[k].astype(o_ref.dtype)
        for o_ref, g, (arr, _, _) in zip(o_refs[len(want):], dp, params):
            first = (i == 0) | (i == 1) if arr.shape[0] == 2 else i == 0

            @pl.when(first)
            def _(o_ref=o_ref):
                o_ref[...] = jnp.zeros_like(o_ref)

            o_ref[...] += g

    res = pl.pallas_call(
        body, name=name, grid=(n_col, n // TT),
        out_shape=[jax.ShapeDtypeStruct((n, grads[k][0]), grads[k][1]) for k in want]
        + [jax.ShapeDtypeStruct(a.shape, F32) for a, _, _ in params],
        in_specs=[_tspec(w, cb) for _, w, cb in tiled] + [_pspec(a, w, cb) for a, w, cb in params]
        + [_tspec(w, cb) for _, w, cb in cots],
        out_specs=[_tspec(grads[k][2], grads[k][3]) for k in want] + [_pspec(a, w, cb) for a, w, cb in params],
        compiler_params=_cparams(2),
    )(*[a for a, _, _ in tiled], *[a for a, _, _ in params], *[a for a, _, _ in cots])
    return res[:len(want)], res[len(want):]


def f_mod(tv, pv, i):
    (x,), (sh, sc) = tv, pv
    return (x * (1.0 + sc) + sh,)


def f_mod_id(tv, pv, i):
    return (f_mod(tv, pv, i)[0], tv[0])


def f_ln(tv, pv, i):
    (x, z), (gate, g, b, sh, sc) = tv, pv
    pre = ALPHA * x + gate * z
    mu = jnp.mean(pre, axis=-1, keepdims=True)
    var = jnp.mean(jnp.square(pre - mu), axis=-1, keepdims=True)
    xn = (pre - mu) * lax.rsqrt(var + LN_EPS) * g + b
    return xn, xn * (1.0 + sc) + sh


def f_mix(tv, pv, i):
    (pu, pg, y0, y1, o0, o1), (d_skip, w_glu, b_glu, norm_w) = tv, pv
    s5y = jax.nn.gelu(y0 + y1 + pu * d_skip)
    s5o = s5y * jax.nn.sigmoid(mm_nn(s5y, w_glu) + b_glu)
    o = o0 + o1
    heads = []
    for h in range(NH):
        oh = o[:, h * HD:(h + 1) * HD]
        heads.append(oh * lax.rsqrt(jnp.mean(jnp.square(oh), axis=-1, keepdims=True) + RMS_EPS) * norm_w)
    hg = jnp.concatenate(heads, axis=-1) * jax.nn.silu(pg)
    return (jnp.concatenate([s5o, hg], axis=-1),)


def f_act(tv, pv, i):
    (ua, ug), (cwa, cwg, cba, cbg) = tv, pv
    t = lax.broadcasted_iota(jnp.int32, (TT, 1), 0)
    lat = i > 0
    m_dn = jnp.where((t == 0) | (lat & (t % GRID_W == 0)), 0.0, 1.0)
    m_up = jnp.where((t == TT - 1) | (lat & (t % GRID_W == GRID_W - 1)), 0.0, 1.0)

    def conv(u, w, b):
        return shift_rows(u, m_dn, 1) * w[0:1] + u * w[1:2] + shift_rows(u, m_up, -1) * w[2:3] + b

    return (jax.nn.silu(conv(ua, cwa, cba)) * conv(ug, cwg, cbg),)


def loss_and_grad(xf, target):
    n = xf.shape[0]

    def body(x_ref, t_ref, dy_ref, l_ref):
        i = pl.program_id(0)

        @pl.when(i == 0)
        def _():
            l_ref[...] = jnp.zeros_like(l_ref)
            dy_ref[...] = jnp.zeros_like(dy_ref)

        @pl.when(i > 0)
        def _():
            e = x_ref[...] - t_ref[...]
            dy_ref[...] = e * (1.0 / D)
            l_ref[...] += 0.5 / D * jnp.sum(jnp.square(e))

    return pl.pallas_call(
        body, name="loss", grid=(n // TT,),
        out_shape=[jax.ShapeDtypeStruct((n, D), F32), jax.ShapeDtypeStruct((8, 128), F32)],
        in_specs=[pl.BlockSpec((TT, D), lambda i: (i, 0)), pl.BlockSpec((TT, D), lambda i: (jnp.maximum(i - 1, 0), 0))],
        out_specs=[pl.BlockSpec((TT, D), lambda i: (i, 0)), pl.BlockSpec((8, 128), lambda i: (0, 0))],
        compiler_params=_cparams(1))(xf, target)


def f_prep(lr, li, ldt, bre, bim, cre, cim):
    gi = lax.broadcasted_iota(jnp.int32, (S5W // S5H, S5P), 0)
    gc = lax.broadcasted_iota(jnp.int32, (S5W // S5H, S5P), 1) // 64
    dt = jnp.exp(jnp.sum(jnp.where(gi == gc, ldt, 0.0), axis=0, keepdims=True))
    mag, ang = jnp.exp(lr * dt), li * dt
    ar, ai = mag * jnp.cos(ang), mag * jnp.sin(ang)
    den = lr * lr + li * li
    nr, ni = ar - 1.0, ai
    cr = (nr * lr + ni * li) / den
    ci = (ni * lr - nr * li) / den
    bbr = cr * bre - ci * bim
    bbi = cr * bim + ci * bre
    rg = lax.broadcasted_iota(jnp.int32, (S5W, S5P), 0) // S5H
    cg = lax.broadcasted_iota(jnp.int32, (S5W, S5P), 1) // 64
    mask = (rg == cg).astype(F32)
    blk = lambda a: jnp.concatenate([a] * (S5W // S5H), axis=0) * mask
    return ar, ai, blk(bbr), blk(bbi), blk(cre), blk(-cim)


def s5_prep(lr, li, ldt, bre, bim, cre, cim):
    n2 = lr.shape[0]

    def body(lr_r, li_r, ldt_r, bre_r, bim_r, cre_r, cim_r, a_ref, b_ref, c_ref):
        ar, ai, bbr, bbi, cbr, cbi = f_prep(lr_r[...], li_r[...], ldt_r[...], bre_r[...], bim_r[...], cre_r[...], cim_r[...])
        a_ref[0], a_ref[1] = ar, ai
        b_ref[0], b_ref[1] = bbr.astype(BF16), bbi.astype(BF16)
        c_ref[0], c_ref[1] = cbr.astype(BF16), cbi.astype(BF16)

    sp = lambda r, c: pl.BlockSpec((None, r, c), lambda i: (i, 0, 0))
    sp4 = lambda r, c: pl.BlockSpec((None, 2, r, c), lambda i: (i, 0, 0, 0))
    return pl.pallas_call(
        body, name="s5_prep", grid=(n2,),
        out_shape=[jax.ShapeDtypeStruct((n2, 2, 1, S5P), F32), jax.ShapeDtypeStruct((n2, 2, S5W, S5P), BF16),
                   jax.ShapeDtypeStruct((n2, 2, S5W, S5P), BF16)],
        in_specs=[sp(1, S5P), sp(1, S5P), sp(32, 1), sp(S5H, S5P), sp(S5H, S5P), sp(S5H, S5P), sp(S5H, S5P)],
        out_specs=[sp4(1, S5P), sp4(S5W, S5P), sp4(S5W, S5P)], compiler_params=_cparams(1),
    )(lr, li, ldt, bre, bim, cre, cim)


def s5_prep_bwd(lr, li, ldt, bre, bim, cre, cim, da, db, dc):
    n2 = lr.shape[0]

    def body(lr_r, li_r, ldt_r, bre_r, bim_r, cre_r, cim_r, da_r, db_r, dc_r, *outs):
        args = [r[...] for r in (lr_r, li_r, ldt_r, bre_r, bim_r, cre_r, cim_r)]
        _, vjp = jax.vjp(f_prep, *args)
        for o_ref, g in zip(outs, vjp((da_r[0], da_r[1], db_r[0], db_r[1], dc_r[0], dc_r[1]))):
            o_ref[...] = g

    sp = lambda r, c: pl.BlockSpec((None, r, c), lambda i: (i, 0, 0))
    sp4 = lambda r, c: pl.BlockSpec((None, 2, r, c), lambda i: (i, 0, 0, 0))
    ins = [sp(1, S5P), sp(1, S5P), sp(32, 1), sp(S5H, S5P), sp(S5H, S5P), sp(S5H, S5P), sp(S5H, S5P)]
    return pl.pallas_call(
        body, name="s5_prep_bwd", grid=(n2,),
        out_shape=[jax.ShapeDtypeStruct(a.shape, F32) for a in (lr, li, ldt, bre, bim, cre, cim)],
        in_specs=ins + [sp4(1, S5P), sp4(S5W, S5P), sp4(S5W, S5P)], out_specs=ins, compiler_params=_cparams(1),
    )(lr, li, ldt, bre, bim, cre, cim, da, db, dc)


S5_DIAG = 2
_CU, _CP = S5W // S5_DIAG, S5P // S5_DIAG


def _bd_nn(u, w_ref, k):
    return jnp.concatenate([_dot(u[:, j * _CU:(j + 1) * _CU], w_ref[k, j * _CU:(j + 1) * _CU, j * _CP:(j + 1) * _CP], NN)
                            for j in range(S5_DIAG)], axis=1)


def _bd_nt(x, w_ref, k):
    return jnp.concatenate([_dot(x[:, j * _CP:(j + 1) * _CP], w_ref[k, j * _CU:(j + 1) * _CU, j * _CP:(j + 1) * _CP], NT)
                            for j in range(S5_DIAG)], axis=1)


def _bd_tn_acc(acc_ref, k, a, g):
    for j in range(S5_DIAG):
        acc_ref[k, j * _CU:(j + 1) * _CU, j * _CP:(j + 1) * _CP] += _dot(a[:, j * _CU:(j + 1) * _CU],
                                                                         g[:, j * _CP:(j + 1) * _CP], TN)


def _scan_rows(xr_ref, xi_ref, ar, ai, desc, cr_ref, ci_ref):
    unroll = 8

    def group(gi, carry):
        cr, ci = carry
        base = gi * unroll
        for j in range(unroll):
            t = TT - 1 - (base + j) if desc else base + j
            nr = ar * cr - ai * ci + xr_ref[pl.ds(t, 1), :]
            ni = ar * ci + ai * cr + xi_ref[pl.ds(t, 1), :]
            xr_ref[pl.ds(t, 1), :] = nr
            xi_ref[pl.ds(t, 1), :] = ni
            cr, ci = nr, ni
        return cr, ci

    cr, ci = lax.fori_loop(0, TT // unroll, group, (cr_ref[...], ci_ref[...]))
    cr_ref[...] = cr
    ci_ref[...] = ci


def s5_fwd(name, proj, a, bb, cb, ld, rev):
    n = proj.shape[0]
    nt = n // TT

    def body(u_ref, a_ref, b_ref, c_ref, xr_ref, xi_ref, y_ref, cr_ref, ci_ref):
        @pl.when(pl.program_id(0) == 0)
        def _():
            cr_ref[...] = jnp.zeros_like(cr_ref)
            ci_ref[...] = jnp.zeros_like(ci_ref)

        u = u_ref[...]
        xr_ref[...] = _bd_nn(u, b_ref, 0)
        xi_ref[...] = _bd_nn(u, b_ref, 1)
        _scan_rows(xr_ref, xi_ref, a_ref[0], a_ref[1], rev, cr_ref, ci_ref)
        y_ref[...] = _bd_nt(xr_ref[...], c_ref, 0) + _bd_nt(xi_ref[...], c_ref, 1)

    tile = lambda w: pl.BlockSpec((TT, w), lambda s: (_scan_tile(s, nt, rev), 0))
    par = lambda r: pl.BlockSpec((None, 2, r, S5P), lambda s: (ld, 0, 0, 0))
    return pl.pallas_call(
        body, name=name, grid=(nt,),
        out_shape=[jax.ShapeDtypeStruct((n, S5P), F32), jax.ShapeDtypeStruct((n, S5P), F32),
                   jax.ShapeDtypeStruct((n, S5W), F32)],
        in_specs=[tile(S5W), par(1), par(S5W), par(S5W)], out_specs=[tile(S5P), tile(S5P), tile(S5W)],
        scratch_shapes=[pltpu.VMEM((1, S5P), F32), pltpu.VMEM((1, S5P), F32)], compiler_params=_cparams(1),
    )(proj, a, bb, cb)


def s5_bwd(name, proj, dy, xr, xi, a, bb, cb, ld, rev, totals):
    n = proj.shape[0]
    nt = n // TT
    tb = TT // 8

    def tile_of(s):
        return _scan_tile(nt - 1 - s, nt, rev)

    def edge_of(s):
        pos = nt - 1 - s
        prev = _scan_tile(jnp.maximum(pos - 1, 0), nt, rev)
        return prev * tb if rev else jnp.maximum(pos * tb - 1, 0)

    def body(u_ref, dy_ref, xr_ref, xi_ref, er_ref, ei_ref, a_ref, b_ref, c_ref, _ta, _tb, _tc,
             du_ref, da_ref, db_ref, dc_ref, gr_ref, gi_ref, cr_ref, ci_ref):
        s = pl.program_id(0)

        @pl.when(s == 0)
        def _():
            cr_ref[...] = jnp.zeros_like(cr_ref)
            ci_ref[...] = jnp.zeros_like(ci_ref)
            da_ref[...] = jnp.zeros_like(da_ref)
            db_ref[...] = jnp.zeros_like(db_ref)
            dc_ref[...] = jnp.zeros_like(dc_ref)

        dyv, u = dy_ref[...], u_ref[...]
        xrv, xiv = xr_ref[...], xi_ref[...]
        gr_ref[...] = _bd_nn(dyv, c_ref, 0)
        gi_ref[...] = _bd_nn(dyv, c_ref, 1)
        _bd_tn_acc(dc_ref, 0, dyv, xrv)
        _bd_tn_acc(dc_ref, 1, dyv, xiv)
        _scan_rows(gr_ref, gi_ref, a_ref[0], -a_ref[1], not rev, cr_ref, ci_ref)
        g_r, g_i = gr_ref[...], gi_ref[...]
        rows = lax.broadcasted_iota(jnp.int32, (TT, 1), 0)
        live = jnp.where(s == nt - 1, 0.0, 1.0)
        if rev:
            pr = jnp.where(rows == TT - 1, er_ref[0:1, :] * live, _roll_rows(xrv, -1))
            pi = jnp.where(rows == TT - 1, ei_ref[0:1, :] * live, _roll_rows(xiv, -1))
        else:
            pr = jnp.where(rows == 0, er_ref[7:8, :] * live, _roll_rows(xrv, 1))
            pi = jnp.where(rows == 0, ei_ref[7:8, :] * live, _roll_rows(xiv, 1))
        da_ref[0] += jnp.sum(g_r * pr + g_i * pi, axis=0, keepdims=True)
        da_ref[1] += jnp.sum(g_i * pr - g_r * pi, axis=0, keepdims=True)
        du_ref[...] = _bd_nt(g_r, b_ref, 0) + _bd_nt(g_i, b_ref, 1)
        _bd_tn_acc(db_ref, 0, u, g_r)
        _bd_tn_acc(db_ref, 1, u, g_i)

    tile = lambda w: pl.BlockSpec((TT, w), lambda s: (tile_of(s), 0))
    edge = pl.BlockSpec((8, S5P), lambda s: (edge_of(s), 0))
    par = lambda r: pl.BlockSpec((None, 2, r, S5P), lambda s: (ld, 0, 0, 0))
    whole = pl.BlockSpec(memory_space=pl.ANY)
    return pl.pallas_call(
        body, name=name, grid=(nt,),
        out_shape=[jax.ShapeDtypeStruct((n, S5W), F32)] + [jax.ShapeDtypeStruct(t.shape, F32) for t in totals],
        in_specs=[tile(S5W), tile(S5W), tile(S5P), tile(S5P), edge, edge, par(1), par(S5W), par(S5W), whole, whole, whole],
        out_specs=[tile(S5W), par(1), par(S5W), par(S5W)], input_output_aliases={9: 1, 10: 2, 11: 3},
        scratch_shapes=[pltpu.VMEM((TT, S5P), F32), pltpu.VMEM((TT, S5P), F32),
                        pltpu.VMEM((1, S5P), F32), pltpu.VMEM((1, S5P), F32)], compiler_params=_cparams(1),
    )(proj, dy, xr, xi, xr, xi, a, bb, cb, *totals)


def gla_tile(r, v, qr, lb, sts, rev):
    ncc = TT // CK
    f = lb + (1.0 - lb) * jax.nn.sigmoid(r)
    k, lf, q = 1.0 - f, jnp.log(f), jax.nn.silu(qr)
    rows = lax.broadcasted_iota(jnp.int32, (TT, 1), 0)
    pos = rows % CK
    b = lf
    for s in (1, 2, 4, 8, 16):
        m = ((pos < CK - s) if rev else (pos >= s)).astype(F32)
        b = b + shift_rows(b, m, -s if rev else s)
    etot = [jnp.sum(lf[c * CK:(c + 1) * CK], axis=0, keepdims=True) for c in range(ncc)]
    e = jnp.concatenate([jnp.broadcast_to(t, (CK, HGW)) for t in etot], axis=0)
    kd, qe, qa = k * jnp.exp(e - b), q * jnp.exp(b), q * jnp.exp(b - e)
    cm = [(rows // CK == c).astype(F32) for c in range(ncc)]
    r2 = lax.broadcasted_iota(jnp.int32, (TT, TT), 0)
    c2 = lax.broadcasted_iota(jnp.int32, (TT, TT), 1)
    amask = (r2 // CK == c2 // CK) & ((r2 <= c2) if rev else (r2 >= c2))
    outs, new_sts = [], []
    for h in range(NH):
        ln = slice(h * HD, (h + 1) * HD)
        kdh, qeh, vh = kd[:, ln], qe[:, ln], v[:, ln]
        att = jnp.where(amask, mm_nt(qa[:, ln], kdh), 0.0)
        ds = mm_tn(jnp.concatenate([kdh * cm[c] for c in range(ncc)], axis=1), vh)
        st, starts = sts[h], [None] * ncc
        for c in (reversed(range(ncc)) if rev else range(ncc)):
            starts[c] = st
            dec = jnp.transpose(jnp.broadcast_to(jnp.exp(etot[c][:, ln]), (HD, HD)))
            st = dec * st + ds[c * HD:(c + 1) * HD]
        new_sts.append(st)
        qex = jnp.concatenate([qeh * cm[c] for c in range(ncc)], axis=1)
        outs.append(mm_nn(att, vh) + mm_nn(qex, jnp.concatenate(starts, axis=0)))
    return jnp.concatenate(outs, axis=1), new_sts


def _gla_specs(n, rev, order):
    nt = n // TT
    fcol = 2 if rev else 1
    tile = lambda cbk: pl.BlockSpec((TT, HGW), lambda s: (order(s), cbk))
    return nt, [tile(fcol), tile(3), tile(4)], tile(0)


def gla_fwd(name, proj, lb, rev):
    n = proj.shape[0]
    nt, in_tiles, out_tile = _gla_specs(n, rev, lambda s: _scan_tile(s, n // TT, rev))

    def body(r_ref, v_ref, q_ref, lb_ref, o_ref, st_ref, s_ref):
        @pl.when(pl.program_id(0) == 0)
        def _():
            s_ref[...] = jnp.zeros_like(s_ref)

        sts = [s_ref[h] for h in range(NH)]
        for h in range(NH):
            st_ref[h] = sts[h]
        o, new = gla_tile(r_ref[...], v_ref[...], q_ref[...], lb_ref[...], sts, rev)
        o_ref[...] = o
        for h in range(NH):
            s_ref[h] = new[h]

    st_spec = pl.BlockSpec((None, NH, HD, HD), lambda s: (_scan_tile(s, nt, rev), 0, 0, 0))
    return pl.pallas_call(
        body, name=name, grid=(nt,),
        out_shape=[jax.ShapeDtypeStruct((n, HGW), F32), jax.ShapeDtypeStruct((nt, NH, HD, HD), F32)],
        in_specs=in_tiles + [pl.BlockSpec((1, HGW), lambda s: (0, 0))], out_specs=[out_tile, st_spec],
        scratch_shapes=[pltpu.VMEM((NH, HD, HD), F32)], compiler_params=_cparams(1),
    )(proj, proj, proj, lb)


def gla_bwd(name, proj, lb, st_all, do, rev):
    n = proj.shape[0]
    order = lambda s: _scan_tile(n // TT - 1 - s, n // TT, rev)
    nt, in_tiles, out_tile = _gla_specs(n, rev, order)

    def body(r_ref, v_ref, q_ref, lb_ref, st_ref, do_ref, dr_ref, dv_ref, dq_ref, dlb_ref, ds_ref):
        @pl.when(pl.program_id(0) == 0)
        def _():
            ds_ref[...] = jnp.zeros_like(ds_ref)
            dlb_ref[...] = jnp.zeros_like(dlb_ref)

        _, vjp = jax.vjp(functools.partial(gla_tile, rev=rev), r_ref[...], v_ref[...], q_ref[...], lb_ref[...],
                         [st_ref[h] for h in range(NH)])
        dr, dv, dq, dlb, dsts = vjp((do_ref[...], [ds_ref[h] for h in range(NH)]))
        dr_ref[...] = dr
        dv_ref[...] = dv
        dq_ref[...] = dq
        dlb_ref[...] += dlb
        for h in range(NH):
            ds_ref[h] = dsts[h]

    st_spec = pl.BlockSpec((None, NH, HD, HD), lambda s: (order(s), 0, 0, 0))
    row = pl.BlockSpec((1, HGW), lambda s: (0, 0))
    return pl.pallas_call(
        body, name=name, grid=(nt,),
        out_shape=[jax.ShapeDtypeStruct((n, HGW), F32)] * 3 + [jax.ShapeDtypeStruct((1, HGW), F32)],
        in_specs=in_tiles + [row, st_spec, out_tile], out_specs=[out_tile] * 3 + [row],
        scratch_shapes=[pltpu.VMEM((NH, HD, HD), F32)], compiler_params=_cparams(1),
    )(proj, proj, proj, lb, st_all, do)


def f_lb(rows):
    mx = functools.reduce(jnp.maximum, rows)
    ex = [jnp.exp(r - mx) for r in rows]
    tot = functools.reduce(jnp.add, ex)
    out, acc = [jnp.zeros_like(rows[0])], None
    for e in ex[1:]:
        acc = e / tot if acc is None else acc + e / tot
        out.append(acc)
    return out


def lb_call(hg, dlb=None):
    nl = hg.shape[0]

    def body(*refs):
        rows = [refs[0][l:l + 1, :] for l in range(nl)]
        if dlb is None:
            res = f_lb(rows)
        else:
            _, vjp = jax.vjp(f_lb, rows)
            (res,) = vjp([refs[1][l:l + 1, :] for l in range(nl)])
        for l in range(nl):
            refs[-1][l:l + 1, :] = res[l]

    args = (hg,) if dlb is None else (hg, dlb)
    return pl.pallas_call(body, name="lower_bounds" if dlb is None else "lower_bounds_bwd",
                          out_shape=jax.ShapeDtypeStruct(hg.shape, F32))(*args)


def mod_fwd(craw, w_mod, b_cols):
    nl, _, cols = w_mod.shape

    def body(c_ref, w_ref, b_ref, o_ref):
        o_ref[...] = _dot(jax.nn.silu(c_ref[...]), w_ref[...], NN) + b_ref[...]

    return pl.pallas_call(
        body, name="mod_fwd", grid=(nl,), out_shape=jax.ShapeDtypeStruct((nl, 16, cols), F32),
        in_specs=[pl.BlockSpec((16, D), lambda l: (0, 0)), pl.BlockSpec((None, D, cols), lambda l: (l, 0, 0)),
                  pl.BlockSpec((None, 1, cols), lambda l: (l, 0, 0))],
        out_specs=pl.BlockSpec((None, 16, cols), lambda l: (l, 0, 0)), compiler_params=_cparams(1))(craw, w_mod, b_cols)


def mod_bwd(craw, w_mod, g):
    nl, _, cols = w_mod.shape

    def body(c_ref, w_ref, g_ref, dw_ref, dc_ref, acc_ref):
        l = pl.program_id(0)

        @pl.when(l == 0)
        def _():
            acc_ref[...] = jnp.zeros_like(acc_ref)

        c = c_ref[...]
        s, vjp = jax.vjp(jax.nn.silu, c)
        dw_ref[...] = _dot(s, g_ref[...], TN)
        acc_ref[...] += _dot(g_ref[...], w_ref[...], NT)

        @pl.when(l == nl - 1)
        def _():
            dc_ref[...] = vjp(acc_ref[...])[0]

    return pl.pallas_call(
        body, name="mod_bwd", grid=(nl,),
        out_shape=[jax.ShapeDtypeStruct(w_mod.shape, F32), jax.ShapeDtypeStruct((16, D), F32)],
        in_specs=[pl.BlockSpec((16, D), lambda l: (0, 0)), pl.BlockSpec((None, D, cols), lambda l: (l, 0, 0)),
                  pl.BlockSpec((None, 16, cols), lambda l: (l, 0, 0))],
        out_specs=[pl.BlockSpec((None, D, cols), lambda l: (l, 0, 0)), pl.BlockSpec((16, D), lambda l: (0, 0))],
        scratch_shapes=[pltpu.VMEM((16, D), F32)], compiler_params=_cparams(1))(craw, w_mod, g)


def sum_parts(parts):
    def body(p_ref, o_ref):
        acc = p_ref[0]
        for k in range(1, parts.shape[0]):
            acc = acc + p_ref[k]
        o_ref[...] = acc

    return pl.pallas_call(body, name="sum_small_grads", out_shape=jax.ShapeDtypeStruct(parts.shape[1:], parts.dtype),
                          compiler_params=pltpu.CompilerParams(vmem_limit_bytes=VMEM_MB << 20))(parts)


def _adamw_body(s):
    def body(w_ref, m_ref, v_ref, g_ref, *rest):
        go_ref, d_ref, mo_ref, vo_ref = rest[-4:]
        g = g_ref[0].astype(F32)
        for k in range(1, s):
            g = g + g_ref[k].astype(F32)
        m_new = B1 * m_ref[...] + (1.0 - B1) * g
        v_new = B2 * v_ref[...] + (1.0 - B2) * jnp.square(g)
        m_hat = m_new / (1.0 - B1 ** STEP)
        v_hat = v_new / (1.0 - B2 ** STEP)
        go_ref[...] = g
        d_ref[...] = -LR * (m_hat / (jnp.sqrt(v_hat) + EPS) + WD * w_ref[...])
        mo_ref[...] = m_new
        vo_ref[...] = v_new

    return body


def _row_tile(r):
    return max([t for t in range(8, 257, 8) if r % t == 0], default=r)


def adamw(name, w, m, v, gs):
    r, c = w.shape
    s = gs.shape[0]
    tr = _row_tile(r)
    blk = pl.BlockSpec((tr, c), lambda i: (i, 0))
    return pl.pallas_call(
        _adamw_body(s), name=name, grid=(r // tr,), out_shape=[jax.ShapeDtypeStruct((r, c), F32)] * 4,
        in_specs=[blk, blk, blk, pl.BlockSpec((s, tr, c), lambda i: (0, i, 0))], out_specs=[blk] * 4,
        compiler_params=_cparams(1))(w, m, v, gs)


def adamw_layer(name, l, w, m, v, gs, outs):
    _, r, c = w.shape
    s = gs.shape[0]
    tr = _row_tile(r)
    blk = pl.BlockSpec((None, tr, c), lambda i: (l, i, 0))
    whole = pl.BlockSpec(memory_space=pl.ANY)
    return pl.pallas_call(
        _adamw_body(s), name=name, grid=(r // tr,), out_shape=[jax.ShapeDtypeStruct(w.shape, F32)] * 4,
        in_specs=[blk, blk, blk, pl.BlockSpec((s, tr, c), lambda i: (0, i, 0))] + [whole] * 4, out_specs=[blk] * 4,
        input_output_aliases={4 + i: i for i in range(4)}, compiler_params=_cparams(1))(w, m, v, gs, *outs)


SMALL = ["c_ctx", "b_mod", "s5_lam_re", "s5_lam_im", "s5_log_dt", "s5_b_re", "s5_b_im", "s5_c_re", "s5_c_im", "s5_d",
         "b_glu", "hg_lb", "hg_norm_w", "ln1_g", "ln1_b", "conv_b", "ln2_g", "ln2_b"]
BIG = ["w_in", "w_glu", "w_out", "w_up", "w_down"]
WEIGHTS = ["c_ctx", "w_mod", "b_mod", "w_in", "s5_lam_re", "s5_lam_im", "s5_log_dt", "s5_b_re", "s5_b_im", "s5_c_re",
           "s5_c_im", "s5_d", "w_glu", "b_glu", "hg_lb", "hg_norm_w", "w_out", "ln1_g", "ln1_b", "w_up", "conv_w",
           "conv_b", "w_down", "ln2_g", "ln2_b"]
PACK_W = 1024


def _pack_rows(k):
    return -(-k // (8 * PACK_W)) * 8


def _pack(arrs):
    parts = []
    for a in arrs:
        flat = a.reshape(-1)
        r = _pack_rows(flat.shape[0])
        parts.append(jnp.pad(flat, (0, r * PACK_W - flat.shape[0])).reshape(r, PACK_W))
    used = sum(q.shape[0] for q in parts)
    parts.append(jnp.zeros((-used % (8 * N_DEV), PACK_W), parts[0].dtype))
    return jnp.concatenate(parts, axis=0)


def _unpack(p, shapes):
    out, o = [], 0
    for s in shapes:
        k = math.prod(s)
        r = _pack_rows(k)
        out.append(p[o:o + r].reshape(-1)[:k].reshape(s))
        o += r
    return out


def _gathered_cols(g):
    return jnp.moveaxis(g, 0, 2).reshape(g.shape[1], g.shape[2], -1)


def _gathered_rows(g):
    return jnp.moveaxis(g, 0, 1).reshape(g.shape[1], -1, g.shape[3])


def _step(p):
    nl = p["w_in"].shape[0]
    me = lax.axis_index("x") * 4 + lax.axis_index("y") * 2 + lax.axis_index("c")
    xc0 = jnp.concatenate([p["ctx"][0], p["x"][0]], axis=0)
    n = xc0.shape[0]
    target = p["loss_target"][0]

    hg3 = jnp.stack([p[k].reshape(-1) for k in ("hg_lb", "m_hg_lb", "v_hg_lb")])
    g_cw, g_c, g_hg = _exchange("gather_inputs", [p["conv_w"], p["c"], hg3], False)
    conv_w = _gathered_cols(g_cw)
    hg_full = jnp.moveaxis(g_hg.reshape(N_DEV, 3, nl, 2, -1), 0, 3).reshape(3, nl, 2 * HGW)
    lb_all = lb_call(hg_full[0])

    craw = jnp.concatenate([g_c.reshape(N_DEV, D), jnp.broadcast_to(p["c_ctx"][None], (8, D))], axis=0)
    cols = p["w_mod"].shape[2]
    b_cols = lax.dynamic_slice_in_dim(p["b_mod"], me * cols, cols, axis=1)[:, None, :]
    (g_mod,) = _exchange("gather_mod", [mod_fwd(craw, p["w_mod"], b_cols)], False)
    mod_all = jnp.moveaxis(g_mod, 0, 2).reshape(nl, 16, 6 * D)
    mod_x = lax.dynamic_index_in_dim(mod_all, me, axis=1, keepdims=False)
    mod2 = jnp.stack([mod_all[:, 8], mod_x], axis=1)
    mvec = lambda l, k: mod2[l, :, k * D:(k + 1) * D][:, None, :]
    gathers = {}

    def start_gather(l, part, ks, after):
        srcs = [p[k][l].astype(BF16) for k in ks]
        gathers[l, part], token = exchange_start(f"gather_start{l}{part}", srcs, [_own_block_set(s, False) for s in srcs],
                                                 False, after=after)
        return token[0, 0]

    parts0 = {"a": BIG[:1], "b": BIG[1:3], "c": BIG[3:]}
    for part, ks in parts0.items():
        start_gather(0, part, ks, g_mod)

    def gathered(l, part, after):
        res = exchange_wait(f"gather_wait{l}{part}", gathers[l, part], after, False)
        ks = parts0[part] if part else BIG
        cols_ = lambda g: jnp.moveaxis(g, 0, 1).reshape(g.shape[1], -1)
        rows_ = lambda g: g.reshape(-1, g.shape[2])
        return {k: (cols_ if k in ("w_in", "w_up") else rows_)(g) for k, g in zip(ks, res)}

    zvec = jnp.zeros((2, 1, D), F32)
    row = lambda a: a.reshape(1, 1, -1)

    to_hp = lambda a: jnp.moveaxis(a, -1, 2).reshape(nl * 2, S5H, S5P)
    prep_in = [p["s5_lam_re"].reshape(nl * 2, 1, S5P), p["s5_lam_im"].reshape(nl * 2, 1, S5P),
               p["s5_log_dt"].reshape(nl * 2, 32, 1), to_hp(p["s5_b_re"]), to_hp(p["s5_b_im"]),
               jnp.swapaxes(p["s5_c_re"], 2, 3).reshape(nl * 2, S5H, S5P),
               jnp.swapaxes(p["s5_c_im"], 2, 3).reshape(nl * 2, S5H, S5P)]
    s5a, s5b, s5c = s5_prep(*prep_in)

    T1 = lambda a, w=D, cb=_c0: (a, w, cb)
    saved = []
    xc = xc0
    (h,) = block_fwd("mod0", f_mod, n, [T1(xc)], [T1(mvec(0, 0)), T1(mvec(0, 1))], [(D, BF16, D, _c0)])
    w_in, w_glu, w_out, w_up, w_down = ([None] * nl for _ in range(5))
    for l in range(nl):
        wl = gathered(l, "", xc) if l else gathered(0, "a", s5a)
        w_in[l] = wl["w_in"]
        proj = dense_nn(f"in_proj{l}", h, w_in[l])
        s5 = [s5_fwd(f"s5_fwd{l}_{d}", proj, s5a, s5b, s5c, 2 * l + d, d == 1) for d in range(2)]
        lbs = [lb_all[l, d * HGW:(d + 1) * HGW][None] for d in range(2)]
        gl = [gla_fwd(f"gla_fwd{l}_{d}", proj, lbs[d], d == 1) for d in range(2)]
        if l == 0:
            wl = gathered(0, "b", gl[1][0])
        w_glu[l], w_out[l] = wl["w_glu"], wl["w_out"]
        started = start_gather(l + 1, "", BIG, w_out[l]) if l + 1 < nl else 0.0
        mix_t = [T1(proj, S5W), T1(proj, HGW, lambda j: 5), T1(s5[0][2], S5W), T1(s5[1][2], S5W),
                 T1(gl[0][0], HGW), T1(gl[1][0], HGW)]
        mix_p = [T1(row(p["s5_d"][l]) + started, S5W), T1(w_glu[l][None], S5W), T1(row(p["b_glu"][l]), S5W),
                 T1(row(p["hg_norm_w"][l]), HD)]
        (y,) = block_fwd(f"mix{l}", f_mix, n, mix_t, mix_p, [(D, BF16, D, _c0)])
        z = dense_nn(f"out_proj{l}", y, w_out[l])
        ln1_p = [T1(mvec(l, 2)), T1(row(p["ln1_g"][l])), T1(row(p["ln1_b"][l])), T1(mvec(l, 3)), T1(mvec(l, 4))]
        x1, h2 = block_fwd(f"ln1_{l}", f_ln, n, [T1(xc), T1(z)], ln1_p, [(D, F32, D, _c0), (D, BF16, D, _c0)])
        if l == 0:
            wl = gathered(0, "c", h2)
        w_up[l], w_down[l] = wl["w_up"], wl["w_down"]
        up = dense_nn(f"up_proj{l}", h2, w_up[l])
        ct = DFF // 2
        act_t = [T1(up, ct, lambda j: j), T1(up, ct, lambda j: j + 2)]
        cb2 = p["conv_b"][l].reshape(1, 1, -1)
        act_p = [T1(conv_w[l][None, :, :DFF], ct, lambda j: j), T1(conv_w[l][None, :, DFF:], ct, lambda j: j),
                 T1(cb2[:, :, :DFF], ct, lambda j: j), T1(cb2[:, :, DFF:], ct, lambda j: j)]
        (act,) = block_fwd(f"act{l}", f_act, n, act_t, act_p, [(DFF, BF16, ct, lambda j: j)], n_col=2)
        dn = dense_nn(f"down_proj{l}", act, w_down[l])
        nxt = (mvec(l + 1, 0), mvec(l + 1, 1)) if l + 1 < nl else (zvec, zvec)
        ln2_p = [T1(mvec(l, 5)), T1(row(p["ln2_g"][l])), T1(row(p["ln2_b"][l])), T1(nxt[0]), T1(nxt[1])]
        x2, hn = block_fwd(f"ln2_{l}", f_ln, n, [T1(x1), T1(dn)], ln2_p, [(D, F32, D, _c0), (D, BF16, D, _c0)])
        saved.append(dict(xc=xc, h=h, proj=proj, s5=s5, gl=gl, lbs=lbs, mix_t=mix_t, mix_p=mix_p, y=y, z=z,
                          ln1_p=ln1_p, x1=x1, h2=h2, act_t=act_t, act_p=act_p, act=act, dn=dn, ln2_p=ln2_p))
        xc, h = x2, hn

    dxc, loss_part = loss_and_grad(xc, target)
    loss = lax.psum(loss_part[0, 0], AXES)

    g = {k: [None] * nl for k in ("w_in", "w_glu", "w_out", "w_up", "w_down", "conv_w", "conv_b", "s5_d", "b_glu",
                                  "hg_norm_w", "ln1_g", "ln1_b", "ln2_g", "ln2_b", "dlb", "s5")}
    dmod = [[None] * 6 for _ in range(nl)]
    scatters = [[] for _ in range(nl)]
    s5_totals = [lax.empty((2 * nl, 2, r, S5P), F32) for r in (1, S5W, S5W)]
    dh_next = jnp.zeros((n, D), F32)
    fgrad = (D, F32, D, _c0)
    for l in reversed(range(nl)):
        sv = saved[l]
        (dx1, d_dn), dp = block_bwd(f"ln2_bwd{l}", f_ln, n, [T1(sv["x1"]), T1(sv["dn"])], sv["ln2_p"],
                                    [T1(dxc), T1(dh_next)], [fgrad, fgrad])
        dmod[l][5], g["ln2_g"][l], g["ln2_b"][l] = dp[0], dp[1], dp[2]
        if l + 1 < nl:
            dmod[l + 1][0], dmod[l + 1][1] = dp[3], dp[4]
        dact = dense_nt(f"down_bwd{l}", d_dn, w_down[l])
        g["w_down"][l] = dense_tn(f"down_wgrad{l}", sv["act"], d_dn)
        ct = DFF // 2
        cj = lambda j: j
        (dua, dug), dp = block_bwd(f"act_bwd{l}", f_act, n, sv["act_t"], sv["act_p"], [T1(dact, ct, cj)],
                                   [(DFF, BF16, ct, cj), (DFF, BF16, ct, cj)], n_col=2)
        g["conv_w"][l] = jnp.concatenate([dp[0][0], dp[1][0]], axis=-1)
        g["conv_b"][l] = jnp.concatenate([dp[2][0, 0], dp[3][0, 0]], axis=-1)
        dh2 = dense_nt2(f"up_bwd{l}", dua, dug, w_up[l])
        g["w_up"][l] = [dense_tn(f"up_wgrad{l}{part}", sv["h2"], du_) for part, du_ in (("a", dua), ("g", dug))]
        (dxc, dz), dp = block_bwd(f"ln1_bwd{l}", f_ln, n, [T1(sv["xc"]), T1(sv["z"])], sv["ln1_p"],
                                  [T1(dx1), T1(dh2)], [fgrad, fgrad])
        dmod[l][2], g["ln1_g"][l], g["ln1_b"][l], dmod[l][3], dmod[l][4] = dp
        dy = dense_nt(f"out_bwd{l}", dz, w_out[l])
        g["w_out"][l] = dense_tn(f"out_wgrad{l}", sv["y"], dz)
        half = (S5W, F32, S5W, _c0)
        (dpu, dpg, dys, dos), dp = block_bwd(f"mix_bwd{l}", f_mix, n, sv["mix_t"], sv["mix_p"], [T1(dy)],
                                                   [half, half, half, None, half, None])
        g["s5_d"][l], g["w_glu"][l], g["b_glu"][l], g["hg_norm_w"][l] = dp[0][0, 0], dp[1][0], dp[2][0, 0], dp[3][0, 0]

        def start_scatter(tag, ks):
            by_cols = lambda a, nb=N_DEV: jnp.moveaxis(a.reshape(a.shape[0], nb, -1), 1, 0)
            by_rows = lambda a: a.reshape(N_DEV, -1, a.shape[1])

            def blocks(k):
                if k == "w_up":
                    return jnp.concatenate([by_cols(half, N_DEV // 2) for half in g[k][l]], axis=0)
                return (by_rows if k in ("w_glu", "w_out", "w_down") else by_cols)(g[k][l].astype(BF16 if k != "conv_w" else F32))

            sends = [blocks(k) for k in ks]
            handle, token = exchange_start(f"scatter_start{l}{tag}", sends, [_own_block_set(s, True) for s in sends], True)
            scatters[l].append((ks, handle))
            return token[0, 0]

        lbs_b = sv["lbs"]
        if l == 0:
            started = start_scatter("a", ["w_glu", "w_out", "w_up", "w_down", "conv_w"])
            lbs_b = [b + started for b in lbs_b]
        gb = [gla_bwd(f"gla_bwd{l}_{d}", sv["proj"], lbs_b[d], sv["gl"][d][1], dos, d == 1) for d in range(2)]
        g["dlb"][l] = jnp.concatenate([gb[0][3], gb[1][3]], axis=-1)[0]
        sb = [None, None]
        for d in range(2):
            sb[d], *s5_totals = s5_bwd(f"s5_bwd{l}_{d}", sv["proj"], dys, sv["s5"][d][0], sv["s5"][d][1], s5a, s5b, s5c,
                                       2 * l + d, d == 1, s5_totals)
        asm_t = [T1(dpu, S5W), T1(sb[0], S5W), T1(sb[1], S5W), T1(gb[0][0], HGW), T1(gb[1][0], HGW),
                 T1(gb[0][1], HGW), T1(gb[1][1], HGW), T1(gb[0][2], HGW), T1(gb[1][2], HGW), T1(dpg, HGW)]
        (dproj,) = block_fwd(
            f"dproj{l}", lambda tv, pv, i: (jnp.concatenate(
                [tv[0] + tv[1] + tv[2], tv[3], tv[4], tv[5] + tv[6], tv[7] + tv[8], tv[9]], axis=-1),),
            n, asm_t, [], [(INC, BF16, INC, _c0)])
        dh_next = dense_nt(f"in_bwd{l}", dproj, w_in[l])
        g["w_in"][l] = dense_tn(f"in_wgrad{l}", sv["h"], dproj)
        started = start_scatter("b", ["w_in"]) if l == 0 else start_scatter("", BIG + ["conv_w"])
        if l:
            gate, wd_, cb_ = saved[l - 1]["ln2_p"][0]
            saved[l - 1]["ln2_p"][0] = (gate + started, wd_, cb_)
    (dxc,), dp = block_bwd("mod0_bwd", f_mod_id, n, [T1(xc0)], [T1(mvec(0, 0)), T1(mvec(0, 1))],
                           [T1(dh_next), T1(dxc)], [fgrad])
    dmod[0][0], dmod[0][1] = dp
    grad_x = dxc[n - p["x"].shape[1]:][None]

    d_prep = s5_prep_bwd(*prep_in, *s5_totals)
    from_hp = lambda a: jnp.moveaxis(a.reshape(nl, 2, S5H, S5W // S5H, 64), 2, -1)
    gs5 = {"s5_lam_re": d_prep[0].reshape(nl, 2, 32, 64), "s5_lam_im": d_prep[1].reshape(nl, 2, 32, 64),
           "s5_log_dt": d_prep[2].reshape(nl, 2, 32), "s5_b_re": from_hp(d_prep[3]), "s5_b_im": from_hp(d_prep[4]),
           "s5_c_re": jnp.swapaxes(d_prep[5].reshape(nl, 2, S5H, 32, 64), 2, 3),
           "s5_c_im": jnp.swapaxes(d_prep[6].reshape(nl, 2, S5H, 32, 64), 2, 3)}
    d_hg = lb_call(hg_full[0], jnp.stack(g["dlb"]))

    dmod_loc = jnp.stack([jnp.concatenate([dmod[l][k][:, 0] for k in range(6)], axis=-1) for l in range(nl)])
    (g_dmod,) = _exchange("gather_dmod", [dmod_loc], False)
    gcols = lax.dynamic_slice_in_dim(g_dmod, me * cols, cols, axis=3)
    g16 = jnp.concatenate([jnp.moveaxis(gcols[:, :, 1], 0, 1), jnp.moveaxis(gcols[:, :, 0], 0, 1)], axis=1)
    grad_w_mod, dcraw = mod_bwd(craw, p["w_mod"], g16)
    d_c_ctx = jnp.sum(dcraw[8:], axis=0)

    stk = lambda k: jnp.stack(g[k])
    small_g = {"c_ctx": d_c_ctx, "b_mod": dmod_loc[:, 0] + dmod_loc[:, 1], "s5_d": stk("s5_d"), "b_glu": stk("b_glu"),
               "hg_lb": d_hg.reshape(nl, 2, HGW), "hg_norm_w": stk("hg_norm_w"), "ln1_g": stk("ln1_g")[:, 0, 0],
               "ln1_b": stk("ln1_b")[:, 0, 0], "conv_b": stk("conv_b"), "ln2_g": stk("ln2_g")[:, 0, 0],
               "ln2_b": stk("ln2_b")[:, 0, 0], **gs5}
    g_pack = _pack([small_g[k] for k in SMALL])
    (g_parts,) = _exchange("scatter_small_grads", [g_pack.reshape(N_DEV, -1, PACK_W)], True)
    (g_small,) = _exchange("gather_small_grads", [sum_parts(g_parts)], False)
    g_small = g_small.reshape(1, -1, PACK_W)

    out = {}
    hgw = {"": hg_full[0].reshape(nl, 2, HGW), "m_": hg_full[1].reshape(nl, 2, HGW), "v_": hg_full[2].reshape(nl, 2, HGW)}
    full = lambda pre, k: hgw[pre] if k == "hg_lb" else p[pre + k]
    shapes = [full("", k).shape for k in SMALL]
    res = adamw("adamw_small", *[_pack([full(pre, k) for k in SMALL]) for pre in ("", "m_", "v_")], g_small)
    for kind, packed in zip(("grad_", "delta_", "new_m_", "new_v_"), res):
        for k, a in zip(SMALL, _unpack(packed, shapes)):
            if k == "hg_lb":
                a = lax.dynamic_slice_in_dim(a, me * (HGW // N_DEV), HGW // N_DEV, axis=2)
            out[kind + k] = a
    kinds = ("grad_", "delta_", "new_m_", "new_v_")
    results = {}

    def update_layer(k, l, gs):
        prev = results.get(k) or [lax.empty(p[k].shape, F32) for _ in kinds]
        results[k] = adamw_layer(f"adamw_{k}{l}", l, p[k], p["m_" + k], p["v_" + k], gs, prev)

    for l in range(nl):
        update_layer("w_mod", l, grad_w_mod[l][None])
    for l in reversed(range(nl)):
        for i, (ks, handle) in enumerate(scatters[l]):
            recv = exchange_wait(f"scatter_wait{l}_{i}", handle, g_small, True)
            for k, gsum in zip(ks, recv):
                update_layer(k, l, gsum)
    for k, res in results.items():
        for kind, a in zip(kinds, res):
            out[kind + k] = a
    return (loss, grad_x, *[out[kind + k] for kind in ("grad_", "delta_", "new_m_", "new_v_") for k in WEIGHTS])


def kernel(x, c, ctx, c_ctx, w_mod, b_mod, w_in, s5_lam_re, s5_lam_im, s5_log_dt, s5_b_re, s5_b_im, s5_c_re, s5_c_im, s5_d, w_glu, b_glu, hg_lb, hg_norm_w, w_out, ln1_g, ln1_b, w_up, conv_w, conv_b, w_down, ln2_g, ln2_b, loss_target, m_c_ctx, m_w_mod, m_b_mod, m_w_in, m_s5_lam_re, m_s5_lam_im, m_s5_log_dt, m_s5_b_re, m_s5_b_im, m_s5_c_re, m_s5_c_im, m_s5_d, m_w_glu, m_b_glu, m_hg_lb, m_hg_norm_w, m_w_out, m_ln1_g, m_ln1_b, m_w_up, m_conv_w, m_conv_b, m_w_down, m_ln2_g, m_ln2_b, v_c_ctx, v_w_mod, v_b_mod, v_w_in, v_s5_lam_re, v_s5_lam_im, v_s5_log_dt, v_s5_b_re, v_s5_b_im, v_s5_c_re, v_s5_c_im, v_s5_d, v_w_glu, v_b_glu, v_hg_lb, v_hg_norm_w, v_w_out, v_ln1_g, v_ln1_b, v_w_up, v_conv_w, v_conv_b, v_w_down, v_ln2_g, v_ln2_b):
    return _step(dict(locals()))
```

```python
import functools
import math

import jax
import jax.numpy as jnp
from jax import lax
from jax.experimental import pallas as pl
from jax.experimental.pallas import tpu as pltpu

F32, BF16 = jnp.float32, jnp.bfloat16
N_DEV = 8
AXES = ("x", "y", "c")
D = 1024
S5W = 512
S5P = 2048
S5H = 16
HGW = 512
HD = 128
NH = 4
CK = 32
DFF = 2816
GRID_W = 64
INC = 3072
ALPHA = 8.0 ** 0.25
LN_EPS = 1e-5
RMS_EPS = 1e-6
LR, B1, B2, EPS, WD, STEP = 0.001, 0.9, 0.999, 1e-08, 0.01, 10
TT = 256
VMEM_MB = 56

NN = ((1,), (0,))
NT = ((1,), (1,))
TN = ((0,), (0,))


def _cparams(n_axes):
    return pltpu.CompilerParams(dimension_semantics=("arbitrary",) * n_axes, vmem_limit_bytes=VMEM_MB << 20)


def _dot(a, b, dims):
    return lax.dot_general(a.astype(BF16), b.astype(BF16), (dims, ((), ())), preferred_element_type=F32)


@jax.custom_vjp
def mm_nn(a, b):
    return _dot(a, b, NN)


@jax.custom_vjp
def mm_nt(a, b):
    return _dot(a, b, NT)


@jax.custom_vjp
def mm_tn(a, b):
    return _dot(a, b, TN)


mm_nn.defvjp(lambda a, b: (_dot(a, b, NN), (a, b)), lambda r, g: (_dot(g, r[1], NT), _dot(r[0], g, TN)))
mm_nt.defvjp(lambda a, b: (_dot(a, b, NT), (a, b)), lambda r, g: (_dot(g, r[1], NN), _dot(g, r[0], TN)))
mm_tn.defvjp(lambda a, b: (_dot(a, b, TN), (a, b)), lambda r, g: (_dot(r[1], g, NT), _dot(r[0], g, NN)))


def _roll_rows(u, s):
    return pltpu.roll(u, s % u.shape[0], 0)


@functools.partial(jax.custom_vjp, nondiff_argnums=(2,))
def shift_rows(u, m, s):
    return _roll_rows(u, s) * m


def _shift_fwd(u, m, s):
    return _roll_rows(u, s) * m, m


def _shift_bwd(s, m, g):
    return _roll_rows(g * m, -s), jnp.zeros_like(m)


shift_rows.defvjp(_shift_fwd, _shift_bwd)


def _scan_tile(pos, nt, rev):
    return jnp.where(pos == 0, 0, nt - pos) if rev else pos


def _exchange(name, arrays, all_to_all):
    k_arr = len(arrays)

    def body(*refs):
        ins, outs = refs[:k_arr], refs[k_arr:2 * k_arr]
        send_sems, recv_sems, local_sems = refs[2 * k_arr:]
        me = lax.axis_index("x") * 4 + lax.axis_index("y") * 2 + lax.axis_index("c")
        local = []
        for k in range(k_arr):
            cp = pltpu.make_async_copy(ins[k].at[me] if all_to_all else ins[k], outs[k].at[me], local_sems.at[k])
            cp.start()
            local.append(cp)
        sends = []
        for d in range(1, N_DEV):
            p = (me + d) % N_DEV
            for k in range(k_arr):
                cp = pltpu.make_async_remote_copy(
                    src_ref=ins[k].at[p] if all_to_all else ins[k], dst_ref=outs[k].at[me],
                    send_sem=send_sems.at[k, d - 1], recv_sem=recv_sems.at[k, d - 1],
                    device_id=(p // 4, (p // 2) % 2, p % 2), device_id_type=pl.DeviceIdType.MESH)
                cp.start()
                sends.append(cp)
        for d in range(1, N_DEV):
            q = (me + N_DEV - d) % N_DEV
            for k in range(k_arr):
                pltpu.make_async_remote_copy(
                    src_ref=ins[k].at[q] if all_to_all else ins[k], dst_ref=outs[k].at[q],
                    send_sem=send_sems.at[k, d - 1], recv_sem=recv_sems.at[k, d - 1],
                    device_id=(q // 4, (q // 2) % 2, q % 2), device_id_type=pl.DeviceIdType.MESH).wait_recv()
        for cp in sends:
            cp.wait_send()
        for cp in local:
            cp.wait()

    shapes = [a.shape if all_to_all else (N_DEV,) + a.shape for a in arrays]
    return pl.pallas_call(
        body, name=name,
        out_shape=[jax.ShapeDtypeStruct(s, a.dtype) for s, a in zip(shapes, arrays)],
        in_specs=[pl.BlockSpec(memory_space=pl.ANY)] * k_arr,
        out_specs=[pl.BlockSpec(memory_space=pl.ANY)] * k_arr,
        scratch_shapes=[pltpu.SemaphoreType.DMA((k_arr, N_DEV - 1)), pltpu.SemaphoreType.DMA((k_arr, N_DEV - 1)),
                        pltpu.SemaphoreType.DMA((k_arr,))],
    )(*arrays)


_HBM = pl.BlockSpec(memory_space=pltpu.HBM)
_SEM = pl.BlockSpec(memory_space=pltpu.SEMAPHORE)
_EFFECT = pltpu.SideEffectType.DATAFLOW_SIDE_EFFECTING


def _peer(i):
    return (i // 4, (i // 2) % 2, i % 2)


def exchange_start(name, srcs, lands, all_to_all, after=None):
    k_arr = len(srcs)
    n_sem = k_arr * (N_DEV - 1)
    extra = [] if after is None else [after]

    def body(*refs):
        ins, lz = refs[:k_arr], refs[k_arr:2 * k_arr]
        first = 2 * k_arr + len(extra)
        send_sems = refs[first:first + n_sem]
        recv_sems = refs[first + n_sem:first + 2 * n_sem]
        me = lax.axis_index("x") * 4 + lax.axis_index("y") * 2 + lax.axis_index("c")
        for d in range(1, N_DEV):
            p = (me + d) % N_DEV
            for k in range(k_arr):
                s = k * (N_DEV - 1) + d - 1
                pltpu.make_async_remote_copy(
                    src_ref=ins[k].at[p] if all_to_all else ins[k], dst_ref=lz[k].at[me],
                    send_sem=send_sems[s], recv_sem=recv_sems[s],
                    device_id=_peer(p), device_id_type=pl.DeviceIdType.MESH).start()
        refs[-1][...] = jnp.zeros_like(refs[-1])

    arrs = list(srcs) + list(lands)
    res = pl.pallas_call(
        body, name=name,
        out_shape=(*[pltpu.SemaphoreType.DMA(())] * (2 * n_sem), *[pltpu.HBM(a.shape, a.dtype) for a in arrs],
                   jax.ShapeDtypeStruct((8, 128), F32)),
        in_specs=[_HBM] * len(arrs) + [pl.BlockSpec(memory_space=pl.ANY)] * len(extra),
        out_specs=(*[_SEM] * (2 * n_sem), *[_HBM] * len(arrs), pl.BlockSpec(memory_space=pltpu.VMEM)),
        input_output_aliases={i: 2 * n_sem + i for i in range(len(arrs))},
        compiler_params=pltpu.CompilerParams(has_side_effects=_EFFECT),
    )(*[pltpu.with_memory_space_constraint(a, pltpu.HBM) for a in arrs], *extra)
    return res[:-1], res[-1]


def exchange_wait(name, handle, after, all_to_all):
    k_arr = len(handle) // (2 * N_DEV)
    n_sem = k_arr * (N_DEV - 1)
    sems, arrs = handle[:2 * n_sem], handle[2 * n_sem:]

    def body(*refs):
        ins, lz = refs[:k_arr], refs[k_arr:2 * k_arr]
        s_sems = refs[2 * k_arr:2 * k_arr + n_sem]
        r_sems = refs[2 * k_arr + n_sem:2 * k_arr + 2 * n_sem]
        me = lax.axis_index("x") * 4 + lax.axis_index("y") * 2 + lax.axis_index("c")
        for d in range(1, N_DEV):
            q = (me + N_DEV - d) % N_DEV
            for k in range(k_arr):
                s = k * (N_DEV - 1) + d - 1
                cp = pltpu.make_async_remote_copy(
                    src_ref=ins[k].at[q] if all_to_all else ins[k], dst_ref=lz[k].at[q],
                    send_sem=s_sems[s], recv_sem=r_sems[s],
                    device_id=_peer(q), device_id_type=pl.DeviceIdType.MESH)
                cp.wait_send()
                cp.wait_recv()

    res = pl.pallas_call(
        body, name=name, out_shape=tuple(pltpu.HBM(a.shape, a.dtype) for a in arrs),
        in_specs=[_HBM] * len(arrs) + [_SEM] * (2 * n_sem) + [pl.BlockSpec(memory_space=pl.ANY)],
        out_specs=tuple([_HBM] * len(arrs)),
        input_output_aliases={i: i for i in range(len(arrs))},
        compiler_params=pltpu.CompilerParams(has_side_effects=_EFFECT),
    )(*arrs, *sems, after)
    return res[k_arr:]


def _own_block_set(src, all_to_all):
    me = lax.axis_index("x") * 4 + lax.axis_index("y") * 2 + lax.axis_index("c")
    own = lax.dynamic_index_in_dim(src, me, 0, keepdims=False) if all_to_all else src
    shape = src.shape if all_to_all else (N_DEV,) + src.shape
    return lax.dynamic_update_index_in_dim(lax.empty(shape, src.dtype), own, me, 0)


def _tile(n, prefs):
    for t in prefs:
        if n % t == 0:
            return t
    raise ValueError(n)


def dense_nn(name, a, w, out_dtype=F32):
    n, k = a.shape
    m = w.shape[1]
    tn, tm = _tile(n, (1088, 256)), _tile(m, (1024, 1408, 512, 256, 128))

    def body(a_ref, w_ref, o_ref):
        o_ref[...] = _dot(a_ref[...], w_ref[...], NN).astype(o_ref.dtype)

    return pl.pallas_call(
        body, name=name, grid=(m // tm, n // tn), out_shape=jax.ShapeDtypeStruct((n, m), out_dtype),
        in_specs=[pl.BlockSpec((tn, k), lambda j, i: (i, 0)), pl.BlockSpec((k, tm), lambda j, i: (0, j))],
        out_specs=pl.BlockSpec((tn, tm), lambda j, i: (i, j)), compiler_params=_cparams(2))(a, w)


def dense_nt(name, g, w, out_dtype=F32):
    n, m = g.shape
    k = w.shape[0]
    tn, tk = _tile(n, (544, 256)), _tile(k, (1024, 1408, 512, 256, 128))

    def body(g_ref, w_ref, o_ref):
        o_ref[...] = _dot(g_ref[...], w_ref[...], NT).astype(o_ref.dtype)

    return pl.pallas_call(
        body, name=name, grid=(k // tk, n // tn), out_shape=jax.ShapeDtypeStruct((n, k), out_dtype),
        in_specs=[pl.BlockSpec((tn, m), lambda j, i: (i, 0)), pl.BlockSpec((tk, m), lambda j, i: (j, 0))],
        out_specs=pl.BlockSpec((tn, tk), lambda j, i: (i, j)), compiler_params=_cparams(2))(g, w)


def dense_nt2(name, g1, g2, w, out_dtype=F32):
    n, m = g1.shape
    k = w.shape[0]
    tn, tk = _tile(n, (544, 256)), _tile(k, (1024, 1408, 512, 256, 128))

    def body(g1_ref, g2_ref, w1_ref, w2_ref, o_ref):
        o_ref[...] = (_dot(g1_ref[...], w1_ref[...], NT) + _dot(g2_ref[...], w2_ref[...], NT)).astype(o_ref.dtype)

    half = lambda h: pl.BlockSpec((tk, m), lambda j, i: (j, h))
    rows = pl.BlockSpec((tn, m), lambda j, i: (i, 0))
    return pl.pallas_call(
        body, name=name, grid=(k // tk, n // tn), out_shape=jax.ShapeDtypeStruct((n, k), out_dtype),
        in_specs=[rows, rows, half(0), half(1)], out_specs=pl.BlockSpec((tn, tk), lambda j, i: (i, j)),
        compiler_params=_cparams(2))(g1, g2, w, w)


def dense_tn(name, a, g, out_dtype=BF16):
    n, k = a.shape
    m = g.shape[1]
    tn = _tile(n, (1088, 256))
    tk = _tile(k, (1024, 1408, 512, 256, 128))
    tm = _tile(m, (1024, 1408, 512, 256, 128))
    nt = n // tn

    def body(a_ref, g_ref, o_ref, acc_ref):
        t = pl.program_id(2)

        @pl.when(t == 0)
        def _():
            acc_ref[...] = jnp.zeros_like(acc_ref)

        acc_ref[...] += _dot(a_ref[...], g_ref[...], TN)

        @pl.when(t == nt - 1)
        def _():
            o_ref[...] = acc_ref[...].astype(o_ref.dtype)

    return pl.pallas_call(
        body, name=name, grid=(k // tk, m // tm, nt), out_shape=jax.ShapeDtypeStruct((k, m), out_dtype),
        in_specs=[pl.BlockSpec((tn, tk), lambda i, j, t: (t, i)), pl.BlockSpec((tn, tm), lambda i, j, t: (t, j))],
        out_specs=pl.BlockSpec((tk, tm), lambda i, j, t: (i, j)),
        scratch_shapes=[pltpu.VMEM((tk, tm), F32)], compiler_params=_cparams(3))(a, g)


def _c0(j):
    return 0


def _tspec(w, cb):
    return pl.BlockSpec((TT, w), lambda j, i: (i, cb(j)))


def _pspec(arr, w, cb):
    two = arr.shape[0] == 2
    return pl.BlockSpec((None, arr.shape[1], w), lambda j, i: (jnp.minimum(i, 1) if two else 0, 0, cb(j)))


def block_fwd(name, fn, n, tiled, params, outs, n_col=1):
    nt_, np_ = len(tiled), len(params)

    def body(*refs):
        i = pl.program_id(1)
        tv = [r[...].astype(F32) for r in refs[:nt_]]
        pv = [r[...].astype(F32) for r in refs[nt_:nt_ + np_]]
        for o_ref, r in zip(refs[nt_ + np_:], fn(tv, pv, i)):
            o_ref[...] = r.astype(o_ref.dtype)

    return pl.pallas_call(
        body, name=name, grid=(n_col, n // TT),
        out_shape=[jax.ShapeDtypeStruct((n, c), dt) for c, dt, _, _ in outs],
        in_specs=[_tspec(w, cb) for _, w, cb in tiled] + [_pspec(a, w, cb) for a, w, cb in params],
        out_specs=[_tspec(w, cb) for _, _, w, cb in outs], compiler_params=_cparams(2),
    )(*[a for a, _, _ in tiled], *[a for a, _, _ in params])


def block_bwd(name, fn, n, tiled, params, cots, grads, n_col=1):
    nt_, np_, nc_ = len(tiled), len(params), len(cots)
    want = [k for k, g in enumerate(grads) if g is not None]

    def body(*refs):
        i = pl.program_id(1)
        tv = [r[...].astype(F32) for r in refs[:nt_]]
        pv = [r[...].astype(F32) for r in refs[nt_:nt_ + np_]]
        cv = [r[...].astype(F32) for r in refs[nt_ + np_:nt_ + np_ + nc_]]
        o_refs = refs[nt_ + np_ + nc_:]
        _, vjp = jax.vjp(lambda t, p: list(fn(t, p, i)), tv, pv)
        dt, dp = vjp(cv)
        for o_ref, k in zip(o_refs, want):
            o_ref[...] = dt[k].astype(o_ref.dtype)
        for o_ref, g, (arr, _, _) in zip(o_refs[len(want):], dp, params):
            first = (i == 0) | (i == 1) if arr.shape[0] == 2 else i == 0

            @pl.when(first)
            def _(o_ref=o_ref):
                o_ref[...] = jnp.zeros_like(o_ref)

            o_ref[...] += g

    res = pl.pallas_call(
        body, name=name, grid=(n_col, n // TT),
        out_shape=[jax.ShapeDtypeStruct((n, grads[k][0]), grads[k][1]) for k in want]
        + [jax.ShapeDtypeStruct(a.shape, F32) for a, _, _ in params],
        in_specs=[_tspec(w, cb) for _, w, cb in tiled] + [_pspec(a, w, cb) for a, w, cb in params]
        + [_tspec(w, cb) for _, w, cb in cots],
        out_specs=[_tspec(grads[k][2], grads[k][3]) for k in want] + [_pspec(a, w, cb) for a, w, cb in params],
        compiler_params=_cparams(2),
    )(*[a for a, _, _ in tiled], *[a for a, _, _ in params], *[a for a, _, _ in cots])
    return res[:len(want)], res[len(want):]


def f_mod(tv, pv, i):
    (x,), (sh, sc) = tv, pv
    return (x * (1.0 + sc) + sh,)


def f_mod_id(tv, pv, i):
    return (f_mod(tv, pv, i)[0], tv[0])


def f_ln(tv, pv, i):
    (x, z), (gate, g, b, sh, sc) = tv, pv
    pre = ALPHA * x + gate * z
    mu = jnp.mean(pre, axis=-1, keepdims=True)
    var = jnp.mean(jnp.square(pre - mu), axis=-1, keepdims=True)
    xn = (pre - mu) * lax.rsqrt(var + LN_EPS) * g + b
    return xn, xn * (1.0 + sc) + sh


def f_mix(tv, pv, i):
    (pu, pg, y0, y1, o0, o1), (d_skip, w_glu, b_glu, norm_w) = tv, pv
    s5y = jax.nn.gelu(y0 + y1 + pu * d_skip)
    s5o = s5y * jax.nn.sigmoid(mm_nn(s5y, w_glu) + b_glu)
    o = o0 + o1
    heads = []
    for h in range(NH):
        oh = o[:, h * HD:(h + 1) * HD]
        heads.append(oh * lax.rsqrt(jnp.mean(jnp.square(oh), axis=-1, keepdims=True) + RMS_EPS) * norm_w)
    hg = jnp.concatenate(heads, axis=-1) * jax.nn.silu(pg)
    return (jnp.concatenate([s5o, hg], axis=-1),)


def f_act(tv, pv, i):
    (ua, ug), (cwa, cwg, cba, cbg) = tv, pv
    t = lax.broadcasted_iota(jnp.int32, (TT, 1), 0)
    lat = i > 0
    m_dn = jnp.where((t == 0) | (lat & (t % GRID_W == 0)), 0.0, 1.0)
    m_up = jnp.where((t == TT - 1) | (lat & (t % GRID_W == GRID_W - 1)), 0.0, 1.0)

    def conv(u, w, b):
        return shift_rows(u, m_dn, 1) * w[0:1] + u * w[1:2] + shift_rows(u, m_up, -1) * w[2:3] + b

    return (jax.nn.silu(conv(ua, cwa, cba)) * conv(ug, cwg, cbg),)


def loss_and_grad(xf, target):
    n = xf.shape[0]

    def body(x_ref, t_ref, dy_ref, l_ref):
        i = pl.program_id(0)

        @pl.when(i == 0)
        def _():
            l_ref[...] = jnp.zeros_like(l_ref)
            dy_ref[...] = jnp.zeros_like(dy_ref)

        @pl.when(i > 0)
        def _():
            e = x_ref[...] - t_ref[...]
            dy_ref[...] = e * (1.0 / D)
            l_ref[...] += 0.5 / D * jnp.sum(jnp.square(e))

    return pl.pallas_call(
        body, name="loss", grid=(n // TT,),
        out_shape=[jax.ShapeDtypeStruct((n, D), F32), jax.ShapeDtypeStruct((8, 128), F32)],
        in_specs=[pl.BlockSpec((TT, D), lambda i: (i, 0)), pl.BlockSpec((TT, D), lambda i: (jnp.maximum(i - 1, 0), 0))],
        out_specs=[pl.BlockSpec((TT, D), lambda i: (i, 0)), pl.BlockSpec((8, 128), lambda i: (0, 0))],
        compiler_params=_cparams(1))(xf, target)


def f_prep(lr, li, ldt, bre, bim, cre, cim):
    gi = lax.broadcasted_iota(jnp.int32, (S5W // S5H, S5P), 0)
    gc = lax.broadcasted_iota(jnp.int32, (S5W // S5H, S5P), 1) // 64
    dt = jnp.exp(jnp.sum(jnp.where(gi == gc, ldt, 0.0), axis=0, keepdims=True))
    mag, ang = jnp.exp(lr * dt), li * dt
    ar, ai = mag * jnp.cos(ang), mag * jnp.sin(ang)
    den = lr * lr + li * li
    nr, ni = ar - 1.0, ai
    cr = (nr * lr + ni * li) / den
    ci = (ni * lr - nr * li) / den
    bbr = cr * bre - ci * bim
    bbi = cr * bim + ci * bre
    rg = lax.broadcasted_iota(jnp.int32, (S5W, S5P), 0) // S5H
    cg = lax.broadcasted_iota(jnp.int32, (S5W, S5P), 1) // 64
    mask = (rg == cg).astype(F32)
    blk = lambda a: jnp.concatenate([a] * (S5W // S5H), axis=0) * mask
    return ar, ai, blk(bbr), blk(bbi), blk(cre), blk(-cim)


def s5_prep(lr, li, ldt, bre, bim, cre, cim):
    n2 = lr.shape[0]

    def body(lr_r, li_r, ldt_r, bre_r, bim_r, cre_r, cim_r, a_ref, b_ref, c_ref):
        ar, ai, bbr, bbi, cbr, cbi = f_prep(lr_r[...], li_r[...], ldt_r[...], bre_r[...], bim_r[...], cre_r[...], cim_r[...])
        a_ref[0], a_ref[1] = ar, ai
        b_ref[0], b_ref[1] = bbr.astype(BF16), bbi.astype(BF16)
        c_ref[0], c_ref[1] = cbr.astype(BF16), cbi.astype(BF16)

    sp = lambda r, c: pl.BlockSpec((None, r, c), lambda i: (i, 0, 0))
    sp4 = lambda r, c: pl.BlockSpec((None, 2, r, c), lambda i: (i, 0, 0, 0))
    return pl.pallas_call(
        body, name="s5_prep", grid=(n2,),
        out_shape=[jax.ShapeDtypeStruct((n2, 2, 1, S5P), F32), jax.ShapeDtypeStruct((n2, 2, S5W, S5P), BF16),
                   jax.ShapeDtypeStruct((n2, 2, S5W, S5P), BF16)],
        in_specs=[sp(1, S5P), sp(1, S5P), sp(32, 1), sp(S5H, S5P), sp(S5H, S5P), sp(S5H, S5P), sp(S5H, S5P)],
        out_specs=[sp4(1, S5P), sp4(S5W, S5P), sp4(S5W, S5P)], compiler_params=_cparams(1),
    )(lr, li, ldt, bre, bim, cre, cim)


def s5_prep_bwd(lr, li, ldt, bre, bim, cre, cim, da, db, dc):
    n2 = lr.shape[0]

    def body(lr_r, li_r, ldt_r, bre_r, bim_r, cre_r, cim_r, da_r, db_r, dc_r, *outs):
        args = [r[...] for r in (lr_r, li_r, ldt_r, bre_r, bim_r, cre_r, cim_r)]
        _, vjp = jax.vjp(f_prep, *args)
        for o_ref, g in zip(outs, vjp((da_r[0], da_r[1], db_r[0], db_r[1], dc_r[0], dc_r[1]))):
            o_ref[...] = g

    sp = lambda r, c: pl.BlockSpec((None, r, c), lambda i: (i, 0, 0))
    sp4 = lambda r, c: pl.BlockSpec((None, 2, r, c), lambda i: (i, 0, 0, 0))
    ins = [sp(1, S5P), sp(1, S5P), sp(32, 1), sp(S5H, S5P), sp(S5H, S5P), sp(S5H, S5P), sp(S5H, S5P)]
    return pl.pallas_call(
        body, name="s5_prep_bwd", grid=(n2,),
        out_shape=[jax.ShapeDtypeStruct(a.shape, F32) for a in (lr, li, ldt, bre, bim, cre, cim)],
        in_specs=ins + [sp4(1, S5P), sp4(S5W, S5P), sp4(S5W, S5P)], out_specs=ins, compiler_params=_cparams(1),
    )(lr, li, ldt, bre, bim, cre, cim, da, db, dc)


S5_DIAG = 2
_CU, _CP = S5W // S5_DIAG, S5P // S5_DIAG


def _bd_nn(u, w_ref, k):
    return jnp.concatenate([_dot(u[:, j * _CU:(j + 1) * _CU], w_ref[k, j * _CU:(j + 1) * _CU, j * _CP:(j + 1) * _CP], NN)
                            for j in range(S5_DIAG)], axis=1)


def _bd_nt(x, w_ref, k):
    return jnp.concatenate([_dot(x[:, j * _CP:(j + 1) * _CP], w_ref[k, j * _CU:(j + 1) * _CU, j * _CP:(j + 1) * _CP], NT)
                            for j in range(S5_DIAG)], axis=1)


def _bd_tn_acc(acc_ref, k, a, g):
    for j in range(S5_DIAG):
        acc_ref[k, j * _CU:(j + 1) * _CU, j * _CP:(j + 1) * _CP] += _dot(a[:, j * _CU:(j + 1) * _CU],
                                                                         g[:, j * _CP:(j + 1) * _CP], TN)


def _scan_rows(xr_ref, xi_ref, ar, ai, desc, cr_ref, ci_ref):
    unroll = 8

    def group(gi, carry):
        cr, ci = carry
        base = gi * unroll
        for j in range(unroll):
            t = TT - 1 - (base + j) if desc else base + j
            nr = ar * cr - ai * ci + xr_ref[pl.ds(t, 1), :]
            ni = ar * ci + ai * cr + xi_ref[pl.ds(t, 1), :]
            xr_ref[pl.ds(t, 1), :] = nr
            xi_ref[pl.ds(t, 1), :] = ni
            cr, ci = nr, ni
        return cr, ci

    cr, ci = lax.fori_loop(0, TT // unroll, group, (cr_ref[...], ci_ref[...]))
    cr_ref[...] = cr
    ci_ref[...] = ci


def s5_fwd(name, proj, a, bb, cb, ld, rev):
    n = proj.shape[0]
    nt = n // TT

    def body(u_ref, a_ref, b_ref, c_ref, xr_ref, xi_ref, y_ref, cr_ref, ci_ref):
        @pl.when(pl.program_id(0) == 0)
        def _():
            cr_ref[...] = jnp.zeros_like(cr_ref)
            ci_ref[...] = jnp.zeros_like(ci_ref)

        u = u_ref[...]
        xr_ref[...] = _bd_nn(u, b_ref, 0)
        xi_ref[...] = _bd_nn(u, b_ref, 1)
        _scan_rows(xr_ref, xi_ref, a_ref[0], a_ref[1], rev, cr_ref, ci_ref)
        y_ref[...] = _bd_nt(xr_ref[...], c_ref, 0) + _bd_nt(xi_ref[...], c_ref, 1)

    tile = lambda w: pl.BlockSpec((TT, w), lambda s: (_scan_tile(s, nt, rev), 0))
    par = lambda r: pl.BlockSpec((None, 2, r, S5P), lambda s: (ld, 0, 0, 0))
    return pl.pallas_call(
        body, name=name, grid=(nt,),
        out_shape=[jax.ShapeDtypeStruct((n, S5P), F32), jax.ShapeDtypeStruct((n, S5P), F32),
                   jax.ShapeDtypeStruct((n, S5W), F32)],
        in_specs=[tile(S5W), par(1), par(S5W), par(S5W)], out_specs=[tile(S5P), tile(S5P), tile(S5W)],
        scratch_shapes=[pltpu.VMEM((1, S5P), F32), pltpu.VMEM((1, S5P), F32)], compiler_params=_cparams(1),
    )(proj, a, bb, cb)


def s5_bwd(name, proj, dy, xr, xi, a, bb, cb, ld, rev, totals):
    n = proj.shape[0]
    nt = n // TT
    tb = TT // 8

    def tile_of(s):
        return _scan_tile(nt - 1 - s, nt, rev)

    def edge_of(s):
        pos = nt - 1 - s
        prev = _scan_tile(jnp.maximum(pos - 1, 0), nt, rev)
        return prev * tb if rev else jnp.maximum(pos * tb - 1, 0)

    def body(u_ref, dy_ref, xr_ref, xi_ref, er_ref, ei_ref, a_ref, b_ref, c_ref, _ta, _tb, _tc,
             du_ref, da_ref, db_ref, dc_ref, gr_ref, gi_ref, cr_ref, ci_ref):
        s = pl.program_id(0)

        @pl.when(s == 0)
        def _():
            cr_ref[...] = jnp.zeros_like(cr_ref)
            ci_ref[...] = jnp.zeros_like(ci_ref)
            da_ref[...] = jnp.zeros_like(da_ref)
            db_ref[...] = jnp.zeros_like(db_ref)
            dc_ref[...] = jnp.zeros_like(dc_ref)

        dyv, u = dy_ref[...], u_ref[...]
        xrv, xiv = xr_ref[...], xi_ref[...]
        gr_ref[...] = _bd_nn(dyv, c_ref, 0)
        gi_ref[...] = _bd_nn(dyv, c_ref, 1)
        _bd_tn_acc(dc_ref, 0, dyv, xrv)
        _bd_tn_acc(dc_ref, 1, dyv, xiv)
        _scan_rows(gr_ref, gi_ref, a_ref[0], -a_ref[1], not rev, cr_ref, ci_ref)
        g_r, g_i = gr_ref[...], gi_ref[...]
        rows = lax.broadcasted_iota(jnp.int32, (TT, 1), 0)
        live = jnp.where(s == nt - 1, 0.0, 1.0)
        if rev:
            pr = jnp.where(rows == TT - 1, er_ref[0:1, :] * live, _roll_rows(xrv, -1))
            pi = jnp.where(rows == TT - 1, ei_ref[0:1, :] * live, _roll_rows(xiv, -1))
        else:
            pr = jnp.where(rows == 0, er_ref[7:8, :] * live, _roll_rows(xrv, 1))
            pi = jnp.where(rows == 0, ei_ref[7:8, :] * live, _roll_rows(xiv, 1))
        da_ref[0] += jnp.sum(g_r * pr + g_i * pi, axis=0, keepdims=True)
        da_ref[1] += jnp.sum(g_i * pr - g_r * pi, axis=0, keepdims=True)
        du_ref[...] = _bd_nt(g_r, b_ref, 0) + _bd_nt(g_i, b_ref, 1)
        _bd_tn_acc(db_ref, 0, u, g_r)
        _bd_tn_acc(db_ref, 1, u, g_i)

    tile = lambda w: pl.BlockSpec((TT, w), lambda s: (tile_of(s), 0))
    edge = pl.BlockSpec((8, S5P), lambda s: (edge_of(s), 0))
    par = lambda r: pl.BlockSpec((None, 2, r, S5P), lambda s: (ld, 0, 0, 0))
    whole = pl.BlockSpec(memory_space=pl.ANY)
    return pl.pallas_call(
        body, name=name, grid=(nt,),
        out_shape=[jax.ShapeDtypeStruct((n, S5W), F32)] + [jax.ShapeDtypeStruct(t.shape, F32) for t in totals],
        in_specs=[tile(S5W), tile(S5W), tile(S5P), tile(S5P), edge, edge, par(1), par(S5W), par(S5W), whole, whole, whole],
        out_specs=[tile(S5W), par(1), par(S5W), par(S5W)], input_output_aliases={9: 1, 10: 2, 11: 3},
        scratch_shapes=[pltpu.VMEM((TT, S5P), F32), pltpu.VMEM((TT, S5P), F32),
                        pltpu.VMEM((1, S5P), F32), pltpu.VMEM((1, S5P), F32)], compiler_params=_cparams(1),
    )(proj, dy, xr, xi, xr, xi, a, bb, cb, *totals)


def gla_tile(r, v, qr, lb, sts, rev):
    ncc = TT // CK
    f = lb + (1.0 - lb) * jax.nn.sigmoid(r)
    k, lf, q = 1.0 - f, jnp.log(f), jax.nn.silu(qr)
    rows = lax.broadcasted_iota(jnp.int32, (TT, 1), 0)
    pos = rows % CK
    b = lf
    for s in (1, 2, 4, 8, 16):
        m = ((pos < CK - s) if rev else (pos >= s)).astype(F32)
        b = b + shift_rows(b, m, -s if rev else s)
    etot = [jnp.sum(lf[c * CK:(c + 1) * CK], axis=0, keepdims=True) for c in range(ncc)]
    e = jnp.concatenate([jnp.broadcast_to(t, (CK, HGW)) for t in etot], axis=0)
    kd, qe, qa = k * jnp.exp(e - b), q * jnp.exp(b), q * jnp.exp(b - e)
    cm = [(rows // CK == c).astype(F32) for c in range(ncc)]
    r2 = lax.broadcasted_iota(jnp.int32, (TT, TT), 0)
    c2 = lax.broadcasted_iota(jnp.int32, (TT, TT), 1)
    amask = (r2 // CK == c2 // CK) & ((r2 <= c2) if rev else (r2 >= c2))
    outs, new_sts = [], []
    for h in range(NH):
        ln = slice(h * HD, (h + 1) * HD)
        kdh, qeh, vh = kd[:, ln], qe[:, ln], v[:, ln]
        att = jnp.where(amask, mm_nt(qa[:, ln], kdh), 0.0)
        ds = mm_tn(jnp.concatenate([kdh * cm[c] for c in range(ncc)], axis=1), vh)
        st, starts = sts[h], [None] * ncc
        for c in (reversed(range(ncc)) if rev else range(ncc)):
            starts[c] = st
            dec = jnp.transpose(jnp.broadcast_to(jnp.exp(etot[c][:, ln]), (HD, HD)))
            st = dec * st + ds[c * HD:(c + 1) * HD]
        new_sts.append(st)
        qex = jnp.concatenate([qeh * cm[c] for c in range(ncc)], axis=1)
        outs.append(mm_nn(att, vh) + mm_nn(qex, jnp.concatenate(starts, axis=0)))
    return jnp.concatenate(outs, axis=1), new_sts


def _gla_specs(n, rev, order):
    nt = n // TT
    fcol = 2 if rev else 1
    tile = lambda cbk: pl.BlockSpec((TT, HGW), lambda s: (order(s), cbk))
    return nt, [tile(fcol), tile(3), tile(4)], tile(0)


def gla_fwd(name, proj, lb, rev):
    n = proj.shape[0]
    nt, in_tiles, out_tile = _gla_specs(n, rev, lambda s: _scan_tile(s, n // TT, rev))

    def body(r_ref, v_ref, q_ref, lb_ref, o_ref, st_ref, s_ref):
        @pl.when(pl.program_id(0) == 0)
        def _():
            s_ref[...] = jnp.zeros_like(s_ref)

        sts = [s_ref[h] for h in range(NH)]
        for h in range(NH):
            st_ref[h] = sts[h]
        o, new = gla_tile(r_ref[...], v_ref[...], q_ref[...], lb_ref[...], sts, rev)
        o_ref[...] = o
        for h in range(NH):
            s_ref[h] = new[h]

    st_spec = pl.BlockSpec((None, NH, HD, HD), lambda s: (_scan_tile(s, nt, rev), 0, 0, 0))
    return pl.pallas_call(
        body, name=name, grid=(nt,),
        out_shape=[jax.ShapeDtypeStruct((n, HGW), F32), jax.ShapeDtypeStruct((nt, NH, HD, HD), F32)],
        in_specs=in_tiles + [pl.BlockSpec((1, HGW), lambda s: (0, 0))], out_specs=[out_tile, st_spec],
        scratch_shapes=[pltpu.VMEM((NH, HD, HD), F32)], compiler_params=_cparams(1),
    )(proj, proj, proj, lb)


def gla_bwd(name, proj, lb, st_all, do, rev):
    n = proj.shape[0]
    order = lambda s: _scan_tile(n // TT - 1 - s, n // TT, rev)
    nt, in_tiles, out_tile = _gla_specs(n, rev, order)

    def body(r_ref, v_ref, q_ref, lb_ref, st_ref, do_ref, dr_ref, dv_ref, dq_ref, dlb_ref, ds_ref):
        @pl.when(pl.program_id(0) == 0)
        def _():
            ds_ref[...] = jnp.zeros_like(ds_ref)
            dlb_ref[...] = jnp.zeros_like(dlb_ref)

        _, vjp = jax.vjp(functools.partial(gla_tile, rev=rev), r_ref[...], v_ref[...], q_ref[...], lb_ref[...],
                         [st_ref[h] for h in range(NH)])
        dr, dv, dq, dlb, dsts = vjp((do_ref[...], [ds_ref[h] for h in range(NH)]))
        dr_ref[...] = dr
        dv_ref[...] = dv
        dq_ref[...] = dq
        dlb_ref[...] += dlb
        for h in range(NH):
            ds_ref[h] = dsts[h]

    st_spec = pl.BlockSpec((None, NH, HD, HD), lambda s: (order(s), 0, 0, 0))
    row = pl.BlockSpec((1, HGW), lambda s: (0, 0))
    return pl.pallas_call(
        body, name=name, grid=(nt,),
        out_shape=[jax.ShapeDtypeStruct((n, HGW), F32)] * 3 + [jax.ShapeDtypeStruct((1, HGW), F32)],
        in_specs=in_tiles + [row, st_spec, out_tile], out_specs=[out_tile] * 3 + [row],
        scratch_shapes=[pltpu.VMEM((NH, HD, HD), F32)], compiler_params=_cparams(1),
    )(proj, proj, proj, lb, st_all, do)


def f_lb(rows):
    mx = functools.reduce(jnp.maximum, rows)
    ex = [jnp.exp(r - mx) for r in rows]
    tot = functools.reduce(jnp.add, ex)
    out, acc = [jnp.zeros_like(rows[0])], None
    for e in ex[1:]:
        acc = e / tot if acc is None else acc + e / tot
        out.append(acc)
    return out


def lb_call(hg, dlb=None):
    nl = hg.shape[0]

    def body(*refs):
        rows = [refs[0][l:l + 1, :] for l in range(nl)]
        if dlb is None:
            res = f_lb(rows)
        else:
            _, vjp = jax.vjp(f_lb, rows)
            (res,) = vjp([refs[1][l:l + 1, :] for l in range(nl)])
        for l in range(nl):
            refs[-1][l:l + 1, :] = res[l]

    args = (hg,) if dlb is None else (hg, dlb)
    return pl.pallas_call(body, name="lower_bounds" if dlb is None else "lower_bounds_bwd",
                          out_shape=jax.ShapeDtypeStruct(hg.shape, F32))(*args)


def mod_fwd(craw, w_mod, b_cols):
    nl, _, cols = w_mod.shape

    def body(c_ref, w_ref, b_ref, o_ref):
        o_ref[...] = _dot(jax.nn.silu(c_ref[...]), w_ref[...], NN) + b_ref[...]

    return pl.pallas_call(
        body, name="mod_fwd", grid=(nl,), out_shape=jax.ShapeDtypeStruct((nl, 16, cols), F32),
        in_specs=[pl.BlockSpec((16, D), lambda l: (0, 0)), pl.BlockSpec((None, D, cols), lambda l: (l, 0, 0)),
                  pl.BlockSpec((None, 1, cols), lambda l: (l, 0, 0))],
        out_specs=pl.BlockSpec((None, 16, cols), lambda l: (l, 0, 0)), compiler_params=_cparams(1))(craw, w_mod, b_cols)


def mod_bwd(craw, w_mod, g):
    nl, _, cols = w_mod.shape

    def body(c_ref, w_ref, g_ref, dw_ref, dc_ref, acc_ref):
        l = pl.program_id(0)

        @pl.when(l == 0)
        def _():
            acc_ref[...] = jnp.zeros_like(acc_ref)

        c = c_ref[...]
        s, vjp = jax.vjp(jax.nn.silu, c)
        dw_ref[...] = _dot(s, g_ref[...], TN)
        acc_ref[...] += _dot(g_ref[...], w_ref[...], NT)

        @pl.when(l == nl - 1)
        def _():
            dc_ref[...] = vjp(acc_ref[...])[0]

    return pl.pallas_call(
        body, name="mod_bwd", grid=(nl,),
        out_shape=[jax.ShapeDtypeStruct(w_mod.shape, F32), jax.ShapeDtypeStruct((16, D), F32)],
        in_specs=[pl.BlockSpec((16, D), lambda l: (0, 0)), pl.BlockSpec((None, D, cols), lambda l: (l, 0, 0)),
                  pl.BlockSpec((None, 16, cols), lambda l: (l, 0, 0))],
        out_specs=[pl.BlockSpec((None, D, cols), lambda l: (l, 0, 0)), pl.BlockSpec((16, D), lambda l: (0, 0))],
        scratch_shapes=[pltpu.VMEM((16, D), F32)], compiler_params=_cparams(1))(craw, w_mod, g)


def sum_parts(parts):
    def body(p_ref, o_ref):
        acc = p_ref[0]
        for k in range(1, parts.shape[0]):
            acc = acc + p_ref[k]
        o_ref[...] = acc

    return pl.pallas_call(body, name="sum_small_grads", out_shape=jax.ShapeDtypeStruct(parts.shape[1:], parts.dtype),
                          compiler_params=pltpu.CompilerParams(vmem_limit_bytes=VMEM_MB << 20))(parts)


def _adamw_body(s):
    def body(w_ref, m_ref, v_ref, g_ref, *rest):
        go_ref, d_ref, mo_ref, vo_ref = rest[-4:]
        g = g_ref[0].astype(F32)
        for k in range(1, s):
            g = g + g_ref[k].astype(F32)
        m_new = B1 * m_ref[...] + (1.0 - B1) * g
        v_new = B2 * v_ref[...] + (1.0 - B2) * jnp.square(g)
        m_hat = m_new / (1.0 - B1 ** STEP)
        v_hat = v_new / (1.0 - B2 ** STEP)
        go_ref[...] = g
        d_ref[...] = -LR * (m_hat / (jnp.sqrt(v_hat) + EPS) + WD * w_ref[...])
        mo_ref[...] = m_new
        vo_ref[...] = v_new

    return body


def _row_tile(r):
    return max([t for t in range(8, 257, 8) if r % t == 0], default=r)


def adamw(name, w, m, v, gs):
    r, c = w.shape
    s = gs.shape[0]
    tr = _row_tile(r)
    blk = pl.BlockSpec((tr, c), lambda i: (i, 0))
    return pl.pallas_call(
        _adamw_body(s), name=name, grid=(r // tr,), out_shape=[jax.ShapeDtypeStruct((r, c), F32)] * 4,
        in_specs=[blk, blk, blk, pl.BlockSpec((s, tr, c), lambda i: (0, i, 0))], out_specs=[blk] * 4,
        compiler_params=_cparams(1))(w, m, v, gs)


def adamw_layer(name, l, w, m, v, gs, outs):
    _, r, c = w.shape
    s = gs.shape[0]
    tr = _row_tile(r)
    blk = pl.BlockSpec((None, tr, c), lambda i: (l, i, 0))
    whole = pl.BlockSpec(memory_space=pl.ANY)
    return pl.pallas_call(
        _adamw_body(s), name=name, grid=(r // tr,), out_shape=[jax.ShapeDtypeStruct(w.shape, F32)] * 4,
        in_specs=[blk, blk, blk, pl.BlockSpec((s, tr, c), lambda i: (0, i, 0))] + [whole] * 4, out_specs=[blk] * 4,
        input_output_aliases={4 + i: i for i in range(4)}, compiler_params=_cparams(1))(w, m, v, gs, *outs)


SMALL = ["c_ctx", "b_mod", "s5_lam_re", "s5_lam_im", "s5_log_dt", "s5_b_re", "s5_b_im", "s5_c_re", "s5_c_im", "s5_d",
         "b_glu", "hg_lb", "hg_norm_w", "ln1_g", "ln1_b", "conv_b", "ln2_g", "ln2_b"]
BIG = ["w_in", "w_glu", "w_out", "w_up", "w_down"]
WEIGHTS = ["c_ctx", "w_mod", "b_mod", "w_in", "s5_lam_re", "s5_lam_im", "s5_log_dt", "s5_b_re", "s5_b_im", "s5_c_re",
           "s5_c_im", "s5_d", "w_glu", "b_glu", "hg_lb", "hg_norm_w", "w_out", "ln1_g", "ln1_b", "w_up", "conv_w",
           "conv_b", "w_down", "ln2_g", "ln2_b"]
PACK_W = 1024


def _pack_rows(k):
    return -(-k // (8 * PACK_W)) * 8


def _pack(arrs):
    parts = []
    for a in arrs:
        flat = a.reshape(-1)
        r = _pack_rows(flat.shape[0])
        parts.append(jnp.pad(flat, (0, r * PACK_W - flat.shape[0])).reshape(r, PACK_W))
    used = sum(q.shape[0] for q in parts)
    parts.append(jnp.zeros((-used % (8 * N_DEV), PACK_W), parts[0].dtype))
    return jnp.concatenate(parts, axis=0)


def _unpack(p, shapes):
    out, o = [], 0
    for s in shapes:
        k = math.prod(s)
        r = _pack_rows(k)
        out.append(p[o:o + r].reshape(-1)[:k].reshape(s))
        o += r
    return out


def _gathered_cols(g):
    return jnp.moveaxis(g, 0, 2).reshape(g.shape[1], g.shape[2], -1)


def _gathered_rows(g):
    return jnp.moveaxis(g, 0, 1).reshape(g.shape[1], -1, g.shape[3])


def _step(p):
    nl = p["w_in"].shape[0]
    me = lax.axis_index("x") * 4 + lax.axis_index("y") * 2 + lax.axis_index("c")
    xc0 = jnp.concatenate([p["ctx"][0], p["x"][0]], axis=0)
    n = xc0.shape[0]
    target = p["loss_target"][0]

    hg3 = jnp.stack([p[k].reshape(-1) for k in ("hg_lb", "m_hg_lb", "v_hg_lb")])
    g_cw, g_c, g_hg = _exchange("gather_inputs", [p["conv_w"], p["c"], hg3], False)
    conv_w = _gathered_cols(g_cw)
    hg_full = jnp.moveaxis(g_hg.reshape(N_DEV, 3, nl, 2, -1), 0, 3).reshape(3, nl, 2 * HGW)
    lb_all = lb_call(hg_full[0])

    craw = jnp.concatenate([g_c.reshape(N_DEV, D), jnp.broadcast_to(p["c_ctx"][None], (8, D))], axis=0)
    cols = p["w_mod"].shape[2]
    b_cols = lax.dynamic_slice_in_dim(p["b_mod"], me * cols, cols, axis=1)[:, None, :]
    (g_mod,) = _exchange("gather_mod", [mod_fwd(craw, p["w_mod"], b_cols)], False)
    mod_all = jnp.moveaxis(g_mod, 0, 2).reshape(nl, 16, 6 * D)
    mod_x = lax.dynamic_index_in_dim(mod_all, me, axis=1, keepdims=False)
    mod2 = jnp.stack([mod_all[:, 8], mod_x], axis=1)
    mvec = lambda l, k: mod2[l, :, k * D:(k + 1) * D][:, None, :]
    gathers = {}

    def start_gather(l, part, ks, after):
        srcs = [p[k][l].astype(BF16) for k in ks]
        gathers[l, part], token = exchange_start(f"gather_start{l}{part}", srcs, [_own_block_set(s, False) for s in srcs],
                                                 False, after=after)
        return token[0, 0]

    parts0 = {"a": BIG[:1], "b": BIG[1:3], "c": BIG[3:]}
    for part, ks in parts0.items():
        start_gather(0, part, ks, g_mod)

    def gathered(l, part, after):
        res = exchange_wait(f"gather_wait{l}{part}", gathers[l, part], after, False)
        ks = parts0[part] if part else BIG
        cols_ = lambda g: jnp.moveaxis(g, 0, 1).reshape(g.shape[1], -1)
        rows_ = lambda g: g.reshape(-1, g.shape[2])
        return {k: (cols_ if k in ("w_in", "w_up") else rows_)(g) for k, g in zip(ks, res)}

    zvec = jnp.zeros((2, 1, D), F32)
    row = lambda a: a.reshape(1, 1, -1)

    to_hp = lambda a: jnp.moveaxis(a, -1, 2).reshape(nl * 2, S5H, S5P)
    prep_in = [p["s5_lam_re"].reshape(nl * 2, 1, S5P), p["s5_lam_im"].reshape(nl * 2, 1, S5P),
               p["s5_log_dt"].reshape(nl * 2, 32, 1), to_hp(p["s5_b_re"]), to_hp(p["s5_b_im"]),
               jnp.swapaxes(p["s5_c_re"], 2, 3).reshape(nl * 2, S5H, S5P),
               jnp.swapaxes(p["s5_c_im"], 2, 3).reshape(nl * 2, S5H, S5P)]
    s5a, s5b, s5c = s5_prep(*prep_in)

    T1 = lambda a, w=D, cb=_c0: (a, w, cb)
    saved = []
    xc = xc0
    (h,) = block_fwd("mod0", f_mod, n, [T1(xc)], [T1(mvec(0, 0)), T1(mvec(0, 1))], [(D, BF16, D, _c0)])
    w_in, w_glu, w_out, w_up, w_down = ([None] * nl for _ in range(5))
    for l in range(nl):
        wl = gathered(l, "", xc) if l else gathered(0, "a", s5a)
        w_in[l] = wl["w_in"]
        proj = dense_nn(f"in_proj{l}", h, w_in[l])
        s5 = [s5_fwd(f"s5_fwd{l}_{d}", proj, s5a, s5b, s5c, 2 * l + d, d == 1) for d in range(2)]
        lbs = [lb_all[l, d * HGW:(d + 1) * HGW][None] for d in range(2)]
        gl = [gla_fwd(f"gla_fwd{l}_{d}", proj, lbs[d], d == 1) for d in range(2)]
        if l == 0:
            wl = gathered(0, "b", gl[1][0])
        w_glu[l], w_out[l] = wl["w_glu"], wl["w_out"]
        started = start_gather(l + 1, "", BIG, w_out[l]) if l + 1 < nl else 0.0
        mix_t = [T1(proj, S5W), T1(proj, HGW, lambda j: 5), T1(s5[0][2], S5W), T1(s5[1][2], S5W),
                 T1(gl[0][0], HGW), T1(gl[1][0], HGW)]
        mix_p = [T1(row(p["s5_d"][l]) + started, S5W), T1(w_glu[l][None], S5W), T1(row(p["b_glu"][l]), S5W),
                 T1(row(p["hg_norm_w"][l]), HD)]
        (y,) = block_fwd(f"mix{l}", f_mix, n, mix_t, mix_p, [(D, BF16, D, _c0)])
        z = dense_nn(f"out_proj{l}", y, w_out[l])
        ln1_p = [T1(mvec(l, 2)), T1(row(p["ln1_g"][l])), T1(row(p["ln1_b"][l])), T1(mvec(l, 3)), T1(mvec(l, 4))]
        x1, h2 = block_fwd(f"ln1_{l}", f_ln, n, [T1(xc), T1(z)], ln1_p, [(D, F32, D, _c0), (D, BF16, D, _c0)])
        if l == 0:
            wl = gathered(0, "c", h2)
        w_up[l], w_down[l] = wl["w_up"], wl["w_down"]
        up = dense_nn(f"up_proj{l}", h2, w_up[l])
        ct = DFF // 2
        act_t = [T1(up, ct, lambda j: j), T1(up, ct, lambda j: j + 2)]
        cb2 = p["conv_b"][l].reshape(1, 1, -1)
        act_p = [T1(conv_w[l][None, :, :DFF], ct, lambda j: j), T1(conv_w[l][None, :, DFF:], ct, lambda j: j),
                 T1(cb2[:, :, :DFF], ct, lambda j: j), T1(cb2[:, :, DFF:], ct, lambda j: j)]
        (act,) = block_fwd(f"act{l}", f_act, n, act_t, act_p, [(DFF, BF16, ct, lambda j: j)], n_col=2)
        dn = dense_nn(f"down_proj{l}", act, w_down[l])
        nxt = (mvec(l + 1, 0), mvec(l + 1, 1)) if l + 1 < nl else (zvec, zvec)
        ln2_p = [T1(mvec(l, 5)), T1(row(p["ln2_g"][l])), T1(row(p["ln2_b"][l])), T1(nxt[0]), T1(nxt[1])]
        x2, hn = block_fwd(f"ln2_{l}", f_ln, n, [T1(x1), T1(dn)], ln2_p, [(D, F32, D, _c0), (D, BF16, D, _c0)])
        saved.append(dict(xc=xc, h=h, proj=proj, s5=s5, gl=gl, lbs=lbs, mix_t=mix_t, mix_p=mix_p, y=y, z=z,
                          ln1_p=ln1_p, x1=x1, h2=h2, act_t=act_t, act_p=act_p, act=act, dn=dn, ln2_p=ln2_p))
        xc, h = x2, hn

    dxc, loss_part = loss_and_grad(xc, target)
    loss = lax.psum(loss_part[0, 0], AXES)

    g = {k: [None] * nl for k in ("w_in", "w_glu", "w_out", "w_up", "w_down", "conv_w", "conv_b", "s5_d", "b_glu",
                                  "hg_norm_w", "ln1_g", "ln1_b", "ln2_g", "ln2_b", "dlb", "s5")}
    dmod = [[None] * 6 for _ in range(nl)]
    scatters = [[] for _ in range(nl)]
    s5_totals = [lax.empty((2 * nl, 2, r, S5P), F32) for r in (1, S5W, S5W)]
    dh_next = jnp.zeros((n, D), F32)
    fgrad = (D, F32, D, _c0)
    for l in reversed(range(nl)):
        sv = saved[l]
        (dx1, d_dn), dp = block_bwd(f"ln2_bwd{l}", f_ln, n, [T1(sv["x1"]), T1(sv["dn"])], sv["ln2_p"],
                                    [T1(dxc), T1(dh_next)], [fgrad, fgrad])
        dmod[l][5], g["ln2_g"][l], g["ln2_b"][l] = dp[0], dp[1], dp[2]
        if l + 1 < nl:
            dmod[l + 1][0], dmod[l + 1][1] = dp[3], dp[4]
        dact = dense_nt(f"down_bwd{l}", d_dn, w_down[l])
        g["w_down"][l] = dense_tn(f"down_wgrad{l}", sv["act"], d_dn)
        ct = DFF // 2
        cj = lambda j: j
        (dua, dug), dp = block_bwd(f"act_bwd{l}", f_act, n, sv["act_t"], sv["act_p"], [T1(dact, ct, cj)],
                                   [(DFF, BF16, ct, cj), (DFF, BF16, ct, cj)], n_col=2)
        g["conv_w"][l] = jnp.concatenate([dp[0][0], dp[1][0]], axis=-1)
        g["conv_b"][l] = jnp.concatenate([dp[2][0, 0], dp[3][0, 0]], axis=-1)
        dh2 = dense_nt2(f"up_bwd{l}", dua, dug, w_up[l])
        g["w_up"][l] = [dense_tn(f"up_wgrad{l}{part}", sv["h2"], du_) for part, du_ in (("a", dua), ("g", dug))]
        (dxc, dz), dp = block_bwd(f"ln1_bwd{l}", f_ln, n, [T1(sv["xc"]), T1(sv["z"])], sv["ln1_p"],
                                  [T1(dx1), T1(dh2)], [fgrad, fgrad])
        dmod[l][2], g["ln1_g"][l], g["ln1_b"][l], dmod[l][3], dmod[l][4] = dp
        dy = dense_nt(f"out_bwd{l}", dz, w_out[l])
        g["w_out"][l] = dense_tn(f"out_wgrad{l}", sv["y"], dz)
        half = (S5W, F32, S5W, _c0)
        (dpu, dpg, dys, dos), dp = block_bwd(f"mix_bwd{l}", f_mix, n, sv["mix_t"], sv["mix_p"], [T1(dy)],
                                                   [half, half, half, None, half, None])
        g["s5_d"][l], g["w_glu"][l], g["b_glu"][l], g["hg_norm_w"][l] = dp[0][0, 0], dp[1][0], dp[2][0, 0], dp[3][0, 0]

        def start_scatter(tag, ks):
            by_cols = lambda a, nb=N_DEV: jnp.moveaxis(a.reshape(a.shape[0], nb, -1), 1, 0)
            by_rows = lambda a: a.reshape(N_DEV, -1, a.shape[1])

            def blocks(k):
                if k == "w_up":
                    return jnp.concatenate([by_cols(half, N_DEV // 2) for half in g[k][l]], axis=0)
                return (by_rows if k in ("w_glu", "w_out", "w_down") else by_cols)(g[k][l].astype(BF16 if k != "conv_w" else F32))

            sends = [blocks(k) for k in ks]
            handle, token = exchange_start(f"scatter_start{l}{tag}", sends, [_own_block_set(s, True) for s in sends], True)
            scatters[l].append((ks, handle))
            return token[0, 0]

        lbs_b = sv["lbs"]
        if l == 0:
            started = start_scatter("a", ["w_glu", "w_out", "w_up", "w_down", "conv_w"])
            lbs_b = [b + started for b in lbs_b]
        gb = [gla_bwd(f"gla_bwd{l}_{d}", sv["proj"], lbs_b[d], sv["gl"][d][1], dos, d == 1) for d in range(2)]
        g["dlb"][l] = jnp.concatenate([gb[0][3], gb[1][3]], axis=-1)[0]
        sb = [None, None]
        for d in range(2):
            sb[d], *s5_totals = s5_bwd(f"s5_bwd{l}_{d}", sv["proj"], dys, sv["s5"][d][0], sv["s5"][d][1], s5a, s5b, s5c,
                                       2 * l + d, d == 1, s5_totals)
        asm_t = [T1(dpu, S5W), T1(sb[0], S5W), T1(sb[1], S5W), T1(gb[0][0], HGW), T1(gb[1][0], HGW),
                 T1(gb[0][1], HGW), T1(gb[1][1], HGW), T1(gb[0][2], HGW), T1(gb[1][2], HGW), T1(dpg, HGW)]
        (dproj,) = block_fwd(
            f"dproj{l}", lambda tv, pv, i: (jnp.concatenate(
                [tv[0] + tv[1] + tv[2], tv[3], tv[4], tv[5] + tv[6], tv[7] + tv[8], tv[9]], axis=-1),),
            n, asm_t, [], [(INC, BF16, INC, _c0)])
        dh_next = dense_nt(f"in_bwd{l}", dproj, w_in[l])
        g["w_in"][l] = dense_tn(f"in_wgrad{l}", sv["h"], dproj)
        started = start_scatter("b", ["w_in"]) if l == 0 else start_scatter("", BIG + ["conv_w"])
        if l:
            gate, wd_, cb_ = saved[l - 1]["ln2_p"][0]
            saved[l - 1]["ln2_p"][0] = (gate + started, wd_, cb_)
    (dxc,), dp = block_bwd("mod0_bwd", f_mod_id, n, [T1(xc0)], [T1(mvec(0, 0)), T1(mvec(0, 1))],
                           [T1(dh_next), T1(dxc)], [fgrad])
    dmod[0][0], dmod[0][1] = dp
    grad_x = dxc[n - p["x"].shape[1]:][None]

    d_prep = s5_prep_bwd(*prep_in, *s5_totals)
    from_hp = lambda a: jnp.moveaxis(a.reshape(nl, 2, S5H, S5W // S5H, 64), 2, -1)
    gs5 = {"s5_lam_re": d_prep[0].reshape(nl, 2, 32, 64), "s5_lam_im": d_prep[1].reshape(nl, 2, 32, 64),
           "s5_log_dt": d_prep[2].reshape(nl, 2, 32), "s5_b_re": from_hp(d_prep[3]), "s5_b_im": from_hp(d_prep[4]),
           "s5_c_re": jnp.swapaxes(d_prep[5].reshape(nl, 2, S5H, 32, 64), 2, 3),
           "s5_c_im": jnp.swapaxes(d_prep[6].reshape(nl, 2, S5H, 32, 64), 2, 3)}
    d_hg = lb_call(hg_full[0], jnp.stack(g["dlb"]))

    dmod_loc = jnp.stack([jnp.concatenate([dmod[l][k][:, 0] for k in range(6)], axis=-1) for l in range(nl)])
    (g_dmod,) = _exchange("gather_dmod", [dmod_loc], False)
    gcols = lax.dynamic_slice_in_dim(g_dmod, me * cols, cols, axis=3)
    g16 = jnp.concatenate([jnp.moveaxis(gcols[:, :, 1], 0, 1), jnp.moveaxis(gcols[:, :, 0], 0, 1)], axis=1)
    grad_w_mod, dcraw = mod_bwd(craw, p["w_mod"], g16)
    d_c_ctx = jnp.sum(dcraw[8:], axis=0)

    stk = lambda k: jnp.stack(g[k])
    small_g = {"c_ctx": d_c_ctx, "b_mod": dmod_loc[:, 0] + dmod_loc[:, 1], "s5_d": stk("s5_d"), "b_glu": stk("b_glu"),
               "hg_lb": d_hg.reshape(nl, 2, HGW), "hg_norm_w": stk("hg_norm_w"), "ln1_g": stk("ln1_g")[:, 0, 0],
               "ln1_b": stk("ln1_b")[:, 0, 0], "conv_b": stk("conv_b"), "ln2_g": stk("ln2_g")[:, 0, 0],
               "ln2_b": stk("ln2_b")[:, 0, 0], **gs5}
    g_pack = _pack([small_g[k] for k in SMALL]).reshape(N_DEV, -1, PACK_W)
    small_scatter, _ = exchange_start("scatter_small_start", [g_pack], [_own_block_set(g_pack, True)], True)

    out = {}
    kinds = ("grad_", "delta_", "new_m_", "new_v_")
    results = {}

    def update_layer(k, l, gs):
        prev = results.get(k) or [lax.empty(p[k].shape, F32) for _ in kinds]
        results[k] = adamw_layer(f"adamw_{k}{l}", l, p[k], p["m_" + k], p["v_" + k], gs, prev)

    for l in range(nl):
        update_layer("w_mod", l, grad_w_mod[l][None])
    for l in reversed(range(nl)):
        for i, (ks, handle) in enumerate(scatters[l]):
            recv = exchange_wait(f"scatter_wait{l}_{i}", handle, results["w_mod"][0], True)
            for k, gsum in zip(ks, recv):
                update_layer(k, l, gsum)
    for k, res in results.items():
        for kind, a in zip(kinds, res):
            out[kind + k] = a

    (g_parts,) = exchange_wait("scatter_small_wait", small_scatter, results["w_in"][0], True)
    (g_small,) = _exchange("gather_small_grads", [sum_parts(g_parts)], False)
    g_small = g_small.reshape(1, -1, PACK_W)
    hgw = {"": hg_full[0].reshape(nl, 2, HGW), "m_": hg_full[1].reshape(nl, 2, HGW), "v_": hg_full[2].reshape(nl, 2, HGW)}
    full = lambda pre, k: hgw[pre] if k == "hg_lb" else p[pre + k]
    shapes = [full("", k).shape for k in SMALL]
    res = adamw("adamw_small", *[_pack([full(pre, k) for k in SMALL]) for pre in ("", "m_", "v_")], g_small)
    for kind, packed in zip(kinds, res):
        for k, a in zip(SMALL, _unpack(packed, shapes)):
            if k == "hg_lb":
                a = lax.dynamic_slice_in_dim(a, me * (HGW // N_DEV), HGW // N_DEV, axis=2)
            out[kind + k] = a
    return (loss, grad_x, *[out[kind + k] for kind in ("grad_", "delta_", "new_m_", "new_v_") for k in WEIGHTS])


def kernel(x, c, ctx, c_ctx, w_mod, b_mod, w_in, s5_lam_re, s5_lam_im, s5_log_dt, s5_b_re, s5_b_im, s5_c_re, s5_c_im, s5_d, w_glu, b_glu, hg_lb, hg_norm_w, w_out, ln1_g, ln1_b, w_up, conv_w, conv_b, w_down, ln2_g, ln2_b, loss_target, m_c_ctx, m_w_mod, m_b_mod, m_w_in, m_s5_lam_re, m_s5_lam_im, m_s5_log_dt, m_s5_b_re, m_s5_b_im, m_s5_c_re, m_s5_c_im, m_s5_d, m_w_glu, m_b_glu, m_hg_lb, m_hg_norm_w, m_w_out, m_ln1_g, m_ln1_b, m_w_up, m_conv_w, m_conv_b, m_w_down, m_ln2_g, m_ln2_b, v_c_ctx, v_w_mod, v_b_mod, v_w_in, v_s5_lam_re, v_s5_lam_im, v_s5_log_dt, v_s5_b_re, v_s5_b_im, v_s5_c_re, v_s5_c_im, v_s5_d, v_w_glu, v_b_glu, v_hg_lb, v_hg_norm_w, v_w_out, v_ln1_g, v_ln1_b, v_w_up, v_conv_w, v_conv_b, v_w_down, v_ln2_g, v_ln2_b):
    return _step(dict(locals()))
```

```python
import functools
import math

import jax
import jax.numpy as jnp
from jax import lax
from jax.experimental import pallas as pl
from jax.experimental.pallas import tpu as pltpu

F32, BF16 = jnp.float32, jnp.bfloat16
N_DEV = 8
AXES = ("x", "y", "c")
D = 1024
S5W = 512
S5P = 2048
S5H = 16
HGW = 512
HD = 128
NH = 4
CK = 32
DFF = 2816
GRID_W = 64
INC = 3072
ALPHA = 8.0 ** 0.25
LN_EPS = 1e-5
RMS_EPS = 1e-6
LR, B1, B2, EPS, WD, STEP = 0.001, 0.9, 0.999, 1e-08, 0.01, 10
TT = 256
VMEM_MB = 56

NN = ((1,), (0,))
NT = ((1,), (1,))
TN = ((0,), (0,))


def _cparams(n_axes):
    return pltpu.CompilerParams(dimension_semantics=("arbitrary",) * n_axes, vmem_limit_bytes=VMEM_MB << 20)


def _dot(a, b, dims):
    return lax.dot_general(a.astype(BF16), b.astype(BF16), (dims, ((), ())), preferred_element_type=F32)


@jax.custom_vjp
def mm_nn(a, b):
    return _dot(a, b, NN)


@jax.custom_vjp
def mm_nt(a, b):
    return _dot(a, b, NT)


@jax.custom_vjp
def mm_tn(a, b):
    return _dot(a, b, TN)


mm_nn.defvjp(lambda a, b: (_dot(a, b, NN), (a, b)), lambda r, g: (_dot(g, r[1], NT), _dot(r[0], g, TN)))
mm_nt.defvjp(lambda a, b: (_dot(a, b, NT), (a, b)), lambda r, g: (_dot(g, r[1], NN), _dot(g, r[0], TN)))
mm_tn.defvjp(lambda a, b: (_dot(a, b, TN), (a, b)), lambda r, g: (_dot(r[1], g, NT), _dot(r[0], g, NN)))


def _roll_rows(u, s):
    return pltpu.roll(u, s % u.shape[0], 0)


@functools.partial(jax.custom_vjp, nondiff_argnums=(2,))
def shift_rows(u, m, s):
    return _roll_rows(u, s) * m


def _shift_fwd(u, m, s):
    return _roll_rows(u, s) * m, m


def _shift_bwd(s, m, g):
    return _roll_rows(g * m, -s), jnp.zeros_like(m)


shift_rows.defvjp(_shift_fwd, _shift_bwd)


def _scan_tile(pos, nt, rev):
    return jnp.where(pos == 0, 0, nt - pos) if rev else pos


def _exchange(name, arrays, all_to_all):
    k_arr = len(arrays)

    def body(*refs):
        ins, outs = refs[:k_arr], refs[k_arr:2 * k_arr]
        send_sems, recv_sems, local_sems = refs[2 * k_arr:]
        me = lax.axis_index("x") * 4 + lax.axis_index("y") * 2 + lax.axis_index("c")
        local = []
        for k in range(k_arr):
            cp = pltpu.make_async_copy(ins[k].at[me] if all_to_all else ins[k], outs[k].at[me], local_sems.at[k])
            cp.start()
            local.append(cp)
        sends = []
        for d in range(1, N_DEV):
            p = (me + d) % N_DEV
            for k in range(k_arr):
                cp = pltpu.make_async_remote_copy(
                    src_ref=ins[k].at[p] if all_to_all else ins[k], dst_ref=outs[k].at[me],
                    send_sem=send_sems.at[k, d - 1], recv_sem=recv_sems.at[k, d - 1],
                    device_id=(p // 4, (p // 2) % 2, p % 2), device_id_type=pl.DeviceIdType.MESH)
                cp.start()
                sends.append(cp)
        for d in range(1, N_DEV):
            q = (me + N_DEV - d) % N_DEV
            for k in range(k_arr):
                pltpu.make_async_remote_copy(
                    src_ref=ins[k].at[q] if all_to_all else ins[k], dst_ref=outs[k].at[q],
                    send_sem=send_sems.at[k, d - 1], recv_sem=recv_sems.at[k, d - 1],
                    device_id=(q // 4, (q // 2) % 2, q % 2), device_id_type=pl.DeviceIdType.MESH).wait_recv()
        for cp in sends:
            cp.wait_send()
        for cp in local:
            cp.wait()

    shapes = [a.shape if all_to_all else (N_DEV,) + a.shape for a in arrays]
    return pl.pallas_call(
        body, name=name,
        out_shape=[jax.ShapeDtypeStruct(s, a.dtype) for s, a in zip(shapes, arrays)],
        in_specs=[pl.BlockSpec(memory_space=pl.ANY)] * k_arr,
        out_specs=[pl.BlockSpec(memory_space=pl.ANY)] * k_arr,
        scratch_shapes=[pltpu.SemaphoreType.DMA((k_arr, N_DEV - 1)), pltpu.SemaphoreType.DMA((k_arr, N_DEV - 1)),
                        pltpu.SemaphoreType.DMA((k_arr,))],
    )(*arrays)


_HBM = pl.BlockSpec(memory_space=pltpu.HBM)
_SEM = pl.BlockSpec(memory_space=pltpu.SEMAPHORE)
_EFFECT = pltpu.SideEffectType.DATAFLOW_SIDE_EFFECTING


def _peer(i):
    return (i // 4, (i // 2) % 2, i % 2)


def exchange_start(name, srcs, lands, all_to_all, after=None):
    k_arr = len(srcs)
    n_sem = k_arr * (N_DEV - 1)
    extra = [] if after is None else [after]

    def body(*refs):
        ins, lz = refs[:k_arr], refs[k_arr:2 * k_arr]
        first = 2 * k_arr + len(extra)
        send_sems = refs[first:first + n_sem]
        recv_sems = refs[first + n_sem:first + 2 * n_sem]
        me = lax.axis_index("x") * 4 + lax.axis_index("y") * 2 + lax.axis_index("c")
        for d in range(1, N_DEV):
            p = (me + d) % N_DEV
            for k in range(k_arr):
                s = k * (N_DEV - 1) + d - 1
                pltpu.make_async_remote_copy(
                    src_ref=ins[k].at[p] if all_to_all else ins[k], dst_ref=lz[k].at[me],
                    send_sem=send_sems[s], recv_sem=recv_sems[s],
                    device_id=_peer(p), device_id_type=pl.DeviceIdType.MESH).start()
        refs[-1][...] = jnp.zeros_like(refs[-1])

    arrs = list(srcs) + list(lands)
    res = pl.pallas_call(
        body, name=name,
        out_shape=(*[pltpu.SemaphoreType.DMA(())] * (2 * n_sem), *[pltpu.HBM(a.shape, a.dtype) for a in arrs],
                   jax.ShapeDtypeStruct((8, 128), F32)),
        in_specs=[_HBM] * len(arrs) + [pl.BlockSpec(memory_space=pl.ANY)] * len(extra),
        out_specs=(*[_SEM] * (2 * n_sem), *[_HBM] * len(arrs), pl.BlockSpec(memory_space=pltpu.VMEM)),
        input_output_aliases={i: 2 * n_sem + i for i in range(len(arrs))},
        compiler_params=pltpu.CompilerParams(has_side_effects=_EFFECT),
    )(*[pltpu.with_memory_space_constraint(a, pltpu.HBM) for a in arrs], *extra)
    return res[:-1], res[-1]


def exchange_wait(name, handle, after, all_to_all):
    k_arr = len(handle) // (2 * N_DEV)
    n_sem = k_arr * (N_DEV - 1)
    sems, arrs = handle[:2 * n_sem], handle[2 * n_sem:]

    def body(*refs):
        ins, lz = refs[:k_arr], refs[k_arr:2 * k_arr]
        s_sems = refs[2 * k_arr:2 * k_arr + n_sem]
        r_sems = refs[2 * k_arr + n_sem:2 * k_arr + 2 * n_sem]
        me = lax.axis_index("x") * 4 + lax.axis_index("y") * 2 + lax.axis_index("c")
        for d in range(1, N_DEV):
            q = (me + N_DEV - d) % N_DEV
            for k in range(k_arr):
                s = k * (N_DEV - 1) + d - 1
                cp = pltpu.make_async_remote_copy(
                    src_ref=ins[k].at[q] if all_to_all else ins[k], dst_ref=lz[k].at[q],
                    send_sem=s_sems[s], recv_sem=r_sems[s],
                    device_id=_peer(q), device_id_type=pl.DeviceIdType.MESH)
                cp.wait_send()
                cp.wait_recv()

    res = pl.pallas_call(
        body, name=name, out_shape=tuple(pltpu.HBM(a.shape, a.dtype) for a in arrs),
        in_specs=[_HBM] * len(arrs) + [_SEM] * (2 * n_sem) + [pl.BlockSpec(memory_space=pl.ANY)],
        out_specs=tuple([_HBM] * len(arrs)),
        input_output_aliases={i: i for i in range(len(arrs))},
        compiler_params=pltpu.CompilerParams(has_side_effects=_EFFECT),
    )(*arrs, *sems, after)
    return res[k_arr:]


def _own_block_set(src, all_to_all):
    me = lax.axis_index("x") * 4 + lax.axis_index("y") * 2 + lax.axis_index("c")
    own = lax.dynamic_index_in_dim(src, me, 0, keepdims=False) if all_to_all else src
    shape = src.shape if all_to_all else (N_DEV,) + src.shape
    return lax.dynamic_update_index_in_dim(lax.empty(shape, src.dtype), own, me, 0)


def _tile(n, prefs):
    for t in prefs:
        if n % t == 0:
            return t
    raise ValueError(n)


def dense_nn(name, a, w, out_dtype=F32):
    n, k = a.shape
    m = w.shape[1]
    tn, tm = _tile(n, (1088, 256)), _tile(m, (1536, 1024, 1408, 512, 256, 128))

    def body(a_ref, w_ref, o_ref):
        o_ref[...] = _dot(a_ref[...], w_ref[...], NN).astype(o_ref.dtype)

    return pl.pallas_call(
        body, name=name, grid=(m // tm, n // tn), out_shape=jax.ShapeDtypeStruct((n, m), out_dtype),
        in_specs=[pl.BlockSpec((tn, k), lambda j, i: (i, 0)), pl.BlockSpec((k, tm), lambda j, i: (0, j))],
        out_specs=pl.BlockSpec((tn, tm), lambda j, i: (i, j)), compiler_params=_cparams(2))(a, w)


def dense_nt(name, g, w, out_dtype=F32):
    n, m = g.shape
    k = w.shape[0]
    tn, tk = _tile(n, (1088, 256)), _tile(k, (1024, 1408, 512, 256, 128))

    def body(g_ref, w_ref, o_ref):
        o_ref[...] = _dot(g_ref[...], w_ref[...], NT).astype(o_ref.dtype)

    return pl.pallas_call(
        body, name=name, grid=(k // tk, n // tn), out_shape=jax.ShapeDtypeStruct((n, k), out_dtype),
        in_specs=[pl.BlockSpec((tn, m), lambda j, i: (i, 0)), pl.BlockSpec((tk, m), lambda j, i: (j, 0))],
        out_specs=pl.BlockSpec((tn, tk), lambda j, i: (i, j)), compiler_params=_cparams(2))(g, w)


def dense_nt2(name, g1, g2, w, out_dtype=F32):
    n, m = g1.shape
    k = w.shape[0]
    tn, tk = _tile(n, (544, 256)), _tile(k, (1024, 1408, 512, 256, 128))

    def body(g1_ref, g2_ref, w1_ref, w2_ref, o_ref):
        o_ref[...] = (_dot(g1_ref[...], w1_ref[...], NT) + _dot(g2_ref[...], w2_ref[...], NT)).astype(o_ref.dtype)

    half = lambda h: pl.BlockSpec((tk, m), lambda j, i: (j, h))
    rows = pl.BlockSpec((tn, m), lambda j, i: (i, 0))
    return pl.pallas_call(
        body, name=name, grid=(k // tk, n // tn), out_shape=jax.ShapeDtypeStruct((n, k), out_dtype),
        in_specs=[rows, rows, half(0), half(1)], out_specs=pl.BlockSpec((tn, tk), lambda j, i: (i, j)),
        compiler_params=_cparams(2))(g1, g2, w, w)


def dense_tn(name, a, g, out_dtype=BF16):
    n, k = a.shape
    m = g.shape[1]
    tn = _tile(n, (1088, 256))
    tk = _tile(k, (1024, 1408, 512, 256, 128))
    tm = _tile(m, (1024, 1408, 512, 256, 128))
    nt = n // tn

    def body(a_ref, g_ref, o_ref, acc_ref):
        t = pl.program_id(2)

        @pl.when(t == 0)
        def _():
            acc_ref[...] = jnp.zeros_like(acc_ref)

        acc_ref[...] += _dot(a_ref[...], g_ref[...], TN)

        @pl.when(t == nt - 1)
        def _():
            o_ref[...] = acc_ref[...].astype(o_ref.dtype)

    return pl.pallas_call(
        body, name=name, grid=(k // tk, m // tm, nt), out_shape=jax.ShapeDtypeStruct((k, m), out_dtype),
        in_specs=[pl.BlockSpec((tn, tk), lambda i, j, t: (t, i)), pl.BlockSpec((tn, tm), lambda i, j, t: (t, j))],
        out_specs=pl.BlockSpec((tk, tm), lambda i, j, t: (i, j)),
        scratch_shapes=[pltpu.VMEM((tk, tm), F32)], compiler_params=_cparams(3))(a, g)


def _c0(j):
    return 0


def _tspec(w, cb):
    return pl.BlockSpec((TT, w), lambda j, i: (i, cb(j)))


def _pspec(arr, w, cb):
    two = arr.shape[0] == 2
    return pl.BlockSpec((None, arr.shape[1], w), lambda j, i: (jnp.minimum(i, 1) if two else 0, 0, cb(j)))


def block_fwd(name, fn, n, tiled, params, outs, n_col=1):
    nt_, np_ = len(tiled), len(params)

    def body(*refs):
        i = pl.program_id(1)
        tv = [r[...].astype(F32) for r in refs[:nt_]]
        pv = [r[...].astype(F32) for r in refs[nt_:nt_ + np_]]
        for o_ref, r in zip(refs[nt_ + np_:], fn(tv, pv, i)):
            o_ref[...] = r.astype(o_ref.dtype)

    return pl.pallas_call(
        body, name=name, grid=(n_col, n // TT),
        out_shape=[jax.ShapeDtypeStruct((n, c), dt) for c, dt, _, _ in outs],
        in_specs=[_tspec(w, cb) for _, w, cb in tiled] + [_pspec(a, w, cb) for a, w, cb in params],
        out_specs=[_tspec(w, cb) for _, _, w, cb in outs], compiler_params=_cparams(2),
    )(*[a for a, _, _ in tiled], *[a for a, _, _ in params])


def block_bwd(name, fn, n, tiled, params, cots, grads, n_col=1):
    nt_, np_, nc_ = len(tiled), len(params), len(cots)
    want = [k for k, g in enumerate(grads) if g is not None]

    def body(*refs):
        i = pl.program_id(1)
        tv = [r[...].astype(F32) for r in refs[:nt_]]
        pv = [r[...].astype(F32) for r in refs[nt_:nt_ + np_]]
        cv = [r[...].astype(F32) for r in refs[nt_ + np_:nt_ + np_ + nc_]]
        o_refs = refs[nt_ + np_ + nc_:]
        _, vjp = jax.vjp(lambda t, p: list(fn(t, p, i)), tv, pv)
        dt, dp = vjp(cv)
        for o_ref, k in zip(o_refs, want):
            o_ref[...] = dt[k].astype(o_ref.dtype)
        for o_ref, g, (arr, _, _) in zip(o_refs[len(want):], dp, params):
            first = (i == 0) | (i == 1) if arr.shape[0] == 2 else i == 0

            @pl.when(first)
            def _(o_ref=o_ref):
                o_ref[...] = jnp.zeros_like(o_ref)

            o_ref[...] += g

    res = pl.pallas_call(
        body, name=name, grid=(n_col, n // TT),
        out_shape=[jax.ShapeDtypeStruct((n, grads[k][0]), grads[k][1]) for k in want]
        + [jax.ShapeDtypeStruct(a.shape, F32) for a, _, _ in params],
        in_specs=[_tspec(w, cb) for _, w, cb in tiled] + [_pspec(a, w, cb) for a, w, cb in params]
        + [_tspec(w, cb) for _, w, cb in cots],
        out_specs=[_tspec(grads[k][2], grads[k][3]) for k in want] + [_pspec(a, w, cb) for a, w, cb in params],
        compiler_params=_cparams(2),
    )(*[a for a, _, _ in tiled], *[a for a, _, _ in params], *[a for a, _, _ in cots])
    return res[:len(want)], res[len(want):]


def f_mod(tv, pv, i):
    (x,), (sh, sc) = tv, pv
    return (x * (1.0 + sc) + sh,)


def f_mod_id(tv, pv, i):
    return (f_mod(tv, pv, i)[0], tv[0])


def f_ln(tv, pv, i):
    (x, z), (gate, g, b, sh, sc) = tv, pv
    pre = ALPHA * x + gate * z
    mu = jnp.mean(pre, axis=-1, keepdims=True)
    var = jnp.mean(jnp.square(pre - mu), axis=-1, keepdims=True)
    xn = (pre - mu) * lax.rsqrt(var + LN_EPS) * g + b
    return xn, xn * (1.0 + sc) + sh


def f_mix(tv, pv, i):
    (pu, pg, y0, y1, o0, o1), (d_skip, w_glu, b_glu, norm_w) = tv, pv
    s5y = jax.nn.gelu(y0 + y1 + pu * d_skip)
    s5o = s5y * jax.nn.sigmoid(mm_nn(s5y, w_glu) + b_glu)
    o = o0 + o1
    heads = []
    for h in range(NH):
        oh = o[:, h * HD:(h + 1) * HD]
        heads.append(oh * lax.rsqrt(jnp.mean(jnp.square(oh), axis=-1, keepdims=True) + RMS_EPS) * norm_w)
    hg = jnp.concatenate(heads, axis=-1) * jax.nn.silu(pg)
    return (jnp.concatenate([s5o, hg], axis=-1),)


def f_act(tv, pv, i):
    (ua, ug), (cwa, cwg, cba, cbg) = tv, pv
    t = lax.broadcasted_iota(jnp.int32, (TT, 1), 0)
    lat = i > 0
    m_dn = jnp.where((t == 0) | (lat & (t % GRID_W == 0)), 0.0, 1.0)
    m_up = jnp.where((t == TT - 1) | (lat & (t % GRID_W == GRID_W - 1)), 0.0, 1.0)

    def conv(u, w, b):
        return shift_rows(u, m_dn, 1) * w[0:1] + u * w[1:2] + shift_rows(u, m_up, -1) * w[2:3] + b

    return (jax.nn.silu(conv(ua, cwa, cba)) * conv(ug, cwg, cbg),)


def loss_and_grad(xf, target):
    n = xf.shape[0]

    def body(x_ref, t_ref, dy_ref, l_ref):
        i = pl.program_id(0)

        @pl.when(i == 0)
        def _():
            l_ref[...] = jnp.zeros_like(l_ref)
            dy_ref[...] = jnp.zeros_like(dy_ref)

        @pl.when(i > 0)
        def _():
            e = x_ref[...] - t_ref[...]
            dy_ref[...] = e * (1.0 / D)
            l_ref[...] += 0.5 / D * jnp.sum(jnp.square(e))

    return pl.pallas_call(
        body, name="loss", grid=(n // TT,),
        out_shape=[jax.ShapeDtypeStruct((n, D), F32), jax.ShapeDtypeStruct((8, 128), F32)],
        in_specs=[pl.BlockSpec((TT, D), lambda i: (i, 0)), pl.BlockSpec((TT, D), lambda i: (jnp.maximum(i - 1, 0), 0))],
        out_specs=[pl.BlockSpec((TT, D), lambda i: (i, 0)), pl.BlockSpec((8, 128), lambda i: (0, 0))],
        compiler_params=_cparams(1))(xf, target)


def f_prep(lr, li, ldt, bre, bim, cre, cim):
    gi = lax.broadcasted_iota(jnp.int32, (S5W // S5H, S5P), 0)
    gc = lax.broadcasted_iota(jnp.int32, (S5W // S5H, S5P), 1) // 64
    dt = jnp.exp(jnp.sum(jnp.where(gi == gc, ldt, 0.0), axis=0, keepdims=True))
    mag, ang = jnp.exp(lr * dt), li * dt
    ar, ai = mag * jnp.cos(ang), mag * jnp.sin(ang)
    den = lr * lr + li * li
    nr, ni = ar - 1.0, ai
    cr = (nr * lr + ni * li) / den
    ci = (ni * lr - nr * li) / den
    bbr = cr * bre - ci * bim
    bbi = cr * bim + ci * bre
    rg = lax.broadcasted_iota(jnp.int32, (S5W, S5P), 0) // S5H
    cg = lax.broadcasted_iota(jnp.int32, (S5W, S5P), 1) // 64
    mask = (rg == cg).astype(F32)
    blk = lambda a: jnp.concatenate([a] * (S5W // S5H), axis=0) * mask
    return ar, ai, blk(bbr), blk(bbi), blk(cre), blk(-cim)


def s5_prep(lr, li, ldt, bre, bim, cre, cim):
    n2 = lr.shape[0]

    def body(lr_r, li_r, ldt_r, bre_r, bim_r, cre_r, cim_r, a_ref, b_ref, c_ref):
        ar, ai, bbr, bbi, cbr, cbi = f_prep(lr_r[...], li_r[...], ldt_r[...], bre_r[...], bim_r[...], cre_r[...], cim_r[...])
        a_ref[0], a_ref[1] = ar, ai
        b_ref[0], b_ref[1] = bbr.astype(BF16), bbi.astype(BF16)
        c_ref[0], c_ref[1] = cbr.astype(BF16), cbi.astype(BF16)

    sp = lambda r, c: pl.BlockSpec((None, r, c), lambda i: (i, 0, 0))
    sp4 = lambda r, c: pl.BlockSpec((None, 2, r, c), lambda i: (i, 0, 0, 0))
    return pl.pallas_call(
        body, name="s5_prep", grid=(n2,),
        out_shape=[jax.ShapeDtypeStruct((n2, 2, 1, S5P), F32), jax.ShapeDtypeStruct((n2, 2, S5W, S5P), BF16),
                   jax.ShapeDtypeStruct((n2, 2, S5W, S5P), BF16)],
        in_specs=[sp(1, S5P), sp(1, S5P), sp(32, 1), sp(S5H, S5P), sp(S5H, S5P), sp(S5H, S5P), sp(S5H, S5P)],
        out_specs=[sp4(1, S5P), sp4(S5W, S5P), sp4(S5W, S5P)], compiler_params=_cparams(1),
    )(lr, li, ldt, bre, bim, cre, cim)


def s5_prep_bwd(lr, li, ldt, bre, bim, cre, cim, da, db, dc):
    n2 = lr.shape[0]

    def body(lr_r, li_r, ldt_r, bre_r, bim_r, cre_r, cim_r, da_r, db_r, dc_r, *outs):
        args = [r[...] for r in (lr_r, li_r, ldt_r, bre_r, bim_r, cre_r, cim_r)]
        _, vjp = jax.vjp(f_prep, *args)
        for o_ref, g in zip(outs, vjp((da_r[0], da_r[1], db_r[0], db_r[1], dc_r[0], dc_r[1]))):
            o_ref[...] = g

    sp = lambda r, c: pl.BlockSpec((None, r, c), lambda i: (i, 0, 0))
    sp4 = lambda r, c: pl.BlockSpec((None, 2, r, c), lambda i: (i, 0, 0, 0))
    ins = [sp(1, S5P), sp(1, S5P), sp(32, 1), sp(S5H, S5P), sp(S5H, S5P), sp(S5H, S5P), sp(S5H, S5P)]
    return pl.pallas_call(
        body, name="s5_prep_bwd", grid=(n2,),
        out_shape=[jax.ShapeDtypeStruct(a.shape, F32) for a in (lr, li, ldt, bre, bim, cre, cim)],
        in_specs=ins + [sp4(1, S5P), sp4(S5W, S5P), sp4(S5W, S5P)], out_specs=ins, compiler_params=_cparams(1),
    )(lr, li, ldt, bre, bim, cre, cim, da, db, dc)


S5_DIAG = 2
_CU, _CP = S5W // S5_DIAG, S5P // S5_DIAG


def _bd_nn(u, w_ref, k):
    return jnp.concatenate([_dot(u[:, j * _CU:(j + 1) * _CU], w_ref[k, j * _CU:(j + 1) * _CU, j * _CP:(j + 1) * _CP], NN)
                            for j in range(S5_DIAG)], axis=1)


def _bd_nt(x, w_ref, k):
    return jnp.concatenate([_dot(x[:, j * _CP:(j + 1) * _CP], w_ref[k, j * _CU:(j + 1) * _CU, j * _CP:(j + 1) * _CP], NT)
                            for j in range(S5_DIAG)], axis=1)


def _bd_tn_acc(acc_ref, k, a, g):
    for j in range(S5_DIAG):
        acc_ref[k, j * _CU:(j + 1) * _CU, j * _CP:(j + 1) * _CP] += _dot(a[:, j * _CU:(j + 1) * _CU],
                                                                         g[:, j * _CP:(j + 1) * _CP], TN)


def _scan_rows(xr_ref, xi_ref, ar, ai, desc, cr_ref, ci_ref):
    unroll = 8

    def group(gi, carry):
        cr, ci = carry
        base = gi * unroll
        for j in range(unroll):
            t = TT - 1 - (base + j) if desc else base + j
            nr = ar * cr - ai * ci + xr_ref[pl.ds(t, 1), :]
            ni = ar * ci + ai * cr + xi_ref[pl.ds(t, 1), :]
            xr_ref[pl.ds(t, 1), :] = nr
            xi_ref[pl.ds(t, 1), :] = ni
            cr, ci = nr, ni
        return cr, ci

    cr, ci = lax.fori_loop(0, TT // unroll, group, (cr_ref[...], ci_ref[...]))
    cr_ref[...] = cr
    ci_ref[...] = ci


def s5_fwd(name, proj, a, bb, cb, ld, rev):
    n = proj.shape[0]
    nt = n // TT

    def body(u_ref, a_ref, b_ref, c_ref, xr_ref, xi_ref, y_ref, cr_ref, ci_ref):
        @pl.when(pl.program_id(0) == 0)
        def _():
            cr_ref[...] = jnp.zeros_like(cr_ref)
            ci_ref[...] = jnp.zeros_like(ci_ref)

        u = u_ref[...]
        xr_ref[...] = _bd_nn(u, b_ref, 0)
        xi_ref[...] = _bd_nn(u, b_ref, 1)
        _scan_rows(xr_ref, xi_ref, a_ref[0], a_ref[1], rev, cr_ref, ci_ref)
        y_ref[...] = _bd_nt(xr_ref[...], c_ref, 0) + _bd_nt(xi_ref[...], c_ref, 1)

    tile = lambda w: pl.BlockSpec((TT, w), lambda s: (_scan_tile(s, nt, rev), 0))
    par = lambda r: pl.BlockSpec((None, 2, r, S5P), lambda s: (ld, 0, 0, 0))
    return pl.pallas_call(
        body, name=name, grid=(nt,),
        out_shape=[jax.ShapeDtypeStruct((n, S5P), F32), jax.ShapeDtypeStruct((n, S5P), F32),
                   jax.ShapeDtypeStruct((n, S5W), F32)],
        in_specs=[tile(S5W), par(1), par(S5W), par(S5W)], out_specs=[tile(S5P), tile(S5P), tile(S5W)],
        scratch_shapes=[pltpu.VMEM((1, S5P), F32), pltpu.VMEM((1, S5P), F32)], compiler_params=_cparams(1),
    )(proj, a, bb, cb)


def s5_bwd(name, proj, dy, xr, xi, a, bb, cb, ld, rev, totals):
    n = proj.shape[0]
    nt = n // TT
    tb = TT // 8

    def tile_of(s):
        return _scan_tile(nt - 1 - s, nt, rev)

    def edge_of(s):
        pos = nt - 1 - s
        prev = _scan_tile(jnp.maximum(pos - 1, 0), nt, rev)
        return prev * tb if rev else jnp.maximum(pos * tb - 1, 0)

    def body(u_ref, dy_ref, xr_ref, xi_ref, er_ref, ei_ref, a_ref, b_ref, c_ref, _ta, _tb, _tc,
             du_ref, da_ref, db_ref, dc_ref, gr_ref, gi_ref, cr_ref, ci_ref):
        s = pl.program_id(0)

        @pl.when(s == 0)
        def _():
            cr_ref[...] = jnp.zeros_like(cr_ref)
            ci_ref[...] = jnp.zeros_like(ci_ref)
            da_ref[...] = jnp.zeros_like(da_ref)
            db_ref[...] = jnp.zeros_like(db_ref)
            dc_ref[...] = jnp.zeros_like(dc_ref)

        dyv, u = dy_ref[...], u_ref[...]
        xrv, xiv = xr_ref[...], xi_ref[...]
        gr_ref[...] = _bd_nn(dyv, c_ref, 0)
        gi_ref[...] = _bd_nn(dyv, c_ref, 1)
        _bd_tn_acc(dc_ref, 0, dyv, xrv)
        _bd_tn_acc(dc_ref, 1, dyv, xiv)
        _scan_rows(gr_ref, gi_ref, a_ref[0], -a_ref[1], not rev, cr_ref, ci_ref)
        g_r, g_i = gr_ref[...], gi_ref[...]
        rows = lax.broadcasted_iota(jnp.int32, (TT, 1), 0)
        live = jnp.where(s == nt - 1, 0.0, 1.0)
        if rev:
            pr = jnp.where(rows == TT - 1, er_ref[0:1, :] * live, _roll_rows(xrv, -1))
            pi = jnp.where(rows == TT - 1, ei_ref[0:1, :] * live, _roll_rows(xiv, -1))
        else:
            pr = jnp.where(rows == 0, er_ref[7:8, :] * live, _roll_rows(xrv, 1))
            pi = jnp.where(rows == 0, ei_ref[7:8, :] * live, _roll_rows(xiv, 1))
        da_ref[0] += jnp.sum(g_r * pr + g_i * pi, axis=0, keepdims=True)
        da_ref[1] += jnp.sum(g_i * pr - g_r * pi, axis=0, keepdims=True)
        du_ref[...] = _bd_nt(g_r, b_ref, 0) + _bd_nt(g_i, b_ref, 1)
        _bd_tn_acc(db_ref, 0, u, g_r)
        _bd_tn_acc(db_ref, 1, u, g_i)

    tile = lambda w: pl.BlockSpec((TT, w), lambda s: (tile_of(s), 0))
    edge = pl.BlockSpec((8, S5P), lambda s: (edge_of(s), 0))
    par = lambda r: pl.BlockSpec((None, 2, r, S5P), lambda s: (ld, 0, 0, 0))
    whole = pl.BlockSpec(memory_space=pl.ANY)
    return pl.pallas_call(
        body, name=name, grid=(nt,),
        out_shape=[jax.ShapeDtypeStruct((n, S5W), F32)] + [jax.ShapeDtypeStruct(t.shape, F32) for t in totals],
        in_specs=[tile(S5W), tile(S5W), tile(S5P), tile(S5P), edge, edge, par(1), par(S5W), par(S5W), whole, whole, whole],
        out_specs=[tile(S5W), par(1), par(S5W), par(S5W)], input_output_aliases={9: 1, 10: 2, 11: 3},
        scratch_shapes=[pltpu.VMEM((TT, S5P), F32), pltpu.VMEM((TT, S5P), F32),
                        pltpu.VMEM((1, S5P), F32), pltpu.VMEM((1, S5P), F32)], compiler_params=_cparams(1),
    )(proj, dy, xr, xi, xr, xi, a, bb, cb, *totals)


def gla_tile(r, v, qr, lb, sts, rev):
    ncc = TT // CK
    f = lb + (1.0 - lb) * jax.nn.sigmoid(r)
    k, lf, q = 1.0 - f, jnp.log(f), jax.nn.silu(qr)
    rows = lax.broadcasted_iota(jnp.int32, (TT, 1), 0)
    pos = rows % CK
    b = lf
    for s in (1, 2, 4, 8, 16):
        m = ((pos < CK - s) if rev else (pos >= s)).astype(F32)
        b = b + shift_rows(b, m, -s if rev else s)
    etot = [jnp.sum(lf[c * CK:(c + 1) * CK], axis=0, keepdims=True) for c in range(ncc)]
    e = jnp.concatenate([jnp.broadcast_to(t, (CK, HGW)) for t in etot], axis=0)
    kd, qe, qa = k * jnp.exp(e - b), q * jnp.exp(b), q * jnp.exp(b - e)
    cm = [(rows // CK == c).astype(F32) for c in range(ncc)]
    r2 = lax.broadcasted_iota(jnp.int32, (TT, TT), 0)
    c2 = lax.broadcasted_iota(jnp.int32, (TT, TT), 1)
    amask = (r2 // CK == c2 // CK) & ((r2 <= c2) if rev else (r2 >= c2))
    outs, new_sts = [], []
    for h in range(NH):
        ln = slice(h * HD, (h + 1) * HD)
        kdh, qeh, vh = kd[:, ln], qe[:, ln], v[:, ln]
        att = jnp.where(amask, mm_nt(qa[:, ln], kdh), 0.0)
        ds = mm_tn(jnp.concatenate([kdh * cm[c] for c in range(ncc)], axis=1), vh)
        st, starts = sts[h], [None] * ncc
        for c in (reversed(range(ncc)) if rev else range(ncc)):
            starts[c] = st
            dec = jnp.transpose(jnp.broadcast_to(jnp.exp(etot[c][:, ln]), (HD, HD)))
            st = dec * st + ds[c * HD:(c + 1) * HD]
        new_sts.append(st)
        qex = jnp.concatenate([qeh * cm[c] for c in range(ncc)], axis=1)
        outs.append(mm_nn(att, vh) + mm_nn(qex, jnp.concatenate(starts, axis=0)))
    return jnp.concatenate(outs, axis=1), new_sts


def _gla_specs(n, rev, order):
    nt = n // TT
    fcol = 2 if rev else 1
    tile = lambda cbk: pl.BlockSpec((TT, HGW), lambda s: (order(s), cbk))
    return nt, [tile(fcol), tile(3), tile(4)], tile(0)


def gla_fwd(name, proj, lb, rev):
    n = proj.shape[0]
    nt, in_tiles, out_tile = _gla_specs(n, rev, lambda s: _scan_tile(s, n // TT, rev))

    def body(r_ref, v_ref, q_ref, lb_ref, o_ref, st_ref, s_ref):
        @pl.when(pl.program_id(0) == 0)
        def _():
            s_ref[...] = jnp.zeros_like(s_ref)

        sts = [s_ref[h] for h in range(NH)]
        for h in range(NH):
            st_ref[h] = sts[h]
        o, new = gla_tile(r_ref[...], v_ref[...], q_ref[...], lb_ref[...], sts, rev)
        o_ref[...] = o
        for h in range(NH):
            s_ref[h] = new[h]

    st_spec = pl.BlockSpec((None, NH, HD, HD), lambda s: (_scan_tile(s, nt, rev), 0, 0, 0))
    return pl.pallas_call(
        body, name=name, grid=(nt,),
        out_shape=[jax.ShapeDtypeStruct((n, HGW), F32), jax.ShapeDtypeStruct((nt, NH, HD, HD), F32)],
        in_specs=in_tiles + [pl.BlockSpec((1, HGW), lambda s: (0, 0))], out_specs=[out_tile, st_spec],
        scratch_shapes=[pltpu.VMEM((NH, HD, HD), F32)], compiler_params=_cparams(1),
    )(proj, proj, proj, lb)


def gla_bwd(name, proj, lb, st_all, do, rev):
    n = proj.shape[0]
    order = lambda s: _scan_tile(n // TT - 1 - s, n // TT, rev)
    nt, in_tiles, out_tile = _gla_specs(n, rev, order)

    def body(r_ref, v_ref, q_ref, lb_ref, st_ref, do_ref, dr_ref, dv_ref, dq_ref, dlb_ref, ds_ref):
        @pl.when(pl.program_id(0) == 0)
        def _():
            ds_ref[...] = jnp.zeros_like(ds_ref)
            dlb_ref[...] = jnp.zeros_like(dlb_ref)

        _, vjp = jax.vjp(functools.partial(gla_tile, rev=rev), r_ref[...], v_ref[...], q_ref[...], lb_ref[...],
                         [st_ref[h] for h in range(NH)])
        dr, dv, dq, dlb, dsts = vjp((do_ref[...], [ds_ref[h] for h in range(NH)]))
        dr_ref[...] = dr
        dv_ref[...] = dv
        dq_ref[...] = dq
        dlb_ref[...] += dlb
        for h in range(NH):
            ds_ref[h] = dsts[h]

    st_spec = pl.BlockSpec((None, NH, HD, HD), lambda s: (order(s), 0, 0, 0))
    row = pl.BlockSpec((1, HGW), lambda s: (0, 0))
    return pl.pallas_call(
        body, name=name, grid=(nt,),
        out_shape=[jax.ShapeDtypeStruct((n, HGW), F32)] * 3 + [jax.ShapeDtypeStruct((1, HGW), F32)],
        in_specs=in_tiles + [row, st_spec, out_tile], out_specs=[out_tile] * 3 + [row],
        scratch_shapes=[pltpu.VMEM((NH, HD, HD), F32)], compiler_params=_cparams(1),
    )(proj, proj, proj, lb, st_all, do)


def f_lb(rows):
    mx = functools.reduce(jnp.maximum, rows)
    ex = [jnp.exp(r - mx) for r in rows]
    tot = functools.reduce(jnp.add, ex)
    out, acc = [jnp.zeros_like(rows[0])], None
    for e in ex[1:]:
        acc = e / tot if acc is None else acc + e / tot
        out.append(acc)
    return out


def lb_call(hg, dlb=None):
    nl = hg.shape[0]

    def body(*refs):
        rows = [refs[0][l:l + 1, :] for l in range(nl)]
        if dlb is None:
            res = f_lb(rows)
        else:
            _, vjp = jax.vjp(f_lb, rows)
            (res,) = vjp([refs[1][l:l + 1, :] for l in range(nl)])
        for l in range(nl):
            refs[-1][l:l + 1, :] = res[l]

    args = (hg,) if dlb is None else (hg, dlb)
    return pl.pallas_call(body, name="lower_bounds" if dlb is None else "lower_bounds_bwd",
                          out_shape=jax.ShapeDtypeStruct(hg.shape, F32))(*args)


def mod_fwd(craw, w_mod, b_cols):
    nl, _, cols = w_mod.shape

    def body(c_ref, w_ref, b_ref, o_ref):
        o_ref[...] = _dot(jax.nn.silu(c_ref[...]), w_ref[...], NN) + b_ref[...]

    return pl.pallas_call(
        body, name="mod_fwd", grid=(nl,), out_shape=jax.ShapeDtypeStruct((nl, 16, cols), F32),
        in_specs=[pl.BlockSpec((16, D), lambda l: (0, 0)), pl.BlockSpec((None, D, cols), lambda l: (l, 0, 0)),
                  pl.BlockSpec((None, 1, cols), lambda l: (l, 0, 0))],
        out_specs=pl.BlockSpec((None, 16, cols), lambda l: (l, 0, 0)), compiler_params=_cparams(1))(craw, w_mod, b_cols)


def mod_bwd(craw, w_mod, g):
    nl, _, cols = w_mod.shape

    def body(c_ref, w_ref, g_ref, dw_ref, dc_ref, acc_ref):
        l = pl.program_id(0)

        @pl.when(l == 0)
        def _():
            acc_ref[...] = jnp.zeros_like(acc_ref)

        c = c_ref[...]
        s, vjp = jax.vjp(jax.nn.silu, c)
        dw_ref[...] = _dot(s, g_ref[...], TN)
        acc_ref[...] += _dot(g_ref[...], w_ref[...], NT)

        @pl.when(l == nl - 1)
        def _():
            dc_ref[...] = vjp(acc_ref[...])[0]

    return pl.pallas_call(
        body, name="mod_bwd", grid=(nl,),
        out_shape=[jax.ShapeDtypeStruct(w_mod.shape, F32), jax.ShapeDtypeStruct((16, D), F32)],
        in_specs=[pl.BlockSpec((16, D), lambda l: (0, 0)), pl.BlockSpec((None, D, cols), lambda l: (l, 0, 0)),
                  pl.BlockSpec((None, 16, cols), lambda l: (l, 0, 0))],
        out_specs=[pl.BlockSpec((None, D, cols), lambda l: (l, 0, 0)), pl.BlockSpec((16, D), lambda l: (0, 0))],
        scratch_shapes=[pltpu.VMEM((16, D), F32)], compiler_params=_cparams(1))(craw, w_mod, g)


def sum_parts(parts):
    def body(p_ref, o_ref):
        acc = p_ref[0]
        for k in range(1, parts.shape[0]):
            acc = acc + p_ref[k]
        o_ref[...] = acc

    return pl.pallas_call(body, name="sum_small_grads", out_shape=jax.ShapeDtypeStruct(parts.shape[1:], parts.dtype),
                          compiler_params=pltpu.CompilerParams(vmem_limit_bytes=VMEM_MB << 20))(parts)


def _adamw_body(s):
    def body(w_ref, m_ref, v_ref, g_ref, *rest):
        go_ref, d_ref, mo_ref, vo_ref = rest[-4:]
        g = g_ref[0].astype(F32)
        for k in range(1, s):
            g = g + g_ref[k].astype(F32)
        m_new = B1 * m_ref[...] + (1.0 - B1) * g
        v_new = B2 * v_ref[...] + (1.0 - B2) * jnp.square(g)
        m_hat = m_new / (1.0 - B1 ** STEP)
        v_hat = v_new / (1.0 - B2 ** STEP)
        go_ref[...] = g
        d_ref[...] = -LR * (m_hat / (jnp.sqrt(v_hat) + EPS) + WD * w_ref[...])
        mo_ref[...] = m_new
        vo_ref[...] = v_new

    return body


def _row_tile(r):
    return max([t for t in range(8, 257, 8) if r % t == 0], default=r)


def adamw(name, w, m, v, gs):
    r, c = w.shape
    s = gs.shape[0]
    tr = _row_tile(r)
    blk = pl.BlockSpec((tr, c), lambda i: (i, 0))
    return pl.pallas_call(
        _adamw_body(s), name=name, grid=(r // tr,), out_shape=[jax.ShapeDtypeStruct((r, c), F32)] * 4,
        in_specs=[blk, blk, blk, pl.BlockSpec((s, tr, c), lambda i: (0, i, 0))], out_specs=[blk] * 4,
        compiler_params=_cparams(1))(w, m, v, gs)


def adamw_layer(name, l, w, m, v, gs, outs):
    _, r, c = w.shape
    s = gs.shape[0]
    tr = _row_tile(r)
    blk = pl.BlockSpec((None, tr, c), lambda i: (l, i, 0))
    whole = pl.BlockSpec(memory_space=pl.ANY)
    return pl.pallas_call(
        _adamw_body(s), name=name, grid=(r // tr,), out_shape=[jax.ShapeDtypeStruct(w.shape, F32)] * 4,
        in_specs=[blk, blk, blk, pl.BlockSpec((s, tr, c), lambda i: (0, i, 0))] + [whole] * 4, out_specs=[blk] * 4,
        input_output_aliases={4 + i: i for i in range(4)}, compiler_params=_cparams(1))(w, m, v, gs, *outs)


SMALL = ["c_ctx", "b_mod", "s5_lam_re", "s5_lam_im", "s5_log_dt", "s5_b_re", "s5_b_im", "s5_c_re", "s5_c_im", "s5_d",
         "b_glu", "hg_lb", "hg_norm_w", "ln1_g", "ln1_b", "conv_b", "ln2_g", "ln2_b"]
BIG = ["w_in", "w_glu", "w_out", "w_up", "w_down"]
WEIGHTS = ["c_ctx", "w_mod", "b_mod", "w_in", "s5_lam_re", "s5_lam_im", "s5_log_dt", "s5_b_re", "s5_b_im", "s5_c_re",
           "s5_c_im", "s5_d", "w_glu", "b_glu", "hg_lb", "hg_norm_w", "w_out", "ln1_g", "ln1_b", "w_up", "conv_w",
           "conv_b", "w_down", "ln2_g", "ln2_b"]
PACK_W = 1024


def _pack_rows(k):
    return -(-k // (8 * PACK_W)) * 8


def _pack(arrs):
    parts = []
    for a in arrs:
        flat = a.reshape(-1)
        r = _pack_rows(flat.shape[0])
        parts.append(jnp.pad(flat, (0, r * PACK_W - flat.shape[0])).reshape(r, PACK_W))
    used = sum(q.shape[0] for q in parts)
    parts.append(jnp.zeros((-used % (8 * N_DEV), PACK_W), parts[0].dtype))
    return jnp.concatenate(parts, axis=0)


def _unpack(p, shapes):
    out, o = [], 0
    for s in shapes:
        k = math.prod(s)
        r = _pack_rows(k)
        out.append(p[o:o + r].reshape(-1)[:k].reshape(s))
        o += r
    return out


def _gathered_cols(g):
    return jnp.moveaxis(g, 0, 2).reshape(g.shape[1], g.shape[2], -1)


def _gathered_rows(g):
    return jnp.moveaxis(g, 0, 1).reshape(g.shape[1], -1, g.shape[3])


def _step(p):
    nl = p["w_in"].shape[0]
    me = lax.axis_index("x") * 4 + lax.axis_index("y") * 2 + lax.axis_index("c")
    xc0 = jnp.concatenate([p["ctx"][0], p["x"][0]], axis=0)
    n = xc0.shape[0]
    target = p["loss_target"][0]

    hg3 = jnp.stack([p[k].reshape(-1) for k in ("hg_lb", "m_hg_lb", "v_hg_lb")])
    g_cw, g_c, g_hg = _exchange("gather_inputs", [p["conv_w"], p["c"], hg3], False)
    conv_w = _gathered_cols(g_cw)
    hg_full = jnp.moveaxis(g_hg.reshape(N_DEV, 3, nl, 2, -1), 0, 3).reshape(3, nl, 2 * HGW)
    lb_all = lb_call(hg_full[0])

    craw = jnp.concatenate([g_c.reshape(N_DEV, D), jnp.broadcast_to(p["c_ctx"][None], (8, D))], axis=0)
    cols = p["w_mod"].shape[2]
    b_cols = lax.dynamic_slice_in_dim(p["b_mod"], me * cols, cols, axis=1)[:, None, :]
    (g_mod,) = _exchange("gather_mod", [mod_fwd(craw, p["w_mod"], b_cols)], False)
    mod_all = jnp.moveaxis(g_mod, 0, 2).reshape(nl, 16, 6 * D)
    mod_x = lax.dynamic_index_in_dim(mod_all, me, axis=1, keepdims=False)
    mod2 = jnp.stack([mod_all[:, 8], mod_x], axis=1)
    mvec = lambda l, k: mod2[l, :, k * D:(k + 1) * D][:, None, :]
    gathers = {}

    def start_gather(l, part, ks, after):
        srcs = [p[k][l].astype(BF16) for k in ks]
        gathers[l, part], token = exchange_start(f"gather_start{l}{part}", srcs, [_own_block_set(s, False) for s in srcs],
                                                 False, after=after)
        return token[0, 0]

    parts0 = {"a": BIG[:1], "b": BIG[1:3], "c": BIG[3:]}
    for part, ks in parts0.items():
        start_gather(0, part, ks, g_mod)

    def gathered(l, part, after):
        res = exchange_wait(f"gather_wait{l}{part}", gathers[l, part], after, False)
        ks = parts0[part] if part else BIG
        cols_ = lambda g: jnp.moveaxis(g, 0, 1).reshape(g.shape[1], -1)
        rows_ = lambda g: g.reshape(-1, g.shape[2])
        return {k: (cols_ if k in ("w_in", "w_up") else rows_)(g) for k, g in zip(ks, res)}

    zvec = jnp.zeros((2, 1, D), F32)
    row = lambda a: a.reshape(1, 1, -1)

    to_hp = lambda a: jnp.moveaxis(a, -1, 2).reshape(nl * 2, S5H, S5P)
    prep_in = [p["s5_lam_re"].reshape(nl * 2, 1, S5P), p["s5_lam_im"].reshape(nl * 2, 1, S5P),
               p["s5_log_dt"].reshape(nl * 2, 32, 1), to_hp(p["s5_b_re"]), to_hp(p["s5_b_im"]),
               jnp.swapaxes(p["s5_c_re"], 2, 3).reshape(nl * 2, S5H, S5P),
               jnp.swapaxes(p["s5_c_im"], 2, 3).reshape(nl * 2, S5H, S5P)]
    s5a, s5b, s5c = s5_prep(*prep_in)

    T1 = lambda a, w=D, cb=_c0: (a, w, cb)
    saved = []
    xc = xc0
    (h,) = block_fwd("mod0", f_mod, n, [T1(xc)], [T1(mvec(0, 0)), T1(mvec(0, 1))], [(D, BF16, D, _c0)])
    w_in, w_glu, w_out, w_up, w_down = ([None] * nl for _ in range(5))
    for l in range(nl):
        wl = gathered(l, "", xc) if l else gathered(0, "a", s5a)
        w_in[l] = wl["w_in"]
        proj = dense_nn(f"in_proj{l}", h, w_in[l])
        s5 = [s5_fwd(f"s5_fwd{l}_{d}", proj, s5a, s5b, s5c, 2 * l + d, d == 1) for d in range(2)]
        lbs = [lb_all[l, d * HGW:(d + 1) * HGW][None] for d in range(2)]
        gl = [gla_fwd(f"gla_fwd{l}_{d}", proj, lbs[d], d == 1) for d in range(2)]
        if l == 0:
            wl = gathered(0, "b", gl[1][0])
        w_glu[l], w_out[l] = wl["w_glu"], wl["w_out"]
        started = start_gather(l + 1, "", BIG, w_out[l]) if l + 1 < nl else 0.0
        mix_t = [T1(proj, S5W), T1(proj, HGW, lambda j: 5), T1(s5[0][2], S5W), T1(s5[1][2], S5W),
                 T1(gl[0][0], HGW), T1(gl[1][0], HGW)]
        mix_p = [T1(row(p["s5_d"][l]) + started, S5W), T1(w_glu[l][None], S5W), T1(row(p["b_glu"][l]), S5W),
                 T1(row(p["hg_norm_w"][l]), HD)]
        (y,) = block_fwd(f"mix{l}", f_mix, n, mix_t, mix_p, [(D, BF16, D, _c0)])
        z = dense_nn(f"out_proj{l}", y, w_out[l])
        ln1_p = [T1(mvec(l, 2)), T1(row(p["ln1_g"][l])), T1(row(p["ln1_b"][l])), T1(mvec(l, 3)), T1(mvec(l, 4))]
        x1, h2 = block_fwd(f"ln1_{l}", f_ln, n, [T1(xc), T1(z)], ln1_p, [(D, F32, D, _c0), (D, BF16, D, _c0)])
        if l == 0:
            wl = gathered(0, "c", h2)
        w_up[l], w_down[l] = wl["w_up"], wl["w_down"]
        up = dense_nn(f"up_proj{l}", h2, w_up[l])
        ct = DFF // 2
        act_t = [T1(up, ct, lambda j: j), T1(up, ct, lambda j: j + 2)]
        cb2 = p["conv_b"][l].reshape(1, 1, -1)
        act_p = [T1(conv_w[l][None, :, :DFF], ct, lambda j: j), T1(conv_w[l][None, :, DFF:], ct, lambda j: j),
                 T1(cb2[:, :, :DFF], ct, lambda j: j), T1(cb2[:, :, DFF:], ct, lambda j: j)]
        (act,) = block_fwd(f"act{l}", f_act, n, act_t, act_p, [(DFF, BF16, ct, lambda j: j)], n_col=2)
        dn = dense_nn(f"down_proj{l}", act, w_down[l])
        nxt = (mvec(l + 1, 0), mvec(l + 1, 1)) if l + 1 < nl else (zvec, zvec)
        ln2_p = [T1(mvec(l, 5)), T1(row(p["ln2_g"][l])), T1(row(p["ln2_b"][l])), T1(nxt[0]), T1(nxt[1])]
        x2, hn = block_fwd(f"ln2_{l}", f_ln, n, [T1(x1), T1(dn)], ln2_p, [(D, F32, D, _c0), (D, BF16, D, _c0)])
        saved.append(dict(xc=xc, h=h, proj=proj, s5=s5, gl=gl, lbs=lbs, mix_t=mix_t, mix_p=mix_p, y=y, z=z,
                          ln1_p=ln1_p, x1=x1, h2=h2, act_t=act_t, act_p=act_p, act=act, dn=dn, ln2_p=ln2_p))
        xc, h = x2, hn

    dxc, loss_part = loss_and_grad(xc, target)
    loss = lax.psum(loss_part[0, 0], AXES)

    g = {k: [None] * nl for k in ("w_in", "w_glu", "w_out", "w_up", "w_down", "conv_w", "conv_b", "s5_d", "b_glu",
                                  "hg_norm_w", "ln1_g", "ln1_b", "ln2_g", "ln2_b", "dlb", "s5")}
    dmod = [[None] * 6 for _ in range(nl)]
    scatters = [[] for _ in range(nl)]
    s5_totals = [lax.empty((2 * nl, 2, r, S5P), F32) for r in (1, S5W, S5W)]
    dh_next = jnp.zeros((n, D), F32)
    fgrad = (D, F32, D, _c0)
    for l in reversed(range(nl)):
        sv = saved[l]
        (dx1, d_dn), dp = block_bwd(f"ln2_bwd{l}", f_ln, n, [T1(sv["x1"]), T1(sv["dn"])], sv["ln2_p"],
                                    [T1(dxc), T1(dh_next)], [fgrad, fgrad])
        dmod[l][5], g["ln2_g"][l], g["ln2_b"][l] = dp[0], dp[1], dp[2]
        if l + 1 < nl:
            dmod[l + 1][0], dmod[l + 1][1] = dp[3], dp[4]
        dact = dense_nt(f"down_bwd{l}", d_dn, w_down[l])
        g["w_down"][l] = dense_tn(f"down_wgrad{l}", sv["act"], d_dn)
        ct = DFF // 2
        cj = lambda j: j
        (dua, dug), dp = block_bwd(f"act_bwd{l}", f_act, n, sv["act_t"], sv["act_p"], [T1(dact, ct, cj)],
                                   [(DFF, BF16, ct, cj), (DFF, BF16, ct, cj)], n_col=2)
        g["conv_w"][l] = jnp.concatenate([dp[0][0], dp[1][0]], axis=-1)
        g["conv_b"][l] = jnp.concatenate([dp[2][0, 0], dp[3][0, 0]], axis=-1)
        dh2 = dense_nt2(f"up_bwd{l}", dua, dug, w_up[l])
        g["w_up"][l] = [dense_tn(f"up_wgrad{l}{part}", sv["h2"], du_) for part, du_ in (("a", dua), ("g", dug))]
        (dxc, dz), dp = block_bwd(f"ln1_bwd{l}", f_ln, n, [T1(sv["xc"]), T1(sv["z"])], sv["ln1_p"],
                                  [T1(dx1), T1(dh2)], [fgrad, fgrad])
        dmod[l][2], g["ln1_g"][l], g["ln1_b"][l], dmod[l][3], dmod[l][4] = dp
        dy = dense_nt(f"out_bwd{l}", dz, w_out[l])
        g["w_out"][l] = dense_tn(f"out_wgrad{l}", sv["y"], dz)
        half = (S5W, F32, S5W, _c0)
        (dpu, dpg, dys, dos), dp = block_bwd(f"mix_bwd{l}", f_mix, n, sv["mix_t"], sv["mix_p"], [T1(dy)],
                                                   [half, half, half, None, half, None])
        g["s5_d"][l], g["w_glu"][l], g["b_glu"][l], g["hg_norm_w"][l] = dp[0][0, 0], dp[1][0], dp[2][0, 0], dp[3][0, 0]

        def start_scatter(tag, ks):
            by_cols = lambda a, nb=N_DEV: jnp.moveaxis(a.reshape(a.shape[0], nb, -1), 1, 0)
            by_rows = lambda a: a.reshape(N_DEV, -1, a.shape[1])

            def blocks(k):
                if k == "w_up":
                    return jnp.concatenate([by_cols(half, N_DEV // 2) for half in g[k][l]], axis=0)
                return (by_rows if k in ("w_glu", "w_out", "w_down") else by_cols)(g[k][l].astype(BF16 if k != "conv_w" else F32))

            sends = [blocks(k) for k in ks]
            handle, token = exchange_start(f"scatter_start{l}{tag}", sends, [_own_block_set(s, True) for s in sends], True)
            scatters[l].append((ks, handle))
            return token[0, 0]

        lbs_b = sv["lbs"]
        if l == 0:
            started = start_scatter("a", ["w_glu", "w_out", "w_up", "w_down", "conv_w"])
            lbs_b = [b + started for b in lbs_b]
        gb = [gla_bwd(f"gla_bwd{l}_{d}", sv["proj"], lbs_b[d], sv["gl"][d][1], dos, d == 1) for d in range(2)]
        g["dlb"][l] = jnp.concatenate([gb[0][3], gb[1][3]], axis=-1)[0]
        sb = [None, None]
        for d in range(2):
            sb[d], *s5_totals = s5_bwd(f"s5_bwd{l}_{d}", sv["proj"], dys, sv["s5"][d][0], sv["s5"][d][1], s5a, s5b, s5c,
                                       2 * l + d, d == 1, s5_totals)
        asm_t = [T1(dpu, S5W), T1(sb[0], S5W), T1(sb[1], S5W), T1(gb[0][0], HGW), T1(gb[1][0], HGW),
                 T1(gb[0][1], HGW), T1(gb[1][1], HGW), T1(gb[0][2], HGW), T1(gb[1][2], HGW), T1(dpg, HGW)]
        (dproj,) = block_fwd(
            f"dproj{l}", lambda tv, pv, i: (jnp.concatenate(
                [tv[0] + tv[1] + tv[2], tv[3], tv[4], tv[5] + tv[6], tv[7] + tv[8], tv[9]], axis=-1),),
            n, asm_t, [], [(INC, BF16, INC, _c0)])
        dh_next = dense_nt(f"in_bwd{l}", dproj, w_in[l])
        g["w_in"][l] = dense_tn(f"in_wgrad{l}", sv["h"], dproj)
        started = start_scatter("b", ["w_in"]) if l == 0 else start_scatter("", BIG + ["conv_w"])
        if l:
            gate, wd_, cb_ = saved[l - 1]["ln2_p"][0]
            saved[l - 1]["ln2_p"][0] = (gate + started, wd_, cb_)
    (dxc,), dp = block_bwd("mod0_bwd", f_mod_id, n, [T1(xc0)], [T1(mvec(0, 0)), T1(mvec(0, 1))],
                           [T1(dh_next), T1(dxc)], [fgrad])
    dmod[0][0], dmod[0][1] = dp
    grad_x = dxc[n - p["x"].shape[1]:][None]

    d_prep = s5_prep_bwd(*prep_in, *s5_totals)
    from_hp = lambda a: jnp.moveaxis(a.reshape(nl, 2, S5H, S5W // S5H, 64), 2, -1)
    gs5 = {"s5_lam_re": d_prep[0].reshape(nl, 2, 32, 64), "s5_lam_im": d_prep[1].reshape(nl, 2, 32, 64),
           "s5_log_dt": d_prep[2].reshape(nl, 2, 32), "s5_b_re": from_hp(d_prep[3]), "s5_b_im": from_hp(d_prep[4]),
           "s5_c_re": jnp.swapaxes(d_prep[5].reshape(nl, 2, S5H, 32, 64), 2, 3),
           "s5_c_im": jnp.swapaxes(d_prep[6].reshape(nl, 2, S5H, 32, 64), 2, 3)}
    d_hg = lb_call(hg_full[0], jnp.stack(g["dlb"]))

    dmod_loc = jnp.stack([jnp.concatenate([dmod[l][k][:, 0] for k in range(6)], axis=-1) for l in range(nl)])
    (g_dmod,) = _exchange("gather_dmod", [dmod_loc], False)
    gcols = lax.dynamic_slice_in_dim(g_dmod, me * cols, cols, axis=3)
    g16 = jnp.concatenate([jnp.moveaxis(gcols[:, :, 1], 0, 1), jnp.moveaxis(gcols[:, :, 0], 0, 1)], axis=1)
    grad_w_mod, dcraw = mod_bwd(craw, p["w_mod"], g16)
    d_c_ctx = jnp.sum(dcraw[8:], axis=0)

    stk = lambda k: jnp.stack(g[k])
    small_g = {"c_ctx": d_c_ctx, "b_mod": dmod_loc[:, 0] + dmod_loc[:, 1], "s5_d": stk("s5_d"), "b_glu": stk("b_glu"),
               "hg_lb": d_hg.reshape(nl, 2, HGW), "hg_norm_w": stk("hg_norm_w"), "ln1_g": stk("ln1_g")[:, 0, 0],
               "ln1_b": stk("ln1_b")[:, 0, 0], "conv_b": stk("conv_b"), "ln2_g": stk("ln2_g")[:, 0, 0],
               "ln2_b": stk("ln2_b")[:, 0, 0], **gs5}
    g_pack = _pack([small_g[k] for k in SMALL]).reshape(N_DEV, -1, PACK_W)
    small_scatter, _ = exchange_start("scatter_small_start", [g_pack], [_own_block_set(g_pack, True)], True)

    out = {}
    kinds = ("grad_", "delta_", "new_m_", "new_v_")
    results = {}

    def update_layer(k, l, gs):
        prev = results.get(k) or [lax.empty(p[k].shape, F32) for _ in kinds]
        results[k] = adamw_layer(f"adamw_{k}{l}", l, p[k], p["m_" + k], p["v_" + k], gs, prev)

    for l in range(nl):
        update_layer("w_mod", l, grad_w_mod[l][None])
    for l in reversed(range(nl)):
        for i, (ks, handle) in enumerate(scatters[l]):
            recv = exchange_wait(f"scatter_wait{l}_{i}", handle, results["w_mod"][0], True)
            for k, gsum in zip(ks, recv):
                update_layer(k, l, gsum)
    for k, res in results.items():
        for kind, a in zip(kinds, res):
            out[kind + k] = a

    (g_parts,) = exchange_wait("scatter_small_wait", small_scatter, results["w_in"][0], True)
    (g_small,) = _exchange("gather_small_grads", [sum_parts(g_parts)], False)
    g_small = g_small.reshape(1, -1, PACK_W)
    hgw = {"": hg_full[0].reshape(nl, 2, HGW), "m_": hg_full[1].reshape(nl, 2, HGW), "v_": hg_full[2].reshape(nl, 2, HGW)}
    full = lambda pre, k: hgw[pre] if k == "hg_lb" else p[pre + k]
    shapes = [full("", k).shape for k in SMALL]
    res = adamw("adamw_small", *[_pack([full(pre, k) for k in SMALL]) for pre in ("", "m_", "v_")], g_small)
    for kind, packed in zip(kinds, res):
        for k, a in zip(SMALL, _unpack(packed, shapes)):
            if k == "hg_lb":
                a = lax.dynamic_slice_in_dim(a, me * (HGW // N_DEV), HGW // N_DEV, axis=2)
            out[kind + k] = a
    return (loss, grad_x, *[out[kind + k] for kind in ("grad_", "delta_", "new_m_", "new_v_") for k in WEIGHTS])


def kernel(x, c, ctx, c_ctx, w_mod, b_mod, w_in, s5_lam_re, s5_lam_im, s5_log_dt, s5_b_re, s5_b_im, s5_c_re, s5_c_im, s5_d, w_glu, b_glu, hg_lb, hg_norm_w, w_out, ln1_g, ln1_b, w_up, conv_w, conv_b, w_down, ln2_g, ln2_b, loss_target, m_c_ctx, m_w_mod, m_b_mod, m_w_in, m_s5_lam_re, m_s5_lam_im, m_s5_log_dt, m_s5_b_re, m_s5_b_im, m_s5_c_re, m_s5_c_im, m_s5_d, m_w_glu, m_b_glu, m_hg_lb, m_hg_norm_w, m_w_out, m_ln1_g, m_ln1_b, m_w_up, m_conv_w, m_conv_b, m_w_down, m_ln2_g, m_ln2_b, v_c_ctx, v_w_mod, v_b_mod, v_w_in, v_s5_lam_re, v_s5_lam_im, v_s5_log_dt, v_s5_b_re, v_s5_b_im, v_s5_c_re, v_s5_c_im, v_s5_d, v_w_glu, v_b_glu, v_hg_lb, v_hg_norm_w, v_w_out, v_ln1_g, v_ln1_b, v_w_up, v_conv_w, v_conv_b, v_w_down, v_ln2_g, v_ln2_b):
    return _step(dict(locals()))
```

```python
import functools
import math

import jax
import jax.numpy as jnp
from jax import lax
from jax.experimental import pallas as pl
from jax.experimental.pallas import tpu as pltpu

F32, BF16 = jnp.float32, jnp.bfloat16
N_DEV = 8
AXES = ("x", "y", "c")
D = 1024
S5W = 512
S5P = 2048
S5H = 16
HGW = 512
HD = 128
NH = 4
CK = 32
DFF = 2816
GRID_W = 64
INC = 3072
ALPHA = 8.0 ** 0.25
LN_EPS = 1e-5
RMS_EPS = 1e-6
LR, B1, B2, EPS, WD, STEP = 0.001, 0.9, 0.999, 1e-08, 0.01, 10
TT = 256
VMEM_MB = 56

NN = ((1,), (0,))
NT = ((1,), (1,))
TN = ((0,), (0,))


def _cparams(n_axes):
    return pltpu.CompilerParams(dimension_semantics=("arbitrary",) * n_axes, vmem_limit_bytes=VMEM_MB << 20)


def _dot(a, b, dims):
    return lax.dot_general(a.astype(BF16), b.astype(BF16), (dims, ((), ())), preferred_element_type=F32)


@jax.custom_vjp
def mm_nn(a, b):
    return _dot(a, b, NN)


@jax.custom_vjp
def mm_nt(a, b):
    return _dot(a, b, NT)


@jax.custom_vjp
def mm_tn(a, b):
    return _dot(a, b, TN)


mm_nn.defvjp(lambda a, b: (_dot(a, b, NN), (a, b)), lambda r, g: (_dot(g, r[1], NT), _dot(r[0], g, TN)))
mm_nt.defvjp(lambda a, b: (_dot(a, b, NT), (a, b)), lambda r, g: (_dot(g, r[1], NN), _dot(g, r[0], TN)))
mm_tn.defvjp(lambda a, b: (_dot(a, b, TN), (a, b)), lambda r, g: (_dot(r[1], g, NT), _dot(r[0], g, NN)))


def _roll_rows(u, s):
    return pltpu.roll(u, s % u.shape[0], 0)


@functools.partial(jax.custom_vjp, nondiff_argnums=(2,))
def shift_rows(u, m, s):
    return _roll_rows(u, s) * m


def _shift_fwd(u, m, s):
    return _roll_rows(u, s) * m, m


def _shift_bwd(s, m, g):
    return _roll_rows(g * m, -s), jnp.zeros_like(m)


shift_rows.defvjp(_shift_fwd, _shift_bwd)


def _scan_tile(pos, nt, rev):
    return jnp.where(pos == 0, 0, nt - pos) if rev else pos


def _exchange(name, arrays, all_to_all):
    k_arr = len(arrays)

    def body(*refs):
        ins, outs = refs[:k_arr], refs[k_arr:2 * k_arr]
        send_sems, recv_sems, local_sems = refs[2 * k_arr:]
        me = lax.axis_index("x") * 4 + lax.axis_index("y") * 2 + lax.axis_index("c")
        local = []
        for k in range(k_arr):
            cp = pltpu.make_async_copy(ins[k].at[me] if all_to_all else ins[k], outs[k].at[me], local_sems.at[k])
            cp.start()
            local.append(cp)
        sends = []
        for d in range(1, N_DEV):
            p = (me + d) % N_DEV
            for k in range(k_arr):
                cp = pltpu.make_async_remote_copy(
                    src_ref=ins[k].at[p] if all_to_all else ins[k], dst_ref=outs[k].at[me],
                    send_sem=send_sems.at[k, d - 1], recv_sem=recv_sems.at[k, d - 1],
                    device_id=(p // 4, (p // 2) % 2, p % 2), device_id_type=pl.DeviceIdType.MESH)
                cp.start()
                sends.append(cp)
        for d in range(1, N_DEV):
            q = (me + N_DEV - d) % N_DEV
            for k in range(k_arr):
                pltpu.make_async_remote_copy(
                    src_ref=ins[k].at[q] if all_to_all else ins[k], dst_ref=outs[k].at[q],
                    send_sem=send_sems.at[k, d - 1], recv_sem=recv_sems.at[k, d - 1],
                    device_id=(q // 4, (q // 2) % 2, q % 2), device_id_type=pl.DeviceIdType.MESH).wait_recv()
        for cp in sends:
            cp.wait_send()
        for cp in local:
            cp.wait()

    shapes = [a.shape if all_to_all else (N_DEV,) + a.shape for a in arrays]
    return pl.pallas_call(
        body, name=name,
        out_shape=[jax.ShapeDtypeStruct(s, a.dtype) for s, a in zip(shapes, arrays)],
        in_specs=[pl.BlockSpec(memory_space=pl.ANY)] * k_arr,
        out_specs=[pl.BlockSpec(memory_space=pl.ANY)] * k_arr,
        scratch_shapes=[pltpu.SemaphoreType.DMA((k_arr, N_DEV - 1)), pltpu.SemaphoreType.DMA((k_arr, N_DEV - 1)),
                        pltpu.SemaphoreType.DMA((k_arr,))],
    )(*arrays)


_HBM = pl.BlockSpec(memory_space=pltpu.HBM)
_SEM = pl.BlockSpec(memory_space=pltpu.SEMAPHORE)
_EFFECT = pltpu.SideEffectType.DATAFLOW_SIDE_EFFECTING


def _peer(i):
    return (i // 4, (i // 2) % 2, i % 2)


def exchange_start(name, srcs, lands, all_to_all, after=None):
    k_arr = len(srcs)
    n_sem = k_arr * (N_DEV - 1)
    extra = [] if after is None else [after]

    def body(*refs):
        ins, lz = refs[:k_arr], refs[k_arr:2 * k_arr]
        first = 2 * k_arr + len(extra)
        send_sems = refs[first:first + n_sem]
        recv_sems = refs[first + n_sem:first + 2 * n_sem]
        me = lax.axis_index("x") * 4 + lax.axis_index("y") * 2 + lax.axis_index("c")
        for d in range(1, N_DEV):
            p = (me + d) % N_DEV
            for k in range(k_arr):
                s = k * (N_DEV - 1) + d - 1
                pltpu.make_async_remote_copy(
                    src_ref=ins[k].at[p] if all_to_all else ins[k], dst_ref=lz[k].at[me],
                    send_sem=send_sems[s], recv_sem=recv_sems[s],
                    device_id=_peer(p), device_id_type=pl.DeviceIdType.MESH).start()
        refs[-1][...] = jnp.zeros_like(refs[-1])

    arrs = list(srcs) + list(lands)
    res = pl.pallas_call(
        body, name=name,
        out_shape=(*[pltpu.SemaphoreType.DMA(())] * (2 * n_sem), *[pltpu.HBM(a.shape, a.dtype) for a in arrs],
                   jax.ShapeDtypeStruct((8, 128), F32)),
        in_specs=[_HBM] * len(arrs) + [pl.BlockSpec(memory_space=pl.ANY)] * len(extra),
        out_specs=(*[_SEM] * (2 * n_sem), *[_HBM] * len(arrs), pl.BlockSpec(memory_space=pltpu.VMEM)),
        input_output_aliases={i: 2 * n_sem + i for i in range(len(arrs))},
        compiler_params=pltpu.CompilerParams(has_side_effects=_EFFECT),
    )(*[pltpu.with_memory_space_constraint(a, pltpu.HBM) for a in arrs], *extra)
    return res[:-1], res[-1]


def exchange_wait(name, handle, after, all_to_all):
    k_arr = len(handle) // (2 * N_DEV)
    n_sem = k_arr * (N_DEV - 1)
    sems, arrs = handle[:2 * n_sem], handle[2 * n_sem:]

    def body(*refs):
        ins, lz = refs[:k_arr], refs[k_arr:2 * k_arr]
        s_sems = refs[2 * k_arr:2 * k_arr + n_sem]
        r_sems = refs[2 * k_arr + n_sem:2 * k_arr + 2 * n_sem]
        me = lax.axis_index("x") * 4 + lax.axis_index("y") * 2 + lax.axis_index("c")
        for d in range(1, N_DEV):
            q = (me + N_DEV - d) % N_DEV
            for k in range(k_arr):
                s = k * (N_DEV - 1) + d - 1
                cp = pltpu.make_async_remote_copy(
                    src_ref=ins[k].at[q] if all_to_all else ins[k], dst_ref=lz[k].at[q],
                    send_sem=s_sems[s], recv_sem=r_sems[s],
                    device_id=_peer(q), device_id_type=pl.DeviceIdType.MESH)
                cp.wait_send()
                cp.wait_recv()

    res = pl.pallas_call(
        body, name=name, out_shape=tuple(pltpu.HBM(a.shape, a.dtype) for a in arrs),
        in_specs=[_HBM] * len(arrs) + [_SEM] * (2 * n_sem) + [pl.BlockSpec(memory_space=pl.ANY)],
        out_specs=tuple([_HBM] * len(arrs)),
        input_output_aliases={i: i for i in range(len(arrs))},
        compiler_params=pltpu.CompilerParams(has_side_effects=_EFFECT),
    )(*arrs, *sems, after)
    return res[k_arr:]


def _own_block_set(src, all_to_all):
    me = lax.axis_index("x") * 4 + lax.axis_index("y") * 2 + lax.axis_index("c")
    own = lax.dynamic_index_in_dim(src, me, 0, keepdims=False) if all_to_all else src
    shape = src.shape if all_to_all else (N_DEV,) + src.shape
    return lax.dynamic_update_index_in_dim(lax.empty(shape, src.dtype), own, me, 0)


def _tile(n, prefs):
    for t in prefs:
        if n % t == 0:
            return t
    raise ValueError(n)


def dense_nn(name, a, w, out_dtype=F32):
    n, k = a.shape
    m = w.shape[1]
    tn, tm = _tile(n, (1088, 256)), _tile(m, (1536, 1024, 1408, 512, 256, 128))

    def body(a_ref, w_ref, o_ref):
        o_ref[...] = _dot(a_ref[...], w_ref[...], NN).astype(o_ref.dtype)

    return pl.pallas_call(
        body, name=name, grid=(m // tm, n // tn), out_shape=jax.ShapeDtypeStruct((n, m), out_dtype),
        in_specs=[pl.BlockSpec((tn, k), lambda j, i: (i, 0)), pl.BlockSpec((k, tm), lambda j, i: (0, j))],
        out_specs=pl.BlockSpec((tn, tm), lambda j, i: (i, j)), compiler_params=_cparams(2))(a, w)


def dense_nt(name, g, w, out_dtype=F32):
    n, m = g.shape
    k = w.shape[0]
    tn, tk = _tile(n, (1088, 256)), _tile(k, (1024, 1408, 512, 256, 128))

    def body(g_ref, w_ref, o_ref):
        o_ref[...] = _dot(g_ref[...], w_ref[...], NT).astype(o_ref.dtype)

    return pl.pallas_call(
        body, name=name, grid=(k // tk, n // tn), out_shape=jax.ShapeDtypeStruct((n, k), out_dtype),
        in_specs=[pl.BlockSpec((tn, m), lambda j, i: (i, 0)), pl.BlockSpec((tk, m), lambda j, i: (j, 0))],
        out_specs=pl.BlockSpec((tn, tk), lambda j, i: (i, j)), compiler_params=_cparams(2))(g, w)


def dense_nt2(name, g1, g2, w, out_dtype=F32):
    n, m = g1.shape
    k = w.shape[0]
    tn, tk = _tile(n, (544, 256)), _tile(k, (1024, 1408, 512, 256, 128))

    def body(g1_ref, g2_ref, w1_ref, w2_ref, o_ref):
        o_ref[...] = (_dot(g1_ref[...], w1_ref[...], NT) + _dot(g2_ref[...], w2_ref[...], NT)).astype(o_ref.dtype)

    half = lambda h: pl.BlockSpec((tk, m), lambda j, i: (j, h))
    rows = pl.BlockSpec((tn, m), lambda j, i: (i, 0))
    return pl.pallas_call(
        body, name=name, grid=(k // tk, n // tn), out_shape=jax.ShapeDtypeStruct((n, k), out_dtype),
        in_specs=[rows, rows, half(0), half(1)], out_specs=pl.BlockSpec((tn, tk), lambda j, i: (i, j)),
        compiler_params=_cparams(2))(g1, g2, w, w)


def dense_tn(name, a, g, out_dtype=BF16):
    n, k = a.shape
    m = g.shape[1]
    tn = _tile(n, (2176, 256) if g.dtype == BF16 else (1088, 256))
    tk = _tile(k, (1024, 1408, 512, 256, 128))
    tm = _tile(m, (1024, 1408, 512, 256, 128))
    nt = n // tn

    def body(a_ref, g_ref, o_ref, acc_ref):
        t = pl.program_id(2)

        @pl.when(t == 0)
        def _():
            acc_ref[...] = jnp.zeros_like(acc_ref)

        acc_ref[...] += _dot(a_ref[...], g_ref[...], TN)

        @pl.when(t == nt - 1)
        def _():
            o_ref[...] = acc_ref[...].astype(o_ref.dtype)

    return pl.pallas_call(
        body, name=name, grid=(k // tk, m // tm, nt), out_shape=jax.ShapeDtypeStruct((k, m), out_dtype),
        in_specs=[pl.BlockSpec((tn, tk), lambda i, j, t: (t, i)), pl.BlockSpec((tn, tm), lambda i, j, t: (t, j))],
        out_specs=pl.BlockSpec((tk, tm), lambda i, j, t: (i, j)),
        scratch_shapes=[pltpu.VMEM((tk, tm), F32)], compiler_params=_cparams(3))(a, g)


def _c0(j):
    return 0


def _tspec(w, cb):
    return pl.BlockSpec((TT, w), lambda j, i: (i, cb(j)))


def _pspec(arr, w, cb):
    two = arr.shape[0] == 2
    return pl.BlockSpec((None, arr.shape[1], w), lambda j, i: (jnp.minimum(i, 1) if two else 0, 0, cb(j)))


def block_fwd(name, fn, n, tiled, params, outs, n_col=1):
    nt_, np_ = len(tiled), len(params)

    def body(*refs):
        i = pl.program_id(1)
        tv = [r[...].astype(F32) for r in refs[:nt_]]
        pv = [r[...].astype(F32) for r in refs[nt_:nt_ + np_]]
        for o_ref, r in zip(refs[nt_ + np_:], fn(tv, pv, i)):
            o_ref[...] = r.astype(o_ref.dtype)

    return pl.pallas_call(
        body, name=name, grid=(n_col, n // TT),
        out_shape=[jax.ShapeDtypeStruct((n, c), dt) for c, dt, _, _ in outs],
        in_specs=[_tspec(w, cb) for _, w, cb in tiled] + [_pspec(a, w, cb) for a, w, cb in params],
        out_specs=[_tspec(w, cb) for _, _, w, cb in outs], compiler_params=_cparams(2),
    )(*[a for a, _, _ in tiled], *[a for a, _, _ in params])


def block_bwd(name, fn, n, tiled, params, cots, grads, n_col=1):
    nt_, np_, nc_ = len(tiled), len(params), len(cots)
    want = [k for k, g in enumerate(grads) if g is not None]

    def body(*refs):
        i = pl.program_id(1)
        tv = [r[...].astype(F32) for r in refs[:nt_]]
        pv = [r[...].astype(F32) for r in refs[nt_:nt_ + np_]]
        cv = [r[...].astype(F32) for r in refs[nt_ + np_:nt_ + np_ + nc_]]
        o_refs = refs[nt_ + np_ + nc_:]
        _, vjp = jax.vjp(lambda t, p: list(fn(t, p, i)), tv, pv)
        dt, dp = vjp(cv)
        for o_ref, k in zip(o_refs, want):
            o_ref[...] = dt[k].astype(o_ref.dtype)
        for o_ref, g, (arr, _, _) in zip(o_refs[len(want):], dp, params):
            first = (i == 0) | (i == 1) if arr.shape[0] == 2 else i == 0

            @pl.when(first)
            def _(o_ref=o_ref):
                o_ref[...] = jnp.zeros_like(o_ref)

            o_ref[...] += g

    res = pl.pallas_call(
        body, name=name, grid=(n_col, n // TT),
        out_shape=[jax.ShapeDtypeStruct((n, grads[k][0]), grads[k][1]) for k in want]
        + [jax.ShapeDtypeStruct(a.shape, F32) for a, _, _ in params],
        in_specs=[_tspec(w, cb) for _, w, cb in tiled] + [_pspec(a, w, cb) for a, w, cb in params]
        + [_tspec(w, cb) for _, w, cb in cots],
        out_specs=[_tspec(grads[k][2], grads[k][3]) for k in want] + [_pspec(a, w, cb) for a, w, cb in params],
        compiler_params=_cparams(2),
    )(*[a for a, _, _ in tiled], *[a for a, _, _ in params], *[a for a, _, _ in cots])
    return res[:len(want)], res[len(want):]


def f_mod(tv, pv, i):
    (x,), (sh, sc) = tv, pv
    return (x * (1.0 + sc) + sh,)


def f_mod_id(tv, pv, i):
    return (f_mod(tv, pv, i)[0], tv[0])


def f_ln(tv, pv, i):
    (x, z), (gate, g, b, sh, sc) = tv, pv
    pre = ALPHA * x + gate * z
    mu = jnp.mean(pre, axis=-1, keepdims=True)
    var = jnp.mean(jnp.square(pre - mu), axis=-1, keepdims=True)
    xn = (pre - mu) * lax.rsqrt(var + LN_EPS) * g + b
    return xn, xn * (1.0 + sc) + sh


def f_mix(tv, pv, i):
    (pu, pg, y0, y1, o0, o1), (d_skip, w_glu, b_glu, norm_w) = tv, pv
    s5y = jax.nn.gelu(y0 + y1 + pu * d_skip)
    s5o = s5y * jax.nn.sigmoid(mm_nn(s5y, w_glu) + b_glu)
    o = o0 + o1
    heads = []
    for h in range(NH):
        oh = o[:, h * HD:(h + 1) * HD]
        heads.append(oh * lax.rsqrt(jnp.mean(jnp.square(oh), axis=-1, keepdims=True) + RMS_EPS) * norm_w)
    hg = jnp.concatenate(heads, axis=-1) * jax.nn.silu(pg)
    return (jnp.concatenate([s5o, hg], axis=-1),)


def f_act(tv, pv, i):
    (ua, ug), (cwa, cwg, cba, cbg) = tv, pv
    t = lax.broadcasted_iota(jnp.int32, (TT, 1), 0)
    lat = i > 0
    m_dn = jnp.where((t == 0) | (lat & (t % GRID_W == 0)), 0.0, 1.0)
    m_up = jnp.where((t == TT - 1) | (lat & (t % GRID_W == GRID_W - 1)), 0.0, 1.0)

    def conv(u, w, b):
        return shift_rows(u, m_dn, 1) * w[0:1] + u * w[1:2] + shift_rows(u, m_up, -1) * w[2:3] + b

    return (jax.nn.silu(conv(ua, cwa, cba)) * conv(ug, cwg, cbg),)


def loss_and_grad(xf, target):
    n = xf.shape[0]

    def body(x_ref, t_ref, dy_ref, l_ref):
        i = pl.program_id(0)

        @pl.when(i == 0)
        def _():
            l_ref[...] = jnp.zeros_like(l_ref)
            dy_ref[...] = jnp.zeros_like(dy_ref)

        @pl.when(i > 0)
        def _():
            e = x_ref[...] - t_ref[...]
            dy_ref[...] = e * (1.0 / D)
            l_ref[...] += 0.5 / D * jnp.sum(jnp.square(e))

    return pl.pallas_call(
        body, name="loss", grid=(n // TT,),
        out_shape=[jax.ShapeDtypeStruct((n, D), F32), jax.ShapeDtypeStruct((8, 128), F32)],
        in_specs=[pl.BlockSpec((TT, D), lambda i: (i, 0)), pl.BlockSpec((TT, D), lambda i: (jnp.maximum(i - 1, 0), 0))],
        out_specs=[pl.BlockSpec((TT, D), lambda i: (i, 0)), pl.BlockSpec((8, 128), lambda i: (0, 0))],
        compiler_params=_cparams(1))(xf, target)


def f_prep(lr, li, ldt, bre, bim, cre, cim):
    gi = lax.broadcasted_iota(jnp.int32, (S5W // S5H, S5P), 0)
    gc = lax.broadcasted_iota(jnp.int32, (S5W // S5H, S5P), 1) // 64
    dt = jnp.exp(jnp.sum(jnp.where(gi == gc, ldt, 0.0), axis=0, keepdims=True))
    mag, ang = jnp.exp(lr * dt), li * dt
    ar, ai = mag * jnp.cos(ang), mag * jnp.sin(ang)
    den = lr * lr + li * li
    nr, ni = ar - 1.0, ai
    cr = (nr * lr + ni * li) / den
    ci = (ni * lr - nr * li) / den
    bbr = cr * bre - ci * bim
    bbi = cr * bim + ci * bre
    rg = lax.broadcasted_iota(jnp.int32, (S5W, S5P), 0) // S5H
    cg = lax.broadcasted_iota(jnp.int32, (S5W, S5P), 1) // 64
    mask = (rg == cg).astype(F32)
    blk = lambda a: jnp.concatenate([a] * (S5W // S5H), axis=0) * mask
    return ar, ai, blk(bbr), blk(bbi), blk(cre), blk(-cim)


def s5_prep(lr, li, ldt, bre, bim, cre, cim):
    n2 = lr.shape[0]

    def body(lr_r, li_r, ldt_r, bre_r, bim_r, cre_r, cim_r, a_ref, b_ref, c_ref):
        ar, ai, bbr, bbi, cbr, cbi = f_prep(lr_r[...], li_r[...], ldt_r[...], bre_r[...], bim_r[...], cre_r[...], cim_r[...])
        a_ref[0], a_ref[1] = ar, ai
        b_ref[0], b_ref[1] = bbr.astype(BF16), bbi.astype(BF16)
        c_ref[0], c_ref[1] = cbr.astype(BF16), cbi.astype(BF16)

    sp = lambda r, c: pl.BlockSpec((None, r, c), lambda i: (i, 0, 0))
    sp4 = lambda r, c: pl.BlockSpec((None, 2, r, c), lambda i: (i, 0, 0, 0))
    return pl.pallas_call(
        body, name="s5_prep", grid=(n2,),
        out_shape=[jax.ShapeDtypeStruct((n2, 2, 1, S5P), F32), jax.ShapeDtypeStruct((n2, 2, S5W, S5P), BF16),
                   jax.ShapeDtypeStruct((n2, 2, S5W, S5P), BF16)],
        in_specs=[sp(1, S5P), sp(1, S5P), sp(32, 1), sp(S5H, S5P), sp(S5H, S5P), sp(S5H, S5P), sp(S5H, S5P)],
        out_specs=[sp4(1, S5P), sp4(S5W, S5P), sp4(S5W, S5P)], compiler_params=_cparams(1),
    )(lr, li, ldt, bre, bim, cre, cim)


def s5_prep_bwd(lr, li, ldt, bre, bim, cre, cim, da, db, dc):
    n2 = lr.shape[0]

    def body(lr_r, li_r, ldt_r, bre_r, bim_r, cre_r, cim_r, da_r, db_r, dc_r, *outs):
        args = [r[...] for r in (lr_r, li_r, ldt_r, bre_r, bim_r, cre_r, cim_r)]
        _, vjp = jax.vjp(f_prep, *args)
        for o_ref, g in zip(outs, vjp((da_r[0], da_r[1], db_r[0], db_r[1], dc_r[0], dc_r[1]))):
            o_ref[...] = g

    sp = lambda r, c: pl.BlockSpec((None, r, c), lambda i: (i, 0, 0))
    sp4 = lambda r, c: pl.BlockSpec((None, 2, r, c), lambda i: (i, 0, 0, 0))
    ins = [sp(1, S5P), sp(1, S5P), sp(32, 1), sp(S5H, S5P), sp(S5H, S5P), sp(S5H, S5P), sp(S5H, S5P)]
    return pl.pallas_call(
        body, name="s5_prep_bwd", grid=(n2,),
        out_shape=[jax.ShapeDtypeStruct(a.shape, F32) for a in (lr, li, ldt, bre, bim, cre, cim)],
        in_specs=ins + [sp4(1, S5P), sp4(S5W, S5P), sp4(S5W, S5P)], out_specs=ins, compiler_params=_cparams(1),
    )(lr, li, ldt, bre, bim, cre, cim, da, db, dc)


S5_DIAG = 2
_CU, _CP = S5W // S5_DIAG, S5P // S5_DIAG


def _bd_nn(u, w_ref, k):
    return jnp.concatenate([_dot(u[:, j * _CU:(j + 1) * _CU], w_ref[k, j * _CU:(j + 1) * _CU, j * _CP:(j + 1) * _CP], NN)
                            for j in range(S5_DIAG)], axis=1)


def _bd_nt(x, w_ref, k):
    return jnp.concatenate([_dot(x[:, j * _CP:(j + 1) * _CP], w_ref[k, j * _CU:(j + 1) * _CU, j * _CP:(j + 1) * _CP], NT)
                            for j in range(S5_DIAG)], axis=1)


def _bd_tn_acc(acc_ref, k, a, g):
    for j in range(S5_DIAG):
        acc_ref[k, j * _CU:(j + 1) * _CU, j * _CP:(j + 1) * _CP] += _dot(a[:, j * _CU:(j + 1) * _CU],
                                                                         g[:, j * _CP:(j + 1) * _CP], TN)


def _scan_rows(xr_ref, xi_ref, ar, ai, desc, cr_ref, ci_ref):
    unroll = 8

    def group(gi, carry):
        cr, ci = carry
        base = gi * unroll
        for j in range(unroll):
            t = TT - 1 - (base + j) if desc else base + j
            nr = ar * cr - ai * ci + xr_ref[pl.ds(t, 1), :]
            ni = ar * ci + ai * cr + xi_ref[pl.ds(t, 1), :]
            xr_ref[pl.ds(t, 1), :] = nr
            xi_ref[pl.ds(t, 1), :] = ni
            cr, ci = nr, ni
        return cr, ci

    cr, ci = lax.fori_loop(0, TT // unroll, group, (cr_ref[...], ci_ref[...]))
    cr_ref[...] = cr
    ci_ref[...] = ci


def s5_fwd(name, proj, a, bb, cb, ld, rev):
    n = proj.shape[0]
    nt = n // TT

    def body(u_ref, a_ref, b_ref, c_ref, xr_ref, xi_ref, y_ref, cr_ref, ci_ref):
        @pl.when(pl.program_id(0) == 0)
        def _():
            cr_ref[...] = jnp.zeros_like(cr_ref)
            ci_ref[...] = jnp.zeros_like(ci_ref)

        u = u_ref[...]
        xr_ref[...] = _bd_nn(u, b_ref, 0)
        xi_ref[...] = _bd_nn(u, b_ref, 1)
        _scan_rows(xr_ref, xi_ref, a_ref[0], a_ref[1], rev, cr_ref, ci_ref)
        y_ref[...] = _bd_nt(xr_ref[...], c_ref, 0) + _bd_nt(xi_ref[...], c_ref, 1)

    tile = lambda w: pl.BlockSpec((TT, w), lambda s: (_scan_tile(s, nt, rev), 0))
    par = lambda r: pl.BlockSpec((None, 2, r, S5P), lambda s: (ld, 0, 0, 0))
    return pl.pallas_call(
        body, name=name, grid=(nt,),
        out_shape=[jax.ShapeDtypeStruct((n, S5P), F32), jax.ShapeDtypeStruct((n, S5P), F32),
                   jax.ShapeDtypeStruct((n, S5W), F32)],
        in_specs=[tile(S5W), par(1), par(S5W), par(S5W)], out_specs=[tile(S5P), tile(S5P), tile(S5W)],
        scratch_shapes=[pltpu.VMEM((1, S5P), F32), pltpu.VMEM((1, S5P), F32)], compiler_params=_cparams(1),
    )(proj, a, bb, cb)


def s5_bwd(name, proj, dy, xr, xi, a, bb, cb, ld, rev, totals):
    n = proj.shape[0]
    nt = n // TT
    tb = TT // 8

    def tile_of(s):
        return _scan_tile(nt - 1 - s, nt, rev)

    def edge_of(s):
        pos = nt - 1 - s
        prev = _scan_tile(jnp.maximum(pos - 1, 0), nt, rev)
        return prev * tb if rev else jnp.maximum(pos * tb - 1, 0)

    def body(u_ref, dy_ref, xr_ref, xi_ref, er_ref, ei_ref, a_ref, b_ref, c_ref, _ta, _tb, _tc,
             du_ref, da_ref, db_ref, dc_ref, gr_ref, gi_ref, cr_ref, ci_ref):
        s = pl.program_id(0)

        @pl.when(s == 0)
        def _():
            cr_ref[...] = jnp.zeros_like(cr_ref)
            ci_ref[...] = jnp.zeros_like(ci_ref)
            da_ref[...] = jnp.zeros_like(da_ref)
            db_ref[...] = jnp.zeros_like(db_ref)
            dc_ref[...] = jnp.zeros_like(dc_ref)

        dyv, u = dy_ref[...], u_ref[...]
        xrv, xiv = xr_ref[...], xi_ref[...]
        gr_ref[...] = _bd_nn(dyv, c_ref, 0)
        gi_ref[...] = _bd_nn(dyv, c_ref, 1)
        _bd_tn_acc(dc_ref, 0, dyv, xrv)
        _bd_tn_acc(dc_ref, 1, dyv, xiv)
        _scan_rows(gr_ref, gi_ref, a_ref[0], -a_ref[1], not rev, cr_ref, ci_ref)
        g_r, g_i = gr_ref[...], gi_ref[...]
        rows = lax.broadcasted_iota(jnp.int32, (TT, 1), 0)
        live = jnp.where(s == nt - 1, 0.0, 1.0)
        if rev:
            pr = jnp.where(rows == TT - 1, er_ref[0:1, :] * live, _roll_rows(xrv, -1))
            pi = jnp.where(rows == TT - 1, ei_ref[0:1, :] * live, _roll_rows(xiv, -1))
        else:
            pr = jnp.where(rows == 0, er_ref[7:8, :] * live, _roll_rows(xrv, 1))
            pi = jnp.where(rows == 0, ei_ref[7:8, :] * live, _roll_rows(xiv, 1))
        da_ref[0] += jnp.sum(g_r * pr + g_i * pi, axis=0, keepdims=True)
        da_ref[1] += jnp.sum(g_i * pr - g_r * pi, axis=0, keepdims=True)
        du_ref[...] = _bd_nt(g_r, b_ref, 0) + _bd_nt(g_i, b_ref, 1)
        _bd_tn_acc(db_ref, 0, u, g_r)
        _bd_tn_acc(db_ref, 1, u, g_i)

    tile = lambda w: pl.BlockSpec((TT, w), lambda s: (tile_of(s), 0))
    edge = pl.BlockSpec((8, S5P), lambda s: (edge_of(s), 0))
    par = lambda r: pl.BlockSpec((None, 2, r, S5P), lambda s: (ld, 0, 0, 0))
    whole = pl.BlockSpec(memory_space=pl.ANY)
    return pl.pallas_call(
        body, name=name, grid=(nt,),
        out_shape=[jax.ShapeDtypeStruct((n, S5W), F32)] + [jax.ShapeDtypeStruct(t.shape, F32) for t in totals],
        in_specs=[tile(S5W), tile(S5W), tile(S5P), tile(S5P), edge, edge, par(1), par(S5W), par(S5W), whole, whole, whole],
        out_specs=[tile(S5W), par(1), par(S5W), par(S5W)], input_output_aliases={9: 1, 10: 2, 11: 3},
        scratch_shapes=[pltpu.VMEM((TT, S5P), F32), pltpu.VMEM((TT, S5P), F32),
                        pltpu.VMEM((1, S5P), F32), pltpu.VMEM((1, S5P), F32)], compiler_params=_cparams(1),
    )(proj, dy, xr, xi, xr, xi, a, bb, cb, *totals)


def gla_tile(r, v, qr, lb, sts, rev):
    ncc = TT // CK
    f = lb + (1.0 - lb) * jax.nn.sigmoid(r)
    k, lf, q = 1.0 - f, jnp.log(f), jax.nn.silu(qr)
    rows = lax.broadcasted_iota(jnp.int32, (TT, 1), 0)
    pos = rows % CK
    b = lf
    for s in (1, 2, 4, 8, 16):
        m = ((pos < CK - s) if rev else (pos >= s)).astype(F32)
        b = b + shift_rows(b, m, -s if rev else s)
    etot = [jnp.sum(lf[c * CK:(c + 1) * CK], axis=0, keepdims=True) for c in range(ncc)]
    e = jnp.concatenate([jnp.broadcast_to(t, (CK, HGW)) for t in etot], axis=0)
    kd, qe, qa = k * jnp.exp(e - b), q * jnp.exp(b), q * jnp.exp(b - e)
    cm = [(rows // CK == c).astype(F32) for c in range(ncc)]
    r2 = lax.broadcasted_iota(jnp.int32, (TT, TT), 0)
    c2 = lax.broadcasted_iota(jnp.int32, (TT, TT), 1)
    amask = (r2 // CK == c2 // CK) & ((r2 <= c2) if rev else (r2 >= c2))
    outs, new_sts = [], []
    for h in range(NH):
        ln = slice(h * HD, (h + 1) * HD)
        kdh, qeh, vh = kd[:, ln], qe[:, ln], v[:, ln]
        att = jnp.where(amask, mm_nt(qa[:, ln], kdh), 0.0)
        ds = mm_tn(jnp.concatenate([kdh * cm[c] for c in range(ncc)], axis=1), vh)
        st, starts = sts[h], [None] * ncc
        for c in (reversed(range(ncc)) if rev else range(ncc)):
            starts[c] = st
            dec = jnp.transpose(jnp.broadcast_to(jnp.exp(etot[c][:, ln]), (HD, HD)))
            st = dec * st + ds[c * HD:(c + 1) * HD]
        new_sts.append(st)
        qex = jnp.concatenate([qeh * cm[c] for c in range(ncc)], axis=1)
        outs.append(mm_nn(att, vh) + mm_nn(qex, jnp.concatenate(starts, axis=0)))
    return jnp.concatenate(outs, axis=1), new_sts


def _gla_specs(n, rev, order):
    nt = n // TT
    fcol = 2 if rev else 1
    tile = lambda cbk: pl.BlockSpec((TT, HGW), lambda s: (order(s), cbk))
    return nt, [tile(fcol), tile(3), tile(4)], tile(0)


def gla_fwd(name, proj, lb, rev):
    n = proj.shape[0]
    nt, in_tiles, out_tile = _gla_specs(n, rev, lambda s: _scan_tile(s, n // TT, rev))

    def body(r_ref, v_ref, q_ref, lb_ref, o_ref, st_ref, s_ref):
        @pl.when(pl.program_id(0) == 0)
        def _():
            s_ref[...] = jnp.zeros_like(s_ref)

        sts = [s_ref[h] for h in range(NH)]
        for h in range(NH):
            st_ref[h] = sts[h]
        o, new = gla_tile(r_ref[...], v_ref[...], q_ref[...], lb_ref[...], sts, rev)
        o_ref[...] = o
        for h in range(NH):
            s_ref[h] = new[h]

    st_spec = pl.BlockSpec((None, NH, HD, HD), lambda s: (_scan_tile(s, nt, rev), 0, 0, 0))
    return pl.pallas_call(
        body, name=name, grid=(nt,),
        out_shape=[jax.ShapeDtypeStruct((n, HGW), F32), jax.ShapeDtypeStruct((nt, NH, HD, HD), F32)],
        in_specs=in_tiles + [pl.BlockSpec((1, HGW), lambda s: (0, 0))], out_specs=[out_tile, st_spec],
        scratch_shapes=[pltpu.VMEM((NH, HD, HD), F32)], compiler_params=_cparams(1),
    )(proj, proj, proj, lb)


def gla_bwd(name, proj, lb, st_all, do, rev):
    n = proj.shape[0]
    order = lambda s: _scan_tile(n // TT - 1 - s, n // TT, rev)
    nt, in_tiles, out_tile = _gla_specs(n, rev, order)

    def body(r_ref, v_ref, q_ref, lb_ref, st_ref, do_ref, dr_ref, dv_ref, dq_ref, dlb_ref, ds_ref):
        @pl.when(pl.program_id(0) == 0)
        def _():
            ds_ref[...] = jnp.zeros_like(ds_ref)
            dlb_ref[...] = jnp.zeros_like(dlb_ref)

        _, vjp = jax.vjp(functools.partial(gla_tile, rev=rev), r_ref[...], v_ref[...], q_ref[...], lb_ref[...],
                         [st_ref[h] for h in range(NH)])
        dr, dv, dq, dlb, dsts = vjp((do_ref[...], [ds_ref[h] for h in range(NH)]))
        dr_ref[...] = dr
        dv_ref[...] = dv
        dq_ref[...] = dq
        dlb_ref[...] += dlb
        for h in range(NH):
            ds_ref[h] = dsts[h]

    st_spec = pl.BlockSpec((None, NH, HD, HD), lambda s: (order(s), 0, 0, 0))
    row = pl.BlockSpec((1, HGW), lambda s: (0, 0))
    return pl.pallas_call(
        body, name=name, grid=(nt,),
        out_shape=[jax.ShapeDtypeStruct((n, HGW), F32)] * 3 + [jax.ShapeDtypeStruct((1, HGW), F32)],
        in_specs=in_tiles + [row, st_spec, out_tile], out_specs=[out_tile] * 3 + [row],
        scratch_shapes=[pltpu.VMEM((NH, HD, HD), F32)], compiler_params=_cparams(1),
    )(proj, proj, proj, lb, st_all, do)


def f_lb(rows):
    mx = functools.reduce(jnp.maximum, rows)
    ex = [jnp.exp(r - mx) for r in rows]
    tot = functools.reduce(jnp.add, ex)
    out, acc = [jnp.zeros_like(rows[0])], None
    for e in ex[1:]:
        acc = e / tot if acc is None else acc + e / tot
        out.append(acc)
    return out


def lb_call(hg, dlb=None):
    nl = hg.shape[0]

    def body(*refs):
        rows = [refs[0][l:l + 1, :] for l in range(nl)]
        if dlb is None:
            res = f_lb(rows)
        else:
            _, vjp = jax.vjp(f_lb, rows)
            (res,) = vjp([refs[1][l:l + 1, :] for l in range(nl)])
        for l in range(nl):
            refs[-1][l:l + 1, :] = res[l]

    args = (hg,) if dlb is None else (hg, dlb)
    return pl.pallas_call(body, name="lower_bounds" if dlb is None else "lower_bounds_bwd",
                          out_shape=jax.ShapeDtypeStruct(hg.shape, F32))(*args)


def mod_fwd(craw, w_mod, b_cols):
    nl, _, cols = w_mod.shape

    def body(c_ref, w_ref, b_ref, o_ref):
        o_ref[...] = _dot(jax.nn.silu(c_ref[...]), w_ref[...], NN) + b_ref[...]

    return pl.pallas_call(
        body, name="mod_fwd", grid=(nl,), out_shape=jax.ShapeDtypeStruct((nl, 16, cols), F32),
        in_specs=[pl.BlockSpec((16, D), lambda l: (0, 0)), pl.BlockSpec((None, D, cols), lambda l: (l, 0, 0)),
                  pl.BlockSpec((None, 1, cols), lambda l: (l, 0, 0))],
        out_specs=pl.BlockSpec((None, 16, cols), lambda l: (l, 0, 0)), compiler_params=_cparams(1))(craw, w_mod, b_cols)


def mod_bwd(craw, w_mod, g):
    nl, _, cols = w_mod.shape

    def body(c_ref, w_ref, g_ref, dw_ref, dc_ref, acc_ref):
        l = pl.program_id(0)

        @pl.when(l == 0)
        def _():
            acc_ref[...] = jnp.zeros_like(acc_ref)

        c = c_ref[...]
        s, vjp = jax.vjp(jax.nn.silu, c)
        dw_ref[...] = _dot(s, g_ref[...], TN)
        acc_ref[...] += _dot(g_ref[...], w_ref[...], NT)

        @pl.when(l == nl - 1)
        def _():
            dc_ref[...] = vjp(acc_ref[...])[0]

    return pl.pallas_call(
        body, name="mod_bwd", grid=(nl,),
        out_shape=[jax.ShapeDtypeStruct(w_mod.shape, F32), jax.ShapeDtypeStruct((16, D), F32)],
        in_specs=[pl.BlockSpec((16, D), lambda l: (0, 0)), pl.BlockSpec((None, D, cols), lambda l: (l, 0, 0)),
                  pl.BlockSpec((None, 16, cols), lambda l: (l, 0, 0))],
        out_specs=[pl.BlockSpec((None, D, cols), lambda l: (l, 0, 0)), pl.BlockSpec((16, D), lambda l: (0, 0))],
        scratch_shapes=[pltpu.VMEM((16, D), F32)], compiler_params=_cparams(1))(craw, w_mod, g)


def sum_parts(parts):
    def body(p_ref, o_ref):
        acc = p_ref[0]
        for k in range(1, parts.shape[0]):
            acc = acc + p_ref[k]
        o_ref[...] = acc

    return pl.pallas_call(body, name="sum_small_grads", out_shape=jax.ShapeDtypeStruct(parts.shape[1:], parts.dtype),
                          compiler_params=pltpu.CompilerParams(vmem_limit_bytes=VMEM_MB << 20))(parts)


def _adamw_body(s):
    def body(w_ref, m_ref, v_ref, g_ref, *rest):
        go_ref, d_ref, mo_ref, vo_ref = rest[-4:]
        g = g_ref[0].astype(F32)
        for k in range(1, s):
            g = g + g_ref[k].astype(F32)
        m_new = B1 * m_ref[...] + (1.0 - B1) * g
        v_new = B2 * v_ref[...] + (1.0 - B2) * jnp.square(g)
        m_hat = m_new / (1.0 - B1 ** STEP)
        v_hat = v_new / (1.0 - B2 ** STEP)
        go_ref[...] = g
        d_ref[...] = -LR * (m_hat / (jnp.sqrt(v_hat) + EPS) + WD * w_ref[...])
        mo_ref[...] = m_new
        vo_ref[...] = v_new

    return body


def _row_tile(r):
    return max([t for t in range(8, 257, 8) if r % t == 0], default=r)


def adamw(name, w, m, v, gs):
    r, c = w.shape
    s = gs.shape[0]
    tr = _row_tile(r)
    blk = pl.BlockSpec((tr, c), lambda i: (i, 0))
    return pl.pallas_call(
        _adamw_body(s), name=name, grid=(r // tr,), out_shape=[jax.ShapeDtypeStruct((r, c), F32)] * 4,
        in_specs=[blk, blk, blk, pl.BlockSpec((s, tr, c), lambda i: (0, i, 0))], out_specs=[blk] * 4,
        compiler_params=_cparams(1))(w, m, v, gs)


def adamw_layer(name, l, w, m, v, gs, outs):
    _, r, c = w.shape
    s = gs.shape[0]
    tr = _row_tile(r)
    blk = pl.BlockSpec((None, tr, c), lambda i: (l, i, 0))
    whole = pl.BlockSpec(memory_space=pl.ANY)
    return pl.pallas_call(
        _adamw_body(s), name=name, grid=(r // tr,), out_shape=[jax.ShapeDtypeStruct(w.shape, F32)] * 4,
        in_specs=[blk, blk, blk, pl.BlockSpec((s, tr, c), lambda i: (0, i, 0))] + [whole] * 4, out_specs=[blk] * 4,
        input_output_aliases={4 + i: i for i in range(4)}, compiler_params=_cparams(1))(w, m, v, gs, *outs)


SMALL = ["c_ctx", "b_mod", "s5_lam_re", "s5_lam_im", "s5_log_dt", "s5_b_re", "s5_b_im", "s5_c_re", "s5_c_im", "s5_d",
         "b_glu", "hg_lb", "hg_norm_w", "ln1_g", "ln1_b", "conv_b", "ln2_g", "ln2_b"]
BIG = ["w_in", "w_glu", "w_out", "w_up", "w_down"]
WEIGHTS = ["c_ctx", "w_mod", "b_mod", "w_in", "s5_lam_re", "s5_lam_im", "s5_log_dt", "s5_b_re", "s5_b_im", "s5_c_re",
           "s5_c_im", "s5_d", "w_glu", "b_glu", "hg_lb", "hg_norm_w", "w_out", "ln1_g", "ln1_b", "w_up", "conv_w",
           "conv_b", "w_down", "ln2_g", "ln2_b"]
PACK_W = 1024


def _pack_rows(k):
    return -(-k // (8 * PACK_W)) * 8


def _pack(arrs):
    parts = []
    for a in arrs:
        flat = a.reshape(-1)
        r = _pack_rows(flat.shape[0])
        parts.append(jnp.pad(flat, (0, r * PACK_W - flat.shape[0])).reshape(r, PACK_W))
    used = sum(q.shape[0] for q in parts)
    parts.append(jnp.zeros((-used % (8 * N_DEV), PACK_W), parts[0].dtype))
    return jnp.concatenate(parts, axis=0)


def _unpack(p, shapes):
    out, o = [], 0
    for s in shapes:
        k = math.prod(s)
        r = _pack_rows(k)
        out.append(p[o:o + r].reshape(-1)[:k].reshape(s))
        o += r
    return out


def _gathered_cols(g):
    return jnp.moveaxis(g, 0, 2).reshape(g.shape[1], g.shape[2], -1)


def _gathered_rows(g):
    return jnp.moveaxis(g, 0, 1).reshape(g.shape[1], -1, g.shape[3])


def _step(p):
    nl = p["w_in"].shape[0]
    me = lax.axis_index("x") * 4 + lax.axis_index("y") * 2 + lax.axis_index("c")
    xc0 = jnp.concatenate([p["ctx"][0], p["x"][0]], axis=0)
    n = xc0.shape[0]
    target = p["loss_target"][0]

    hg3 = jnp.stack([p[k].reshape(-1) for k in ("hg_lb", "m_hg_lb", "v_hg_lb")])
    g_cw, g_c, g_hg = _exchange("gather_inputs", [p["conv_w"], p["c"], hg3], False)
    conv_w = _gathered_cols(g_cw)
    hg_full = jnp.moveaxis(g_hg.reshape(N_DEV, 3, nl, 2, -1), 0, 3).reshape(3, nl, 2 * HGW)
    lb_all = lb_call(hg_full[0])

    craw = jnp.concatenate([g_c.reshape(N_DEV, D), jnp.broadcast_to(p["c_ctx"][None], (8, D))], axis=0)
    cols = p["w_mod"].shape[2]
    b_cols = lax.dynamic_slice_in_dim(p["b_mod"], me * cols, cols, axis=1)[:, None, :]
    (g_mod,) = _exchange("gather_mod", [mod_fwd(craw, p["w_mod"], b_cols)], False)
    mod_all = jnp.moveaxis(g_mod, 0, 2).reshape(nl, 16, 6 * D)
    mod_x = lax.dynamic_index_in_dim(mod_all, me, axis=1, keepdims=False)
    mod2 = jnp.stack([mod_all[:, 8], mod_x], axis=1)
    mvec = lambda l, k: mod2[l, :, k * D:(k + 1) * D][:, None, :]
    gathers = {}

    def start_gather(l, part, ks, after):
        srcs = [p[k][l].astype(BF16) for k in ks]
        gathers[l, part], token = exchange_start(f"gather_start{l}{part}", srcs, [_own_block_set(s, False) for s in srcs],
                                                 False, after=after)
        return token[0, 0]

    parts0 = {"a": BIG[:1], "b": BIG[1:3], "c": BIG[3:]}
    for part, ks in parts0.items():
        start_gather(0, part, ks, g_mod)

    def gathered(l, part, after):
        res = exchange_wait(f"gather_wait{l}{part}", gathers[l, part], after, False)
        ks = parts0[part] if part else BIG
        cols_ = lambda g: jnp.moveaxis(g, 0, 1).reshape(g.shape[1], -1)
        rows_ = lambda g: g.reshape(-1, g.shape[2])
        return {k: (cols_ if k in ("w_in", "w_up") else rows_)(g) for k, g in zip(ks, res)}

    zvec = jnp.zeros((2, 1, D), F32)
    row = lambda a: a.reshape(1, 1, -1)

    to_hp = lambda a: jnp.moveaxis(a, -1, 2).reshape(nl * 2, S5H, S5P)
    prep_in = [p["s5_lam_re"].reshape(nl * 2, 1, S5P), p["s5_lam_im"].reshape(nl * 2, 1, S5P),
               p["s5_log_dt"].reshape(nl * 2, 32, 1), to_hp(p["s5_b_re"]), to_hp(p["s5_b_im"]),
               jnp.swapaxes(p["s5_c_re"], 2, 3).reshape(nl * 2, S5H, S5P),
               jnp.swapaxes(p["s5_c_im"], 2, 3).reshape(nl * 2, S5H, S5P)]
    s5a, s5b, s5c = s5_prep(*prep_in)

    T1 = lambda a, w=D, cb=_c0: (a, w, cb)
    saved = []
    xc = xc0
    (h,) = block_fwd("mod0", f_mod, n, [T1(xc)], [T1(mvec(0, 0)), T1(mvec(0, 1))], [(D, BF16, D, _c0)])
    w_in, w_glu, w_out, w_up, w_down = ([None] * nl for _ in range(5))
    for l in range(nl):
        wl = gathered(l, "", xc) if l else gathered(0, "a", s5a)
        w_in[l] = wl["w_in"]
        proj = dense_nn(f"in_proj{l}", h, w_in[l])
        s5 = [s5_fwd(f"s5_fwd{l}_{d}", proj, s5a, s5b, s5c, 2 * l + d, d == 1) for d in range(2)]
        lbs = [lb_all[l, d * HGW:(d + 1) * HGW][None] for d in range(2)]
        gl = [gla_fwd(f"gla_fwd{l}_{d}", proj, lbs[d], d == 1) for d in range(2)]
        if l == 0:
            wl = gathered(0, "b", gl[1][0])
        w_glu[l], w_out[l] = wl["w_glu"], wl["w_out"]
        started = start_gather(l + 1, "", BIG, w_out[l]) if l + 1 < nl else 0.0
        mix_t = [T1(proj, S5W), T1(proj, HGW, lambda j: 5), T1(s5[0][2], S5W), T1(s5[1][2], S5W),
                 T1(gl[0][0], HGW), T1(gl[1][0], HGW)]
        mix_p = [T1(row(p["s5_d"][l]) + started, S5W), T1(w_glu[l][None], S5W), T1(row(p["b_glu"][l]), S5W),
                 T1(row(p["hg_norm_w"][l]), HD)]
        (y,) = block_fwd(f"mix{l}", f_mix, n, mix_t, mix_p, [(D, BF16, D, _c0)])
        z = dense_nn(f"out_proj{l}", y, w_out[l])
        ln1_p = [T1(mvec(l, 2)), T1(row(p["ln1_g"][l])), T1(row(p["ln1_b"][l])), T1(mvec(l, 3)), T1(mvec(l, 4))]
        x1, h2 = block_fwd(f"ln1_{l}", f_ln, n, [T1(xc), T1(z)], ln1_p, [(D, F32, D, _c0), (D, BF16, D, _c0)])
        if l == 0:
            wl = gathered(0, "c", h2)
        w_up[l], w_down[l] = wl["w_up"], wl["w_down"]
        up = dense_nn(f"up_proj{l}", h2, w_up[l])
        ct = DFF // 2
        act_t = [T1(up, ct, lambda j: j), T1(up, ct, lambda j: j + 2)]
        cb2 = p["conv_b"][l].reshape(1, 1, -1)
        act_p = [T1(conv_w[l][None, :, :DFF], ct, lambda j: j), T1(conv_w[l][None, :, DFF:], ct, lambda j: j),
                 T1(cb2[:, :, :DFF], ct, lambda j: j), T1(cb2[:, :, DFF:], ct, lambda j: j)]
        (act,) = block_fwd(f"act{l}", f_act, n, act_t, act_p, [(DFF, BF16, ct, lambda j: j)], n_col=2)
        dn = dense_nn(f"down_proj{l}", act, w_down[l])
        nxt = (mvec(l + 1, 0), mvec(l + 1, 1)) if l + 1 < nl else (zvec, zvec)
        ln2_p = [T1(mvec(l, 5)), T1(row(p["ln2_g"][l])), T1(row(p["ln2_b"][l])), T1(nxt[0]), T1(nxt[1])]
        x2, hn = block_fwd(f"ln2_{l}", f_ln, n, [T1(x1), T1(dn)], ln2_p, [(D, F32, D, _c0), (D, BF16, D, _c0)])
        saved.append(dict(xc=xc, h=h, proj=proj, s5=s5, gl=gl, lbs=lbs, mix_t=mix_t, mix_p=mix_p, y=y, z=z,
                          ln1_p=ln1_p, x1=x1, h2=h2, act_t=act_t, act_p=act_p, act=act, dn=dn, ln2_p=ln2_p))
        xc, h = x2, hn

    dxc, loss_part = loss_and_grad(xc, target)
    loss = lax.psum(loss_part[0, 0], AXES)

    g = {k: [None] * nl for k in ("w_in", "w_glu", "w_out", "w_up", "w_down", "conv_w", "conv_b", "s5_d", "b_glu",
                                  "hg_norm_w", "ln1_g", "ln1_b", "ln2_g", "ln2_b", "dlb", "s5")}
    dmod = [[None] * 6 for _ in range(nl)]
    scatters = [[] for _ in range(nl)]
    s5_totals = [lax.empty((2 * nl, 2, r, S5P), F32) for r in (1, S5W, S5W)]
    dh_next = jnp.zeros((n, D), F32)
    fgrad = (D, F32, D, _c0)
    for l in reversed(range(nl)):
        sv = saved[l]
        (dx1, d_dn), dp = block_bwd(f"ln2_bwd{l}", f_ln, n, [T1(sv["x1"]), T1(sv["dn"])], sv["ln2_p"],
                                    [T1(dxc), T1(dh_next)], [fgrad, fgrad])
        dmod[l][5], g["ln2_g"][l], g["ln2_b"][l] = dp[0], dp[1], dp[2]
        if l + 1 < nl:
            dmod[l + 1][0], dmod[l + 1][1] = dp[3], dp[4]
        dact = dense_nt(f"down_bwd{l}", d_dn, w_down[l])
        g["w_down"][l] = dense_tn(f"down_wgrad{l}", sv["act"], d_dn)
        ct = DFF // 2
        cj = lambda j: j
        (dua, dug), dp = block_bwd(f"act_bwd{l}", f_act, n, sv["act_t"], sv["act_p"], [T1(dact, ct, cj)],
                                   [(DFF, BF16, ct, cj), (DFF, BF16, ct, cj)], n_col=2)
        g["conv_w"][l] = jnp.concatenate([dp[0][0], dp[1][0]], axis=-1)
        g["conv_b"][l] = jnp.concatenate([dp[2][0, 0], dp[3][0, 0]], axis=-1)
        dh2 = dense_nt2(f"up_bwd{l}", dua, dug, w_up[l])
        g["w_up"][l] = [dense_tn(f"up_wgrad{l}{part}", sv["h2"], du_) for part, du_ in (("a", dua), ("g", dug))]
        (dxc, dz), dp = block_bwd(f"ln1_bwd{l}", f_ln, n, [T1(sv["xc"]), T1(sv["z"])], sv["ln1_p"],
                                  [T1(dx1), T1(dh2)], [fgrad, fgrad])
        dmod[l][2], g["ln1_g"][l], g["ln1_b"][l], dmod[l][3], dmod[l][4] = dp
        dy = dense_nt(f"out_bwd{l}", dz, w_out[l])
        g["w_out"][l] = dense_tn(f"out_wgrad{l}", sv["y"], dz)
        half = (S5W, F32, S5W, _c0)
        (dpu, dpg, dys, dos), dp = block_bwd(f"mix_bwd{l}", f_mix, n, sv["mix_t"], sv["mix_p"], [T1(dy)],
                                                   [half, half, half, None, half, None])
        g["s5_d"][l], g["w_glu"][l], g["b_glu"][l], g["hg_norm_w"][l] = dp[0][0, 0], dp[1][0], dp[2][0, 0], dp[3][0, 0]

        def start_scatter(tag, ks):
            by_cols = lambda a, nb=N_DEV: jnp.moveaxis(a.reshape(a.shape[0], nb, -1), 1, 0)
            by_rows = lambda a: a.reshape(N_DEV, -1, a.shape[1])

            def blocks(k):
                if k == "w_up":
                    return jnp.concatenate([by_cols(half, N_DEV // 2) for half in g[k][l]], axis=0)
                return (by_rows if k in ("w_glu", "w_out", "w_down") else by_cols)(g[k][l].astype(BF16 if k != "conv_w" else F32))

            sends = [blocks(k) for k in ks]
            handle, token = exchange_start(f"scatter_start{l}{tag}", sends, [_own_block_set(s, True) for s in sends], True)
            scatters[l].append((ks, handle))
            return token[0, 0]

        lbs_b = sv["lbs"]
        if l == 0:
            started = start_scatter("a", ["w_glu", "w_out", "w_up", "w_down", "conv_w"])
            lbs_b = [b + started for b in lbs_b]
        gb = [gla_bwd(f"gla_bwd{l}_{d}", sv["proj"], lbs_b[d], sv["gl"][d][1], dos, d == 1) for d in range(2)]
        g["dlb"][l] = jnp.concatenate([gb[0][3], gb[1][3]], axis=-1)[0]
        sb = [None, None]
        for d in range(2):
            sb[d], *s5_totals = s5_bwd(f"s5_bwd{l}_{d}", sv["proj"], dys, sv["s5"][d][0], sv["s5"][d][1], s5a, s5b, s5c,
                                       2 * l + d, d == 1, s5_totals)
        asm_t = [T1(dpu, S5W), T1(sb[0], S5W), T1(sb[1], S5W), T1(gb[0][0], HGW), T1(gb[1][0], HGW),
                 T1(gb[0][1], HGW), T1(gb[1][1], HGW), T1(gb[0][2], HGW), T1(gb[1][2], HGW), T1(dpg, HGW)]
        (dproj,) = block_fwd(
            f"dproj{l}", lambda tv, pv, i: (jnp.concatenate(
                [tv[0] + tv[1] + tv[2], tv[3], tv[4], tv[5] + tv[6], tv[7] + tv[8], tv[9]], axis=-1),),
            n, asm_t, [], [(INC, BF16, INC, _c0)])
        dh_next = dense_nt(f"in_bwd{l}", dproj, w_in[l])
        g["w_in"][l] = dense_tn(f"in_wgrad{l}", sv["h"], dproj)
        started = start_scatter("b", ["w_in"]) if l == 0 else start_scatter("", BIG + ["conv_w"])
        if l:
            gate, wd_, cb_ = saved[l - 1]["ln2_p"][0]
            saved[l - 1]["ln2_p"][0] = (gate + started, wd_, cb_)
    (dxc,), dp = block_bwd("mod0_bwd", f_mod_id, n, [T1(xc0)], [T1(mvec(0, 0)), T1(mvec(0, 1))],
                           [T1(dh_next), T1(dxc)], [fgrad])
    dmod[0][0], dmod[0][1] = dp
    grad_x = dxc[n - p["x"].shape[1]:][None]

    d_prep = s5_prep_bwd(*prep_in, *s5_totals)
    from_hp = lambda a: jnp.moveaxis(a.reshape(nl, 2, S5H, S5W // S5H, 64), 2, -1)
    gs5 = {"s5_lam_re": d_prep[0].reshape(nl, 2, 32, 64), "s5_lam_im": d_prep[1].reshape(nl, 2, 32, 64),
           "s5_log_dt": d_prep[2].reshape(nl, 2, 32), "s5_b_re": from_hp(d_prep[3]), "s5_b_im": from_hp(d_prep[4]),
           "s5_c_re": jnp.swapaxes(d_prep[5].reshape(nl, 2, S5H, 32, 64), 2, 3),
           "s5_c_im": jnp.swapaxes(d_prep[6].reshape(nl, 2, S5H, 32, 64), 2, 3)}
    d_hg = lb_call(hg_full[0], jnp.stack(g["dlb"]))

    dmod_loc = jnp.stack([jnp.concatenate([dmod[l][k][:, 0] for k in range(6)], axis=-1) for l in range(nl)])
    (g_dmod,) = _exchange("gather_dmod", [dmod_loc], False)
    gcols = lax.dynamic_slice_in_dim(g_dmod, me * cols, cols, axis=3)
    g16 = jnp.concatenate([jnp.moveaxis(gcols[:, :, 1], 0, 1), jnp.moveaxis(gcols[:, :, 0], 0, 1)], axis=1)
    grad_w_mod, dcraw = mod_bwd(craw, p["w_mod"], g16)
    d_c_ctx = jnp.sum(dcraw[8:], axis=0)

    stk = lambda k: jnp.stack(g[k])
    small_g = {"c_ctx": d_c_ctx, "b_mod": dmod_loc[:, 0] + dmod_loc[:, 1], "s5_d": stk("s5_d"), "b_glu": stk("b_glu"),
               "hg_lb": d_hg.reshape(nl, 2, HGW), "hg_norm_w": stk("hg_norm_w"), "ln1_g": stk("ln1_g")[:, 0, 0],
               "ln1_b": stk("ln1_b")[:, 0, 0], "conv_b": stk("conv_b"), "ln2_g": stk("ln2_g")[:, 0, 0],
               "ln2_b": stk("ln2_b")[:, 0, 0], **gs5}
    g_pack = _pack([small_g[k] for k in SMALL]).reshape(N_DEV, -1, PACK_W)
    small_scatter, _ = exchange_start("scatter_small_start", [g_pack], [_own_block_set(g_pack, True)], True)

    out = {}
    kinds = ("grad_", "delta_", "new_m_", "new_v_")
    results = {}

    def update_layer(k, l, gs):
        prev = results.get(k) or [lax.empty(p[k].shape, F32) for _ in kinds]
        results[k] = adamw_layer(f"adamw_{k}{l}", l, p[k], p["m_" + k], p["v_" + k], gs, prev)

    for l in range(nl):
        update_layer("w_mod", l, grad_w_mod[l][None])
    for l in reversed(range(nl)):
        for i, (ks, handle) in enumerate(scatters[l]):
            recv = exchange_wait(f"scatter_wait{l}_{i}", handle, results["w_mod"][0], True)
            for k, gsum in zip(ks, recv):
                update_layer(k, l, gsum)
    for k, res in results.items():
        for kind, a in zip(kinds, res):
            out[kind + k] = a

    (g_parts,) = exchange_wait("scatter_small_wait", small_scatter, results["w_in"][0], True)
    (g_small,) = _exchange("gather_small_grads", [sum_parts(g_parts)], False)
    g_small = g_small.reshape(1, -1, PACK_W)
    hgw = {"": hg_full[0].reshape(nl, 2, HGW), "m_": hg_full[1].reshape(nl, 2, HGW), "v_": hg_full[2].reshape(nl, 2, HGW)}
    full = lambda pre, k: hgw[pre] if k == "hg_lb" else p[pre + k]
    shapes = [full("", k).shape for k in SMALL]
    res = adamw("adamw_small", *[_pack([full(pre, k) for k in SMALL]) for pre in ("", "m_", "v_")], g_small)
    for kind, packed in zip(kinds, res):
        for k, a in zip(SMALL, _unpack(packed, shapes)):
            if k == "hg_lb":
                a = lax.dynamic_slice_in_dim(a, me * (HGW // N_DEV), HGW // N_DEV, axis=2)
            out[kind + k] = a
    return (loss, grad_x, *[out[kind + k] for kind in ("grad_", "delta_", "new_m_", "new_v_") for k in WEIGHTS])


def kernel(x, c, ctx, c_ctx, w_mod, b_mod, w_in, s5_lam_re, s5_lam_im, s5_log_dt, s5_b_re, s5_b_im, s5_c_re, s5_c_im, s5_d, w_glu, b_glu, hg_lb, hg_norm_w, w_out, ln1_g, ln1_b, w_up, conv_w, conv_b, w_down, ln2_g, ln2_b, loss_target, m_c_ctx, m_w_mod, m_b_mod, m_w_in, m_s5_lam_re, m_s5_lam_im, m_s5_log_dt, m_s5_b_re, m_s5_b_im, m_s5_c_re, m_s5_c_im, m_s5_d, m_w_glu, m_b_glu, m_hg_lb, m_hg_norm_w, m_w_out, m_ln1_g, m_ln1_b, m_w_up, m_conv_w, m_conv_b, m_w_down, m_ln2_g, m_ln2_b, v_c_ctx, v_w_mod, v_b_mod, v_w_in, v_s5_lam_re, v_s5_lam_im, v_s5_log_dt, v_s5_b_re, v_s5_b_im, v_s5_c_re, v_s5_c_im, v_s5_d, v_w_glu, v_b_glu, v_hg_lb, v_hg_norm_w, v_w_out, v_ln1_g, v_ln1_b, v_w_up, v_conv_w, v_conv_b, v_w_down, v_ln2_g, v_ln2_b):
    return _step(dict(locals()))
```

```python
import functools
import math

import jax
import jax.numpy as jnp
from jax import lax
from jax.experimental import pallas as pl
from jax.experimental.pallas import tpu as pltpu

F32, BF16 = jnp.float32, jnp.bfloat16
N_DEV = 8
AXES = ("x", "y", "c")
D = 1024
S5W = 512
S5P = 2048
S5H = 16
HGW = 512
HD = 128
NH = 4
CK = 32
DFF = 2816
GRID_W = 64
INC = 3072
ALPHA = 8.0 ** 0.25
LN_EPS = 1e-5
RMS_EPS = 1e-6
LR, B1, B2, EPS, WD, STEP = 0.001, 0.9, 0.999, 1e-08, 0.01, 10
TT = 256
VMEM_MB = 56

NN = ((1,), (0,))
NT = ((1,), (1,))
TN = ((0,), (0,))


def _cparams(n_axes):
    return pltpu.CompilerParams(dimension_semantics=("arbitrary",) * n_axes, vmem_limit_bytes=VMEM_MB << 20)


def _dot(a, b, dims):
    return lax.dot_general(a.astype(BF16), b.astype(BF16), (dims, ((), ())), preferred_element_type=F32)


@jax.custom_vjp
def mm_nn(a, b):
    return _dot(a, b, NN)


@jax.custom_vjp
def mm_nt(a, b):
    return _dot(a, b, NT)


@jax.custom_vjp
def mm_tn(a, b):
    return _dot(a, b, TN)


mm_nn.defvjp(lambda a, b: (_dot(a, b, NN), (a, b)), lambda r, g: (_dot(g, r[1], NT), _dot(r[0], g, TN)))
mm_nt.defvjp(lambda a, b: (_dot(a, b, NT), (a, b)), lambda r, g: (_dot(g, r[1], NN), _dot(g, r[0], TN)))
mm_tn.defvjp(lambda a, b: (_dot(a, b, TN), (a, b)), lambda r, g: (_dot(r[1], g, NT), _dot(r[0], g, NN)))


def _roll_rows(u, s):
    return pltpu.roll(u, s % u.shape[0], 0)


@functools.partial(jax.custom_vjp, nondiff_argnums=(2,))
def shift_rows(u, m, s):
    return _roll_rows(u, s) * m


def _shift_fwd(u, m, s):
    return _roll_rows(u, s) * m, m


def _shift_bwd(s, m, g):
    return _roll_rows(g * m, -s), jnp.zeros_like(m)


shift_rows.defvjp(_shift_fwd, _shift_bwd)


def _scan_tile(pos, nt, rev):
    return jnp.where(pos == 0, 0, nt - pos) if rev else pos


def _exchange(name, arrays, all_to_all):
    k_arr = len(arrays)

    def body(*refs):
        ins, outs = refs[:k_arr], refs[k_arr:2 * k_arr]
        send_sems, recv_sems, local_sems = refs[2 * k_arr:]
        me = lax.axis_index("x") * 4 + lax.axis_index("y") * 2 + lax.axis_index("c")
        local = []
        for k in range(k_arr):
            cp = pltpu.make_async_copy(ins[k].at[me] if all_to_all else ins[k], outs[k].at[me], local_sems.at[k])
            cp.start()
            local.append(cp)
        sends = []
        for d in range(1, N_DEV):
            p = (me + d) % N_DEV
            for k in range(k_arr):
                cp = pltpu.make_async_remote_copy(
                    src_ref=ins[k].at[p] if all_to_all else ins[k], dst_ref=outs[k].at[me],
                    send_sem=send_sems.at[k, d - 1], recv_sem=recv_sems.at[k, d - 1],
                    device_id=(p // 4, (p // 2) % 2, p % 2), device_id_type=pl.DeviceIdType.MESH)
                cp.start()
                sends.append(cp)
        for d in range(1, N_DEV):
            q = (me + N_DEV - d) % N_DEV
            for k in range(k_arr):
                pltpu.make_async_remote_copy(
                    src_ref=ins[k].at[q] if all_to_all else ins[k], dst_ref=outs[k].at[q],
                    send_sem=send_sems.at[k, d - 1], recv_sem=recv_sems.at[k, d - 1],
                    device_id=(q // 4, (q // 2) % 2, q % 2), device_id_type=pl.DeviceIdType.MESH).wait_recv()
        for cp in sends:
            cp.wait_send()
        for cp in local:
            cp.wait()

    shapes = [a.shape if all_to_all else (N_DEV,) + a.shape for a in arrays]
    return pl.pallas_call(
        body, name=name,
        out_shape=[jax.ShapeDtypeStruct(s, a.dtype) for s, a in zip(shapes, arrays)],
        in_specs=[pl.BlockSpec(memory_space=pl.ANY)] * k_arr,
        out_specs=[pl.BlockSpec(memory_space=pl.ANY)] * k_arr,
        scratch_shapes=[pltpu.SemaphoreType.DMA((k_arr, N_DEV - 1)), pltpu.SemaphoreType.DMA((k_arr, N_DEV - 1)),
                        pltpu.SemaphoreType.DMA((k_arr,))],
    )(*arrays)


_HBM = pl.BlockSpec(memory_space=pltpu.HBM)
_SEM = pl.BlockSpec(memory_space=pltpu.SEMAPHORE)
_EFFECT = pltpu.SideEffectType.DATAFLOW_SIDE_EFFECTING


def _peer(i):
    return (i // 4, (i // 2) % 2, i % 2)


def exchange_start(name, srcs, lands, all_to_all, after=None):
    k_arr = len(srcs)
    n_sem = k_arr * (N_DEV - 1)
    extra = [] if after is None else [after]

    def body(*refs):
        ins, lz = refs[:k_arr], refs[k_arr:2 * k_arr]
        first = 2 * k_arr + len(extra)
        send_sems = refs[first:first + n_sem]
        recv_sems = refs[first + n_sem:first + 2 * n_sem]
        me = lax.axis_index("x") * 4 + lax.axis_index("y") * 2 + lax.axis_index("c")
        for d in range(1, N_DEV):
            p = (me + d) % N_DEV
            for k in range(k_arr):
                s = k * (N_DEV - 1) + d - 1
                pltpu.make_async_remote_copy(
                    src_ref=ins[k].at[p] if all_to_all else ins[k], dst_ref=lz[k].at[me],
                    send_sem=send_sems[s], recv_sem=recv_sems[s],
                    device_id=_peer(p), device_id_type=pl.DeviceIdType.MESH).start()
        refs[-1][...] = jnp.zeros_like(refs[-1])

    arrs = list(srcs) + list(lands)
    res = pl.pallas_call(
        body, name=name,
        out_shape=(*[pltpu.SemaphoreType.DMA(())] * (2 * n_sem), *[pltpu.HBM(a.shape, a.dtype) for a in arrs],
                   jax.ShapeDtypeStruct((8, 128), F32)),
        in_specs=[_HBM] * len(arrs) + [pl.BlockSpec(memory_space=pl.ANY)] * len(extra),
        out_specs=(*[_SEM] * (2 * n_sem), *[_HBM] * len(arrs), pl.BlockSpec(memory_space=pltpu.VMEM)),
        input_output_aliases={i: 2 * n_sem + i for i in range(len(arrs))},
        compiler_params=pltpu.CompilerParams(has_side_effects=_EFFECT),
    )(*[pltpu.with_memory_space_constraint(a, pltpu.HBM) for a in arrs], *extra)
    return res[:-1], res[-1]


def exchange_wait(name, handle, after, all_to_all):
    k_arr = len(handle) // (2 * N_DEV)
    n_sem = k_arr * (N_DEV - 1)
    sems, arrs = handle[:2 * n_sem], handle[2 * n_sem:]

    def body(*refs):
        ins, lz = refs[:k_arr], refs[k_arr:2 * k_arr]
        s_sems = refs[2 * k_arr:2 * k_arr + n_sem]
        r_sems = refs[2 * k_arr + n_sem:2 * k_arr + 2 * n_sem]
        me = lax.axis_index("x") * 4 + lax.axis_index("y") * 2 + lax.axis_index("c")
        for d in range(1, N_DEV):
            q = (me + N_DEV - d) % N_DEV
            for k in range(k_arr):
                s = k * (N_DEV - 1) + d - 1
                cp = pltpu.make_async_remote_copy(
                    src_ref=ins[k].at[q] if all_to_all else ins[k], dst_ref=lz[k].at[q],
                    send_sem=s_sems[s], recv_sem=r_sems[s],
                    device_id=_peer(q), device_id_type=pl.DeviceIdType.MESH)
                cp.wait_send()
                cp.wait_recv()

    res = pl.pallas_call(
        body, name=name, out_shape=tuple(pltpu.HBM(a.shape, a.dtype) for a in arrs),
        in_specs=[_HBM] * len(arrs) + [_SEM] * (2 * n_sem) + [pl.BlockSpec(memory_space=pl.ANY)],
        out_specs=tuple([_HBM] * len(arrs)),
        input_output_aliases={i: i for i in range(len(arrs))},
        compiler_params=pltpu.CompilerParams(has_side_effects=_EFFECT),
    )(*arrs, *sems, after)
    return res[k_arr:]


def _own_block_set(src, all_to_all):
    me = lax.axis_index("x") * 4 + lax.axis_index("y") * 2 + lax.axis_index("c")
    own = lax.dynamic_index_in_dim(src, me, 0, keepdims=False) if all_to_all else src
    shape = src.shape if all_to_all else (N_DEV,) + src.shape
    return lax.dynamic_update_index_in_dim(lax.empty(shape, src.dtype), own, me, 0)


def _tile(n, prefs):
    for t in prefs:
        if n % t == 0:
            return t
    raise ValueError(n)


def dense_nn(name, a, w, out_dtype=F32):
    n, k = a.shape
    m = w.shape[1]
    tn, tm = _tile(n, (1088, 256)), _tile(m, (1536, 1024, 1408, 512, 256, 128))

    def body(a_ref, w_ref, o_ref):
        o_ref[...] = _dot(a_ref[...], w_ref[...], NN).astype(o_ref.dtype)

    return pl.pallas_call(
        body, name=name, grid=(m // tm, n // tn), out_shape=jax.ShapeDtypeStruct((n, m), out_dtype),
        in_specs=[pl.BlockSpec((tn, k), lambda j, i: (i, 0)), pl.BlockSpec((k, tm), lambda j, i: (0, j))],
        out_specs=pl.BlockSpec((tn, tm), lambda j, i: (i, j)), compiler_params=_cparams(2))(a, w)


def dense_nt(name, g, w, out_dtype=F32):
    n, m = g.shape
    k = w.shape[0]
    tn, tk = _tile(n, (1088, 256)), _tile(k, (1024, 1408, 512, 256, 128))

    def body(g_ref, w_ref, o_ref):
        o_ref[...] = _dot(g_ref[...], w_ref[...], NT).astype(o_ref.dtype)

    return pl.pallas_call(
        body, name=name, grid=(k // tk, n // tn), out_shape=jax.ShapeDtypeStruct((n, k), out_dtype),
        in_specs=[pl.BlockSpec((tn, m), lambda j, i: (i, 0)), pl.BlockSpec((tk, m), lambda j, i: (j, 0))],
        out_specs=pl.BlockSpec((tn, tk), lambda j, i: (i, j)), compiler_params=_cparams(2))(g, w)


def dense_nt2(name, g1, g2, w, out_dtype=F32):
    n, m = g1.shape
    k = w.shape[0]
    tn, tk = _tile(n, (544, 256)), _tile(k, (1024, 1408, 512, 256, 128))

    def body(g1_ref, g2_ref, w1_ref, w2_ref, o_ref):
        o_ref[...] = (_dot(g1_ref[...], w1_ref[...], NT) + _dot(g2_ref[...], w2_ref[...], NT)).astype(o_ref.dtype)

    half = lambda h: pl.BlockSpec((tk, m), lambda j, i: (j, h))
    rows = pl.BlockSpec((tn, m), lambda j, i: (i, 0))
    return pl.pallas_call(
        body, name=name, grid=(k // tk, n // tn), out_shape=jax.ShapeDtypeStruct((n, k), out_dtype),
        in_specs=[rows, rows, half(0), half(1)], out_specs=pl.BlockSpec((tn, tk), lambda j, i: (i, j)),
        compiler_params=_cparams(2))(g1, g2, w, w)


def dense_tn(name, a, g, out_dtype=BF16):
    n, k = a.shape
    m = g.shape[1]
    tn = _tile(n, (2176, 256) if g.dtype == BF16 else (1088, 256))
    tk = _tile(k, (1024, 1408, 512, 256, 128))
    tm = _tile(m, (1024, 1408, 512, 256, 128))
    nt = n // tn

    def body(a_ref, g_ref, o_ref, acc_ref):
        t = pl.program_id(2)

        @pl.when(t == 0)
        def _():
            acc_ref[...] = jnp.zeros_like(acc_ref)

        acc_ref[...] += _dot(a_ref[...], g_ref[...], TN)

        @pl.when(t == nt - 1)
        def _():
            o_ref[...] = acc_ref[...].astype(o_ref.dtype)

    return pl.pallas_call(
        body, name=name, grid=(k // tk, m // tm, nt), out_shape=jax.ShapeDtypeStruct((k, m), out_dtype),
        in_specs=[pl.BlockSpec((tn, tk), lambda i, j, t: (t, i)), pl.BlockSpec((tn, tm), lambda i, j, t: (t, j))],
        out_specs=pl.BlockSpec((tk, tm), lambda i, j, t: (i, j)),
        scratch_shapes=[pltpu.VMEM((tk, tm), F32)], compiler_params=_cparams(3))(a, g)


def _c0(j):
    return 0


def _tspec(w, cb):
    return pl.BlockSpec((TT, w), lambda j, i: (i, cb(j)))


def _pspec(arr, w, cb):
    two = arr.shape[0] == 2
    return pl.BlockSpec((None, arr.shape[1], w), lambda j, i: (jnp.minimum(i, 1) if two else 0, 0, cb(j)))


def block_fwd(name, fn, n, tiled, params, outs, n_col=1):
    nt_, np_ = len(tiled), len(params)

    def body(*refs):
        i = pl.program_id(1)
        tv = [r[...].astype(F32) for r in refs[:nt_]]
        pv = [r[...].astype(F32) for r in refs[nt_:nt_ + np_]]
        for o_ref, r in zip(refs[nt_ + np_:], fn(tv, pv, i)):
            o_ref[...] = r.astype(o_ref.dtype)

    return pl.pallas_call(
        body, name=name, grid=(n_col, n // TT),
        out_shape=[jax.ShapeDtypeStruct((n, c), dt) for c, dt, _, _ in outs],
        in_specs=[_tspec(w, cb) for _, w, cb in tiled] + [_pspec(a, w, cb) for a, w, cb in params],
        out_specs=[_tspec(w, cb) for _, _, w, cb in outs], compiler_params=_cparams(2),
    )(*[a for a, _, _ in tiled], *[a for a, _, _ in params])


def block_bwd(name, fn, n, tiled, params, cots, grads, n_col=1):
    nt_, np_, nc_ = len(tiled), len(params), len(cots)
    want = [k for k, g in enumerate(grads) if g is not None]

    def body(*refs):
        i = pl.program_id(1)
        tv = [r[...].astype(F32) for r in refs[:nt_]]
        pv = [r[...].astype(F32) for r in refs[nt_:nt_ + np_]]
        cv = [r[...].astype(F32) for r in refs[nt_ + np_:nt_ + np_ + nc_]]
        o_refs = refs[nt_ + np_ + nc_:]
        _, vjp = jax.vjp(lambda t, p: list(fn(t, p, i)), tv, pv)
        dt, dp = vjp(cv)
        for o_ref, k in zip(o_refs, want):
            o_ref[...] = dt[k].astype(o_ref.dtype)
        for o_ref, g, (arr, _, _) in zip(o_refs[len(want):], dp, params):
            first = (i == 0) | (i == 1) if arr.shape[0] == 2 else i == 0

            @pl.when(first)
            def _(o_ref=o_ref):
                o_ref[...] = jnp.zeros_like(o_ref)

            o_ref[...] += g

    res = pl.pallas_call(
        body, name=name, grid=(n_col, n // TT),
        out_shape=[jax.ShapeDtypeStruct((n, grads[k][0]), grads[k][1]) for k in want]
        + [jax.ShapeDtypeStruct(a.shape, F32) for a, _, _ in params],
        in_specs=[_tspec(w, cb) for _, w, cb in tiled] + [_pspec(a, w, cb) for a, w, cb in params]
        + [_tspec(w, cb) for _, w, cb in cots],
        out_specs=[_tspec(grads[k][2], grads[k][3]) for k in want] + [_pspec(a, w, cb) for a, w, cb in params],
        compiler_params=_cparams(2),
    )(*[a for a, _, _ in tiled], *[a for a, _, _ in params], *[a for a, _, _ in cots])
    return res[:len(want)], res[len(want):]


def f_mod(tv, pv, i):
    (x,), (sh, sc) = tv, pv
    return (x * (1.0 + sc) + sh,)


def f_mod_id(tv, pv, i):
    return (f_mod(tv, pv, i)[0], tv[0])


def f_ln(tv, pv, i):
    (x, z), (gate, g, b, sh, sc) = tv, pv
    pre = ALPHA * x + gate * z
    mu = jnp.mean(pre, axis=-1, keepdims=True)
    var = jnp.mean(jnp.square(pre - mu), axis=-1, keepdims=True)
    xn = (pre - mu) * lax.rsqrt(var + LN_EPS) * g + b
    return xn, xn * (1.0 + sc) + sh


def f_mix(tv, pv, i):
    (pu, pg, y0, y1, o0, o1), (d_skip, w_glu, b_glu, norm_w) = tv, pv
    s5y = jax.nn.gelu(y0 + y1 + pu * d_skip)
    s5o = s5y * jax.nn.sigmoid(mm_nn(s5y, w_glu) + b_glu)
    o = o0 + o1
    heads = []
    for h in range(NH):
        oh = o[:, h * HD:(h + 1) * HD]
        heads.append(oh * lax.rsqrt(jnp.mean(jnp.square(oh), axis=-1, keepdims=True) + RMS_EPS) * norm_w)
    hg = jnp.concatenate(heads, axis=-1) * jax.nn.silu(pg)
    return (jnp.concatenate([s5o, hg], axis=-1),)


def f_act(tv, pv, i):
    (ua, ug), (cwa, cwg, cba, cbg) = tv, pv
    t = lax.broadcasted_iota(jnp.int32, (TT, 1), 0)
    lat = i > 0
    m_dn = jnp.where((t == 0) | (lat & (t % GRID_W == 0)), 0.0, 1.0)
    m_up = jnp.where((t == TT - 1) | (lat & (t % GRID_W == GRID_W - 1)), 0.0, 1.0)

    def conv(u, w, b):
        return shift_rows(u, m_dn, 1) * w[0:1] + u * w[1:2] + shift_rows(u, m_up, -1) * w[2:3] + b

    return (jax.nn.silu(conv(ua, cwa, cba)) * conv(ug, cwg, cbg),)


def loss_and_grad(xf, target):
    n = xf.shape[0]

    def body(x_ref, t_ref, dy_ref, l_ref):
        i = pl.program_id(0)

        @pl.when(i == 0)
        def _():
            l_ref[...] = jnp.zeros_like(l_ref)
            dy_ref[...] = jnp.zeros_like(dy_ref)

        @pl.when(i > 0)
        def _():
            e = x_ref[...] - t_ref[...]
            dy_ref[...] = e * (1.0 / D)
            l_ref[...] += 0.5 / D * jnp.sum(jnp.square(e))

    return pl.pallas_call(
        body, name="loss", grid=(n // TT,),
        out_shape=[jax.ShapeDtypeStruct((n, D), F32), jax.ShapeDtypeStruct((8, 128), F32)],
        in_specs=[pl.BlockSpec((TT, D), lambda i: (i, 0)), pl.BlockSpec((TT, D), lambda i: (jnp.maximum(i - 1, 0), 0))],
        out_specs=[pl.BlockSpec((TT, D), lambda i: (i, 0)), pl.BlockSpec((8, 128), lambda i: (0, 0))],
        compiler_params=_cparams(1))(xf, target)


def f_prep(lr, li, ldt, bre, bim, cre, cim):
    gi = lax.broadcasted_iota(jnp.int32, (S5W // S5H, S5P), 0)
    gc = lax.broadcasted_iota(jnp.int32, (S5W // S5H, S5P), 1) // 64
    dt = jnp.exp(jnp.sum(jnp.where(gi == gc, ldt, 0.0), axis=0, keepdims=True))
    mag, ang = jnp.exp(lr * dt), li * dt
    ar, ai = mag * jnp.cos(ang), mag * jnp.sin(ang)
    den = lr * lr + li * li
    nr, ni = ar - 1.0, ai
    cr = (nr * lr + ni * li) / den
    ci = (ni * lr - nr * li) / den
    bbr = cr * bre - ci * bim
    bbi = cr * bim + ci * bre
    rg = lax.broadcasted_iota(jnp.int32, (S5W, S5P), 0) // S5H
    cg = lax.broadcasted_iota(jnp.int32, (S5W, S5P), 1) // 64
    mask = (rg == cg).astype(F32)
    blk = lambda a: jnp.concatenate([a] * (S5W // S5H), axis=0) * mask
    return ar, ai, blk(bbr), blk(bbi), blk(cre), blk(-cim)


def s5_prep(lr, li, ldt, bre, bim, cre, cim):
    n2 = lr.shape[0]

    def body(lr_r, li_r, ldt_r, bre_r, bim_r, cre_r, cim_r, a_ref, b_ref, c_ref):
        ar, ai, bbr, bbi, cbr, cbi = f_prep(lr_r[...], li_r[...], ldt_r[...], bre_r[...], bim_r[...], cre_r[...], cim_r[...])
        a_ref[0], a_ref[1] = ar, ai
        b_ref[0], b_ref[1] = bbr.astype(BF16), bbi.astype(BF16)
        c_ref[0], c_ref[1] = cbr.astype(BF16), cbi.astype(BF16)

    sp = lambda r, c: pl.BlockSpec((None, r, c), lambda i: (i, 0, 0))
    sp4 = lambda r, c: pl.BlockSpec((None, 2, r, c), lambda i: (i, 0, 0, 0))
    return pl.pallas_call(
        body, name="s5_prep", grid=(n2,),
        out_shape=[jax.ShapeDtypeStruct((n2, 2, 1, S5P), F32), jax.ShapeDtypeStruct((n2, 2, S5W, S5P), BF16),
                   jax.ShapeDtypeStruct((n2, 2, S5W, S5P), BF16)],
        in_specs=[sp(1, S5P), sp(1, S5P), sp(32, 1), sp(S5H, S5P), sp(S5H, S5P), sp(S5H, S5P), sp(S5H, S5P)],
        out_specs=[sp4(1, S5P), sp4(S5W, S5P), sp4(S5W, S5P)], compiler_params=_cparams(1),
    )(lr, li, ldt, bre, bim, cre, cim)


def s5_prep_bwd(lr, li, ldt, bre, bim, cre, cim, da, db, dc):
    n2 = lr.shape[0]

    def body(lr_r, li_r, ldt_r, bre_r, bim_r, cre_r, cim_r, da_r, db_r, dc_r, *outs):
        args = [r[...] for r in (lr_r, li_r, ldt_r, bre_r, bim_r, cre_r, cim_r)]
        _, vjp = jax.vjp(f_prep, *args)
        for o_ref, g in zip(outs, vjp((da_r[0], da_r[1], db_r[0], db_r[1], dc_r[0], dc_r[1]))):
            o_ref[...] = g

    sp = lambda r, c: pl.BlockSpec((None, r, c), lambda i: (i, 0, 0))
    sp4 = lambda r, c: pl.BlockSpec((None, 2, r, c), lambda i: (i, 0, 0, 0))
    ins = [sp(1, S5P), sp(1, S5P), sp(32, 1), sp(S5H, S5P), sp(S5H, S5P), sp(S5H, S5P), sp(S5H, S5P)]
    return pl.pallas_call(
        body, name="s5_prep_bwd", grid=(n2,),
        out_shape=[jax.ShapeDtypeStruct(a.shape, F32) for a in (lr, li, ldt, bre, bim, cre, cim)],
        in_specs=ins + [sp4(1, S5P), sp4(S5W, S5P), sp4(S5W, S5P)], out_specs=ins, compiler_params=_cparams(1),
    )(lr, li, ldt, bre, bim, cre, cim, da, db, dc)


S5_DIAG = 4
_CU, _CP = S5W // S5_DIAG, S5P // S5_DIAG


def _bd_nn(u, w_ref, k):
    return jnp.concatenate([_dot(u[:, j * _CU:(j + 1) * _CU], w_ref[k, j * _CU:(j + 1) * _CU, j * _CP:(j + 1) * _CP], NN)
                            for j in range(S5_DIAG)], axis=1)


def _bd_nt(x, w_ref, k):
    return jnp.concatenate([_dot(x[:, j * _CP:(j + 1) * _CP], w_ref[k, j * _CU:(j + 1) * _CU, j * _CP:(j + 1) * _CP], NT)
                            for j in range(S5_DIAG)], axis=1)


def _bd_tn_acc(acc_ref, k, a, g):
    for j in range(S5_DIAG):
        acc_ref[k, j * _CU:(j + 1) * _CU, j * _CP:(j + 1) * _CP] += _dot(a[:, j * _CU:(j + 1) * _CU],
                                                                         g[:, j * _CP:(j + 1) * _CP], TN)


def _scan_rows(xr_ref, xi_ref, ar, ai, desc, cr_ref, ci_ref):
    unroll = 8

    def group(gi, carry):
        cr, ci = carry
        base = gi * unroll
        for j in range(unroll):
            t = TT - 1 - (base + j) if desc else base + j
            nr = ar * cr - ai * ci + xr_ref[pl.ds(t, 1), :]
            ni = ar * ci + ai * cr + xi_ref[pl.ds(t, 1), :]
            xr_ref[pl.ds(t, 1), :] = nr
            xi_ref[pl.ds(t, 1), :] = ni
            cr, ci = nr, ni
        return cr, ci

    cr, ci = lax.fori_loop(0, TT // unroll, group, (cr_ref[...], ci_ref[...]))
    cr_ref[...] = cr
    ci_ref[...] = ci


def s5_fwd(name, proj, a, bb, cb, ld, rev):
    n = proj.shape[0]
    nt = n // TT

    def body(u_ref, a_ref, b_ref, c_ref, xr_ref, xi_ref, y_ref, cr_ref, ci_ref):
        @pl.when(pl.program_id(0) == 0)
        def _():
            cr_ref[...] = jnp.zeros_like(cr_ref)
            ci_ref[...] = jnp.zeros_like(ci_ref)

        u = u_ref[...]
        xr_ref[...] = _bd_nn(u, b_ref, 0)
        xi_ref[...] = _bd_nn(u, b_ref, 1)
        _scan_rows(xr_ref, xi_ref, a_ref[0], a_ref[1], rev, cr_ref, ci_ref)
        y_ref[...] = _bd_nt(xr_ref[...], c_ref, 0) + _bd_nt(xi_ref[...], c_ref, 1)

    tile = lambda w: pl.BlockSpec((TT, w), lambda s: (_scan_tile(s, nt, rev), 0))
    par = lambda r: pl.BlockSpec((None, 2, r, S5P), lambda s: (ld, 0, 0, 0))
    return pl.pallas_call(
        body, name=name, grid=(nt,),
        out_shape=[jax.ShapeDtypeStruct((n, S5P), F32), jax.ShapeDtypeStruct((n, S5P), F32),
                   jax.ShapeDtypeStruct((n, S5W), F32)],
        in_specs=[tile(S5W), par(1), par(S5W), par(S5W)], out_specs=[tile(S5P), tile(S5P), tile(S5W)],
        scratch_shapes=[pltpu.VMEM((1, S5P), F32), pltpu.VMEM((1, S5P), F32)], compiler_params=_cparams(1),
    )(proj, a, bb, cb)


def s5_bwd(name, proj, dy, xr, xi, a, bb, cb, ld, rev, totals):
    n = proj.shape[0]
    nt = n // TT
    tb = TT // 8

    def tile_of(s):
        return _scan_tile(nt - 1 - s, nt, rev)

    def edge_of(s):
        pos = nt - 1 - s
        prev = _scan_tile(jnp.maximum(pos - 1, 0), nt, rev)
        return prev * tb if rev else jnp.maximum(pos * tb - 1, 0)

    def body(u_ref, dy_ref, xr_ref, xi_ref, er_ref, ei_ref, a_ref, b_ref, c_ref, _ta, _tb, _tc,
             du_ref, da_ref, db_ref, dc_ref, gr_ref, gi_ref, cr_ref, ci_ref):
        s = pl.program_id(0)

        @pl.when(s == 0)
        def _():
            cr_ref[...] = jnp.zeros_like(cr_ref)
            ci_ref[...] = jnp.zeros_like(ci_ref)
            da_ref[...] = jnp.zeros_like(da_ref)
            db_ref[...] = jnp.zeros_like(db_ref)
            dc_ref[...] = jnp.zeros_like(dc_ref)

        dyv, u = dy_ref[...], u_ref[...]
        xrv, xiv = xr_ref[...], xi_ref[...]
        gr_ref[...] = _bd_nn(dyv, c_ref, 0)
        gi_ref[...] = _bd_nn(dyv, c_ref, 1)
        _bd_tn_acc(dc_ref, 0, dyv, xrv)
        _bd_tn_acc(dc_ref, 1, dyv, xiv)
        _scan_rows(gr_ref, gi_ref, a_ref[0], -a_ref[1], not rev, cr_ref, ci_ref)
        g_r, g_i = gr_ref[...], gi_ref[...]
        rows = lax.broadcasted_iota(jnp.int32, (TT, 1), 0)
        live = jnp.where(s == nt - 1, 0.0, 1.0)
        if rev:
            pr = jnp.where(rows == TT - 1, er_ref[0:1, :] * live, _roll_rows(xrv, -1))
            pi = jnp.where(rows == TT - 1, ei_ref[0:1, :] * live, _roll_rows(xiv, -1))
        else:
            pr = jnp.where(rows == 0, er_ref[7:8, :] * live, _roll_rows(xrv, 1))
            pi = jnp.where(rows == 0, ei_ref[7:8, :] * live, _roll_rows(xiv, 1))
        da_ref[0] += jnp.sum(g_r * pr + g_i * pi, axis=0, keepdims=True)
        da_ref[1] += jnp.sum(g_i * pr - g_r * pi, axis=0, keepdims=True)
        du_ref[...] = _bd_nt(g_r, b_ref, 0) + _bd_nt(g_i, b_ref, 1)
        _bd_tn_acc(db_ref, 0, u, g_r)
        _bd_tn_acc(db_ref, 1, u, g_i)

    tile = lambda w: pl.BlockSpec((TT, w), lambda s: (tile_of(s), 0))
    edge = pl.BlockSpec((8, S5P), lambda s: (edge_of(s), 0))
    par = lambda r: pl.BlockSpec((None, 2, r, S5P), lambda s: (ld, 0, 0, 0))
    whole = pl.BlockSpec(memory_space=pl.ANY)
    return pl.pallas_call(
        body, name=name, grid=(nt,),
        out_shape=[jax.ShapeDtypeStruct((n, S5W), F32)] + [jax.ShapeDtypeStruct(t.shape, F32) for t in totals],
        in_specs=[tile(S5W), tile(S5W), tile(S5P), tile(S5P), edge, edge, par(1), par(S5W), par(S5W), whole, whole, whole],
        out_specs=[tile(S5W), par(1), par(S5W), par(S5W)], input_output_aliases={9: 1, 10: 2, 11: 3},
        scratch_shapes=[pltpu.VMEM((TT, S5P), F32), pltpu.VMEM((TT, S5P), F32),
                        pltpu.VMEM((1, S5P), F32), pltpu.VMEM((1, S5P), F32)], compiler_params=_cparams(1),
    )(proj, dy, xr, xi, xr, xi, a, bb, cb, *totals)


def gla_tile(r, v, qr, lb, sts, rev):
    ncc = TT // CK
    f = lb + (1.0 - lb) * jax.nn.sigmoid(r)
    k, lf, q = 1.0 - f, jnp.log(f), jax.nn.silu(qr)
    rows = lax.broadcasted_iota(jnp.int32, (TT, 1), 0)
    pos = rows % CK
    b = lf
    for s in (1, 2, 4, 8, 16):
        m = ((pos < CK - s) if rev else (pos >= s)).astype(F32)
        b = b + shift_rows(b, m, -s if rev else s)
    etot = [jnp.sum(lf[c * CK:(c + 1) * CK], axis=0, keepdims=True) for c in range(ncc)]
    e = jnp.concatenate([jnp.broadcast_to(t, (CK, HGW)) for t in etot], axis=0)
    kd, qe, qa = k * jnp.exp(e - b), q * jnp.exp(b), q * jnp.exp(b - e)
    cm = [(rows // CK == c).astype(F32) for c in range(ncc)]
    r2 = lax.broadcasted_iota(jnp.int32, (TT, TT), 0)
    c2 = lax.broadcasted_iota(jnp.int32, (TT, TT), 1)
    amask = (r2 // CK == c2 // CK) & ((r2 <= c2) if rev else (r2 >= c2))
    outs, new_sts = [], []
    for h in range(NH):
        ln = slice(h * HD, (h + 1) * HD)
        kdh, qeh, vh = kd[:, ln], qe[:, ln], v[:, ln]
        att = jnp.where(amask, mm_nt(qa[:, ln], kdh), 0.0)
        ds = mm_tn(jnp.concatenate([kdh * cm[c] for c in range(ncc)], axis=1), vh)
        st, starts = sts[h], [None] * ncc
        for c in (reversed(range(ncc)) if rev else range(ncc)):
            starts[c] = st
            dec = jnp.transpose(jnp.broadcast_to(jnp.exp(etot[c][:, ln]), (HD, HD)))
            st = dec * st + ds[c * HD:(c + 1) * HD]
        new_sts.append(st)
        qex = jnp.concatenate([qeh * cm[c] for c in range(ncc)], axis=1)
        outs.append(mm_nn(att, vh) + mm_nn(qex, jnp.concatenate(starts, axis=0)))
    return jnp.concatenate(outs, axis=1), new_sts


def _gla_specs(n, rev, order):
    nt = n // TT
    fcol = 2 if rev else 1
    tile = lambda cbk: pl.BlockSpec((TT, HGW), lambda s: (order(s), cbk))
    return nt, [tile(fcol), tile(3), tile(4)], tile(0)


def gla_fwd(name, proj, lb, rev):
    n = proj.shape[0]
    nt, in_tiles, out_tile = _gla_specs(n, rev, lambda s: _scan_tile(s, n // TT, rev))

    def body(r_ref, v_ref, q_ref, lb_ref, o_ref, st_ref, s_ref):
        @pl.when(pl.program_id(0) == 0)
        def _():
            s_ref[...] = jnp.zeros_like(s_ref)

        sts = [s_ref[h] for h in range(NH)]
        for h in range(NH):
            st_ref[h] = sts[h]
        o, new = gla_tile(r_ref[...], v_ref[...], q_ref[...], lb_ref[...], sts, rev)
        o_ref[...] = o
        for h in range(NH):
            s_ref[h] = new[h]

    st_spec = pl.BlockSpec((None, NH, HD, HD), lambda s: (_scan_tile(s, nt, rev), 0, 0, 0))
    return pl.pallas_call(
        body, name=name, grid=(nt,),
        out_shape=[jax.ShapeDtypeStruct((n, HGW), F32), jax.ShapeDtypeStruct((nt, NH, HD, HD), F32)],
        in_specs=in_tiles + [pl.BlockSpec((1, HGW), lambda s: (0, 0))], out_specs=[out_tile, st_spec],
        scratch_shapes=[pltpu.VMEM((NH, HD, HD), F32)], compiler_params=_cparams(1),
    )(proj, proj, proj, lb)


def gla_bwd(name, proj, lb, st_all, do, rev):
    n = proj.shape[0]
    order = lambda s: _scan_tile(n // TT - 1 - s, n // TT, rev)
    nt, in_tiles, out_tile = _gla_specs(n, rev, order)

    def body(r_ref, v_ref, q_ref, lb_ref, st_ref, do_ref, dr_ref, dv_ref, dq_ref, dlb_ref, ds_ref):
        @pl.when(pl.program_id(0) == 0)
        def _():
            ds_ref[...] = jnp.zeros_like(ds_ref)
            dlb_ref[...] = jnp.zeros_like(dlb_ref)

        _, vjp = jax.vjp(functools.partial(gla_tile, rev=rev), r_ref[...], v_ref[...], q_ref[...], lb_ref[...],
                         [st_ref[h] for h in range(NH)])
        dr, dv, dq, dlb, dsts = vjp((do_ref[...], [ds_ref[h] for h in range(NH)]))
        dr_ref[...] = dr
        dv_ref[...] = dv
        dq_ref[...] = dq
        dlb_ref[...] += dlb
        for h in range(NH):
            ds_ref[h] = dsts[h]

    st_spec = pl.BlockSpec((None, NH, HD, HD), lambda s: (order(s), 0, 0, 0))
    row = pl.BlockSpec((1, HGW), lambda s: (0, 0))
    return pl.pallas_call(
        body, name=name, grid=(nt,),
        out_shape=[jax.ShapeDtypeStruct((n, HGW), F32)] * 3 + [jax.ShapeDtypeStruct((1, HGW), F32)],
        in_specs=in_tiles + [row, st_spec, out_tile], out_specs=[out_tile] * 3 + [row],
        scratch_shapes=[pltpu.VMEM((NH, HD, HD), F32)], compiler_params=_cparams(1),
    )(proj, proj, proj, lb, st_all, do)


def f_lb(rows):
    mx = functools.reduce(jnp.maximum, rows)
    ex = [jnp.exp(r - mx) for r in rows]
    tot = functools.reduce(jnp.add, ex)
    out, acc = [jnp.zeros_like(rows[0])], None
    for e in ex[1:]:
        acc = e / tot if acc is None else acc + e / tot
        out.append(acc)
    return out


def lb_call(hg, dlb=None):
    nl = hg.shape[0]

    def body(*refs):
        rows = [refs[0][l:l + 1, :] for l in range(nl)]
        if dlb is None:
            res = f_lb(rows)
        else:
            _, vjp = jax.vjp(f_lb, rows)
            (res,) = vjp([refs[1][l:l + 1, :] for l in range(nl)])
        for l in range(nl):
            refs[-1][l:l + 1, :] = res[l]

    args = (hg,) if dlb is None else (hg, dlb)
    return pl.pallas_call(body, name="lower_bounds" if dlb is None else "lower_bounds_bwd",
                          out_shape=jax.ShapeDtypeStruct(hg.shape, F32))(*args)


def mod_fwd(craw, w_mod, b_cols):
    nl, _, cols = w_mod.shape

    def body(c_ref, w_ref, b_ref, o_ref):
        o_ref[...] = _dot(jax.nn.silu(c_ref[...]), w_ref[...], NN) + b_ref[...]

    return pl.pallas_call(
        body, name="mod_fwd", grid=(nl,), out_shape=jax.ShapeDtypeStruct((nl, 16, cols), F32),
        in_specs=[pl.BlockSpec((16, D), lambda l: (0, 0)), pl.BlockSpec((None, D, cols), lambda l: (l, 0, 0)),
                  pl.BlockSpec((None, 1, cols), lambda l: (l, 0, 0))],
        out_specs=pl.BlockSpec((None, 16, cols), lambda l: (l, 0, 0)), compiler_params=_cparams(1))(craw, w_mod, b_cols)


def mod_bwd(craw, w_mod, g):
    nl, _, cols = w_mod.shape

    def body(c_ref, w_ref, g_ref, dw_ref, dc_ref, acc_ref):
        l = pl.program_id(0)

        @pl.when(l == 0)
        def _():
            acc_ref[...] = jnp.zeros_like(acc_ref)

        c = c_ref[...]
        s, vjp = jax.vjp(jax.nn.silu, c)
        dw_ref[...] = _dot(s, g_ref[...], TN)
        acc_ref[...] += _dot(g_ref[...], w_ref[...], NT)

        @pl.when(l == nl - 1)
        def _():
            dc_ref[...] = vjp(acc_ref[...])[0]

    return pl.pallas_call(
        body, name="mod_bwd", grid=(nl,),
        out_shape=[jax.ShapeDtypeStruct(w_mod.shape, F32), jax.ShapeDtypeStruct((16, D), F32)],
        in_specs=[pl.BlockSpec((16, D), lambda l: (0, 0)), pl.BlockSpec((None, D, cols), lambda l: (l, 0, 0)),
                  pl.BlockSpec((None, 16, cols), lambda l: (l, 0, 0))],
        out_specs=[pl.BlockSpec((None, D, cols), lambda l: (l, 0, 0)), pl.BlockSpec((16, D), lambda l: (0, 0))],
        scratch_shapes=[pltpu.VMEM((16, D), F32)], compiler_params=_cparams(1))(craw, w_mod, g)


def sum_parts(parts):
    def body(p_ref, o_ref):
        acc = p_ref[0]
        for k in range(1, parts.shape[0]):
            acc = acc + p_ref[k]
        o_ref[...] = acc

    return pl.pallas_call(body, name="sum_small_grads", out_shape=jax.ShapeDtypeStruct(parts.shape[1:], parts.dtype),
                          compiler_params=pltpu.CompilerParams(vmem_limit_bytes=VMEM_MB << 20))(parts)


def _adamw_body(s):
    def body(w_ref, m_ref, v_ref, g_ref, *rest):
        go_ref, d_ref, mo_ref, vo_ref = rest[-4:]
        g = g_ref[0].astype(F32)
        for k in range(1, s):
            g = g + g_ref[k].astype(F32)
        m_new = B1 * m_ref[...] + (1.0 - B1) * g
        v_new = B2 * v_ref[...] + (1.0 - B2) * jnp.square(g)
        m_hat = m_new / (1.0 - B1 ** STEP)
        v_hat = v_new / (1.0 - B2 ** STEP)
        go_ref[...] = g
        d_ref[...] = -LR * (m_hat / (jnp.sqrt(v_hat) + EPS) + WD * w_ref[...])
        mo_ref[...] = m_new
        vo_ref[...] = v_new

    return body


def _row_tile(r):
    return max([t for t in range(8, 257, 8) if r % t == 0], default=r)


def adamw(name, w, m, v, gs):
    r, c = w.shape
    s = gs.shape[0]
    tr = _row_tile(r)
    blk = pl.BlockSpec((tr, c), lambda i: (i, 0))
    return pl.pallas_call(
        _adamw_body(s), name=name, grid=(r // tr,), out_shape=[jax.ShapeDtypeStruct((r, c), F32)] * 4,
        in_specs=[blk, blk, blk, pl.BlockSpec((s, tr, c), lambda i: (0, i, 0))], out_specs=[blk] * 4,
        compiler_params=_cparams(1))(w, m, v, gs)


def adamw_layer(name, l, w, m, v, gs, outs):
    _, r, c = w.shape
    s = gs.shape[0]
    tr = _row_tile(r)
    blk = pl.BlockSpec((None, tr, c), lambda i: (l, i, 0))
    whole = pl.BlockSpec(memory_space=pl.ANY)
    return pl.pallas_call(
        _adamw_body(s), name=name, grid=(r // tr,), out_shape=[jax.ShapeDtypeStruct(w.shape, F32)] * 4,
        in_specs=[blk, blk, blk, pl.BlockSpec((s, tr, c), lambda i: (0, i, 0))] + [whole] * 4, out_specs=[blk] * 4,
        input_output_aliases={4 + i: i for i in range(4)}, compiler_params=_cparams(1))(w, m, v, gs, *outs)


SMALL = ["c_ctx", "b_mod", "s5_lam_re", "s5_lam_im", "s5_log_dt", "s5_b_re", "s5_b_im", "s5_c_re", "s5_c_im", "s5_d",
         "b_glu", "hg_lb", "hg_norm_w", "ln1_g", "ln1_b", "conv_b", "ln2_g", "ln2_b"]
BIG = ["w_in", "w_glu", "w_out", "w_up", "w_down"]
WEIGHTS = ["c_ctx", "w_mod", "b_mod", "w_in", "s5_lam_re", "s5_lam_im", "s5_log_dt", "s5_b_re", "s5_b_im", "s5_c_re",
           "s5_c_im", "s5_d", "w_glu", "b_glu", "hg_lb", "hg_norm_w", "w_out", "ln1_g", "ln1_b", "w_up", "conv_w",
           "conv_b", "w_down", "ln2_g", "ln2_b"]
PACK_W = 1024


def _pack_rows(k):
    return -(-k // (8 * PACK_W)) * 8


def _pack(arrs):
    parts = []
    for a in arrs:
        flat = a.reshape(-1)
        r = _pack_rows(flat.shape[0])
        parts.append(jnp.pad(flat, (0, r * PACK_W - flat.shape[0])).reshape(r, PACK_W))
    used = sum(q.shape[0] for q in parts)
    parts.append(jnp.zeros((-used % (8 * N_DEV), PACK_W), parts[0].dtype))
    return jnp.concatenate(parts, axis=0)


def _unpack(p, shapes):
    out, o = [], 0
    for s in shapes:
        k = math.prod(s)
        r = _pack_rows(k)
        out.append(p[o:o + r].reshape(-1)[:k].reshape(s))
        o += r
    return out


def _gathered_cols(g):
    return jnp.moveaxis(g, 0, 2).reshape(g.shape[1], g.shape[2], -1)


def _gathered_rows(g):
    return jnp.moveaxis(g, 0, 1).reshape(g.shape[1], -1, g.shape[3])


def _step(p):
    nl = p["w_in"].shape[0]
    me = lax.axis_index("x") * 4 + lax.axis_index("y") * 2 + lax.axis_index("c")
    xc0 = jnp.concatenate([p["ctx"][0], p["x"][0]], axis=0)
    n = xc0.shape[0]
    target = p["loss_target"][0]

    hg3 = jnp.stack([p[k].reshape(-1) for k in ("hg_lb", "m_hg_lb", "v_hg_lb")])
    g_cw, g_c, g_hg = _exchange("gather_inputs", [p["conv_w"], p["c"], hg3], False)
    conv_w = _gathered_cols(g_cw)
    hg_full = jnp.moveaxis(g_hg.reshape(N_DEV, 3, nl, 2, -1), 0, 3).reshape(3, nl, 2 * HGW)
    lb_all = lb_call(hg_full[0])

    craw = jnp.concatenate([g_c.reshape(N_DEV, D), jnp.broadcast_to(p["c_ctx"][None], (8, D))], axis=0)
    cols = p["w_mod"].shape[2]
    b_cols = lax.dynamic_slice_in_dim(p["b_mod"], me * cols, cols, axis=1)[:, None, :]
    (g_mod,) = _exchange("gather_mod", [mod_fwd(craw, p["w_mod"], b_cols)], False)
    mod_all = jnp.moveaxis(g_mod, 0, 2).reshape(nl, 16, 6 * D)
    mod_x = lax.dynamic_index_in_dim(mod_all, me, axis=1, keepdims=False)
    mod2 = jnp.stack([mod_all[:, 8], mod_x], axis=1)
    mvec = lambda l, k: mod2[l, :, k * D:(k + 1) * D][:, None, :]
    gathers = {}

    def start_gather(l, part, ks, after):
        srcs = [p[k][l].astype(BF16) for k in ks]
        gathers[l, part], token = exchange_start(f"gather_start{l}{part}", srcs, [_own_block_set(s, False) for s in srcs],
                                                 False, after=after)
        return token[0, 0]

    parts0 = {"a": BIG[:1], "b": BIG[1:3], "c": BIG[3:]}
    for part, ks in parts0.items():
        start_gather(0, part, ks, g_mod)

    def gathered(l, part, after):
        res = exchange_wait(f"gather_wait{l}{part}", gathers[l, part], after, False)
        ks = parts0[part] if part else BIG
        cols_ = lambda g: jnp.moveaxis(g, 0, 1).reshape(g.shape[1], -1)
        rows_ = lambda g: g.reshape(-1, g.shape[2])
        return {k: (cols_ if k in ("w_in", "w_up") else rows_)(g) for k, g in zip(ks, res)}

    zvec = jnp.zeros((2, 1, D), F32)
    row = lambda a: a.reshape(1, 1, -1)

    to_hp = lambda a: jnp.moveaxis(a, -1, 2).reshape(nl * 2, S5H, S5P)
    prep_in = [p["s5_lam_re"].reshape(nl * 2, 1, S5P), p["s5_lam_im"].reshape(nl * 2, 1, S5P),
               p["s5_log_dt"].reshape(nl * 2, 32, 1), to_hp(p["s5_b_re"]), to_hp(p["s5_b_im"]),
               jnp.swapaxes(p["s5_c_re"], 2, 3).reshape(nl * 2, S5H, S5P),
               jnp.swapaxes(p["s5_c_im"], 2, 3).reshape(nl * 2, S5H, S5P)]
    s5a, s5b, s5c = s5_prep(*prep_in)

    T1 = lambda a, w=D, cb=_c0: (a, w, cb)
    saved = []
    xc = xc0
    (h,) = block_fwd("mod0", f_mod, n, [T1(xc)], [T1(mvec(0, 0)), T1(mvec(0, 1))], [(D, BF16, D, _c0)])
    w_in, w_glu, w_out, w_up, w_down = ([None] * nl for _ in range(5))
    for l in range(nl):
        wl = gathered(l, "", xc) if l else gathered(0, "a", s5a)
        w_in[l] = wl["w_in"]
        proj = dense_nn(f"in_proj{l}", h, w_in[l])
        s5 = [s5_fwd(f"s5_fwd{l}_{d}", proj, s5a, s5b, s5c, 2 * l + d, d == 1) for d in range(2)]
        lbs = [lb_all[l, d * HGW:(d + 1) * HGW][None] for d in range(2)]
        gl = [gla_fwd(f"gla_fwd{l}_{d}", proj, lbs[d], d == 1) for d in range(2)]
        if l == 0:
            wl = gathered(0, "b", gl[1][0])
        w_glu[l], w_out[l] = wl["w_glu"], wl["w_out"]
        started = start_gather(l + 1, "", BIG, w_out[l]) if l + 1 < nl else 0.0
        mix_t = [T1(proj, S5W), T1(proj, HGW, lambda j: 5), T1(s5[0][2], S5W), T1(s5[1][2], S5W),
                 T1(gl[0][0], HGW), T1(gl[1][0], HGW)]
        mix_p = [T1(row(p["s5_d"][l]) + started, S5W), T1(w_glu[l][None], S5W), T1(row(p["b_glu"][l]), S5W),
                 T1(row(p["hg_norm_w"][l]), HD)]
        (y,) = block_fwd(f"mix{l}", f_mix, n, mix_t, mix_p, [(D, BF16, D, _c0)])
        z = dense_nn(f"out_proj{l}", y, w_out[l])
        ln1_p = [T1(mvec(l, 2)), T1(row(p["ln1_g"][l])), T1(row(p["ln1_b"][l])), T1(mvec(l, 3)), T1(mvec(l, 4))]
        x1, h2 = block_fwd(f"ln1_{l}", f_ln, n, [T1(xc), T1(z)], ln1_p, [(D, F32, D, _c0), (D, BF16, D, _c0)])
        if l == 0:
            wl = gathered(0, "c", h2)
        w_up[l], w_down[l] = wl["w_up"], wl["w_down"]
        up = dense_nn(f"up_proj{l}", h2, w_up[l])
        ct = DFF // 2
        act_t = [T1(up, ct, lambda j: j), T1(up, ct, lambda j: j + 2)]
        cb2 = p["conv_b"][l].reshape(1, 1, -1)
        act_p = [T1(conv_w[l][None, :, :DFF], ct, lambda j: j), T1(conv_w[l][None, :, DFF:], ct, lambda j: j),
                 T1(cb2[:, :, :DFF], ct, lambda j: j), T1(cb2[:, :, DFF:], ct, lambda j: j)]
        (act,) = block_fwd(f"act{l}", f_act, n, act_t, act_p, [(DFF, BF16, ct, lambda j: j)], n_col=2)
        dn = dense_nn(f"down_proj{l}", act, w_down[l])
        nxt = (mvec(l + 1, 0), mvec(l + 1, 1)) if l + 1 < nl else (zvec, zvec)
        ln2_p = [T1(mvec(l, 5)), T1(row(p["ln2_g"][l])), T1(row(p["ln2_b"][l])), T1(nxt[0]), T1(nxt[1])]
        x2, hn = block_fwd(f"ln2_{l}", f_ln, n, [T1(x1), T1(dn)], ln2_p, [(D, F32, D, _c0), (D, BF16, D, _c0)])
        saved.append(dict(xc=xc, h=h, proj=proj, s5=s5, gl=gl, lbs=lbs, mix_t=mix_t, mix_p=mix_p, y=y, z=z,
                          ln1_p=ln1_p, x1=x1, h2=h2, act_t=act_t, act_p=act_p, act=act, dn=dn, ln2_p=ln2_p))
        xc, h = x2, hn

    dxc, loss_part = loss_and_grad(xc, target)
    loss = lax.psum(loss_part[0, 0], AXES)

    g = {k: [None] * nl for k in ("w_in", "w_glu", "w_out", "w_up", "w_down", "conv_w", "conv_b", "s5_d", "b_glu",
                                  "hg_norm_w", "ln1_g", "ln1_b", "ln2_g", "ln2_b", "dlb", "s5")}
    dmod = [[None] * 6 for _ in range(nl)]
    scatters = [[] for _ in range(nl)]
    s5_totals = [lax.empty((2 * nl, 2, r, S5P), F32) for r in (1, S5W, S5W)]
    dh_next = jnp.zeros((n, D), F32)
    fgrad = (D, F32, D, _c0)
    for l in reversed(range(nl)):
        sv = saved[l]
        (dx1, d_dn), dp = block_bwd(f"ln2_bwd{l}", f_ln, n, [T1(sv["x1"]), T1(sv["dn"])], sv["ln2_p"],
                                    [T1(dxc), T1(dh_next)], [fgrad, fgrad])
        dmod[l][5], g["ln2_g"][l], g["ln2_b"][l] = dp[0], dp[1], dp[2]
        if l + 1 < nl:
            dmod[l + 1][0], dmod[l + 1][1] = dp[3], dp[4]
        dact = dense_nt(f"down_bwd{l}", d_dn, w_down[l])
        g["w_down"][l] = dense_tn(f"down_wgrad{l}", sv["act"], d_dn)
        ct = DFF // 2
        cj = lambda j: j
        (dua, dug), dp = block_bwd(f"act_bwd{l}", f_act, n, sv["act_t"], sv["act_p"], [T1(dact, ct, cj)],
                                   [(DFF, BF16, ct, cj), (DFF, BF16, ct, cj)], n_col=2)
        g["conv_w"][l] = jnp.concatenate([dp[0][0], dp[1][0]], axis=-1)
        g["conv_b"][l] = jnp.concatenate([dp[2][0, 0], dp[3][0, 0]], axis=-1)
        dh2 = dense_nt2(f"up_bwd{l}", dua, dug, w_up[l])
        g["w_up"][l] = [dense_tn(f"up_wgrad{l}{part}", sv["h2"], du_) for part, du_ in (("a", dua), ("g", dug))]
        (dxc, dz), dp = block_bwd(f"ln1_bwd{l}", f_ln, n, [T1(sv["xc"]), T1(sv["z"])], sv["ln1_p"],
                                  [T1(dx1), T1(dh2)], [fgrad, fgrad])
        dmod[l][2], g["ln1_g"][l], g["ln1_b"][l], dmod[l][3], dmod[l][4] = dp
        dy = dense_nt(f"out_bwd{l}", dz, w_out[l])
        g["w_out"][l] = dense_tn(f"out_wgrad{l}", sv["y"], dz)
        half = (S5W, F32, S5W, _c0)
        (dpu, dpg, dys, dos), dp = block_bwd(f"mix_bwd{l}", f_mix, n, sv["mix_t"], sv["mix_p"], [T1(dy)],
                                                   [half, half, half, None, half, None])
        g["s5_d"][l], g["w_glu"][l], g["b_glu"][l], g["hg_norm_w"][l] = dp[0][0, 0], dp[1][0], dp[2][0, 0], dp[3][0, 0]

        def start_scatter(tag, ks):
            by_cols = lambda a, nb=N_DEV: jnp.moveaxis(a.reshape(a.shape[0], nb, -1), 1, 0)
            by_rows = lambda a: a.reshape(N_DEV, -1, a.shape[1])

            def blocks(k):
                if k == "w_up":
                    return jnp.concatenate([by_cols(half, N_DEV // 2) for half in g[k][l]], axis=0)
                return (by_rows if k in ("w_glu", "w_out", "w_down") else by_cols)(g[k][l].astype(BF16 if k != "conv_w" else F32))

            sends = [blocks(k) for k in ks]
            handle, token = exchange_start(f"scatter_start{l}{tag}", sends, [_own_block_set(s, True) for s in sends], True)
            scatters[l].append((ks, handle))
            return token[0, 0]

        lbs_b = sv["lbs"]
        if l == 0:
            started = start_scatter("a", ["w_glu", "w_out", "w_up", "w_down", "conv_w"])
            lbs_b = [b + started for b in lbs_b]
        gb = [gla_bwd(f"gla_bwd{l}_{d}", sv["proj"], lbs_b[d], sv["gl"][d][1], dos, d == 1) for d in range(2)]
        g["dlb"][l] = jnp.concatenate([gb[0][3], gb[1][3]], axis=-1)[0]
        sb = [None, None]
        for d in range(2):
            sb[d], *s5_totals = s5_bwd(f"s5_bwd{l}_{d}", sv["proj"], dys, sv["s5"][d][0], sv["s5"][d][1], s5a, s5b, s5c,
                                       2 * l + d, d == 1, s5_totals)
        asm_t = [T1(dpu, S5W), T1(sb[0], S5W), T1(sb[1], S5W), T1(gb[0][0], HGW), T1(gb[1][0], HGW),
                 T1(gb[0][1], HGW), T1(gb[1][1], HGW), T1(gb[0][2], HGW), T1(gb[1][2], HGW), T1(dpg, HGW)]
        (dproj,) = block_fwd(
            f"dproj{l}", lambda tv, pv, i: (jnp.concatenate(
                [tv[0] + tv[1] + tv[2], tv[3], tv[4], tv[5] + tv[6], tv[7] + tv[8], tv[9]], axis=-1),),
            n, asm_t, [], [(INC, BF16, INC, _c0)])
        dh_next = dense_nt(f"in_bwd{l}", dproj, w_in[l])
        g["w_in"][l] = dense_tn(f"in_wgrad{l}", sv["h"], dproj)
        started = start_scatter("b", ["w_in"]) if l == 0 else start_scatter("", BIG + ["conv_w"])
        if l:
            gate, wd_, cb_ = saved[l - 1]["ln2_p"][0]
            saved[l - 1]["ln2_p"][0] = (gate + started, wd_, cb_)
    (dxc,), dp = block_bwd("mod0_bwd", f_mod_id, n, [T1(xc0)], [T1(mvec(0, 0)), T1(mvec(0, 1))],
                           [T1(dh_next), T1(dxc)], [fgrad])
    dmod[0][0], dmod[0][1] = dp
    grad_x = dxc[n - p["x"].shape[1]:][None]

    d_prep = s5_prep_bwd(*prep_in, *s5_totals)
    from_hp = lambda a: jnp.moveaxis(a.reshape(nl, 2, S5H, S5W // S5H, 64), 2, -1)
    gs5 = {"s5_lam_re": d_prep[0].reshape(nl, 2, 32, 64), "s5_lam_im": d_prep[1].reshape(nl, 2, 32, 64),
           "s5_log_dt": d_prep[2].reshape(nl, 2, 32), "s5_b_re": from_hp(d_prep[3]), "s5_b_im": from_hp(d_prep[4]),
           "s5_c_re": jnp.swapaxes(d_prep[5].reshape(nl, 2, S5H, 32, 64), 2, 3),
           "s5_c_im": jnp.swapaxes(d_prep[6].reshape(nl, 2, S5H, 32, 64), 2, 3)}
    d_hg = lb_call(hg_full[0], jnp.stack(g["dlb"]))

    dmod_loc = jnp.stack([jnp.concatenate([dmod[l][k][:, 0] for k in range(6)], axis=-1) for l in range(nl)])
    (g_dmod,) = _exchange("gather_dmod", [dmod_loc], False)
    gcols = lax.dynamic_slice_in_dim(g_dmod, me * cols, cols, axis=3)
    g16 = jnp.concatenate([jnp.moveaxis(gcols[:, :, 1], 0, 1), jnp.moveaxis(gcols[:, :, 0], 0, 1)], axis=1)
    grad_w_mod, dcraw = mod_bwd(craw, p["w_mod"], g16)
    d_c_ctx = jnp.sum(dcraw[8:], axis=0)

    stk = lambda k: jnp.stack(g[k])
    small_g = {"c_ctx": d_c_ctx, "b_mod": dmod_loc[:, 0] + dmod_loc[:, 1], "s5_d": stk("s5_d"), "b_glu": stk("b_glu"),
               "hg_lb": d_hg.reshape(nl, 2, HGW), "hg_norm_w": stk("hg_norm_w"), "ln1_g": stk("ln1_g")[:, 0, 0],
               "ln1_b": stk("ln1_b")[:, 0, 0], "conv_b": stk("conv_b"), "ln2_g": stk("ln2_g")[:, 0, 0],
               "ln2_b": stk("ln2_b")[:, 0, 0], **gs5}
    g_pack = _pack([small_g[k] for k in SMALL]).reshape(N_DEV, -1, PACK_W)
    small_scatter, _ = exchange_start("scatter_small_start", [g_pack], [_own_block_set(g_pack, True)], True)

    out = {}
    kinds = ("grad_", "delta_", "new_m_", "new_v_")
    results = {}

    def update_layer(k, l, gs):
        prev = results.get(k) or [lax.empty(p[k].shape, F32) for _ in kinds]
        results[k] = adamw_layer(f"adamw_{k}{l}", l, p[k], p["m_" + k], p["v_" + k], gs, prev)

    for l in range(nl):
        update_layer("w_mod", l, grad_w_mod[l][None])
    for l in reversed(range(nl)):
        for i, (ks, handle) in enumerate(scatters[l]):
            recv = exchange_wait(f"scatter_wait{l}_{i}", handle, results["w_mod"][0], True)
            for k, gsum in zip(ks, recv):
                update_layer(k, l, gsum)
    for k, res in results.items():
        for kind, a in zip(kinds, res):
            out[kind + k] = a

    (g_parts,) = exchange_wait("scatter_small_wait", small_scatter, results["w_in"][0], True)
    (g_small,) = _exchange("gather_small_grads", [sum_parts(g_parts)], False)
    g_small = g_small.reshape(1, -1, PACK_W)
    hgw = {"": hg_full[0].reshape(nl, 2, HGW), "m_": hg_full[1].reshape(nl, 2, HGW), "v_": hg_full[2].reshape(nl, 2, HGW)}
    full = lambda pre, k: hgw[pre] if k == "hg_lb" else p[pre + k]
    shapes = [full("", k).shape for k in SMALL]
    res = adamw("adamw_small", *[_pack([full(pre, k) for k in SMALL]) for pre in ("", "m_", "v_")], g_small)
    for kind, packed in zip(kinds, res):
        for k, a in zip(SMALL, _unpack(packed, shapes)):
            if k == "hg_lb":
                a = lax.dynamic_slice_in_dim(a, me * (HGW // N_DEV), HGW // N_DEV, axis=2)
            out[kind + k] = a
    return (loss, grad_x, *[out[kind + k] for kind in ("grad_", "delta_", "new_m_", "new_v_") for k in WEIGHTS])


def kernel(x, c, ctx, c_ctx, w_mod, b_mod, w_in, s5_lam_re, s5_lam_im, s5_log_dt, s5_b_re, s5_b_im, s5_c_re, s5_c_im, s5_d, w_glu, b_glu, hg_lb, hg_norm_w, w_out, ln1_g, ln1_b, w_up, conv_w, conv_b, w_down, ln2_g, ln2_b, loss_target, m_c_ctx, m_w_mod, m_b_mod, m_w_in, m_s5_lam_re, m_s5_lam_im, m_s5_log_dt, m_s5_b_re, m_s5_b_im, m_s5_c_re, m_s5_c_im, m_s5_d, m_w_glu, m_b_glu, m_hg_lb, m_hg_norm_w, m_w_out, m_ln1_g, m_ln1_b, m_w_up, m_conv_w, m_conv_b, m_w_down, m_ln2_g, m_ln2_b, v_c_ctx, v_w_mod, v_b_mod, v_w_in, v_s5_lam_re, v_s5_lam_im, v_s5_log_dt, v_s5_b_re, v_s5_b_im, v_s5_c_re, v_s5_c_im, v_s5_d, v_w_glu, v_b_glu, v_hg_lb, v_hg_norm_w, v_w_out, v_ln1_g, v_ln1_b, v_w_up, v_conv_w, v_conv_b, v_w_down, v_ln2_g, v_ln2_b):
    return _step(dict(locals()))
```
